```python
import math
import jax, jax.numpy as jnp
from jax import lax
import numpy as np

D_MODEL = 1024
BATCH = 8
SEQ = 2048
DEPTH = 1

CHUNK = 64
N_META = 16
Q_BLOCK = 128
FRONT_PAD = Q_BLOCK - N_META

GLA_HEADS = 4
GLA_DV = D_MODEL // 8
GLA_DK = GLA_DV // 2
GLA_GATE_RANK = 16
GLA_TAU = 16.0
DIFF_HEADS = 4
DIFF_DH = D_MODEL // 16
DIFF_DV = 2 * DIFF_DH
MIX_WIDTH = GLA_HEADS * GLA_DV + DIFF_HEADS * DIFF_DV

IN_SIZES = (GLA_HEADS * GLA_DK, GLA_HEADS * GLA_DK, GLA_HEADS * GLA_DV, GLA_HEADS * GLA_DV, GLA_GATE_RANK,
            DIFF_HEADS * 2 * DIFF_DH, DIFF_HEADS * 2 * DIFF_DH, DIFF_HEADS * DIFF_DV)
IN_WIDTH = sum(IN_SIZES)
IN_SPLIT_POINTS = tuple(int(c) for c in np.cumsum(IN_SIZES)[:-1])
VALUE_BLOCKS = (2, 7)

N_EXPERTS = 32
TOP_K = 4
D_FF = D_MODEL
SWIGLU_LIMIT = 7.0
SWIGLU_ALPHA = 1.702
MOE_BLOCK = 128

DEEPNORM_ALPHA = (2.0 * DEPTH) ** 0.25
DEEPNORM_BETA = (8.0 * DEPTH) ** -0.25
LN_EPS = 1e-5
NEG_INF = -1e30

kernel_name = "hymba_gla_diffattn_alibi_deepnorm_moe"


def layer_norm(x, g, b):
    xf = x.astype(jnp.float32)
    mu = jnp.mean(xf, axis=-1, keepdims=True)
    var = jnp.mean(jnp.square(xf - mu), axis=-1, keepdims=True)
    return ((xf - mu) * lax.rsqrt(var + LN_EPS) * g.astype(jnp.float32) + b.astype(jnp.float32)).astype(x.dtype)


def rms_norm(x, g):
    xf = x.astype(jnp.float32)
    ms = jnp.mean(jnp.square(xf), axis=-1, keepdims=True)
    return (xf * lax.rsqrt(ms + LN_EPS) * g.astype(jnp.float32)).astype(x.dtype)


def gla_mixer(q, k, v, r, a_lr, wa2, ba, norm_g, valid):
    B, Lp, _ = q.shape
    nc = Lp // CHUNK
    f32 = jnp.float32
    log_a = jax.nn.log_sigmoid((a_lr @ wa2 + ba).astype(f32)) / GLA_TAU
    log_a = jnp.where(valid[None, :, None], log_a, 0.0)

    def chunks(t, d):
        return t.reshape(B, nc, CHUNK, GLA_HEADS, d)

    qc = chunks(q.astype(f32) * GLA_DK ** -0.5, GLA_DK)
    kc = chunks(k.astype(f32), GLA_DK)
    vc = chunks(v.astype(f32), GLA_DV)
    cum = jnp.cumsum(chunks(log_a, GLA_DK), axis=2)
    tot = cum[:, :, -1]
    kdec = kc * jnp.exp(tot[:, :, None] - cum)
    upd = jnp.einsum('bnchk,bnchv->bnhkv', kdec, vc)

    def step(S, inp):
        d, u = inp
        S = jnp.exp(d)[..., None] * S + u
        return S, S

    S0 = jnp.zeros((B, GLA_HEADS, GLA_DK, GLA_DV), f32)
    _, states = lax.scan(step, S0, (jnp.moveaxis(tot, 1, 0), jnp.moveaxis(upd, 1, 0)))
    o = jnp.einsum('bnchk,nbhkv->bnchv', qc, states).reshape(B, Lp, GLA_HEADS, GLA_DV)
    o = rms_norm(o, norm_g) * jax.nn.silu(r.reshape(B, Lp, GLA_HEADS, GLA_DV).astype(f32))
    return o.reshape(B, Lp, GLA_HEADS * GLA_DV).astype(q.dtype)


def diff_attention(q, k, v, lq1, lk1, lq2, lk2, norm_g, valid, lambda_init):
    B, Lp, _ = q.shape
    f32 = jnp.float32
    q = q.reshape(B, Lp, DIFF_HEADS, 2, DIFF_DH)
    k = k.reshape(B, Lp, DIFF_HEADS, 2, DIFF_DH)
    v = v.reshape(B, Lp, DIFF_HEADS, DIFF_DV)
    lam = (jnp.exp(jnp.sum(lq1.astype(f32) * lk1.astype(f32)))
           - jnp.exp(jnp.sum(lq2.astype(f32) * lk2.astype(f32))) + lambda_init)
    slopes = 2.0 ** (-8.0 * jnp.arange(1, DIFF_HEADS + 1, dtype=f32) / DIFF_HEADS)
    pos = jnp.arange(Lp)
    chunk = pos // CHUNK
    scale = DIFF_DH ** -0.5
    outs = []
    for j in range(Lp // Q_BLOCK):
        qs, ke = j * Q_BLOCK, (j + 1) * Q_BLOCK
        s = jnp.einsum('bqhmd,bkhmd->bhmqk', q[:, qs:ke], k[:, :ke]).astype(f32) * scale
        dist = jnp.abs(pos[qs:ke, None] - pos[None, :ke]).astype(f32)
        bias = -slopes[:, None, None, None] * dist
        allowed = (chunk[None, :ke] <= chunk[qs:ke, None]) & valid[None, :ke]
        s = jnp.where(allowed, s + bias, NEG_INF)
        p = jax.nn.softmax(s, axis=-1)
        a = p[:, :, 0] - lam * p[:, :, 1]
        outs.append(jnp.einsum('bhqk,bkhv->bqhv', a.astype(v.dtype), v[:, :ke]))
    o = jnp.concatenate(outs, axis=1)
    o = rms_norm(o, norm_g) * (1.0 - lambda_init)
    return o.reshape(B, Lp, DIFF_HEADS * DIFF_DV).astype(v.dtype)


def moe_ffn(h, router_w, router_b, w_gate_up, b_gate_up, w_down, b_down):
    B, L, D = h.shape
    n_tok = B * L
    xt = h.reshape(n_tok, D)
    logits = (xt @ router_w + router_b).astype(jnp.float32)
    top_logit, top_idx = lax.top_k(logits, TOP_K)
    gates = jax.nn.softmax(top_logit, axis=-1).astype(h.dtype)
    n_assign = n_tok * TOP_K
    flat_e = top_idx.reshape(n_assign)
    order = jnp.argsort(flat_e)
    e_sorted = flat_e[order]
    tok_sorted = (order // TOP_K).astype(jnp.int32)
    counts = jnp.bincount(flat_e, length=N_EXPERTS)
    padded = (counts + MOE_BLOCK - 1) // MOE_BLOCK * MOE_BLOCK
    pad_end = jnp.cumsum(padded)
    pad_start = pad_end - padded
    grp_start = jnp.cumsum(counts) - counts
    dest = (pad_start[e_sorted] + jnp.arange(n_assign) - grp_start[e_sorted]).astype(jnp.int32)
    n_slots = -(-(n_assign + N_EXPERTS * (MOE_BLOCK - 1)) // MOE_BLOCK) * MOE_BLOCK
    n_blocks = n_slots // MOE_BLOCK
    src_tok = jnp.full((n_slots,), n_tok, jnp.int32).at[dest].set(tok_sorted)
    x_ext = jnp.concatenate([xt, jnp.zeros((1, D), h.dtype)], axis=0)
    x_blocks = x_ext[src_tok].reshape(n_blocks, MOE_BLOCK, D)
    block_expert = jnp.minimum(
        jnp.searchsorted(pad_end, jnp.arange(n_blocks) * MOE_BLOCK, side='right'), N_EXPERTS - 1)

    def expert_block(args):
        xb, e = args
        gu = xb @ w_gate_up[e] + b_gate_up[e]
        g = jnp.minimum(gu[:, 0::2], SWIGLU_LIMIT)
        u = jnp.clip(gu[:, 1::2], -SWIGLU_LIMIT, SWIGLU_LIMIT)
        glu = g * jax.nn.sigmoid(SWIGLU_ALPHA * g)
        return ((u + 1.0) * glu) @ w_down[e] + b_down[e]

    y_slots = lax.map(expert_block, (x_blocks, block_expert)).reshape(n_slots, D)
    slot = jnp.zeros((n_assign,), jnp.int32).at[order].set(dest)
    y = jnp.sum(y_slots[slot].reshape(n_tok, TOP_K, D) * gates[..., None], axis=1)
    return y.reshape(B, L, D)


def setup_inputs(seed: int = 0) -> dict:
    key = jax.random.key(seed)
    ks = jax.random.split(key, 24)
    f32 = jnp.float32

    def nrm(k, shape, s):
        return jax.random.normal(k, shape, f32) * s

    in_keys = jax.random.split(ks[3], len(IN_SIZES))
    w_in = jnp.concatenate(
        [nrm(in_keys[i], (DEPTH, D_MODEL, n), D_MODEL ** -0.5 * (DEEPNORM_BETA if i in VALUE_BLOCKS else 1.0))
         for i, n in enumerate(IN_SIZES)], axis=-1)
    return {
        "x": nrm(ks[0], (BATCH, SEQ, D_MODEL), 1.0),
        "meta_tokens": nrm(ks[1], (N_META, D_MODEL), 1.0),
        "ln_emb_g": 1.0 + nrm(ks[2], (D_MODEL,), 0.01),
        "ln_emb_b": nrm(ks[4], (D_MODEL,), 0.01),
        "w_in": w_in,
        "gla_wa2": nrm(ks[5], (DEPTH, GLA_GATE_RANK, GLA_HEADS * GLA_DK), GLA_GATE_RANK ** -0.5),
        "gla_ba": nrm(ks[6], (DEPTH, GLA_HEADS * GLA_DK), 0.1),
        "gla_norm_g": 1.0 + nrm(ks[7], (DEPTH, GLA_DV), 0.01),
        "diff_lambda_q1": nrm(ks[8], (DEPTH, DIFF_DH), 0.1),
        "diff_lambda_k1": nrm(ks[9], (DEPTH, DIFF_DH), 0.1),
        "diff_lambda_q2": nrm(ks[10], (DEPTH, DIFF_DH), 0.1),
        "diff_lambda_k2": nrm(ks[11], (DEPTH, DIFF_DH), 0.1),
        "diff_norm_g": 1.0 + nrm(ks[12], (DEPTH, DIFF_DV), 0.01),
        "w_out": nrm(ks[13], (DEPTH, MIX_WIDTH, D_MODEL), MIX_WIDTH ** -0.5 * DEEPNORM_BETA),
        "ln1_g": 1.0 + nrm(ks[14], (DEPTH, D_MODEL), 0.01),
        "ln1_b": nrm(ks[15], (DEPTH, D_MODEL), 0.01),
        "router_w": nrm(ks[16], (DEPTH, D_MODEL, N_EXPERTS), D_MODEL ** -0.5),
        "router_b": nrm(ks[17], (DEPTH, N_EXPERTS), 0.01),
        "w_gate_up": nrm(ks[18], (DEPTH, N_EXPERTS, D_MODEL, 2 * D_FF), D_MODEL ** -0.5 * DEEPNORM_BETA),
        "b_gate_up": nrm(ks[19], (DEPTH, N_EXPERTS, 2 * D_FF), 0.01),
        "w_down": nrm(ks[20], (DEPTH, N_EXPERTS, D_FF, D_MODEL), D_FF ** -0.5 * DEEPNORM_BETA),
        "b_down": nrm(ks[21], (DEPTH, N_EXPERTS, D_MODEL), 0.01),
        "ln2_g": 1.0 + nrm(ks[22], (DEPTH, D_MODEL), 0.01),
        "ln2_b": nrm(ks[23], (DEPTH, D_MODEL), 0.01),
    }


def reference(x, meta_tokens, ln_emb_g, ln_emb_b, w_in, gla_wa2, gla_ba, gla_norm_g,
              diff_lambda_q1, diff_lambda_k1, diff_lambda_q2, diff_lambda_k2, diff_norm_g,
              w_out, ln1_g, ln1_b, router_w, router_b, w_gate_up, b_gate_up, w_down, b_down,
              ln2_g, ln2_b):
    B = x.shape[0]
    meta = jnp.broadcast_to(meta_tokens[None].astype(x.dtype), (B, N_META, D_MODEL))
    h = layer_norm(jnp.concatenate([meta, x], axis=1), ln_emb_g, ln_emb_b)
    Lp = h.shape[1] + FRONT_PAD
    valid = jnp.arange(Lp) >= FRONT_PAD
    for l in range(DEPTH):
        lambda_init = 0.8 - 0.6 * math.exp(-0.3 * l)
        proj = jnp.pad(h @ w_in[l], ((0, 0), (FRONT_PAD, 0), (0, 0)))
        gq, gk, gv, gr, ga, dq, dk, dv = jnp.split(proj, IN_SPLIT_POINTS, axis=-1)
        o_gla = gla_mixer(gq, gk, gv, gr, ga, gla_wa2[l], gla_ba[l], gla_norm_g[l], valid)
        o_diff = diff_attention(dq, dk, dv, diff_lambda_q1[l], diff_lambda_k1[l], diff_lambda_q2[l],
                                diff_lambda_k2[l], diff_norm_g[l], valid, lambda_init)
        mix = jnp.concatenate([o_gla, o_diff], axis=-1)[:, FRONT_PAD:] @ w_out[l]
        h = layer_norm(DEEPNORM_ALPHA * h + mix, ln1_g[l], ln1_b[l])
        ffn = moe_ffn(h, router_w[l], router_b[l], w_gate_up[l], b_gate_up[l], w_down[l], b_down[l])
        h = layer_norm(DEEPNORM_ALPHA * h + ffn, ln2_g[l], ln2_b[l])
    return h[:, N_META:]
```

```python
import functools
import math

import jax
import jax.numpy as jnp
from jax import lax
from jax.experimental import pallas as pl
from jax.experimental.pallas import tpu as pltpu

F32 = jnp.float32
BF16 = jnp.bfloat16

D_MODEL = 1024
N_META = 16
ROW_BLOCK = 128
FRONT_PAD = ROW_BLOCK - N_META
CHUNK = 64

GLA_HEADS = 4
GLA_DK = 64
GLA_DV = 128
GLA_RANK = 16
GLA_TAU = 16.0
DIFF_HEADS = 4
DIFF_DH = 64
DIFF_DV = 128

N_EXPERTS = 32
TOP_K = 4
D_FF = 1024
SWIGLU_LIMIT = 7.0
SWIGLU_ALPHA = 1.702
MOE_ROWS = 128

DEEPNORM_ALPHA = 2.0 ** 0.25
LAMBDA_INIT = 0.8 - 0.6 * math.exp(0.0)
LN_EPS = 1e-5
NEG_INF = -1e30

ROUTE_TILE = 512
VMEM_LIMIT = 48 * 1024 * 1024

_GQ, _GK, _GV, _GR, _DQ, _DK, _DV = 0, 256, 512, 1024, 1536, 2048, 2560
_MAIN_WIDTH = 3072


def _layer_norm(x, g, b):
    mu = jnp.mean(x, axis=-1, keepdims=True)
    xc = x - mu
    var = jnp.mean(xc * xc, axis=-1, keepdims=True)
    return xc * lax.rsqrt(var + LN_EPS) * g + b


def _split3_bf16(x):
    hi = x.astype(BF16)
    r1 = x - hi.astype(F32)
    mid = r1.astype(BF16)
    lo = (r1 - mid.astype(F32)).astype(BF16)
    return hi, mid, lo


def _inproj_body(x_ref, meta_ref, g_ref, b_ref, w_ref, wga_ref, wa2_ref, ba_ref,
                 h0_ref, gq_ref, gk_ref, gv_ref, gr_ref, la_ref, dq_ref, dk_ref, dv_ref):
    j = pl.program_id(1)
    rows = lax.broadcasted_iota(jnp.int32, (ROW_BLOCK, 1), 0)
    valid = jnp.logical_or(j > 0, rows >= FRONT_PAD)
    xin = jnp.where(j > 0, x_ref[0], meta_ref[...])
    h = jnp.where(valid, _layer_norm(xin, g_ref[...], b_ref[...]), 0.0)
    h0_ref[0] = h
    hb = h.astype(BF16)

    def proj(off, width):
        return jnp.dot(hb, w_ref[:, off:off + width], preferred_element_type=F32)

    gq_ref[0] = (proj(_GQ, 256) * GLA_DK ** -0.5).astype(BF16)
    gk_ref[0] = proj(_GK, 256)
    gv_ref[0] = proj(_GV, 512).astype(BF16)
    gr_ref[0] = proj(_GR, 512).astype(BF16)
    dq_ref[0] = (proj(_DQ, 512) * DIFF_DH ** -0.5).astype(BF16)
    dk_ref[0] = proj(_DK, 512).astype(BF16)
    dv_ref[0] = proj(_DV, 512).astype(BF16)

    a_lr = jnp.dot(hb, wga_ref[...], preferred_element_type=F32)
    z = jnp.dot(a_lr.astype(BF16), wa2_ref[...], preferred_element_type=F32) + ba_ref[...]
    log_sig = jnp.minimum(z, 0.0) - jnp.log1p(jnp.exp(-jnp.abs(z)))
    la_ref[0] = jnp.where(valid, log_sig / GLA_TAU, 0.0)


def _inproj(x, meta_pad, ln_g, ln_b, w_main, w_ga, wa2, ba):
    B, S, D = x.shape
    nb = S // ROW_BLOCK + 1
    LP = nb * ROW_BLOCK

    def row_spec(width):
        return pl.BlockSpec((1, ROW_BLOCK, width), lambda b, j: (b, j, 0))

    def full(shape):
        return pl.BlockSpec(shape, lambda b, j: (0,) * len(shape))

    out_shapes = [
        jax.ShapeDtypeStruct((B, LP, D), F32),
        jax.ShapeDtypeStruct((B, LP, 256), BF16),
        jax.ShapeDtypeStruct((B, LP, 256), F32),
        jax.ShapeDtypeStruct((B, LP, 512), BF16),
        jax.ShapeDtypeStruct((B, LP, 512), BF16),
        jax.ShapeDtypeStruct((B, LP, 256), F32),
        jax.ShapeDtypeStruct((B, LP, 512), BF16),
        jax.ShapeDtypeStruct((B, LP, 512), BF16),
        jax.ShapeDtypeStruct((B, LP, 512), BF16),
    ]
    return pl.pallas_call(
        _inproj_body,
        grid=(B, nb),
        in_specs=[
            pl.BlockSpec((1, ROW_BLOCK, D), lambda b, j: (b, jnp.maximum(j - 1, 0), 0)),
            full((ROW_BLOCK, D)), full((1, D)), full((1, D)),
            full((D, _MAIN_WIDTH)), full((D, GLA_RANK)), full((GLA_RANK, 256)), full((1, 256)),
        ],
        out_specs=[row_spec(s.shape[-1]) for s in out_shapes],
        out_shape=out_shapes,
        compiler_params=pltpu.CompilerParams(
            dimension_semantics=("parallel", "arbitrary"), vmem_limit_bytes=VMEM_LIMIT),
        name="inproj",
    )(x, meta_pad, ln_g, ln_b, w_main, w_ga, wa2, ba)


def _gla_body(q_ref, k_ref, v_ref, r_ref, la_ref, g_ref, o_ref, st_ref):
    n_chunks = q_ref.shape[1] // CHUNK
    st_ref[...] = jnp.zeros_like(st_ref)
    ri = lax.broadcasted_iota(jnp.int32, (CHUNK, CHUNK), 0)
    ci = lax.broadcasted_iota(jnp.int32, (CHUNK, CHUNK), 1)
    tri = (ri >= ci).astype(BF16)
    gain = g_ref[...]

    def chunk(c, carry):
        r0 = pl.multiple_of(c * CHUNK, CHUNK)
        la = la_ref[0, pl.ds(r0, CHUNK), :]
        hi, mid, lo = _split3_bf16(la)
        cum = (jnp.dot(tri, hi, preferred_element_type=F32)
               + jnp.dot(tri, mid, preferred_element_type=F32)
               + jnp.dot(tri, lo, preferred_element_type=F32))
        tot = cum[CHUNK - 1:CHUNK, :]
        kdec = (k_ref[0, pl.ds(r0, CHUNK), :] * jnp.exp(tot - cum)).astype(BF16)
        v = v_ref[0, pl.ds(r0, CHUNK), :]
        q = q_ref[0, pl.ds(r0, CHUNK), :]
        r = r_ref[0, pl.ds(r0, CHUNK), :].astype(F32)
        upd = [lax.dot_general(v[:, h * GLA_DV:(h + 1) * GLA_DV], kdec[:, h * GLA_DK:(h + 1) * GLA_DK],
                               (((0,), (0,)), ((), ())), preferred_element_type=F32)
               for h in range(GLA_HEADS)]
        st = st_ref[...] * jnp.exp(tot) + jnp.concatenate(upd, axis=1)
        st_ref[...] = st
        stb = st.astype(BF16)
        for h in range(GLA_HEADS):
            o = lax.dot_general(q[:, h * GLA_DK:(h + 1) * GLA_DK], stb[:, h * GLA_DK:(h + 1) * GLA_DK],
                                (((1,), (1,)), ((), ())), preferred_element_type=F32)
            ms = jnp.mean(o * o, axis=-1, keepdims=True)
            rh = r[:, h * GLA_DV:(h + 1) * GLA_DV]
            out = o * lax.rsqrt(ms + LN_EPS) * gain * (rh * jax.nn.sigmoid(rh))
            o_ref[0, pl.ds(r0, CHUNK), h * GLA_DV:(h + 1) * GLA_DV] = out.astype(BF16)
        return carry

    lax.fori_loop(0, n_chunks, chunk, 0)


def _gla(gq, gk, gv, gr, la, norm_g):
    B, LP, _ = gq.shape

    def seq(width):
        return pl.BlockSpec((1, LP, width), lambda b: (b, 0, 0))

    return pl.pallas_call(
        _gla_body,
        grid=(B,),
        in_specs=[seq(256), seq(256), seq(512), seq(512), seq(256),
                  pl.BlockSpec((1, GLA_DV), lambda b: (0, 0))],
        out_specs=seq(512),
        out_shape=jax.ShapeDtypeStruct((B, LP, 512), BF16),
        scratch_shapes=[pltpu.VMEM((GLA_DV, GLA_HEADS * GLA_DK), F32)],
        compiler_params=pltpu.CompilerParams(
            dimension_semantics=("parallel",), vmem_limit_bytes=VMEM_LIMIT),
        name="gla",
    )(gq, gk, gv, gr, la, norm_g)


def _diff_body(q_ref, k_ref, v_ref, lq1_ref, lk1_ref, lq2_ref, lk2_ref, g_ref, o_ref,
               m_ref, l_ref, acc_ref):
    j = pl.program_id(1)
    lam = (jnp.exp(jnp.sum(lq1_ref[...] * lk1_ref[...], axis=-1, keepdims=True))
           - jnp.exp(jnp.sum(lq2_ref[...] * lk2_ref[...], axis=-1, keepdims=True)) + LAMBDA_INIT)
    m_ref[...] = jnp.full_like(m_ref, NEG_INF)
    l_ref[...] = jnp.zeros_like(l_ref)
    acc_ref[...] = jnp.zeros_like(acc_ref)

    qpos = j * ROW_BLOCK + lax.broadcasted_iota(jnp.int32, (ROW_BLOCK, ROW_BLOCK), 0)
    col = lax.broadcasted_iota(jnp.int32, (ROW_BLOCK, ROW_BLOCK), 1)
    lane = lax.broadcasted_iota(jnp.int32, (ROW_BLOCK, 2 * DIFF_DH), 1)

    def kv_step(i, carry):
        k0 = pl.multiple_of(i * ROW_BLOCK, ROW_BLOCK)
        kpos = i * ROW_BLOCK + col
        dist = jnp.abs(qpos - kpos).astype(F32)
        allowed = jnp.logical_and(kpos // CHUNK <= qpos // CHUNK, kpos >= FRONT_PAD)
        for h in range(DIFF_HEADS):
            slope = 2.0 ** (-8.0 * (h + 1) / DIFF_HEADS)
            qh = q_ref[0, :, h * 128:(h + 1) * 128]
            kh = k_ref[0, pl.ds(k0, ROW_BLOCK), h * 128:(h + 1) * 128]
            vh = v_ref[0, pl.ds(k0, ROW_BLOCK), h * 128:(h + 1) * 128]
            bias = slope * dist
            for mp in range(2):
                c = 2 * h + mp
                qz = jnp.where(lane >= DIFF_DH if mp == 1 else lane < DIFF_DH, qh, jnp.zeros_like(qh))
                s = lax.dot_general(qz, kh, (((1,), (1,)), ((), ())), preferred_element_type=F32)
                s = jnp.where(allowed, s - bias, NEG_INF)
                m_old = m_ref[c]
                m_new = jnp.maximum(m_old, jnp.max(s, axis=-1, keepdims=True))
                alpha = jnp.exp(m_old - m_new)
                p = jnp.exp(s - m_new)
                l_ref[c] = alpha * l_ref[c] + jnp.sum(p, axis=-1, keepdims=True)
                acc_ref[c] = alpha * acc_ref[c] + jnp.dot(p.astype(BF16), vh, preferred_element_type=F32)
                m_ref[c] = m_new
        return carry

    lax.fori_loop(0, j + 1, kv_step, 0)

    gain = g_ref[...]
    for h in range(DIFF_HEADS):
        o = acc_ref[2 * h] / l_ref[2 * h] - lam * (acc_ref[2 * h + 1] / l_ref[2 * h + 1])
        ms = jnp.mean(o * o, axis=-1, keepdims=True)
        out = o * lax.rsqrt(ms + LN_EPS) * gain * (1.0 - LAMBDA_INIT)
        o_ref[0, :, h * DIFF_DV:(h + 1) * DIFF_DV] = out.astype(BF16)


def _diff_attention(dq, dk, dv, lq1, lk1, lq2, lk2, norm_g):
    B, LP, W = dq.shape
    nb = LP // ROW_BLOCK
    small = pl.BlockSpec((1, DIFF_DH), lambda b, j: (0, 0))
    seq = pl.BlockSpec((1, LP, W), lambda b, j: (b, 0, 0))
    blk = pl.BlockSpec((1, ROW_BLOCK, W), lambda b, j: (b, j, 0))
    n_chain = 2 * DIFF_HEADS
    return pl.pallas_call(
        _diff_body,
        grid=(B, nb),
        in_specs=[blk, seq, seq, small, small, small, small,
                  pl.BlockSpec((1, DIFF_DV), lambda b, j: (0, 0))],
        out_specs=blk,
        out_shape=jax.ShapeDtypeStruct((B, LP, W), BF16),
        scratch_shapes=[pltpu.VMEM((n_chain, ROW_BLOCK, 1), F32),
                        pltpu.VMEM((n_chain, ROW_BLOCK, 1), F32),
                        pltpu.VMEM((n_chain, ROW_BLOCK, DIFF_DV), F32)],
        compiler_params=pltpu.CompilerParams(
            dimension_semantics=("parallel", "arbitrary"), vmem_limit_bytes=VMEM_LIMIT),
        name="diff_attn",
    )(dq, dk, dv, lq1, lk1, lq2, lk2, norm_g)


def _route_body(og_ref, od_ref, h0_ref, wog_ref, wod_ref, g_ref, b_ref, rw_ref, rb_ref,
                h1_ref, route_ref, cnt_ref, carry_ref, *, rows_per_seq):
    t = pl.program_id(0)

    @pl.when(t == 0)
    def _():
        carry_ref[...] = jnp.zeros_like(carry_ref)

    mix = (jnp.dot(og_ref[...], wog_ref[...], preferred_element_type=F32)
           + jnp.dot(od_ref[...], wod_ref[...], preferred_element_type=F32))
    h1 = _layer_norm(DEEPNORM_ALPHA * h0_ref[...] + mix, g_ref[...], b_ref[...])
    h1_ref[...] = h1

    hh, hm, hl = _split3_bf16(h1)
    wh, wm, wl = _split3_bf16(rw_ref[...])
    logits = rb_ref[...]
    for a, b in ((hl, wh), (hh, wl), (hm, wm), (hm, wh), (hh, wm), (hh, wh)):
        logits = logits + jnp.dot(a, b, preferred_element_type=F32)

    T = logits.shape[0]
    lane = lax.broadcasted_iota(jnp.int32, (T, N_EXPERTS), 1)
    grow = t * T + lax.broadcasted_iota(jnp.int32, (T, 1), 0)
    valid = (grow % rows_per_seq) >= FRONT_PAD

    work = logits
    top_v, top_i = [], []
    for _ in range(TOP_K):
        mx = jnp.max(work, axis=-1, keepdims=True)
        idx = jnp.min(jnp.where(work == mx, lane, N_EXPERTS), axis=-1, keepdims=True)
        top_v.append(mx)
        top_i.append(idx)
        work = jnp.where(lane == idx, -jnp.inf, work)
    ex = [jnp.exp(v - top_v[0]) for v in top_v]
    den = ex[0] + ex[1] + ex[2] + ex[3]
    gates = [e / den for e in ex]

    onehot = jnp.zeros((T, N_EXPERTS), F32)
    for idx in top_i:
        onehot = onehot + (lane == idx).astype(F32)
    onehot = jnp.where(valid, onehot, 0.0)

    ri = lax.broadcasted_iota(jnp.int32, (T, T), 0)
    ci = lax.broadcasted_iota(jnp.int32, (T, T), 1)
    strict = (ri > ci).astype(BF16)
    before = jnp.dot(strict, onehot.astype(BF16), preferred_element_type=F32) + carry_ref[...]
    ranks = [jnp.sum(jnp.where(lane == idx, before, 0.0), axis=-1, keepdims=True).astype(jnp.int32)
             for idx in top_i]
    carry_ref[...] = carry_ref[...] + jnp.sum(onehot, axis=0, keepdims=True)
    cnt_ref[...] = carry_ref[...]

    lane_o = lax.broadcasted_iota(jnp.int32, (T, 128), 1)
    packed = jnp.zeros((T, 128), jnp.int32)
    for k in range(TOP_K):
        packed = jnp.where(lane_o == k, top_i[k], packed)
        packed = jnp.where(lane_o == TOP_K + k, ranks[k], packed)
        packed = jnp.where(lane_o == 2 * TOP_K + k, lax.bitcast_convert_type(gates[k], jnp.int32), packed)
    route_ref[...] = packed


def _outproj_route(og, od, h0, wo_g, wo_d, ln_g, ln_b, rw, rb, rows_per_seq):
    NP, D = h0.shape
    T = ROUTE_TILE

    def rows(width):
        return pl.BlockSpec((T, width), lambda t: (t, 0))

    def full(shape):
        return pl.BlockSpec(shape, lambda t: (0,) * len(shape))

    return pl.pallas_call(
        functools.partial(_route_body, rows_per_seq=rows_per_seq),
        grid=(NP // T,),
        in_specs=[rows(512), rows(512), rows(D), full((512, D)), full((512, D)),
                  full((1, D)), full((1, D)), full((D, N_EXPERTS)), full((1, N_EXPERTS))],
        out_specs=[rows(D), rows(128), full((1, N_EXPERTS))],
        out_shape=[jax.ShapeDtypeStruct((NP, D), F32),
                   jax.ShapeDtypeStruct((NP, 128), jnp.int32),
                   jax.ShapeDtypeStruct((1, N_EXPERTS), F32)],
        scratch_shapes=[pltpu.VMEM((1, N_EXPERTS), F32)],
        compiler_params=pltpu.CompilerParams(
            dimension_semantics=("arbitrary",), vmem_limit_bytes=VMEM_LIMIT),
        name="outproj_route",
    )(og, od, h0, wo_g, wo_d, ln_g, ln_b, rw, rb)


def _expert_body(be_ref, nb_ref, x_ref, wgu_ref, bg_ref, bu_ref, wd_ref, bd_ref, y_ref,
                 wg_s, wu_s, wd_s):
    i = pl.program_id(0)
    prev = be_ref[jnp.maximum(i - 1, 0)]
    active = i < nb_ref[0]
    new_expert = jnp.logical_or(i == 0, be_ref[i] != prev)

    @pl.when(jnp.logical_and(active, new_expert))
    def _():
        r = lax.broadcasted_iota(jnp.int32, (256, 256), 0)
        c = lax.broadcasted_iota(jnp.int32, (256, 256), 1)
        perm = (r == jnp.where(c < 128, 2 * c, 2 * (c - 128) + 1)).astype(BF16)
        for tt in range(2 * D_FF // 256):
            wt = wgu_ref[0, :, tt * 256:(tt + 1) * 256].astype(BF16)
            sp = jnp.dot(wt, perm, preferred_element_type=F32)
            wg_s[:, tt * 128:(tt + 1) * 128] = sp[:, :128].astype(BF16)
            wu_s[:, tt * 128:(tt + 1) * 128] = sp[:, 128:].astype(BF16)
        wd_s[...] = wd_ref[0].astype(BF16)

    @pl.when(active)
    def _():
        x = x_ref[...].astype(BF16)
        g = jnp.dot(x, wg_s[...], preferred_element_type=F32) + bg_ref[0]
        u = jnp.dot(x, wu_s[...], preferred_element_type=F32) + bu_ref[0]
        g = jnp.minimum(g, SWIGLU_LIMIT)
        u = jnp.clip(u, -SWIGLU_LIMIT, SWIGLU_LIMIT)
        act = (u + 1.0) * (g * jax.nn.sigmoid(SWIGLU_ALPHA * g))
        y_ref[...] = jnp.dot(act.astype(BF16), wd_s[...], preferred_element_type=F32) + bd_ref[0]


def _experts(block_expert, n_blocks, x_slots, w_gate_up, b_gate, b_up, w_down, b_down):
    n_slots, D = x_slots.shape
    max_blocks = n_slots // MOE_ROWS

    def blk(i, be, nb):
        return (jnp.minimum(i, nb[0] - 1), 0)

    def per_expert(i, be, nb):
        return (be[jnp.minimum(i, nb[0] - 1)], 0, 0)

    grid_spec = pltpu.PrefetchScalarGridSpec(
        num_scalar_prefetch=2,
        grid=(max_blocks,),
        in_specs=[
            pl.BlockSpec((MOE_ROWS, D), blk),
            pl.BlockSpec((1, D, 2 * D_FF), per_expert),
            pl.BlockSpec((1, 1, D_FF), per_expert),
            pl.BlockSpec((1, 1, D_FF), per_expert),
            pl.BlockSpec((1, D_FF, D), per_expert),
            pl.BlockSpec((1, 1, D), per_expert),
        ],
        out_specs=pl.BlockSpec((MOE_ROWS, D), blk),
        scratch_shapes=[pltpu.VMEM((D, D_FF), BF16), pltpu.VMEM((D, D_FF), BF16),
                        pltpu.VMEM((D_FF, D), BF16)],
    )
    return pl.pallas_call(
        _expert_body,
        grid_spec=grid_spec,
        out_shape=jax.ShapeDtypeStruct((n_slots, D), F32),
        compiler_params=pltpu.CompilerParams(
            dimension_semantics=("arbitrary",), vmem_limit_bytes=VMEM_LIMIT),
        name="experts",
    )(block_expert, n_blocks, x_slots, w_gate_up, b_gate, b_up, w_down, b_down)


def _combine_body(h1_ref, yk_ref, route_ref, g_ref, b_ref, o_ref):
    D = h1_ref.shape[-1]
    route = route_ref[0]
    y = jnp.zeros(h1_ref.shape[1:], F32)
    for k in range(TOP_K):
        gate = lax.bitcast_convert_type(route[:, 2 * TOP_K + k:2 * TOP_K + k + 1], F32)
        y = y + gate * yk_ref[0, :, k * D:(k + 1) * D]
    o_ref[0] = _layer_norm(DEEPNORM_ALPHA * h1_ref[0] + y, g_ref[...], b_ref[...])


def _combine(h1, yk, route, ln_g, ln_b, seq_len):
    B, LP, D = h1.shape
    nb = seq_len // ROW_BLOCK

    def rows(width):
        return pl.BlockSpec((1, ROW_BLOCK, width), lambda b, j: (b, j + 1, 0))

    return pl.pallas_call(
        _combine_body,
        grid=(B, nb),
        in_specs=[rows(D), rows(TOP_K * D), rows(128),
                  pl.BlockSpec((1, D), lambda b, j: (0, 0)), pl.BlockSpec((1, D), lambda b, j: (0, 0))],
        out_specs=pl.BlockSpec((1, ROW_BLOCK, D), lambda b, j: (b, j, 0)),
        out_shape=jax.ShapeDtypeStruct((B, seq_len, D), F32),
        compiler_params=pltpu.CompilerParams(
            dimension_semantics=("parallel", "parallel"), vmem_limit_bytes=VMEM_LIMIT),
        name="combine",
    )(h1, yk, route, ln_g, ln_b)


def kernel(x, meta_tokens, ln_emb_g, ln_emb_b, w_in, gla_wa2, gla_ba, gla_norm_g, diff_lambda_q1, diff_lambda_k1, diff_lambda_q2, diff_lambda_k2, diff_norm_g, w_out, ln1_g, ln1_b, router_w, router_b, w_gate_up, b_gate_up, w_down, b_down, ln2_g, ln2_b):
    B, S, D = x.shape
    LP = S + ROW_BLOCK
    NP = B * LP
    row = lambda v: v.reshape(1, -1)

    w = w_in[0]
    w_main = jnp.concatenate([w[:, :1536], w[:, 1552:]], axis=1).astype(BF16)
    w_ga = w[:, 1536:1552].astype(BF16)
    meta_pad = jnp.pad(meta_tokens, ((FRONT_PAD, 0), (0, 0)))

    h0, gq, gk, gv, gr, la, dq, dk, dv = _inproj(
        x, meta_pad, row(ln_emb_g), row(ln_emb_b), w_main, w_ga,
        gla_wa2[0].astype(BF16), row(gla_ba[0]))

    og = _gla(gq, gk, gv, gr, la, row(gla_norm_g[0]))
    od = _diff_attention(dq, dk, dv, row(diff_lambda_q1[0]), row(diff_lambda_k1[0]),
                         row(diff_lambda_q2[0]), row(diff_lambda_k2[0]), row(diff_norm_g[0]))

    wo = w_out[0].astype(BF16)
    h1, route, counts = _outproj_route(
        og.reshape(NP, 512), od.reshape(NP, 512), h0.reshape(NP, D), wo[:512], wo[512:],
        row(ln1_g[0]), row(ln1_b[0]), router_w[0], row(router_b[0]), LP)

    counts = counts[0].astype(jnp.int32)
    blocks_per_expert = (counts + MOE_ROWS - 1) // MOE_ROWS
    block_end = jnp.cumsum(blocks_per_expert)
    slot_start = (block_end - blocks_per_expert) * MOE_ROWS
    n_assign_max = B * (S + N_META) * TOP_K
    max_blocks = (n_assign_max + N_EXPERTS * (MOE_ROWS - 1)) // MOE_ROWS
    n_slots = max_blocks * MOE_ROWS
    block_expert = jnp.minimum(
        jnp.searchsorted(block_end, jnp.arange(max_blocks), side='right'), N_EXPERTS - 1).astype(jnp.int32)
    n_blocks = block_end[-1:].astype(jnp.int32)

    top_idx = route[:, :TOP_K]
    rank = route[:, TOP_K:2 * TOP_K]
    tok = jnp.arange(NP, dtype=jnp.int32)
    tok_valid = (tok % LP) >= FRONT_PAD
    dest = jnp.where(tok_valid[:, None], slot_start[top_idx] + rank, n_slots)
    src_tok = jnp.zeros((n_slots,), jnp.int32).at[dest.reshape(-1)].set(
        jnp.repeat(tok, TOP_K), mode='drop')

    x_slots = h1[src_tok]
    bgu = b_gate_up[0].reshape(N_EXPERTS, D_FF, 2)
    y_slots = _experts(block_expert, n_blocks, x_slots, w_gate_up[0],
                       bgu[:, :, 0].reshape(N_EXPERTS, 1, D_FF), bgu[:, :, 1].reshape(N_EXPERTS, 1, D_FF),
                       w_down[0], b_down[0].reshape(N_EXPERTS, 1, D))
    yk = y_slots[jnp.minimum(dest, n_slots - 1)].reshape(B, LP, TOP_K * D)

    return _combine(h1.reshape(B, LP, D), yk, route.reshape(B, LP, 128),
                    row(ln2_g[0]), row(ln2_b[0]), S)
```

```python
import functools
import math

import jax
import jax.numpy as jnp
from jax import lax
from jax.experimental import pallas as pl
from jax.experimental.pallas import tpu as pltpu

F32 = jnp.float32
BF16 = jnp.bfloat16

D_MODEL = 1024
N_META = 16
ROW_BLOCK = 128
FRONT_PAD = ROW_BLOCK - N_META
CHUNK = 64

GLA_HEADS = 4
GLA_DK = 64
GLA_DV = 128
GLA_RANK = 16
GLA_TAU = 16.0
DIFF_HEADS = 4
DIFF_DH = 64
DIFF_DV = 128

N_EXPERTS = 32
TOP_K = 4
D_FF = 1024
SWIGLU_LIMIT = 7.0
SWIGLU_ALPHA = 1.702
MOE_ROWS = 128

DEEPNORM_ALPHA = 2.0 ** 0.25
LAMBDA_INIT = 0.8 - 0.6 * math.exp(0.0)
LN_EPS = 1e-5
NEG_INF = -1e30
LOG2E = math.log2(math.e)

ROUTE_TILE = 512
VMEM_LIMIT = 48 * 1024 * 1024

_GQ, _GK, _GV, _GR, _DQ, _DK, _DV = 0, 256, 512, 1024, 1536, 2048, 2560
_MAIN_WIDTH = 3072


def _layer_norm(x, g, b):
    mu = jnp.mean(x, axis=-1, keepdims=True)
    xc = x - mu
    var = jnp.mean(xc * xc, axis=-1, keepdims=True)
    return xc * lax.rsqrt(var + LN_EPS) * g + b


def _split3_bf16(x):
    hi = x.astype(BF16)
    r1 = x - hi.astype(F32)
    mid = r1.astype(BF16)
    lo = (r1 - mid.astype(F32)).astype(BF16)
    return hi, mid, lo


def _inproj_body(x_ref, meta_ref, g_ref, b_ref, w_ref, wga_ref, wa2_ref, ba_ref,
                 h0_ref, gq_ref, gk_ref, gv_ref, gr_ref, la_ref, dq_ref, dk_ref, dv_ref):
    j = pl.program_id(1)
    rows = lax.broadcasted_iota(jnp.int32, (ROW_BLOCK, 1), 0)
    valid = jnp.logical_or(j > 0, rows >= FRONT_PAD)
    xin = jnp.where(j > 0, x_ref[0], meta_ref[...])
    h = jnp.where(valid, _layer_norm(xin, g_ref[...], b_ref[...]), 0.0)
    h0_ref[0] = h
    hb = h.astype(BF16)

    def proj(off, width):
        return jnp.dot(hb, w_ref[:, off:off + width], preferred_element_type=F32)

    gq_ref[0] = (proj(_GQ, 256) * GLA_DK ** -0.5).astype(BF16)
    gk_ref[0] = proj(_GK, 256)
    gv_ref[0] = proj(_GV, 512).astype(BF16)
    gr_ref[0] = proj(_GR, 512).astype(BF16)
    dq_ref[0] = (proj(_DQ, 512) * (DIFF_DH ** -0.5 * LOG2E)).astype(BF16)
    dk_ref[0] = proj(_DK, 512).astype(BF16)
    dv_ref[0] = proj(_DV, 512).astype(BF16)

    a_lr = jnp.dot(hb, wga_ref[...], preferred_element_type=F32)
    z = jnp.dot(a_lr.astype(BF16), wa2_ref[...], preferred_element_type=F32) + ba_ref[...]
    log_sig = jnp.minimum(z, 0.0) - jnp.log1p(jnp.exp(-jnp.abs(z)))
    la_ref[0] = jnp.where(valid, log_sig / GLA_TAU, 0.0)


def _inproj(x, meta_pad, ln_g, ln_b, w_main, w_ga, wa2, ba):
    B, S, D = x.shape
    nb = S // ROW_BLOCK + 1
    LP = nb * ROW_BLOCK

    def row_spec(width):
        return pl.BlockSpec((1, ROW_BLOCK, width), lambda b, j: (b, j, 0))

    def full(shape):
        return pl.BlockSpec(shape, lambda b, j: (0,) * len(shape))

    out_shapes = [
        jax.ShapeDtypeStruct((B, LP, D), F32),
        jax.ShapeDtypeStruct((B, LP, 256), BF16),
        jax.ShapeDtypeStruct((B, LP, 256), F32),
        jax.ShapeDtypeStruct((B, LP, 512), BF16),
        jax.ShapeDtypeStruct((B, LP, 512), BF16),
        jax.ShapeDtypeStruct((B, LP, 256), F32),
        jax.ShapeDtypeStruct((B, LP, 512), BF16),
        jax.ShapeDtypeStruct((B, LP, 512), BF16),
        jax.ShapeDtypeStruct((B, LP, 512), BF16),
    ]
    return pl.pallas_call(
        _inproj_body,
        grid=(B, nb),
        in_specs=[
            pl.BlockSpec((1, ROW_BLOCK, D), lambda b, j: (b, jnp.maximum(j - 1, 0), 0)),
            full((ROW_BLOCK, D)), full((1, D)), full((1, D)),
            full((D, _MAIN_WIDTH)), full((D, GLA_RANK)), full((GLA_RANK, 256)), full((1, 256)),
        ],
        out_specs=[row_spec(s.shape[-1]) for s in out_shapes],
        out_shape=out_shapes,
        compiler_params=pltpu.CompilerParams(
            dimension_semantics=("parallel", "arbitrary"), vmem_limit_bytes=VMEM_LIMIT),
        name="inproj",
    )(x, meta_pad, ln_g, ln_b, w_main, w_ga, wa2, ba)


def _gla_body(q_ref, k_ref, v_ref, r_ref, la_ref, g_ref, o_ref, st_ref):
    n_chunks = q_ref.shape[1] // CHUNK
    st_ref[...] = jnp.zeros_like(st_ref)
    ri = lax.broadcasted_iota(jnp.int32, (CHUNK, CHUNK), 0)
    ci = lax.broadcasted_iota(jnp.int32, (CHUNK, CHUNK), 1)
    tri = (ri >= ci).astype(BF16)
    gain = g_ref[...]

    def chunk(c, carry):
        r0 = pl.multiple_of(c * CHUNK, CHUNK)
        la = la_ref[0, pl.ds(r0, CHUNK), :]
        hi, mid, lo = _split3_bf16(la)
        cum = (jnp.dot(tri, hi, preferred_element_type=F32)
               + jnp.dot(tri, mid, preferred_element_type=F32)
               + jnp.dot(tri, lo, preferred_element_type=F32))
        tot = cum[CHUNK - 1:CHUNK, :]
        kdec = (k_ref[0, pl.ds(r0, CHUNK), :] * jnp.exp(tot - cum)).astype(BF16)
        v = v_ref[0, pl.ds(r0, CHUNK), :]
        q = q_ref[0, pl.ds(r0, CHUNK), :]
        r = r_ref[0, pl.ds(r0, CHUNK), :].astype(F32)
        upd = [lax.dot_general(v[:, h * GLA_DV:(h + 1) * GLA_DV], kdec[:, h * GLA_DK:(h + 1) * GLA_DK],
                               (((0,), (0,)), ((), ())), preferred_element_type=F32)
               for h in range(GLA_HEADS)]
        st = st_ref[...] * jnp.exp(tot) + jnp.concatenate(upd, axis=1)
        st_ref[...] = st
        stb = st.astype(BF16)
        for h in range(GLA_HEADS):
            o = lax.dot_general(q[:, h * GLA_DK:(h + 1) * GLA_DK], stb[:, h * GLA_DK:(h + 1) * GLA_DK],
                                (((1,), (1,)), ((), ())), preferred_element_type=F32)
            ms = jnp.mean(o * o, axis=-1, keepdims=True)
            rh = r[:, h * GLA_DV:(h + 1) * GLA_DV]
            out = o * lax.rsqrt(ms + LN_EPS) * gain * (rh * jax.nn.sigmoid(rh))
            o_ref[0, pl.ds(r0, CHUNK), h * GLA_DV:(h + 1) * GLA_DV] = out.astype(BF16)
        return carry

    lax.fori_loop(0, n_chunks, chunk, 0)


def _gla(gq, gk, gv, gr, la, norm_g):
    B, LP, _ = gq.shape

    def seq(width):
        return pl.BlockSpec((1, LP, width), lambda b: (b, 0, 0))

    return pl.pallas_call(
        _gla_body,
        grid=(B,),
        in_specs=[seq(256), seq(256), seq(512), seq(512), seq(256),
                  pl.BlockSpec((1, GLA_DV), lambda b: (0, 0))],
        out_specs=seq(512),
        out_shape=jax.ShapeDtypeStruct((B, LP, 512), BF16),
        scratch_shapes=[pltpu.VMEM((GLA_DV, GLA_HEADS * GLA_DK), F32)],
        compiler_params=pltpu.CompilerParams(
            dimension_semantics=("parallel",), vmem_limit_bytes=VMEM_LIMIT),
        name="gla",
    )(gq, gk, gv, gr, la, norm_g)


ATT_BLOCK = 256


def _diff_body(q_ref, k_ref, v_ref, lq1_ref, lk1_ref, lq2_ref, lk2_ref, g_ref, o_ref,
               dmask_ref):
    h = pl.program_id(1)
    slope = jnp.where(h == 0, 2.0 ** -2, jnp.where(h == 1, 2.0 ** -4, jnp.where(h == 2, 2.0 ** -6, 2.0 ** -8)))
    slope = slope.astype(F32) * LOG2E
    lam = (jnp.exp(jnp.sum(lq1_ref[...] * lk1_ref[...], axis=-1, keepdims=True))
           - jnp.exp(jnp.sum(lq2_ref[...] * lk2_ref[...], axis=-1, keepdims=True)) + LAMBDA_INIT)
    gain = g_ref[...] * (1.0 - LAMBDA_INIT)
    n_qblocks = (q_ref.shape[1] - ROW_BLOCK) // ATT_BLOCK
    nt = (((1,), (1,)), ((), ()))

    def split_q(q):
        lane = lax.broadcasted_iota(jnp.int32, q.shape, 1)
        zero = jnp.zeros_like(q)
        return jnp.where(lane < DIFF_DH, q, zero), jnp.where(lane >= DIFF_DH, q, zero)

    def with_ones(v):
        return jnp.concatenate([v, jnp.ones_like(v)], axis=1)

    def softmax_av(qz, kk, vext, add_bias):
        s = add_bias(lax.dot_general(qz, kk, nt, preferred_element_type=F32))
        m = jnp.max(s, axis=-1, keepdims=True)
        p = jnp.exp2(s - m).astype(BF16)
        return jnp.dot(p, vext, preferred_element_type=F32)

    def finish(a1, a2):
        o = a1[:, :DIFF_DV] / a1[:, DIFF_DV:] - lam * (a2[:, :DIFF_DV] / a2[:, DIFF_DV:])
        ms = jnp.mean(o * o, axis=-1, keepdims=True)
        return (o * lax.rsqrt(ms + LN_EPS) * gain).astype(BF16)

    r = lax.broadcasted_iota(jnp.int32, (ROW_BLOCK, ROW_BLOCK), 0)
    c = lax.broadcasted_iota(jnp.int32, (ROW_BLOCK, ROW_BLOCK), 1)
    ok = jnp.logical_and(c // CHUNK <= r // CHUNK, c >= FRONT_PAD)
    bias_lead = jnp.where(ok, -slope * jnp.abs(r - c).astype(F32), NEG_INF)
    q1z, q2z = split_q(q_ref[0, 0:ROW_BLOCK, :])
    k_lead = k_ref[0, 0:ROW_BLOCK, :]
    v_lead = with_ones(v_ref[0, 0:ROW_BLOCK, :])
    o_ref[0, 0:ROW_BLOCK, :] = finish(softmax_av(q1z, k_lead, v_lead, lambda s: s + bias_lead),
                                      softmax_av(q2z, k_lead, v_lead, lambda s: s + bias_lead))

    r = lax.broadcasted_iota(jnp.int32, (ATT_BLOCK, ATT_BLOCK), 0)
    c = lax.broadcasted_iota(jnp.int32, (ATT_BLOCK, ATT_BLOCK), 1)
    rel = jnp.where(c <= r, c, 2 * r - c).astype(F32)
    dmask_ref[...] = jnp.where(c // CHUNK <= r // CHUNK, slope * rel, NEG_INF)

    for jq in range(n_qblocks):
        qbase = ROW_BLOCK + jq * ATT_BLOCK
        n_keys = qbase + ATT_BLOCK
        q1z, q2z = split_q(q_ref[0, qbase:qbase + ATT_BLOCK, :])
        kk = k_ref[0, 0:n_keys, :]
        vext = with_ones(v_ref[0, 0:n_keys, :])
        col = lax.broadcasted_iota(jnp.int32, (1, qbase), 1)
        col_bias = jnp.where(col >= FRONT_PAD, slope * (col - qbase).astype(F32), NEG_INF)

        def add_bias(s, col_bias=col_bias, qbase=qbase):
            return jnp.concatenate([s[:, :qbase] + col_bias, s[:, qbase:] + dmask_ref[...]], axis=1)

        o_ref[0, qbase:qbase + ATT_BLOCK, :] = finish(softmax_av(q1z, kk, vext, add_bias),
                                                      softmax_av(q2z, kk, vext, add_bias))


def _diff_attention(dq, dk, dv, lq1, lk1, lq2, lk2, norm_g):
    B, LP, W = dq.shape
    small = pl.BlockSpec((1, DIFF_DH), lambda b, h: (0, 0))
    seq = pl.BlockSpec((1, LP, 2 * DIFF_DH), lambda b, h: (b, 0, h))
    return pl.pallas_call(
        _diff_body,
        grid=(B, DIFF_HEADS),
        in_specs=[seq, seq, seq, small, small, small, small,
                  pl.BlockSpec((1, DIFF_DV), lambda b, h: (0, 0))],
        out_specs=seq,
        out_shape=jax.ShapeDtypeStruct((B, LP, W), BF16),
        scratch_shapes=[pltpu.VMEM((ATT_BLOCK, ATT_BLOCK), F32)],
        compiler_params=pltpu.CompilerParams(
            dimension_semantics=("parallel", "parallel"), vmem_limit_bytes=VMEM_LIMIT),
        name="diff_attn",
    )(dq, dk, dv, lq1, lk1, lq2, lk2, norm_g)


def _route_body(og_ref, od_ref, h0_ref, wog_ref, wod_ref, g_ref, b_ref, rw_ref, rb_ref,
                h1_ref, route_ref, cnt_ref, carry_ref, *, rows_per_seq):
    t = pl.program_id(0)

    @pl.when(t == 0)
    def _():
        carry_ref[...] = jnp.zeros_like(carry_ref)

    mix = (jnp.dot(og_ref[...], wog_ref[...], preferred_element_type=F32)
           + jnp.dot(od_ref[...], wod_ref[...], preferred_element_type=F32))
    h1 = _layer_norm(DEEPNORM_ALPHA * h0_ref[...] + mix, g_ref[...], b_ref[...])
    h1_ref[...] = h1

    hh, hm, hl = _split3_bf16(h1)
    wh, wm, wl = _split3_bf16(rw_ref[...])
    logits = rb_ref[...]
    for a, b in ((hl, wh), (hh, wl), (hm, wm), (hm, wh), (hh, wm), (hh, wh)):
        logits = logits + jnp.dot(a, b, preferred_element_type=F32)

    T = logits.shape[0]
    lane = lax.broadcasted_iota(jnp.int32, (T, N_EXPERTS), 1)
    grow = t * T + lax.broadcasted_iota(jnp.int32, (T, 1), 0)
    valid = (grow % rows_per_seq) >= FRONT_PAD

    work = logits
    top_v, top_i = [], []
    for _ in range(TOP_K):
        mx = jnp.max(work, axis=-1, keepdims=True)
        idx = jnp.min(jnp.where(work == mx, lane, N_EXPERTS), axis=-1, keepdims=True)
        top_v.append(mx)
        top_i.append(idx)
        work = jnp.where(lane == idx, -jnp.inf, work)
    ex = [jnp.exp(v - top_v[0]) for v in top_v]
    den = ex[0] + ex[1] + ex[2] + ex[3]
    gates = [e / den for e in ex]

    onehot = jnp.zeros((T, N_EXPERTS), F32)
    for idx in top_i:
        onehot = onehot + (lane == idx).astype(F32)
    onehot = jnp.where(valid, onehot, 0.0)

    ri = lax.broadcasted_iota(jnp.int32, (T, T), 0)
    ci = lax.broadcasted_iota(jnp.int32, (T, T), 1)
    strict = (ri > ci).astype(BF16)
    before = jnp.dot(strict, onehot.astype(BF16), preferred_element_type=F32) + carry_ref[...]
    ranks = [jnp.sum(jnp.where(lane == idx, before, 0.0), axis=-1, keepdims=True).astype(jnp.int32)
             for idx in top_i]
    carry_ref[...] = carry_ref[...] + jnp.sum(onehot, axis=0, keepdims=True)
    cnt_ref[...] = carry_ref[...]

    lane_o = lax.broadcasted_iota(jnp.int32, (T, 128), 1)
    packed = jnp.zeros((T, 128), jnp.int32)
    for k in range(TOP_K):
        packed = jnp.where(lane_o == k, top_i[k], packed)
        packed = jnp.where(lane_o == TOP_K + k, ranks[k], packed)
        packed = jnp.where(lane_o == 2 * TOP_K + k, lax.bitcast_convert_type(gates[k], jnp.int32), packed)
    route_ref[...] = packed


def _outproj_route(og, od, h0, wo_g, wo_d, ln_g, ln_b, rw, rb, rows_per_seq):
    NP, D = h0.shape
    T = ROUTE_TILE

    def rows(width):
        return pl.BlockSpec((T, width), lambda t: (t, 0))

    def full(shape):
        return pl.BlockSpec(shape, lambda t: (0,) * len(shape))

    return pl.pallas_call(
        functools.partial(_route_body, rows_per_seq=rows_per_seq),
        grid=(NP // T,),
        in_specs=[rows(512), rows(512), rows(D), full((512, D)), full((512, D)),
                  full((1, D)), full((1, D)), full((D, N_EXPERTS)), full((1, N_EXPERTS))],
        out_specs=[rows(D), rows(128), full((1, N_EXPERTS))],
        out_shape=[jax.ShapeDtypeStruct((NP, D), F32),
                   jax.ShapeDtypeStruct((NP, 128), jnp.int32),
                   jax.ShapeDtypeStruct((1, N_EXPERTS), F32)],
        scratch_shapes=[pltpu.VMEM((1, N_EXPERTS), F32)],
        compiler_params=pltpu.CompilerParams(
            dimension_semantics=("arbitrary",), vmem_limit_bytes=VMEM_LIMIT),
        name="outproj_route",
    )(og, od, h0, wo_g, wo_d, ln_g, ln_b, rw, rb)


def _expert_body(be_ref, nb_ref, x_ref, wgu_ref, bg_ref, bu_ref, wd_ref, bd_ref, y_ref,
                 wg_s, wu_s, wd_s):
    i = pl.program_id(0)
    prev = be_ref[jnp.maximum(i - 1, 0)]
    active = i < nb_ref[0]
    new_expert = jnp.logical_or(i == 0, be_ref[i] != prev)

    @pl.when(jnp.logical_and(active, new_expert))
    def _():
        r = lax.broadcasted_iota(jnp.int32, (256, 256), 0)
        c = lax.broadcasted_iota(jnp.int32, (256, 256), 1)
        perm = (r == jnp.where(c < 128, 2 * c, 2 * (c - 128) + 1)).astype(BF16)
        for tt in range(2 * D_FF // 256):
            wt = wgu_ref[0, :, tt * 256:(tt + 1) * 256].astype(BF16)
            sp = jnp.dot(wt, perm, preferred_element_type=F32)
            wg_s[:, tt * 128:(tt + 1) * 128] = sp[:, :128].astype(BF16)
            wu_s[:, tt * 128:(tt + 1) * 128] = sp[:, 128:].astype(BF16)
        wd_s[...] = wd_ref[0].astype(BF16)

    @pl.when(active)
    def _():
        x = x_ref[...].astype(BF16)
        g = jnp.dot(x, wg_s[...], preferred_element_type=F32) + bg_ref[0]
        u = jnp.dot(x, wu_s[...], preferred_element_type=F32) + bu_ref[0]
        g = jnp.minimum(g, SWIGLU_LIMIT)
        u = jnp.clip(u, -SWIGLU_LIMIT, SWIGLU_LIMIT)
        act = (u + 1.0) * (g * jax.nn.sigmoid(SWIGLU_ALPHA * g))
        y_ref[...] = jnp.dot(act.astype(BF16), wd_s[...], preferred_element_type=F32) + bd_ref[0]


def _experts(block_expert, n_blocks, x_slots, w_gate_up, b_gate, b_up, w_down, b_down):
    n_slots, D = x_slots.shape
    max_blocks = n_slots // MOE_ROWS

    def blk(i, be, nb):
        return (jnp.minimum(i, nb[0] - 1), 0)

    def per_expert(i, be, nb):
        return (be[jnp.minimum(i, nb[0] - 1)], 0, 0)

    grid_spec = pltpu.PrefetchScalarGridSpec(
        num_scalar_prefetch=2,
        grid=(max_blocks,),
        in_specs=[
            pl.BlockSpec((MOE_ROWS, D), blk),
            pl.BlockSpec((1, D, 2 * D_FF), per_expert),
            pl.BlockSpec((1, 1, D_FF), per_expert),
            pl.BlockSpec((1, 1, D_FF), per_expert),
            pl.BlockSpec((1, D_FF, D), per_expert),
            pl.BlockSpec((1, 1, D), per_expert),
        ],
        out_specs=pl.BlockSpec((MOE_ROWS, D), blk),
        scratch_shapes=[pltpu.VMEM((D, D_FF), BF16), pltpu.VMEM((D, D_FF), BF16),
                        pltpu.VMEM((D_FF, D), BF16)],
    )
    return pl.pallas_call(
        _expert_body,
        grid_spec=grid_spec,
        out_shape=jax.ShapeDtypeStruct((n_slots, D), F32),
        compiler_params=pltpu.CompilerParams(
            dimension_semantics=("arbitrary",), vmem_limit_bytes=VMEM_LIMIT),
        name="experts",
    )(block_expert, n_blocks, x_slots, w_gate_up, b_gate, b_up, w_down, b_down)


def _combine_body(h1_ref, yk_ref, route_ref, g_ref, b_ref, o_ref):
    D = h1_ref.shape[-1]
    route = route_ref[0]
    y = jnp.zeros(h1_ref.shape[1:], F32)
    for k in range(TOP_K):
        gate = lax.bitcast_convert_type(route[:, 2 * TOP_K + k:2 * TOP_K + k + 1], F32)
        y = y + gate * yk_ref[0, :, k * D:(k + 1) * D]
    o_ref[0] = _layer_norm(DEEPNORM_ALPHA * h1_ref[0] + y, g_ref[...], b_ref[...])


def _combine(h1, yk, route, ln_g, ln_b, seq_len):
    B, LP, D = h1.shape
    nb = seq_len // ROW_BLOCK

    def rows(width):
        return pl.BlockSpec((1, ROW_BLOCK, width), lambda b, j: (b, j + 1, 0))

    return pl.pallas_call(
        _combine_body,
        grid=(B, nb),
        in_specs=[rows(D), rows(TOP_K * D), rows(128),
                  pl.BlockSpec((1, D), lambda b, j: (0, 0)), pl.BlockSpec((1, D), lambda b, j: (0, 0))],
        out_specs=pl.BlockSpec((1, ROW_BLOCK, D), lambda b, j: (b, j, 0)),
        out_shape=jax.ShapeDtypeStruct((B, seq_len, D), F32),
        compiler_params=pltpu.CompilerParams(
            dimension_semantics=("parallel", "parallel"), vmem_limit_bytes=VMEM_LIMIT),
        name="combine",
    )(h1, yk, route, ln_g, ln_b)


def kernel(x, meta_tokens, ln_emb_g, ln_emb_b, w_in, gla_wa2, gla_ba, gla_norm_g, diff_lambda_q1, diff_lambda_k1, diff_lambda_q2, diff_lambda_k2, diff_norm_g, w_out, ln1_g, ln1_b, router_w, router_b, w_gate_up, b_gate_up, w_down, b_down, ln2_g, ln2_b):
    B, S, D = x.shape
    LP = S + ROW_BLOCK
    NP = B * LP
    row = lambda v: v.reshape(1, -1)

    w = w_in[0]
    w_main = jnp.concatenate([w[:, :1536], w[:, 1552:]], axis=1).astype(BF16)
    w_ga = w[:, 1536:1552].astype(BF16)
    meta_pad = jnp.pad(meta_tokens, ((FRONT_PAD, 0), (0, 0)))

    h0, gq, gk, gv, gr, la, dq, dk, dv = _inproj(
        x, meta_pad, row(ln_emb_g), row(ln_emb_b), w_main, w_ga,
        gla_wa2[0].astype(BF16), row(gla_ba[0]))

    og = _gla(gq, gk, gv, gr, la, row(gla_norm_g[0]))
    od = _diff_attention(dq, dk, dv, row(diff_lambda_q1[0]), row(diff_lambda_k1[0]),
                         row(diff_lambda_q2[0]), row(diff_lambda_k2[0]), row(diff_norm_g[0]))

    wo = w_out[0].astype(BF16)
    h1, route, counts = _outproj_route(
        og.reshape(NP, 512), od.reshape(NP, 512), h0.reshape(NP, D), wo[:512], wo[512:],
        row(ln1_g[0]), row(ln1_b[0]), router_w[0], row(router_b[0]), LP)

    counts = counts[0].astype(jnp.int32)
    blocks_per_expert = (counts + MOE_ROWS - 1) // MOE_ROWS
    block_end = jnp.cumsum(blocks_per_expert)
    slot_start = (block_end - blocks_per_expert) * MOE_ROWS
    n_assign_max = B * (S + N_META) * TOP_K
    max_blocks = (n_assign_max + N_EXPERTS * (MOE_ROWS - 1)) // MOE_ROWS
    n_slots = max_blocks * MOE_ROWS
    block_expert = jnp.minimum(
        jnp.sum(jnp.arange(max_blocks)[:, None] >= block_end[None, :], axis=1), N_EXPERTS - 1).astype(jnp.int32)
    n_blocks = block_end[-1:].astype(jnp.int32)

    top_idx = route[:, :TOP_K]
    rank = route[:, TOP_K:2 * TOP_K]
    tok = jnp.arange(NP, dtype=jnp.int32)
    tok_valid = (tok % LP) >= FRONT_PAD
    start_of = jnp.sum(jnp.where(top_idx[:, :, None] == jnp.arange(N_EXPERTS)[None, None, :],
                                 slot_start[None, None, :], 0), axis=-1)
    dest = jnp.where(tok_valid[:, None], start_of + rank, n_slots)
    src_tok = jnp.zeros((n_slots,), jnp.int32).at[dest.reshape(-1)].set(
        jnp.repeat(tok, TOP_K), mode='drop')

    x_slots = h1[src_tok]
    bgu = b_gate_up[0].reshape(N_EXPERTS, D_FF, 2)
    y_slots = _experts(block_expert, n_blocks, x_slots, w_gate_up[0],
                       bgu[:, :, 0].reshape(N_EXPERTS, 1, D_FF), bgu[:, :, 1].reshape(N_EXPERTS, 1, D_FF),
                       w_down[0], b_down[0].reshape(N_EXPERTS, 1, D))
    yk = y_slots[jnp.minimum(dest, n_slots - 1)].reshape(B, LP, TOP_K * D)

    return _combine(h1.reshape(B, LP, D), yk, route.reshape(B, LP, 128),
                    row(ln2_g[0]), row(ln2_b[0]), S)
```

```python
import functools
import math

import jax
import jax.numpy as jnp
from jax import lax
from jax.experimental import pallas as pl
from jax.experimental.pallas import tpu as pltpu

F32 = jnp.float32
BF16 = jnp.bfloat16

D_MODEL = 1024
N_META = 16
ROW_BLOCK = 128
FRONT_PAD = ROW_BLOCK - N_META
CHUNK = 64

GLA_HEADS = 4
GLA_DK = 64
GLA_DV = 128
GLA_RANK = 16
GLA_TAU = 16.0
DIFF_HEADS = 4
DIFF_DH = 64
DIFF_DV = 128

N_EXPERTS = 32
TOP_K = 4
D_FF = 1024
SWIGLU_LIMIT = 7.0
SWIGLU_ALPHA = 1.702
MOE_ROWS = 256

DEEPNORM_ALPHA = 2.0 ** 0.25
LAMBDA_INIT = 0.8 - 0.6 * math.exp(0.0)
LN_EPS = 1e-5
NEG_INF = -1e30
LOG2E = math.log2(math.e)

ROUTE_TILE = 512
VMEM_LIMIT = 48 * 1024 * 1024
EXPERT_VMEM_LIMIT = 56 * 1024 * 1024

_GQ, _GK, _GV, _GR, _DQ, _DK, _DV = 0, 256, 512, 1024, 1536, 2048, 2560
_MAIN_WIDTH = 3072


def _layer_norm(x, g, b):
    mu = jnp.mean(x, axis=-1, keepdims=True)
    xc = x - mu
    var = jnp.mean(xc * xc, axis=-1, keepdims=True)
    return xc * lax.rsqrt(var + LN_EPS) * g + b


def _split3_bf16(x):
    hi = x.astype(BF16)
    r1 = x - hi.astype(F32)
    mid = r1.astype(BF16)
    lo = (r1 - mid.astype(F32)).astype(BF16)
    return hi, mid, lo


TOKEN_SUBROWS = D_MODEL // 128


def _store_token_tiled(ref, val, first_token=0):
    n = val.shape[0]
    for s in range(TOKEN_SUBROWS):
        ref[pl.ds(first_token * TOKEN_SUBROWS + s, n, stride=TOKEN_SUBROWS), :] = val[:, s * 128:(s + 1) * 128]


def _load_token_tiled(ref, n, first_row=0):
    return jnp.concatenate(
        [ref[pl.ds(first_row + s, n, stride=TOKEN_SUBROWS), :] for s in range(TOKEN_SUBROWS)], axis=1)


def _inproj_body(x_ref, meta_ref, g_ref, b_ref, w_ref, wga_ref, wa2_ref, ba_ref,
                 h0_ref, gq_ref, gk_ref, gv_ref, gr_ref, la_ref, dq_ref, dk_ref, dv_ref):
    j = pl.program_id(1)
    rows = lax.broadcasted_iota(jnp.int32, (ROW_BLOCK, 1), 0)
    valid = jnp.logical_or(j > 0, rows >= FRONT_PAD)
    xin = jnp.where(j > 0, x_ref[0], meta_ref[...])
    h = jnp.where(valid, _layer_norm(xin, g_ref[...], b_ref[...]), 0.0)
    h0_ref[0] = h
    hb = h.astype(BF16)

    def proj(off, width):
        return jnp.dot(hb, w_ref[:, off:off + width], preferred_element_type=F32)

    gq_ref[0] = (proj(_GQ, 256) * GLA_DK ** -0.5).astype(BF16)
    gk_ref[0] = proj(_GK, 256)
    gv_ref[0] = proj(_GV, 512).astype(BF16)
    gr_ref[0] = proj(_GR, 512).astype(BF16)
    dq_ref[0] = (proj(_DQ, 512) * (DIFF_DH ** -0.5 * LOG2E)).astype(BF16)
    dk_ref[0] = proj(_DK, 512).astype(BF16)
    dv_ref[0] = proj(_DV, 512).astype(BF16)

    a_lr = jnp.dot(hb, wga_ref[...], preferred_element_type=F32)
    z = jnp.dot(a_lr.astype(BF16), wa2_ref[...], preferred_element_type=F32) + ba_ref[...]
    log_sig = jnp.minimum(z, 0.0) - jnp.log1p(jnp.exp(-jnp.abs(z)))
    la_ref[0] = jnp.where(valid, log_sig / GLA_TAU, 0.0)


def _inproj(x, meta_pad, ln_g, ln_b, w_main, w_ga, wa2, ba):
    B, S, D = x.shape
    nb = S // ROW_BLOCK + 1
    LP = nb * ROW_BLOCK

    def row_spec(width):
        return pl.BlockSpec((1, ROW_BLOCK, width), lambda b, j: (b, j, 0))

    def full(shape):
        return pl.BlockSpec(shape, lambda b, j: (0,) * len(shape))

    out_shapes = [
        jax.ShapeDtypeStruct((B, LP, D), F32),
        jax.ShapeDtypeStruct((B, LP, 256), BF16),
        jax.ShapeDtypeStruct((B, LP, 256), F32),
        jax.ShapeDtypeStruct((B, LP, 512), BF16),
        jax.ShapeDtypeStruct((B, LP, 512), BF16),
        jax.ShapeDtypeStruct((B, LP, 256), F32),
        jax.ShapeDtypeStruct((B, LP, 512), BF16),
        jax.ShapeDtypeStruct((B, LP, 512), BF16),
        jax.ShapeDtypeStruct((B, LP, 512), BF16),
    ]
    return pl.pallas_call(
        _inproj_body,
        grid=(B, nb),
        in_specs=[
            pl.BlockSpec((1, ROW_BLOCK, D), lambda b, j: (b, jnp.maximum(j - 1, 0), 0)),
            full((ROW_BLOCK, D)), full((1, D)), full((1, D)),
            full((D, _MAIN_WIDTH)), full((D, GLA_RANK)), full((GLA_RANK, 256)), full((1, 256)),
        ],
        out_specs=[row_spec(s.shape[-1]) for s in out_shapes],
        out_shape=out_shapes,
        compiler_params=pltpu.CompilerParams(
            dimension_semantics=("parallel", "arbitrary"), vmem_limit_bytes=VMEM_LIMIT),
        name="inproj",
    )(x, meta_pad, ln_g, ln_b, w_main, w_ga, wa2, ba)


def _gla_body(q_ref, k_ref, v_ref, r_ref, la_ref, g_ref, o_ref, st_ref):
    n_chunks = q_ref.shape[1] // CHUNK
    st_ref[...] = jnp.zeros_like(st_ref)
    ri = lax.broadcasted_iota(jnp.int32, (CHUNK, CHUNK), 0)
    ci = lax.broadcasted_iota(jnp.int32, (CHUNK, CHUNK), 1)
    tri = (ri >= ci).astype(BF16)
    gain = g_ref[...]

    def chunk(c, carry):
        r0 = pl.multiple_of(c * CHUNK, CHUNK)
        la = la_ref[0, pl.ds(r0, CHUNK), :]
        hi, mid, lo = _split3_bf16(la)
        cum = (jnp.dot(tri, hi, preferred_element_type=F32)
               + jnp.dot(tri, mid, preferred_element_type=F32)
               + jnp.dot(tri, lo, preferred_element_type=F32))
        tot = cum[CHUNK - 1:CHUNK, :]
        kdec = (k_ref[0, pl.ds(r0, CHUNK), :] * jnp.exp(tot - cum)).astype(BF16)
        v = v_ref[0, pl.ds(r0, CHUNK), :]
        q = q_ref[0, pl.ds(r0, CHUNK), :]
        r = r_ref[0, pl.ds(r0, CHUNK), :].astype(F32)
        upd = [lax.dot_general(v[:, h * GLA_DV:(h + 1) * GLA_DV], kdec[:, h * GLA_DK:(h + 1) * GLA_DK],
                               (((0,), (0,)), ((), ())), preferred_element_type=F32)
               for h in range(GLA_HEADS)]
        st = st_ref[...] * jnp.exp(tot) + jnp.concatenate(upd, axis=1)
        st_ref[...] = st
        stb = st.astype(BF16)
        for h in range(GLA_HEADS):
            o = lax.dot_general(q[:, h * GLA_DK:(h + 1) * GLA_DK], stb[:, h * GLA_DK:(h + 1) * GLA_DK],
                                (((1,), (1,)), ((), ())), preferred_element_type=F32)
            ms = jnp.mean(o * o, axis=-1, keepdims=True)
            rh = r[:, h * GLA_DV:(h + 1) * GLA_DV]
            out = o * lax.rsqrt(ms + LN_EPS) * gain * (rh * jax.nn.sigmoid(rh))
            o_ref[0, pl.ds(r0, CHUNK), h * GLA_DV:(h + 1) * GLA_DV] = out.astype(BF16)
        return carry

    lax.fori_loop(0, n_chunks, chunk, 0)


def _gla(gq, gk, gv, gr, la, norm_g):
    B, LP, _ = gq.shape

    def seq(width):
        return pl.BlockSpec((1, LP, width), lambda b: (b, 0, 0))

    return pl.pallas_call(
        _gla_body,
        grid=(B,),
        in_specs=[seq(256), seq(256), seq(512), seq(512), seq(256),
                  pl.BlockSpec((1, GLA_DV), lambda b: (0, 0))],
        out_specs=seq(512),
        out_shape=jax.ShapeDtypeStruct((B, LP, 512), BF16),
        scratch_shapes=[pltpu.VMEM((GLA_DV, GLA_HEADS * GLA_DK), F32)],
        compiler_params=pltpu.CompilerParams(
            dimension_semantics=("parallel",), vmem_limit_bytes=VMEM_LIMIT),
        name="gla",
    )(gq, gk, gv, gr, la, norm_g)


ATT_BLOCK = 256


def _diff_body(q_ref, k_ref, v_ref, lq1_ref, lk1_ref, lq2_ref, lk2_ref, g_ref, o_ref,
               dmask_ref):
    h = pl.program_id(1)
    slope = jnp.where(h == 0, 2.0 ** -2, jnp.where(h == 1, 2.0 ** -4, jnp.where(h == 2, 2.0 ** -6, 2.0 ** -8)))
    slope = slope.astype(F32) * LOG2E
    lam = (jnp.exp(jnp.sum(lq1_ref[...] * lk1_ref[...], axis=-1, keepdims=True))
           - jnp.exp(jnp.sum(lq2_ref[...] * lk2_ref[...], axis=-1, keepdims=True)) + LAMBDA_INIT)
    gain = g_ref[...] * (1.0 - LAMBDA_INIT)
    n_qblocks = (q_ref.shape[1] - ROW_BLOCK) // ATT_BLOCK
    nt = (((1,), (1,)), ((), ()))

    def split_q(q):
        lane = lax.broadcasted_iota(jnp.int32, q.shape, 1)
        zero = jnp.zeros_like(q)
        return jnp.where(lane < DIFF_DH, q, zero), jnp.where(lane >= DIFF_DH, q, zero)

    def with_ones(v):
        return jnp.concatenate([v, jnp.ones_like(v)], axis=1)

    def softmax_av(qz, kk, vext, add_bias):
        s = add_bias(lax.dot_general(qz, kk, nt, preferred_element_type=F32))
        m = jnp.max(s, axis=-1, keepdims=True)
        p = jnp.exp2(s - m).astype(BF16)
        return jnp.dot(p, vext, preferred_element_type=F32)

    def finish(a1, a2):
        o = a1[:, :DIFF_DV] / a1[:, DIFF_DV:] - lam * (a2[:, :DIFF_DV] / a2[:, DIFF_DV:])
        ms = jnp.mean(o * o, axis=-1, keepdims=True)
        return (o * lax.rsqrt(ms + LN_EPS) * gain).astype(BF16)

    r = lax.broadcasted_iota(jnp.int32, (ROW_BLOCK, ROW_BLOCK), 0)
    c = lax.broadcasted_iota(jnp.int32, (ROW_BLOCK, ROW_BLOCK), 1)
    ok = jnp.logical_and(c // CHUNK <= r // CHUNK, c >= FRONT_PAD)
    bias_lead = jnp.where(ok, -slope * jnp.abs(r - c).astype(F32), NEG_INF)
    q1z, q2z = split_q(q_ref[0, 0:ROW_BLOCK, :])
    k_lead = k_ref[0, 0:ROW_BLOCK, :]
    v_lead = with_ones(v_ref[0, 0:ROW_BLOCK, :])
    o_ref[0, 0:ROW_BLOCK, :] = finish(softmax_av(q1z, k_lead, v_lead, lambda s: s + bias_lead),
                                      softmax_av(q2z, k_lead, v_lead, lambda s: s + bias_lead))

    r = lax.broadcasted_iota(jnp.int32, (ATT_BLOCK, ATT_BLOCK), 0)
    c = lax.broadcasted_iota(jnp.int32, (ATT_BLOCK, ATT_BLOCK), 1)
    rel = jnp.where(c <= r, c, 2 * r - c).astype(F32)
    dmask_ref[...] = jnp.where(c // CHUNK <= r // CHUNK, slope * rel, NEG_INF)

    for jq in range(n_qblocks):
        qbase = ROW_BLOCK + jq * ATT_BLOCK
        n_keys = qbase + ATT_BLOCK
        q1z, q2z = split_q(q_ref[0, qbase:qbase + ATT_BLOCK, :])
        kk = k_ref[0, 0:n_keys, :]
        vext = with_ones(v_ref[0, 0:n_keys, :])
        col = lax.broadcasted_iota(jnp.int32, (1, qbase), 1)
        col_bias = jnp.where(col >= FRONT_PAD, slope * (col - qbase).astype(F32), NEG_INF)

        def add_bias(s, col_bias=col_bias, qbase=qbase):
            return jnp.concatenate([s[:, :qbase] + col_bias, s[:, qbase:] + dmask_ref[...]], axis=1)

        o_ref[0, qbase:qbase + ATT_BLOCK, :] = finish(softmax_av(q1z, kk, vext, add_bias),
                                                      softmax_av(q2z, kk, vext, add_bias))


def _diff_attention(dq, dk, dv, lq1, lk1, lq2, lk2, norm_g):
    B, LP, W = dq.shape
    small = pl.BlockSpec((1, DIFF_DH), lambda b, h: (0, 0))
    seq = pl.BlockSpec((1, LP, 2 * DIFF_DH), lambda b, h: (b, 0, h))
    return pl.pallas_call(
        _diff_body,
        grid=(B, DIFF_HEADS),
        in_specs=[seq, seq, seq, small, small, small, small,
                  pl.BlockSpec((1, DIFF_DV), lambda b, h: (0, 0))],
        out_specs=seq,
        out_shape=jax.ShapeDtypeStruct((B, LP, W), BF16),
        scratch_shapes=[pltpu.VMEM((ATT_BLOCK, ATT_BLOCK), F32)],
        compiler_params=pltpu.CompilerParams(
            dimension_semantics=("parallel", "parallel"), vmem_limit_bytes=VMEM_LIMIT),
        name="diff_attn",
    )(dq, dk, dv, lq1, lk1, lq2, lk2, norm_g)


def _route_body(og_ref, od_ref, h0_ref, wog_ref, wod_ref, g_ref, b_ref, rw_ref, rb_ref,
                h1_ref, route_ref, cnt_ref, carry_ref, *, rows_per_seq):
    t = pl.program_id(0)

    @pl.when(t == 0)
    def _():
        carry_ref[...] = jnp.zeros_like(carry_ref)

    mix = (jnp.dot(og_ref[...], wog_ref[...], preferred_element_type=F32)
           + jnp.dot(od_ref[...], wod_ref[...], preferred_element_type=F32))
    h1 = _layer_norm(DEEPNORM_ALPHA * h0_ref[...] + mix, g_ref[...], b_ref[...])
    _store_token_tiled(h1_ref, h1)

    hh, hm, hl = _split3_bf16(h1)
    wh, wm, wl = _split3_bf16(rw_ref[...])
    logits = rb_ref[...]
    for a, b in ((hl, wh), (hh, wl), (hm, wm), (hm, wh), (hh, wm), (hh, wh)):
        logits = logits + jnp.dot(a, b, preferred_element_type=F32)

    T = logits.shape[0]
    lane = lax.broadcasted_iota(jnp.int32, (T, N_EXPERTS), 1)
    grow = t * T + lax.broadcasted_iota(jnp.int32, (T, 1), 0)
    valid = (grow % rows_per_seq) >= FRONT_PAD

    work = logits
    top_v, top_i = [], []
    for _ in range(TOP_K):
        mx = jnp.max(work, axis=-1, keepdims=True)
        idx = jnp.min(jnp.where(work == mx, lane, N_EXPERTS), axis=-1, keepdims=True)
        top_v.append(mx)
        top_i.append(idx)
        work = jnp.where(lane == idx, -jnp.inf, work)
    ex = [jnp.exp(v - top_v[0]) for v in top_v]
    den = ex[0] + ex[1] + ex[2] + ex[3]
    gates = [e / den for e in ex]

    onehot = jnp.zeros((T, N_EXPERTS), F32)
    for idx in top_i:
        onehot = onehot + (lane == idx).astype(F32)
    onehot = jnp.where(valid, onehot, 0.0)

    carry_ref[...] = carry_ref[...] + jnp.sum(onehot, axis=0, keepdims=True)
    cnt_ref[...] = carry_ref[...]

    lane_o = lax.broadcasted_iota(jnp.int32, (T, 128), 1)
    packed = jnp.zeros((T, 128), jnp.int32)
    for k in range(TOP_K):
        packed = jnp.where(lane_o == k, top_i[k], packed)
        packed = jnp.where(lane_o == 2 * TOP_K + k, lax.bitcast_convert_type(gates[k], jnp.int32), packed)
    route_ref[...] = packed


def _outproj_route(og, od, h0, wo_g, wo_d, ln_g, ln_b, rw, rb, rows_per_seq):
    NP, D = h0.shape
    T = ROUTE_TILE

    def rows(width):
        return pl.BlockSpec((T, width), lambda t: (t, 0))

    def full(shape):
        return pl.BlockSpec(shape, lambda t: (0,) * len(shape))

    return pl.pallas_call(
        functools.partial(_route_body, rows_per_seq=rows_per_seq),
        grid=(NP // T,),
        in_specs=[rows(512), rows(512), rows(D), full((512, D)), full((512, D)),
                  full((1, D)), full((1, D)), full((D, N_EXPERTS)), full((1, N_EXPERTS))],
        out_specs=[pl.BlockSpec((T * TOKEN_SUBROWS, 128), lambda t: (t, 0)), rows(128), full((1, N_EXPERTS))],
        out_shape=[jax.ShapeDtypeStruct((NP * TOKEN_SUBROWS, 128), F32),
                   jax.ShapeDtypeStruct((NP, 128), jnp.int32),
                   jax.ShapeDtypeStruct((1, N_EXPERTS), F32)],
        scratch_shapes=[pltpu.VMEM((1, N_EXPERTS), F32)],
        compiler_params=pltpu.CompilerParams(
            dimension_semantics=("arbitrary",), vmem_limit_bytes=VMEM_LIMIT),
        name="outproj_route",
    )(og, od, h0, wo_g, wo_d, ln_g, ln_b, rw, rb)


def _expert_body(order_ref, p0_ref, nv_ref, bstart_ref, nblk_ref, ntot_ref,
                 h1t_hbm, wgu_ref, bg_ref, bu_ref, wd_ref, bd_ref, yk_hbm,
                 wg_s, wu_s, wd_s, xb0, xb1, yb0, yb1, zbuf, gsem, ssem, zsem, *, n_tokens, rows_per_seq):
    e = pl.program_id(0)
    n_total = ntot_ref[0]
    R = MOE_ROWS
    TS = TOKEN_SUBROWS
    xb = (xb0, xb1)
    yb = (yb0, yb1)
    plane = n_tokens * TS
    spare0 = TOP_K * plane

    def gather_start(b, slot):
        p0 = p0_ref[b + 1]
        for r in range(R):
            tok = order_ref[p0 + r] >> 2
            pltpu.make_async_copy(
                h1t_hbm.at[pl.ds(pl.multiple_of(tok * TS, TS), TS), :],
                xb[slot].at[pl.ds(r * TS, TS), :], gsem.at[slot]).start()

    def gather_wait(slot):
        pltpu.make_async_copy(h1t_hbm.at[pl.ds(0, R * TS), :], xb[slot], gsem.at[slot]).wait()

    def scatter_start(b, slot):
        p0 = p0_ref[b + 1]
        nv = nv_ref[b + 1]
        for r in range(R):
            a = order_ref[p0 + r]
            row = jnp.where(r < nv, (a & (TOP_K - 1)) * plane + (a >> 2) * TS, spare0 + (slot * R + r) * TS)
            pltpu.make_async_copy(
                yb[slot].at[pl.ds(r * TS, TS), :],
                yk_hbm.at[pl.ds(pl.multiple_of(row, TS), TS), :], ssem.at[slot]).start()

    def scatter_wait(slot):
        pltpu.make_async_copy(yb[slot], yk_hbm.at[pl.ds(0, R * TS), :], ssem.at[slot]).wait()

    @pl.when(e == 0)
    def _():
        zbuf[...] = jnp.zeros_like(zbuf)
        yb1[...] = jnp.zeros_like(yb1)
        lead = FRONT_PAD * TS
        fills = [(k * plane + s * rows_per_seq * TS, lead)
                 for k in range(TOP_K) for s in range(n_tokens // rows_per_seq)]
        spare_rows = 2 * R * TS
        fills += [(spare0 + o, min(lead, spare_rows - o)) for o in range(0, spare_rows, lead)]
        copies = [pltpu.make_async_copy(zbuf.at[pl.ds(0, n), :], yk_hbm.at[pl.ds(o, n), :], zsem)
                  for o, n in fills]
        for cp in copies:
            cp.start()
        for cp in copies:
            cp.wait()
        gather_start(0, 0)

    @pl.when(nblk_ref[e] > 0)
    def _():
        r = lax.broadcasted_iota(jnp.int32, (256, 256), 0)
        c = lax.broadcasted_iota(jnp.int32, (256, 256), 1)
        perm = (r == jnp.where(c < 128, 2 * c, 2 * (c - 128) + 1)).astype(BF16)
        for tt in range(2 * D_FF // 256):
            wt = wgu_ref[0, :, tt * 256:(tt + 1) * 256].astype(BF16)
            sp = jnp.dot(wt, perm, preferred_element_type=F32)
            wg_s[:, tt * 128:(tt + 1) * 128] = sp[:, :128].astype(BF16)
            wu_s[:, tt * 128:(tt + 1) * 128] = sp[:, 128:].astype(BF16)
        wd_s[...] = wd_ref[0].astype(BF16)

    def run_block(b, slot):
        @pl.when(b >= 1)
        def _():
            scatter_wait(slot)

        gather_wait(slot)
        gather_start(jnp.minimum(b + 1, n_total - 1), 1 - slot)
        scatter_start(b - 1, 1 - slot)
        x = _load_token_tiled(xb[slot], R).astype(BF16)
        gt = jnp.dot(x, wg_s[...], preferred_element_type=F32) + bg_ref[0]
        up = jnp.dot(x, wu_s[...], preferred_element_type=F32) + bu_ref[0]
        gt = jnp.minimum(gt, SWIGLU_LIMIT)
        up = jnp.clip(up, -SWIGLU_LIMIT, SWIGLU_LIMIT)
        act = (up + 1.0) * (gt * jax.nn.sigmoid(SWIGLU_ALPHA * gt))
        y = jnp.dot(act.astype(BF16), wd_s[...], preferred_element_type=F32) + bd_ref[0]
        _store_token_tiled(yb[slot], y)

    def block(b, carry):
        for slot in range(2):
            pl.when(b % 2 == slot)(functools.partial(run_block, b, slot))
        return carry

    b0 = bstart_ref[e]
    lax.fori_loop(b0, b0 + nblk_ref[e], block, 0)

    @pl.when(e == pl.num_programs(0) - 1)
    def _():
        last = n_total - 1
        for slot in range(2):
            @pl.when(last % 2 == slot)
            def _(slot=slot):
                scatter_start(last, slot)
                gather_wait(1 - slot)
                scatter_wait(1 - slot)
                scatter_wait(slot)


def _experts(order, blk_p0, blk_nv, blk_start, n_blk, n_total, h1t, w_gate_up, b_gate, b_up, w_down,
             b_down, n_tokens, rows_per_seq):
    D = D_MODEL
    R = MOE_ROWS
    buf = pltpu.VMEM((R * TOKEN_SUBROWS, 128), F32)
    per_expert = lambda e, *_: (e, 0, 0)
    grid_spec = pltpu.PrefetchScalarGridSpec(
        num_scalar_prefetch=6,
        grid=(N_EXPERTS,),
        in_specs=[
            pl.BlockSpec(memory_space=pl.ANY),
            pl.BlockSpec((1, D, 2 * D_FF), per_expert),
            pl.BlockSpec((1, 1, D_FF), per_expert),
            pl.BlockSpec((1, 1, D_FF), per_expert),
            pl.BlockSpec((1, D_FF, D), per_expert),
            pl.BlockSpec((1, 1, D), per_expert),
        ],
        out_specs=pl.BlockSpec(memory_space=pl.ANY),
        scratch_shapes=[pltpu.VMEM((D, D_FF), BF16), pltpu.VMEM((D, D_FF), BF16), pltpu.VMEM((D_FF, D), BF16),
                        buf, buf, buf, buf, pltpu.VMEM((FRONT_PAD * TOKEN_SUBROWS, 128), F32),
                        pltpu.SemaphoreType.DMA((2,)), pltpu.SemaphoreType.DMA((2,)), pltpu.SemaphoreType.DMA],
    )
    out_rows = (TOP_K * n_tokens + 2 * R) * TOKEN_SUBROWS
    return pl.pallas_call(
        functools.partial(_expert_body, n_tokens=n_tokens, rows_per_seq=rows_per_seq),
        grid_spec=grid_spec,
        out_shape=jax.ShapeDtypeStruct((out_rows, 128), F32),
        compiler_params=pltpu.CompilerParams(
            dimension_semantics=("arbitrary",), vmem_limit_bytes=EXPERT_VMEM_LIMIT),
        name="experts",
    )(order, blk_p0, blk_nv, blk_start, n_blk, n_total, h1t, w_gate_up, b_gate, b_up, w_down, b_down)


def _combine_body(h1_ref, y0_ref, y1_ref, y2_ref, y3_ref, route_ref, g_ref, b_ref, o_ref):
    route = route_ref[...]
    y = jnp.zeros((ROW_BLOCK, D_MODEL), F32)
    for k, yk_ref in enumerate((y0_ref, y1_ref, y2_ref, y3_ref)):
        gate = lax.bitcast_convert_type(route[:, 2 * TOP_K + k:2 * TOP_K + k + 1], F32)
        y = y + gate * _load_token_tiled(yk_ref, ROW_BLOCK)
    h1 = _load_token_tiled(h1_ref, ROW_BLOCK)
    o_ref[0] = _layer_norm(DEEPNORM_ALPHA * h1 + y, g_ref[...], b_ref[...])


def _combine(h1t, yk, route, ln_g, ln_b, batch, seq_len):
    nb = seq_len // ROW_BLOCK
    blocks_per_seq = nb + 1
    blocks_per_plane = batch * blocks_per_seq
    tile_rows = ROW_BLOCK * TOKEN_SUBROWS

    def token_block(b, j):
        return b * blocks_per_seq + j + 1

    def plane(k):
        return pl.BlockSpec((tile_rows, 128), lambda b, j: (k * blocks_per_plane + token_block(b, j), 0))

    tiles = pl.BlockSpec((tile_rows, 128), lambda b, j: (token_block(b, j), 0))
    vec = pl.BlockSpec((1, D_MODEL), lambda b, j: (0, 0))
    return pl.pallas_call(
        _combine_body,
        grid=(batch, nb),
        in_specs=[tiles, plane(0), plane(1), plane(2), plane(3),
                  pl.BlockSpec((ROW_BLOCK, 128), lambda b, j: (token_block(b, j), 0)), vec, vec],
        out_specs=pl.BlockSpec((1, ROW_BLOCK, D_MODEL), lambda b, j: (b, j, 0)),
        out_shape=jax.ShapeDtypeStruct((batch, seq_len, D_MODEL), F32),
        compiler_params=pltpu.CompilerParams(
            dimension_semantics=("parallel", "parallel"), vmem_limit_bytes=VMEM_LIMIT),
        name="combine",
    )(h1t, yk, yk, yk, yk, route, ln_g, ln_b)


def kernel(x, meta_tokens, ln_emb_g, ln_emb_b, w_in, gla_wa2, gla_ba, gla_norm_g, diff_lambda_q1, diff_lambda_k1, diff_lambda_q2, diff_lambda_k2, diff_norm_g, w_out, ln1_g, ln1_b, router_w, router_b, w_gate_up, b_gate_up, w_down, b_down, ln2_g, ln2_b):
    B, S, D = x.shape
    LP = S + ROW_BLOCK
    NP = B * LP
    row = lambda v: v.reshape(1, -1)

    w = w_in[0]
    w_main = jnp.concatenate([w[:, :1536], w[:, 1552:]], axis=1).astype(BF16)
    w_ga = w[:, 1536:1552].astype(BF16)
    meta_pad = jnp.pad(meta_tokens, ((FRONT_PAD, 0), (0, 0)))

    h0, gq, gk, gv, gr, la, dq, dk, dv = _inproj(
        x, meta_pad, row(ln_emb_g), row(ln_emb_b), w_main, w_ga,
        gla_wa2[0].astype(BF16), row(gla_ba[0]))

    og = _gla(gq, gk, gv, gr, la, row(gla_norm_g[0]))
    od = _diff_attention(dq, dk, dv, row(diff_lambda_q1[0]), row(diff_lambda_k1[0]),
                         row(diff_lambda_q2[0]), row(diff_lambda_k2[0]), row(diff_norm_g[0]))

    wo = w_out[0].astype(BF16)
    h1t, route, counts = _outproj_route(
        og.reshape(NP, 512), od.reshape(NP, 512), h0.reshape(NP, D), wo[:512], wo[512:],
        row(ln1_g[0]), row(ln1_b[0]), router_w[0], row(router_b[0]), LP)

    R = MOE_ROWS
    i32 = jnp.int32
    counts = counts[0].astype(i32)
    n_blk = (counts + R - 1) // R
    blk_end = jnp.cumsum(n_blk)
    blk_start = blk_end - n_blk
    grp_start = jnp.cumsum(counts) - counts
    n_total = blk_end[-1:]
    n_assign_max = B * (S + N_META) * TOP_K
    max_blocks = (n_assign_max + N_EXPERTS * (R - 1)) // R
    g = jnp.minimum(jnp.arange(max_blocks, dtype=i32), n_total[0] - 1)
    is_e = (jnp.minimum(jnp.sum(g[:, None] >= blk_end[None, :], axis=1), N_EXPERTS - 1)[:, None]
            == jnp.arange(N_EXPERTS)[None, :])
    pick = lambda v: jnp.sum(jnp.where(is_e, v[None, :], 0), axis=1)
    local = (g - pick(blk_start)) * R
    zero = jnp.zeros((1,), i32)
    blk_p0 = jnp.concatenate([zero, (pick(grp_start) + local).astype(i32)])
    blk_nv = jnp.concatenate([zero, jnp.clip(pick(counts) - local, 0, R).astype(i32)])

    tok_valid = (jnp.arange(NP, dtype=i32) % LP) >= FRONT_PAD
    flat_e = jnp.where(tok_valid[:, None], route[:, :TOP_K], N_EXPERTS).reshape(-1)
    order = jnp.pad(jnp.argsort(flat_e, stable=True).astype(i32), (0, R))

    bgu = b_gate_up[0].reshape(N_EXPERTS, D_FF, 2)
    yk = _experts(order, blk_p0, blk_nv, blk_start.astype(i32), n_blk.astype(i32), n_total.astype(i32),
                  h1t, w_gate_up[0],
                  bgu[:, :, 0].reshape(N_EXPERTS, 1, D_FF), bgu[:, :, 1].reshape(N_EXPERTS, 1, D_FF),
                  w_down[0], b_down[0].reshape(N_EXPERTS, 1, D), NP, LP)

    return _combine(h1t, yk, route, row(ln2_g[0]), row(ln2_b[0]), B, S)
```

```python
import functools
import math

import jax
import jax.numpy as jnp
from jax import lax
from jax.experimental import pallas as pl
from jax.experimental.pallas import tpu as pltpu

F32 = jnp.float32
BF16 = jnp.bfloat16

D_MODEL = 1024
N_META = 16
ROW_BLOCK = 128
FRONT_PAD = ROW_BLOCK - N_META
CHUNK = 64

GLA_HEADS = 4
GLA_DK = 64
GLA_DV = 128
GLA_RANK = 16
GLA_TAU = 16.0
DIFF_HEADS = 4
DIFF_DH = 64
DIFF_DV = 128

N_EXPERTS = 32
TOP_K = 4
D_FF = 1024
SWIGLU_LIMIT = 7.0
SWIGLU_ALPHA = 1.702
MOE_ROWS = 256

DEEPNORM_ALPHA = 2.0 ** 0.25
LAMBDA_INIT = 0.8 - 0.6 * math.exp(0.0)
LN_EPS = 1e-5
NEG_INF = -1e30
LOG2E = math.log2(math.e)

ROUTE_TILE = 512
VMEM_LIMIT = 48 * 1024 * 1024
EXPERT_VMEM_LIMIT = 56 * 1024 * 1024

_GQ, _GK, _GV, _GR, _DQ, _DK, _DV = 0, 256, 512, 1024, 1536, 2048, 2560
_MAIN_WIDTH = 3072


def _layer_norm(x, g, b):
    mu = jnp.mean(x, axis=-1, keepdims=True)
    xc = x - mu
    var = jnp.mean(xc * xc, axis=-1, keepdims=True)
    return xc * lax.rsqrt(var + LN_EPS) * g + b


def _split3_bf16(x):
    hi = x.astype(BF16)
    r1 = x - hi.astype(F32)
    mid = r1.astype(BF16)
    lo = (r1 - mid.astype(F32)).astype(BF16)
    return hi, mid, lo


TOKEN_SUBROWS = D_MODEL // 128


def _store_token_tiled(ref, val, first_token=0):
    n = val.shape[0]
    for s in range(TOKEN_SUBROWS):
        ref[pl.ds(first_token * TOKEN_SUBROWS + s, n, stride=TOKEN_SUBROWS), :] = val[:, s * 128:(s + 1) * 128]


def _load_token_tiled(ref, n, first_row=0):
    return jnp.concatenate(
        [ref[pl.ds(first_row + s, n, stride=TOKEN_SUBROWS), :] for s in range(TOKEN_SUBROWS)], axis=1)


def _inproj_body(x_ref, meta_ref, g_ref, b_ref, w_ref, wga_ref, wa2_ref, ba_ref,
                 h0_ref, gq_ref, gk_ref, gv_ref, gr_ref, la_ref, dq_ref, dk_ref, dv_ref):
    j = pl.program_id(1)
    rows = lax.broadcasted_iota(jnp.int32, (ROW_BLOCK, 1), 0)
    valid = jnp.logical_or(j > 0, rows >= FRONT_PAD)
    xin = jnp.where(j > 0, x_ref[0], meta_ref[...])
    h = jnp.where(valid, _layer_norm(xin, g_ref[...], b_ref[...]), 0.0)
    h0_ref[0] = h
    hb = h.astype(BF16)

    def proj(off, width):
        return jnp.dot(hb, w_ref[:, off:off + width], preferred_element_type=F32)

    gq_ref[0] = (proj(_GQ, 256) * GLA_DK ** -0.5).astype(BF16)
    gk_ref[0] = proj(_GK, 256)
    gv_ref[0] = proj(_GV, 512).astype(BF16)
    gr_ref[0] = proj(_GR, 512).astype(BF16)
    dq_ref[0] = (proj(_DQ, 512) * (DIFF_DH ** -0.5 * LOG2E)).astype(BF16)
    dk_ref[0] = proj(_DK, 512).astype(BF16)
    dv_ref[0] = proj(_DV, 512).astype(BF16)

    a_lr = jnp.dot(hb, wga_ref[...], preferred_element_type=F32)
    z = jnp.dot(a_lr.astype(BF16), wa2_ref[...], preferred_element_type=F32) + ba_ref[...]
    log_sig = jnp.minimum(z, 0.0) - jnp.log1p(jnp.exp(-jnp.abs(z)))
    la_ref[0] = jnp.where(valid, log_sig / GLA_TAU, 0.0)


def _inproj(x, meta_pad, ln_g, ln_b, w_main, w_ga, wa2, ba):
    B, S, D = x.shape
    nb = S // ROW_BLOCK + 1
    LP = nb * ROW_BLOCK

    def row_spec(width):
        return pl.BlockSpec((1, ROW_BLOCK, width), lambda b, j: (b, j, 0))

    def full(shape):
        return pl.BlockSpec(shape, lambda b, j: (0,) * len(shape))

    out_shapes = [
        jax.ShapeDtypeStruct((B, LP, D), F32),
        jax.ShapeDtypeStruct((B, LP, 256), BF16),
        jax.ShapeDtypeStruct((B, LP, 256), F32),
        jax.ShapeDtypeStruct((B, LP, 512), BF16),
        jax.ShapeDtypeStruct((B, LP, 512), BF16),
        jax.ShapeDtypeStruct((B, LP, 256), F32),
        jax.ShapeDtypeStruct((B, LP, 512), BF16),
        jax.ShapeDtypeStruct((B, LP, 512), BF16),
        jax.ShapeDtypeStruct((B, LP, 512), BF16),
    ]
    return pl.pallas_call(
        _inproj_body,
        grid=(B, nb),
        in_specs=[
            pl.BlockSpec((1, ROW_BLOCK, D), lambda b, j: (b, jnp.maximum(j - 1, 0), 0)),
            full((ROW_BLOCK, D)), full((1, D)), full((1, D)),
            full((D, _MAIN_WIDTH)), full((D, GLA_RANK)), full((GLA_RANK, 256)), full((1, 256)),
        ],
        out_specs=[row_spec(s.shape[-1]) for s in out_shapes],
        out_shape=out_shapes,
        compiler_params=pltpu.CompilerParams(
            dimension_semantics=("parallel", "arbitrary"), vmem_limit_bytes=VMEM_LIMIT),
        name="inproj",
    )(x, meta_pad, ln_g, ln_b, w_main, w_ga, wa2, ba)


def _gla_body(q_ref, k_ref, v_ref, r_ref, la_ref, g_ref, o_ref):
    n_groups = q_ref.shape[1] // ROW_BLOCK
    ri = lax.broadcasted_iota(jnp.int32, (ROW_BLOCK, ROW_BLOCK), 0)
    ci = lax.broadcasted_iota(jnp.int32, (ROW_BLOCK, ROW_BLOCK), 1)
    later = jnp.logical_and(ri // CHUNK == ci // CHUNK, ci > ri).astype(BF16)
    sr = lax.broadcasted_iota(jnp.int32, (GLA_HEADS * GLA_DV, GLA_HEADS * GLA_DK), 0)
    sc = lax.broadcasted_iota(jnp.int32, (GLA_HEADS * GLA_DV, GLA_HEADS * GLA_DK), 1)
    same_head = sr // GLA_DV == sc // GLA_DK
    gain = g_ref[...]
    st = jnp.zeros((GLA_HEADS * GLA_DV, GLA_HEADS * GLA_DK), F32)

    for grp in range(n_groups):
        g0 = grp * ROW_BLOCK
        la = la_ref[0, g0:g0 + ROW_BLOCK, :]
        hi, mid, lo = _split3_bf16(la)
        suffix = (jnp.dot(later, hi, preferred_element_type=F32)
                  + jnp.dot(later, mid, preferred_element_type=F32)
                  + jnp.dot(later, lo, preferred_element_type=F32))
        kdec = (k_ref[0, g0:g0 + ROW_BLOCK, :] * jnp.exp(suffix)).astype(BF16)
        for half in range(ROW_BLOCK // CHUNK):
            f = half * CHUNK
            r0 = g0 + f
            tot = suffix[f:f + 1, :] + la[f:f + 1, :]
            upd = lax.dot_general(v_ref[0, r0:r0 + CHUNK, :], kdec[f:f + CHUNK, :],
                                  (((0,), (0,)), ((), ())), preferred_element_type=F32)
            st = st * jnp.exp(tot) + jnp.where(same_head, upd, 0.0)
            o = lax.dot_general(q_ref[0, r0:r0 + CHUNK, :], st.astype(BF16),
                                (((1,), (1,)), ((), ())), preferred_element_type=F32)
            r = r_ref[0, r0:r0 + CHUNK, :].astype(F32)
            for h in range(GLA_HEADS):
                oh = o[:, h * GLA_DV:(h + 1) * GLA_DV]
                rh = r[:, h * GLA_DV:(h + 1) * GLA_DV]
                ms = jnp.mean(oh * oh, axis=-1, keepdims=True)
                out = oh * lax.rsqrt(ms + LN_EPS) * gain * (rh * jax.nn.sigmoid(rh))
                o_ref[0, r0:r0 + CHUNK, h * GLA_DV:(h + 1) * GLA_DV] = out.astype(BF16)


def _gla(gq, gk, gv, gr, la, norm_g):
    B, LP, _ = gq.shape

    def seq(width):
        return pl.BlockSpec((1, LP, width), lambda b: (b, 0, 0))

    return pl.pallas_call(
        _gla_body,
        grid=(B,),
        in_specs=[seq(256), seq(256), seq(512), seq(512), seq(256),
                  pl.BlockSpec((1, GLA_DV), lambda b: (0, 0))],
        out_specs=seq(512),
        out_shape=jax.ShapeDtypeStruct((B, LP, 512), BF16),
        compiler_params=pltpu.CompilerParams(
            dimension_semantics=("parallel",), vmem_limit_bytes=VMEM_LIMIT),
        name="gla",
    )(gq, gk, gv, gr, la, norm_g)


ATT_BLOCK = 256


def _diff_body(q_ref, k_ref, v_ref, lq1_ref, lk1_ref, lq2_ref, lk2_ref, g_ref, o_ref,
               dmask_ref):
    h = pl.program_id(1)
    slope = jnp.where(h == 0, 2.0 ** -2, jnp.where(h == 1, 2.0 ** -4, jnp.where(h == 2, 2.0 ** -6, 2.0 ** -8)))
    slope = slope.astype(F32) * LOG2E
    lam = (jnp.exp(jnp.sum(lq1_ref[...] * lk1_ref[...], axis=-1, keepdims=True))
           - jnp.exp(jnp.sum(lq2_ref[...] * lk2_ref[...], axis=-1, keepdims=True)) + LAMBDA_INIT)
    gain = g_ref[...] * (1.0 - LAMBDA_INIT)
    n_qblocks = (q_ref.shape[1] - ROW_BLOCK) // ATT_BLOCK
    nt = (((1,), (1,)), ((), ()))

    def split_q(q):
        lane = lax.broadcasted_iota(jnp.int32, q.shape, 1)
        zero = jnp.zeros_like(q)
        return jnp.where(lane < DIFF_DH, q, zero), jnp.where(lane >= DIFF_DH, q, zero)

    def with_ones(v):
        return jnp.concatenate([v, jnp.ones_like(v)], axis=1)

    def softmax_av(qz, kk, vext, add_bias):
        s = add_bias(lax.dot_general(qz, kk, nt, preferred_element_type=F32))
        m = jnp.max(s, axis=-1, keepdims=True)
        p = jnp.exp2(s - m).astype(BF16)
        return jnp.dot(p, vext, preferred_element_type=F32)

    def finish(a1, a2):
        o = a1[:, :DIFF_DV] / a1[:, DIFF_DV:] - lam * (a2[:, :DIFF_DV] / a2[:, DIFF_DV:])
        ms = jnp.mean(o * o, axis=-1, keepdims=True)
        return (o * lax.rsqrt(ms + LN_EPS) * gain).astype(BF16)

    r = lax.broadcasted_iota(jnp.int32, (ROW_BLOCK, ROW_BLOCK), 0)
    c = lax.broadcasted_iota(jnp.int32, (ROW_BLOCK, ROW_BLOCK), 1)
    ok = jnp.logical_and(c // CHUNK <= r // CHUNK, c >= FRONT_PAD)
    bias_lead = jnp.where(ok, -slope * jnp.abs(r - c).astype(F32), NEG_INF)
    q1z, q2z = split_q(q_ref[0, 0:ROW_BLOCK, :])
    k_lead = k_ref[0, 0:ROW_BLOCK, :]
    v_lead = with_ones(v_ref[0, 0:ROW_BLOCK, :])
    o_ref[0, 0:ROW_BLOCK, :] = finish(softmax_av(q1z, k_lead, v_lead, lambda s: s + bias_lead),
                                      softmax_av(q2z, k_lead, v_lead, lambda s: s + bias_lead))

    r = lax.broadcasted_iota(jnp.int32, (ATT_BLOCK, ATT_BLOCK), 0)
    c = lax.broadcasted_iota(jnp.int32, (ATT_BLOCK, ATT_BLOCK), 1)
    rel = jnp.where(c <= r, c, 2 * r - c).astype(F32)
    dmask_ref[...] = jnp.where(c // CHUNK <= r // CHUNK, slope * rel, NEG_INF)

    for jq in range(n_qblocks):
        qbase = ROW_BLOCK + jq * ATT_BLOCK
        n_keys = qbase + ATT_BLOCK
        q1z, q2z = split_q(q_ref[0, qbase:qbase + ATT_BLOCK, :])
        kk = k_ref[0, 0:n_keys, :]
        vext = with_ones(v_ref[0, 0:n_keys, :])
        col = lax.broadcasted_iota(jnp.int32, (1, qbase), 1)
        col_bias = jnp.where(col >= FRONT_PAD, slope * (col - qbase).astype(F32), NEG_INF)

        def add_bias(s, col_bias=col_bias, qbase=qbase):
            return jnp.concatenate([s[:, :qbase] + col_bias, s[:, qbase:] + dmask_ref[...]], axis=1)

        o_ref[0, qbase:qbase + ATT_BLOCK, :] = finish(softmax_av(q1z, kk, vext, add_bias),
                                                      softmax_av(q2z, kk, vext, add_bias))


def _diff_attention(dq, dk, dv, lq1, lk1, lq2, lk2, norm_g):
    B, LP, W = dq.shape
    small = pl.BlockSpec((1, DIFF_DH), lambda b, h: (0, 0))
    seq = pl.BlockSpec((1, LP, 2 * DIFF_DH), lambda b, h: (b, 0, h))
    return pl.pallas_call(
        _diff_body,
        grid=(B, DIFF_HEADS),
        in_specs=[seq, seq, seq, small, small, small, small,
                  pl.BlockSpec((1, DIFF_DV), lambda b, h: (0, 0))],
        out_specs=seq,
        out_shape=jax.ShapeDtypeStruct((B, LP, W), BF16),
        scratch_shapes=[pltpu.VMEM((ATT_BLOCK, ATT_BLOCK), F32)],
        compiler_params=pltpu.CompilerParams(
            dimension_semantics=("parallel", "parallel"), vmem_limit_bytes=VMEM_LIMIT),
        name="diff_attn",
    )(dq, dk, dv, lq1, lk1, lq2, lk2, norm_g)


def _route_body(og_ref, od_ref, h0_ref, wog_ref, wod_ref, g_ref, b_ref, rw_ref, rb_ref,
                h1_ref, route_ref, cnt_ref, carry_ref, *, rows_per_seq):
    t = pl.program_id(0)

    @pl.when(t == 0)
    def _():
        carry_ref[...] = jnp.zeros_like(carry_ref)

    mix = (jnp.dot(og_ref[...], wog_ref[...], preferred_element_type=F32)
           + jnp.dot(od_ref[...], wod_ref[...], preferred_element_type=F32))
    h1 = _layer_norm(DEEPNORM_ALPHA * h0_ref[...] + mix, g_ref[...], b_ref[...])
    _store_token_tiled(h1_ref, h1)

    hh, hm, _ = _split3_bf16(h1)
    wh, wm, _ = _split3_bf16(rw_ref[...])
    head = jnp.dot(hh, jnp.concatenate([wh, wm], axis=1), preferred_element_type=F32)
    logits = (rb_ref[...] + head[:, :N_EXPERTS] + head[:, N_EXPERTS:]
              + jnp.dot(hm, wh, preferred_element_type=F32))

    T = logits.shape[0]
    lane = lax.broadcasted_iota(jnp.int32, (T, N_EXPERTS), 1)
    grow = t * T + lax.broadcasted_iota(jnp.int32, (T, 1), 0)
    valid = (grow % rows_per_seq) >= FRONT_PAD

    work = logits
    top_v, top_i = [], []
    for _ in range(TOP_K):
        mx = jnp.max(work, axis=-1, keepdims=True)
        idx = jnp.min(jnp.where(work == mx, lane, N_EXPERTS), axis=-1, keepdims=True)
        top_v.append(mx)
        top_i.append(idx)
        work = jnp.where(lane == idx, -jnp.inf, work)
    ex = [jnp.exp(v - top_v[0]) for v in top_v]
    den = ex[0] + ex[1] + ex[2] + ex[3]
    gates = [e / den for e in ex]

    onehot = jnp.zeros((T, N_EXPERTS), F32)
    for idx in top_i:
        onehot = onehot + (lane == idx).astype(F32)
    onehot = jnp.where(valid, onehot, 0.0)

    carry_ref[...] = carry_ref[...] + jnp.sum(onehot, axis=0, keepdims=True)
    cnt_ref[...] = carry_ref[...]

    lane_o = lax.broadcasted_iota(jnp.int32, (T, 128), 1)
    packed = jnp.zeros((T, 128), jnp.int32)
    for k in range(TOP_K):
        packed = jnp.where(lane_o == k, top_i[k], packed)
        packed = jnp.where(lane_o == 2 * TOP_K + k, lax.bitcast_convert_type(gates[k], jnp.int32), packed)
    route_ref[...] = packed


def _outproj_route(og, od, h0, wo_g, wo_d, ln_g, ln_b, rw, rb, rows_per_seq):
    NP, D = h0.shape
    T = ROUTE_TILE

    def rows(width):
        return pl.BlockSpec((T, width), lambda t: (t, 0))

    def full(shape):
        return pl.BlockSpec(shape, lambda t: (0,) * len(shape))

    return pl.pallas_call(
        functools.partial(_route_body, rows_per_seq=rows_per_seq),
        grid=(NP // T,),
        in_specs=[rows(512), rows(512), rows(D), full((512, D)), full((512, D)),
                  full((1, D)), full((1, D)), full((D, N_EXPERTS)), full((1, N_EXPERTS))],
        out_specs=[pl.BlockSpec((T * TOKEN_SUBROWS, 128), lambda t: (t, 0)), rows(128), full((1, N_EXPERTS))],
        out_shape=[jax.ShapeDtypeStruct((NP * TOKEN_SUBROWS, 128), F32),
                   jax.ShapeDtypeStruct((NP, 128), jnp.int32),
                   jax.ShapeDtypeStruct((1, N_EXPERTS), F32)],
        scratch_shapes=[pltpu.VMEM((1, N_EXPERTS), F32)],
        compiler_params=pltpu.CompilerParams(
            dimension_semantics=("arbitrary",), vmem_limit_bytes=VMEM_LIMIT),
        name="outproj_route",
    )(og, od, h0, wo_g, wo_d, ln_g, ln_b, rw, rb)


def _expert_body(order_ref, p0_ref, nv_ref, bstart_ref, nblk_ref, ntot_ref,
                 h1t_hbm, wgu_ref, bg_ref, bu_ref, wd_ref, bd_ref, yk_hbm,
                 wg_s, wu_s, wd_s, xb0, xb1, yb0, yb1, zbuf, gsem, ssem, zsem, *, n_tokens, rows_per_seq):
    e = pl.program_id(0)
    n_total = ntot_ref[0]
    R = MOE_ROWS
    TS = TOKEN_SUBROWS
    xb = (xb0, xb1)
    yb = (yb0, yb1)
    plane = n_tokens * TS
    spare0 = TOP_K * plane

    def gather_start(b, slot):
        p0 = p0_ref[b + 1]
        for r in range(R):
            tok = order_ref[p0 + r] >> 2
            pltpu.make_async_copy(
                h1t_hbm.at[pl.ds(pl.multiple_of(tok * TS, TS), TS), :],
                xb[slot].at[pl.ds(r * TS, TS), :], gsem.at[slot]).start(priority=r % 2)

    def gather_wait(slot):
        pltpu.make_async_copy(h1t_hbm.at[pl.ds(0, R * TS), :], xb[slot], gsem.at[slot]).wait()

    def scatter_start(b, slot):
        p0 = p0_ref[b + 1]
        nv = nv_ref[b + 1]
        for r in range(R):
            a = order_ref[p0 + r]
            row = jnp.where(r < nv, (a & (TOP_K - 1)) * plane + (a >> 2) * TS, spare0 + (slot * R + r) * TS)
            pltpu.make_async_copy(
                yb[slot].at[pl.ds(r * TS, TS), :],
                yk_hbm.at[pl.ds(pl.multiple_of(row, TS), TS), :], ssem.at[slot]).start(priority=r % 2)

    def scatter_wait(slot):
        pltpu.make_async_copy(yb[slot], yk_hbm.at[pl.ds(0, R * TS), :], ssem.at[slot]).wait()

    @pl.when(e == 0)
    def _():
        zbuf[...] = jnp.zeros_like(zbuf)
        yb1[...] = jnp.zeros_like(yb1)
        lead = FRONT_PAD * TS
        fills = [(k * plane + s * rows_per_seq * TS, lead)
                 for k in range(TOP_K) for s in range(n_tokens // rows_per_seq)]
        spare_rows = 2 * R * TS
        fills += [(spare0 + o, min(lead, spare_rows - o)) for o in range(0, spare_rows, lead)]
        copies = [pltpu.make_async_copy(zbuf.at[pl.ds(0, n), :], yk_hbm.at[pl.ds(o, n), :], zsem)
                  for o, n in fills]
        for cp in copies:
            cp.start()
        for cp in copies:
            cp.wait()
        gather_start(0, 0)

    @pl.when(nblk_ref[e] > 0)
    def _():
        r = lax.broadcasted_iota(jnp.int32, (256, 256), 0)
        c = lax.broadcasted_iota(jnp.int32, (256, 256), 1)
        perm = (r == jnp.where(c < 128, 2 * c, 2 * (c - 128) + 1)).astype(BF16)
        for tt in range(2 * D_FF // 256):
            wt = wgu_ref[0, :, tt * 256:(tt + 1) * 256].astype(BF16)
            sp = jnp.dot(wt, perm, preferred_element_type=F32)
            wg_s[:, tt * 128:(tt + 1) * 128] = sp[:, :128].astype(BF16)
            wu_s[:, tt * 128:(tt + 1) * 128] = sp[:, 128:].astype(BF16)
        wd_s[...] = wd_ref[0].astype(BF16)

    def run_block(b, slot):
        @pl.when(b >= 1)
        def _():
            scatter_wait(slot)

        gather_wait(slot)
        gather_start(jnp.minimum(b + 1, n_total - 1), 1 - slot)
        scatter_start(b - 1, 1 - slot)
        x = _load_token_tiled(xb[slot], R).astype(BF16)
        gt = jnp.dot(x, wg_s[...], preferred_element_type=F32) + bg_ref[0]
        up = jnp.dot(x, wu_s[...], preferred_element_type=F32) + bu_ref[0]
        gt = jnp.minimum(gt, SWIGLU_LIMIT)
        up = jnp.clip(up, -SWIGLU_LIMIT, SWIGLU_LIMIT)
        act = (up + 1.0) * (gt * jax.nn.sigmoid(SWIGLU_ALPHA * gt))
        y = jnp.dot(act.astype(BF16), wd_s[...], preferred_element_type=F32) + bd_ref[0]
        _store_token_tiled(yb[slot], y)

    def block(b, carry):
        for slot in range(2):
            pl.when(b % 2 == slot)(functools.partial(run_block, b, slot))
        return carry

    b0 = bstart_ref[e]
    lax.fori_loop(b0, b0 + nblk_ref[e], block, 0)

    @pl.when(e == pl.num_programs(0) - 1)
    def _():
        last = n_total - 1
        for slot in range(2):
            @pl.when(last % 2 == slot)
            def _(slot=slot):
                scatter_start(last, slot)
                gather_wait(1 - slot)
                scatter_wait(1 - slot)
                scatter_wait(slot)


def _experts(order, blk_p0, blk_nv, blk_start, n_blk, n_total, h1t, w_gate_up, b_gate, b_up, w_down,
             b_down, n_tokens, rows_per_seq):
    D = D_MODEL
    R = MOE_ROWS
    buf = pltpu.VMEM((R * TOKEN_SUBROWS, 128), F32)
    per_expert = lambda e, *_: (e, 0, 0)
    grid_spec = pltpu.PrefetchScalarGridSpec(
        num_scalar_prefetch=6,
        grid=(N_EXPERTS,),
        in_specs=[
            pl.BlockSpec(memory_space=pl.ANY),
            pl.BlockSpec((1, D, 2 * D_FF), per_expert),
            pl.BlockSpec((1, 1, D_FF), per_expert),
            pl.BlockSpec((1, 1, D_FF), per_expert),
            pl.BlockSpec((1, D_FF, D), per_expert),
            pl.BlockSpec((1, 1, D), per_expert),
        ],
        out_specs=pl.BlockSpec(memory_space=pl.ANY),
        scratch_shapes=[pltpu.VMEM((D, D_FF), BF16), pltpu.VMEM((D, D_FF), BF16), pltpu.VMEM((D_FF, D), BF16),
                        buf, buf, buf, buf, pltpu.VMEM((FRONT_PAD * TOKEN_SUBROWS, 128), F32),
                        pltpu.SemaphoreType.DMA((2,)), pltpu.SemaphoreType.DMA((2,)), pltpu.SemaphoreType.DMA],
    )
    out_rows = (TOP_K * n_tokens + 2 * R) * TOKEN_SUBROWS
    return pl.pallas_call(
        functools.partial(_expert_body, n_tokens=n_tokens, rows_per_seq=rows_per_seq),
        grid_spec=grid_spec,
        out_shape=jax.ShapeDtypeStruct((out_rows, 128), F32),
        compiler_params=pltpu.CompilerParams(
            dimension_semantics=("arbitrary",), vmem_limit_bytes=EXPERT_VMEM_LIMIT),
        name="experts",
    )(order, blk_p0, blk_nv, blk_start, n_blk, n_total, h1t, w_gate_up, b_gate, b_up, w_down, b_down)


def _combine_body(h1_ref, y0_ref, y1_ref, y2_ref, y3_ref, route_ref, g_ref, b_ref, o_ref):
    route = route_ref[...]
    y = jnp.zeros((ROW_BLOCK, D_MODEL), F32)
    for k, yk_ref in enumerate((y0_ref, y1_ref, y2_ref, y3_ref)):
        gate = lax.bitcast_convert_type(route[:, 2 * TOP_K + k:2 * TOP_K + k + 1], F32)
        y = y + gate * _load_token_tiled(yk_ref, ROW_BLOCK)
    h1 = _load_token_tiled(h1_ref, ROW_BLOCK)
    o_ref[0] = _layer_norm(DEEPNORM_ALPHA * h1 + y, g_ref[...], b_ref[...])


def _combine(h1t, yk, route, ln_g, ln_b, batch, seq_len):
    nb = seq_len // ROW_BLOCK
    blocks_per_seq = nb + 1
    blocks_per_plane = batch * blocks_per_seq
    tile_rows = ROW_BLOCK * TOKEN_SUBROWS

    def token_block(b, j):
        return b * blocks_per_seq + j + 1

    def plane(k):
        return pl.BlockSpec((tile_rows, 128), lambda b, j: (k * blocks_per_plane + token_block(b, j), 0))

    tiles = pl.BlockSpec((tile_rows, 128), lambda b, j: (token_block(b, j), 0))
    vec = pl.BlockSpec((1, D_MODEL), lambda b, j: (0, 0))
    return pl.pallas_call(
        _combine_body,
        grid=(batch, nb),
        in_specs=[tiles, plane(0), plane(1), plane(2), plane(3),
                  pl.BlockSpec((ROW_BLOCK, 128), lambda b, j: (token_block(b, j), 0)), vec, vec],
        out_specs=pl.BlockSpec((1, ROW_BLOCK, D_MODEL), lambda b, j: (b, j, 0)),
        out_shape=jax.ShapeDtypeStruct((batch, seq_len, D_MODEL), F32),
        compiler_params=pltpu.CompilerParams(
            dimension_semantics=("parallel", "parallel"), vmem_limit_bytes=VMEM_LIMIT),
        name="combine",
    )(h1t, yk, yk, yk, yk, route, ln_g, ln_b)


def kernel(x, meta_tokens, ln_emb_g, ln_emb_b, w_in, gla_wa2, gla_ba, gla_norm_g, diff_lambda_q1, diff_lambda_k1, diff_lambda_q2, diff_lambda_k2, diff_norm_g, w_out, ln1_g, ln1_b, router_w, router_b, w_gate_up, b_gate_up, w_down, b_down, ln2_g, ln2_b):
    B, S, D = x.shape
    LP = S + ROW_BLOCK
    NP = B * LP
    row = lambda v: v.reshape(1, -1)

    w = w_in[0]
    w_main = jnp.concatenate([w[:, :1536], w[:, 1552:]], axis=1).astype(BF16)
    w_ga = w[:, 1536:1552].astype(BF16)
    meta_pad = jnp.pad(meta_tokens, ((FRONT_PAD, 0), (0, 0)))

    h0, gq, gk, gv, gr, la, dq, dk, dv = _inproj(
        x, meta_pad, row(ln_emb_g), row(ln_emb_b), w_main, w_ga,
        gla_wa2[0].astype(BF16), row(gla_ba[0]))

    og = _gla(gq, gk, gv, gr, la, row(gla_norm_g[0]))
    od = _diff_attention(dq, dk, dv, row(diff_lambda_q1[0]), row(diff_lambda_k1[0]),
                         row(diff_lambda_q2[0]), row(diff_lambda_k2[0]), row(diff_norm_g[0]))

    wo = w_out[0].astype(BF16)
    h1t, route, counts = _outproj_route(
        og.reshape(NP, 512), od.reshape(NP, 512), h0.reshape(NP, D), wo[:512], wo[512:],
        row(ln1_g[0]), row(ln1_b[0]), router_w[0], row(router_b[0]), LP)

    R = MOE_ROWS
    i32 = jnp.int32
    counts = counts[0].astype(i32)
    n_blk = (counts + R - 1) // R
    blk_end = jnp.cumsum(n_blk)
    blk_start = blk_end - n_blk
    grp_start = jnp.cumsum(counts) - counts
    n_total = blk_end[-1:]
    n_assign_max = B * (S + N_META) * TOP_K
    max_blocks = (n_assign_max + N_EXPERTS * (R - 1)) // R
    g = jnp.minimum(jnp.arange(max_blocks, dtype=i32), n_total[0] - 1)
    is_e = (jnp.minimum(jnp.sum(g[:, None] >= blk_end[None, :], axis=1), N_EXPERTS - 1)[:, None]
            == jnp.arange(N_EXPERTS)[None, :])
    pick = lambda v: jnp.sum(jnp.where(is_e, v[None, :], 0), axis=1)
    local = (g - pick(blk_start)) * R
    zero = jnp.zeros((1,), i32)
    blk_p0 = jnp.concatenate([zero, (pick(grp_start) + local).astype(i32)])
    blk_nv = jnp.concatenate([zero, jnp.clip(pick(counts) - local, 0, R).astype(i32)])

    tok_valid = (jnp.arange(NP, dtype=i32) % LP) >= FRONT_PAD
    flat_e = jnp.where(tok_valid[:, None], route[:, :TOP_K], N_EXPERTS).reshape(-1)
    id_bits = (NP * TOP_K - 1).bit_length()
    keyed = jnp.sort(flat_e * (1 << id_bits) + jnp.arange(NP * TOP_K, dtype=i32))
    order = jnp.pad(keyed & ((1 << id_bits) - 1), (0, R))

    bgu = b_gate_up[0].reshape(N_EXPERTS, D_FF, 2)
    yk = _experts(order, blk_p0, blk_nv, blk_start.astype(i32), n_blk.astype(i32), n_total.astype(i32),
                  h1t, w_gate_up[0],
                  bgu[:, :, 0].reshape(N_EXPERTS, 1, D_FF), bgu[:, :, 1].reshape(N_EXPERTS, 1, D_FF),
                  w_down[0], b_down[0].reshape(N_EXPERTS, 1, D), NP, LP)

    return _combine(h1t, yk, route, row(ln2_g[0]), row(ln2_b[0]), B, S)
```

```python
import functools
import math

import jax
import jax.numpy as jnp
from jax import lax
from jax.experimental import pallas as pl
from jax.experimental.pallas import tpu as pltpu

F32 = jnp.float32
BF16 = jnp.bfloat16

D_MODEL = 1024
N_META = 16
ROW_BLOCK = 128
FRONT_PAD = ROW_BLOCK - N_META
CHUNK = 64

GLA_HEADS = 4
GLA_DK = 64
GLA_DV = 128
GLA_RANK = 16
GLA_TAU = 16.0
DIFF_HEADS = 4
DIFF_DH = 64
DIFF_DV = 128

N_EXPERTS = 32
TOP_K = 4
D_FF = 1024
SWIGLU_LIMIT = 7.0
SWIGLU_ALPHA = 1.702
MOE_ROWS = 256

DEEPNORM_ALPHA = 2.0 ** 0.25
LAMBDA_INIT = 0.8 - 0.6 * math.exp(0.0)
LN_EPS = 1e-5
NEG_INF = -1e30
LOG2E = math.log2(math.e)

ROUTE_TILE = 512
VMEM_LIMIT = 48 * 1024 * 1024
EXPERT_VMEM_LIMIT = 56 * 1024 * 1024

_GQ, _GK, _GV, _GR, _DQ, _DK, _DV = 0, 256, 512, 1024, 1536, 2048, 2560
_MAIN_WIDTH = 3072


def _layer_norm(x, g, b):
    mu = jnp.mean(x, axis=-1, keepdims=True)
    xc = x - mu
    var = jnp.mean(xc * xc, axis=-1, keepdims=True)
    return xc * lax.rsqrt(var + LN_EPS) * g + b


def _split3_bf16(x):
    hi = x.astype(BF16)
    r1 = x - hi.astype(F32)
    mid = r1.astype(BF16)
    lo = (r1 - mid.astype(F32)).astype(BF16)
    return hi, mid, lo


TOKEN_SUBROWS = D_MODEL // 128


def _store_token_tiled(ref, val, first_token=0):
    n = val.shape[0]
    for s in range(TOKEN_SUBROWS):
        ref[pl.ds(first_token * TOKEN_SUBROWS + s, n, stride=TOKEN_SUBROWS), :] = val[:, s * 128:(s + 1) * 128]


def _load_token_tiled(ref, n, first_row=0):
    return jnp.concatenate(
        [ref[pl.ds(first_row + s, n, stride=TOKEN_SUBROWS), :] for s in range(TOKEN_SUBROWS)], axis=1)


def _inproj_body(x_ref, meta_ref, g_ref, b_ref, w_ref, wga_ref, wa2_ref, ba_ref,
                 h0_ref, gq_ref, gk_ref, gv_ref, gr_ref, la_ref, dq_ref, dk_ref, dv_ref):
    j = pl.program_id(1)
    rows = lax.broadcasted_iota(jnp.int32, (ROW_BLOCK, 1), 0)
    valid = jnp.logical_or(j > 0, rows >= FRONT_PAD)
    xin = jnp.where(j > 0, x_ref[0], meta_ref[...])
    h = jnp.where(valid, _layer_norm(xin, g_ref[...], b_ref[...]), 0.0)
    h0_ref[0] = h
    hb = h.astype(BF16)

    def proj(off, width):
        return jnp.dot(hb, w_ref[:, off:off + width], preferred_element_type=F32)

    gq_ref[0] = (proj(_GQ, 256) * GLA_DK ** -0.5).astype(BF16)
    gk_ref[0] = proj(_GK, 256)
    gv_ref[0] = proj(_GV, 512).astype(BF16)
    gr_ref[0] = proj(_GR, 512).astype(BF16)
    dq_ref[0] = (proj(_DQ, 512) * (DIFF_DH ** -0.5 * LOG2E)).astype(BF16)
    dk_ref[0] = proj(_DK, 512).astype(BF16)
    dv_ref[0] = proj(_DV, 512).astype(BF16)

    a_lr = jnp.dot(hb, wga_ref[...], preferred_element_type=F32)
    z = jnp.dot(a_lr.astype(BF16), wa2_ref[...], preferred_element_type=F32) + ba_ref[...]
    log_sig = jnp.minimum(z, 0.0) - jnp.log1p(jnp.exp(-jnp.abs(z)))
    la_ref[0] = jnp.where(valid, log_sig / GLA_TAU, 0.0)


def _inproj(x, meta_pad, ln_g, ln_b, w_main, w_ga, wa2, ba):
    B, S, D = x.shape
    nb = S // ROW_BLOCK + 1
    LP = nb * ROW_BLOCK

    def row_spec(width):
        return pl.BlockSpec((1, ROW_BLOCK, width), lambda b, j: (b, j, 0))

    def full(shape):
        return pl.BlockSpec(shape, lambda b, j: (0,) * len(shape))

    out_shapes = [
        jax.ShapeDtypeStruct((B, LP, D), F32),
        jax.ShapeDtypeStruct((B, LP, 256), BF16),
        jax.ShapeDtypeStruct((B, LP, 256), F32),
        jax.ShapeDtypeStruct((B, LP, 512), BF16),
        jax.ShapeDtypeStruct((B, LP, 512), BF16),
        jax.ShapeDtypeStruct((B, LP, 256), F32),
        jax.ShapeDtypeStruct((B, LP, 512), BF16),
        jax.ShapeDtypeStruct((B, LP, 512), BF16),
        jax.ShapeDtypeStruct((B, LP, 512), BF16),
    ]
    return pl.pallas_call(
        _inproj_body,
        grid=(B, nb),
        in_specs=[
            pl.BlockSpec((1, ROW_BLOCK, D), lambda b, j: (b, jnp.maximum(j - 1, 0), 0)),
            full((ROW_BLOCK, D)), full((1, D)), full((1, D)),
            full((D, _MAIN_WIDTH)), full((D, GLA_RANK)), full((GLA_RANK, 256)), full((1, 256)),
        ],
        out_specs=[row_spec(s.shape[-1]) for s in out_shapes],
        out_shape=out_shapes,
        compiler_params=pltpu.CompilerParams(
            dimension_semantics=("parallel", "arbitrary"), vmem_limit_bytes=VMEM_LIMIT),
        name="inproj",
    )(x, meta_pad, ln_g, ln_b, w_main, w_ga, wa2, ba)


def _gla_body(q_ref, k_ref, v_ref, r_ref, la_ref, g_ref, o_ref):
    n_groups = q_ref.shape[1] // ROW_BLOCK
    ri = lax.broadcasted_iota(jnp.int32, (ROW_BLOCK, ROW_BLOCK), 0)
    ci = lax.broadcasted_iota(jnp.int32, (ROW_BLOCK, ROW_BLOCK), 1)
    later = jnp.logical_and(ri // CHUNK == ci // CHUNK, ci > ri).astype(BF16)
    sr = lax.broadcasted_iota(jnp.int32, (GLA_HEADS * GLA_DV, GLA_HEADS * GLA_DK), 0)
    sc = lax.broadcasted_iota(jnp.int32, (GLA_HEADS * GLA_DV, GLA_HEADS * GLA_DK), 1)
    same_head = sr // GLA_DV == sc // GLA_DK
    gain = g_ref[...]
    st = jnp.zeros((GLA_HEADS * GLA_DV, GLA_HEADS * GLA_DK), F32)

    for grp in range(n_groups):
        g0 = grp * ROW_BLOCK
        la = la_ref[0, g0:g0 + ROW_BLOCK, :]
        hi, mid, lo = _split3_bf16(la)
        suffix = (jnp.dot(later, hi, preferred_element_type=F32)
                  + jnp.dot(later, mid, preferred_element_type=F32)
                  + jnp.dot(later, lo, preferred_element_type=F32))
        kdec = (k_ref[0, g0:g0 + ROW_BLOCK, :] * jnp.exp(suffix)).astype(BF16)
        for half in range(ROW_BLOCK // CHUNK):
            f = half * CHUNK
            r0 = g0 + f
            tot = suffix[f:f + 1, :] + la[f:f + 1, :]
            upd = lax.dot_general(v_ref[0, r0:r0 + CHUNK, :], kdec[f:f + CHUNK, :],
                                  (((0,), (0,)), ((), ())), preferred_element_type=F32)
            st = st * jnp.exp(tot) + jnp.where(same_head, upd, 0.0)
            o = lax.dot_general(q_ref[0, r0:r0 + CHUNK, :], st.astype(BF16),
                                (((1,), (1,)), ((), ())), preferred_element_type=F32)
            r = r_ref[0, r0:r0 + CHUNK, :].astype(F32)
            for h in range(GLA_HEADS):
                oh = o[:, h * GLA_DV:(h + 1) * GLA_DV]
                rh = r[:, h * GLA_DV:(h + 1) * GLA_DV]
                ms = jnp.mean(oh * oh, axis=-1, keepdims=True)
                out = oh * lax.rsqrt(ms + LN_EPS) * gain * (rh * jax.nn.sigmoid(rh))
                o_ref[0, r0:r0 + CHUNK, h * GLA_DV:(h + 1) * GLA_DV] = out.astype(BF16)


def _gla(gq, gk, gv, gr, la, norm_g):
    B, LP, _ = gq.shape

    def seq(width):
        return pl.BlockSpec((1, LP, width), lambda b: (b, 0, 0))

    return pl.pallas_call(
        _gla_body,
        grid=(B,),
        in_specs=[seq(256), seq(256), seq(512), seq(512), seq(256),
                  pl.BlockSpec((1, GLA_DV), lambda b: (0, 0))],
        out_specs=seq(512),
        out_shape=jax.ShapeDtypeStruct((B, LP, 512), BF16),
        compiler_params=pltpu.CompilerParams(
            dimension_semantics=("parallel",), vmem_limit_bytes=VMEM_LIMIT),
        name="gla",
    )(gq, gk, gv, gr, la, norm_g)


ATT_BLOCK = 256


def _diff_body(q_ref, k_ref, v_ref, lq1_ref, lk1_ref, lq2_ref, lk2_ref, g_ref, o_ref,
               dmask_ref):
    h = pl.program_id(1)
    slope = jnp.where(h == 0, 2.0 ** -2, jnp.where(h == 1, 2.0 ** -4, jnp.where(h == 2, 2.0 ** -6, 2.0 ** -8)))
    slope = slope.astype(F32) * LOG2E
    lam = (jnp.exp(jnp.sum(lq1_ref[...] * lk1_ref[...], axis=-1, keepdims=True))
           - jnp.exp(jnp.sum(lq2_ref[...] * lk2_ref[...], axis=-1, keepdims=True)) + LAMBDA_INIT)
    gain = g_ref[...] * (1.0 - LAMBDA_INIT)
    n_qblocks = (q_ref.shape[1] - ROW_BLOCK) // ATT_BLOCK
    nt = (((1,), (1,)), ((), ()))

    def split_q(q):
        lane = lax.broadcasted_iota(jnp.int32, q.shape, 1)
        zero = jnp.zeros_like(q)
        return jnp.where(lane < DIFF_DH, q, zero), jnp.where(lane >= DIFF_DH, q, zero)

    def with_ones(v):
        return jnp.concatenate([v, jnp.ones_like(v)], axis=1)

    def softmax_av(qz, kk, vext, add_bias):
        s = add_bias(lax.dot_general(qz, kk, nt, preferred_element_type=F32))
        m = jnp.max(s, axis=-1, keepdims=True)
        p = jnp.exp2(s - m).astype(BF16)
        return jnp.dot(p, vext, preferred_element_type=F32)

    def finish(a1, a2):
        o = a1[:, :DIFF_DV] / a1[:, DIFF_DV:] - lam * (a2[:, :DIFF_DV] / a2[:, DIFF_DV:])
        ms = jnp.mean(o * o, axis=-1, keepdims=True)
        return (o * lax.rsqrt(ms + LN_EPS) * gain).astype(BF16)

    r = lax.broadcasted_iota(jnp.int32, (ROW_BLOCK, ROW_BLOCK), 0)
    c = lax.broadcasted_iota(jnp.int32, (ROW_BLOCK, ROW_BLOCK), 1)
    ok = jnp.logical_and(c // CHUNK <= r // CHUNK, c >= FRONT_PAD)
    bias_lead = jnp.where(ok, -slope * jnp.abs(r - c).astype(F32), NEG_INF)
    q1z, q2z = split_q(q_ref[0, 0:ROW_BLOCK, :])
    k_lead = k_ref[0, 0:ROW_BLOCK, :]
    v_lead = with_ones(v_ref[0, 0:ROW_BLOCK, :])
    o_ref[0, 0:ROW_BLOCK, :] = finish(softmax_av(q1z, k_lead, v_lead, lambda s: s + bias_lead),
                                      softmax_av(q2z, k_lead, v_lead, lambda s: s + bias_lead))

    r = lax.broadcasted_iota(jnp.int32, (ATT_BLOCK, ATT_BLOCK), 0)
    c = lax.broadcasted_iota(jnp.int32, (ATT_BLOCK, ATT_BLOCK), 1)
    rel = jnp.where(c <= r, c, 2 * r - c).astype(F32)
    dmask_ref[...] = jnp.where(c // CHUNK <= r // CHUNK, slope * rel, NEG_INF)

    for jq in range(n_qblocks):
        qbase = ROW_BLOCK + jq * ATT_BLOCK
        n_keys = qbase + ATT_BLOCK
        q1z, q2z = split_q(q_ref[0, qbase:qbase + ATT_BLOCK, :])
        kk = k_ref[0, 0:n_keys, :]
        vext = with_ones(v_ref[0, 0:n_keys, :])
        col = lax.broadcasted_iota(jnp.int32, (1, qbase), 1)
        col_bias = jnp.where(col >= FRONT_PAD, slope * (col - qbase).astype(F32), NEG_INF)

        def add_bias(s, col_bias=col_bias, qbase=qbase):
            return jnp.concatenate([s[:, :qbase] + col_bias, s[:, qbase:] + dmask_ref[...]], axis=1)

        o_ref[0, qbase:qbase + ATT_BLOCK, :] = finish(softmax_av(q1z, kk, vext, add_bias),
                                                      softmax_av(q2z, kk, vext, add_bias))


def _diff_attention(dq, dk, dv, lq1, lk1, lq2, lk2, norm_g):
    B, LP, W = dq.shape
    small = pl.BlockSpec((1, DIFF_DH), lambda b, h: (0, 0))
    seq = pl.BlockSpec((1, LP, 2 * DIFF_DH), lambda b, h: (b, 0, h))
    return pl.pallas_call(
        _diff_body,
        grid=(B, DIFF_HEADS),
        in_specs=[seq, seq, seq, small, small, small, small,
                  pl.BlockSpec((1, DIFF_DV), lambda b, h: (0, 0))],
        out_specs=seq,
        out_shape=jax.ShapeDtypeStruct((B, LP, W), BF16),
        scratch_shapes=[pltpu.VMEM((ATT_BLOCK, ATT_BLOCK), F32)],
        compiler_params=pltpu.CompilerParams(
            dimension_semantics=("parallel", "parallel"), vmem_limit_bytes=VMEM_LIMIT),
        name="diff_attn",
    )(dq, dk, dv, lq1, lk1, lq2, lk2, norm_g)


def _route_body(og_ref, od_ref, h0_ref, wog_ref, wod_ref, g_ref, b_ref, rw_ref, rb_ref,
                h1_ref, route_ref, cnt_ref, carry_ref, *, rows_per_seq):
    t = pl.program_id(0)

    @pl.when(t == 0)
    def _():
        carry_ref[...] = jnp.zeros_like(carry_ref)

    mix = (jnp.dot(og_ref[...], wog_ref[...], preferred_element_type=F32)
           + jnp.dot(od_ref[...], wod_ref[...], preferred_element_type=F32))
    h1 = _layer_norm(DEEPNORM_ALPHA * h0_ref[...] + mix, g_ref[...], b_ref[...])
    _store_token_tiled(h1_ref, h1)

    hh, hm, _ = _split3_bf16(h1)
    wh, wm, _ = _split3_bf16(rw_ref[...])
    head = jnp.dot(hh, jnp.concatenate([wh, wm], axis=1), preferred_element_type=F32)
    logits = (rb_ref[...] + head[:, :N_EXPERTS] + head[:, N_EXPERTS:]
              + jnp.dot(hm, wh, preferred_element_type=F32))

    T = logits.shape[0]
    lane = lax.broadcasted_iota(jnp.int32, (T, N_EXPERTS), 1)
    grow = t * T + lax.broadcasted_iota(jnp.int32, (T, 1), 0)
    valid = (grow % rows_per_seq) >= FRONT_PAD

    work = logits
    top_v, top_i = [], []
    for _ in range(TOP_K):
        mx = jnp.max(work, axis=-1, keepdims=True)
        idx = jnp.min(jnp.where(work == mx, lane, N_EXPERTS), axis=-1, keepdims=True)
        top_v.append(mx)
        top_i.append(idx)
        work = jnp.where(lane == idx, -jnp.inf, work)
    ex = [jnp.exp(v - top_v[0]) for v in top_v]
    den = ex[0] + ex[1] + ex[2] + ex[3]
    gates = [e / den for e in ex]

    onehot = jnp.zeros((T, N_EXPERTS), F32)
    for idx in top_i:
        onehot = onehot + (lane == idx).astype(F32)
    onehot = jnp.where(valid, onehot, 0.0)

    carry_ref[...] = carry_ref[...] + jnp.sum(onehot, axis=0, keepdims=True)
    cnt_ref[...] = carry_ref[...]

    lane_o = lax.broadcasted_iota(jnp.int32, (T, 128), 1)
    packed = jnp.zeros((T, 128), jnp.int32)
    for k in range(TOP_K):
        packed = jnp.where(lane_o == k, top_i[k], packed)
        packed = jnp.where(lane_o == 2 * TOP_K + k, lax.bitcast_convert_type(gates[k], jnp.int32), packed)
    route_ref[...] = packed


def _outproj_route(og, od, h0, wo_g, wo_d, ln_g, ln_b, rw, rb, rows_per_seq):
    NP, D = h0.shape
    T = ROUTE_TILE

    def rows(width):
        return pl.BlockSpec((T, width), lambda t: (t, 0))

    def full(shape):
        return pl.BlockSpec(shape, lambda t: (0,) * len(shape))

    return pl.pallas_call(
        functools.partial(_route_body, rows_per_seq=rows_per_seq),
        grid=(NP // T,),
        in_specs=[rows(512), rows(512), rows(D), full((512, D)), full((512, D)),
                  full((1, D)), full((1, D)), full((D, N_EXPERTS)), full((1, N_EXPERTS))],
        out_specs=[pl.BlockSpec((T * TOKEN_SUBROWS, 128), lambda t: (t, 0)), rows(128), full((1, N_EXPERTS))],
        out_shape=[jax.ShapeDtypeStruct((NP * TOKEN_SUBROWS, 128), F32),
                   jax.ShapeDtypeStruct((NP, 128), jnp.int32),
                   jax.ShapeDtypeStruct((1, N_EXPERTS), F32)],
        scratch_shapes=[pltpu.VMEM((1, N_EXPERTS), F32)],
        compiler_params=pltpu.CompilerParams(
            dimension_semantics=("arbitrary",), vmem_limit_bytes=VMEM_LIMIT),
        name="outproj_route",
    )(og, od, h0, wo_g, wo_d, ln_g, ln_b, rw, rb)


def _expert_body(order_ref, p0_ref, nv_ref, bstart_ref, nblk_ref, ntot_ref,
                 h1t_hbm, wgu_ref, bg_ref, bu_ref, wd_ref, bd_ref, yk_hbm,
                 wg_s, wu_s, wd_s, xb0, xb1, xb2, yb0, yb1, yb2, zbuf, gsem, ssem, zsem,
                 *, n_tokens, rows_per_seq):
    e = pl.program_id(0)
    n_total = ntot_ref[0]
    R = MOE_ROWS
    TS = TOKEN_SUBROWS
    xb = (xb0, xb1, xb2)
    yb = (yb0, yb1, yb2)
    NB = len(xb)
    plane = n_tokens * TS
    spare0 = TOP_K * plane

    def gather_start(b, slot):
        p0 = p0_ref[b + 1]
        for r in range(R):
            tok = order_ref[p0 + r] >> 2
            pltpu.make_async_copy(
                h1t_hbm.at[pl.ds(pl.multiple_of(tok * TS, TS), TS), :],
                xb[slot].at[pl.ds(r * TS, TS), :], gsem.at[slot]).start(priority=r % 2)

    def gather_wait(slot):
        pltpu.make_async_copy(h1t_hbm.at[pl.ds(0, R * TS), :], xb[slot], gsem.at[slot]).wait()

    def scatter_start(b, slot):
        p0 = p0_ref[b + 1]
        nv = nv_ref[b + 1]
        for r in range(R):
            a = order_ref[p0 + r]
            row = jnp.where(r < nv, (a & (TOP_K - 1)) * plane + (a >> 2) * TS, spare0 + (slot * R + r) * TS)
            pltpu.make_async_copy(
                yb[slot].at[pl.ds(r * TS, TS), :],
                yk_hbm.at[pl.ds(pl.multiple_of(row, TS), TS), :], ssem.at[slot]).start(priority=r % 2)

    def scatter_wait(slot):
        pltpu.make_async_copy(yb[slot], yk_hbm.at[pl.ds(0, R * TS), :], ssem.at[slot]).wait()

    @pl.when(e == 0)
    def _():
        zbuf[...] = jnp.zeros_like(zbuf)
        yb2[...] = jnp.zeros_like(yb2)
        lead = FRONT_PAD * TS
        fills = [(k * plane + s * rows_per_seq * TS, lead)
                 for k in range(TOP_K) for s in range(n_tokens // rows_per_seq)]
        spare_rows = NB * R * TS
        fills += [(spare0 + o, min(lead, spare_rows - o)) for o in range(0, spare_rows, lead)]
        copies = [pltpu.make_async_copy(zbuf.at[pl.ds(0, n), :], yk_hbm.at[pl.ds(o, n), :], zsem)
                  for o, n in fills]
        for cp in copies:
            cp.start()
        for cp in copies:
            cp.wait()
        gather_start(0, 0)
        gather_start(jnp.minimum(1, n_total - 1), 1)

    @pl.when(nblk_ref[e] > 0)
    def _():
        r = lax.broadcasted_iota(jnp.int32, (256, 256), 0)
        c = lax.broadcasted_iota(jnp.int32, (256, 256), 1)
        perm = (r == jnp.where(c < 128, 2 * c, 2 * (c - 128) + 1)).astype(BF16)
        for tt in range(2 * D_FF // 256):
            wt = wgu_ref[0, :, tt * 256:(tt + 1) * 256].astype(BF16)
            sp = jnp.dot(wt, perm, preferred_element_type=F32)
            wg_s[:, tt * 128:(tt + 1) * 128] = sp[:, :128].astype(BF16)
            wu_s[:, tt * 128:(tt + 1) * 128] = sp[:, 128:].astype(BF16)
        wd_s[...] = wd_ref[0].astype(BF16)

    def run_block(b, slot):
        nxt, prv = (slot + 1) % NB, (slot + 2) % NB

        @pl.when(b >= 1)
        def _():
            scatter_wait(nxt)

        gather_wait(slot)
        gather_start(jnp.minimum(b + 2, n_total - 1), prv)
        scatter_start(b - 1, prv)
        x = _load_token_tiled(xb[slot], R).astype(BF16)
        gt = jnp.dot(x, wg_s[...], preferred_element_type=F32) + bg_ref[0]
        up = jnp.dot(x, wu_s[...], preferred_element_type=F32) + bu_ref[0]
        gt = jnp.minimum(gt, SWIGLU_LIMIT)
        up = jnp.clip(up, -SWIGLU_LIMIT, SWIGLU_LIMIT)
        act = (up + 1.0) * (gt * jax.nn.sigmoid(SWIGLU_ALPHA * gt))
        y = jnp.dot(act.astype(BF16), wd_s[...], preferred_element_type=F32) + bd_ref[0]
        _store_token_tiled(yb[slot], y)

    def block(b, carry):
        for slot in range(NB):
            pl.when(b % NB == slot)(functools.partial(run_block, b, slot))
        return carry

    b0 = bstart_ref[e]
    lax.fori_loop(b0, b0 + nblk_ref[e], block, 0)

    @pl.when(e == pl.num_programs(0) - 1)
    def _():
        last = n_total - 1
        for slot in range(NB):
            @pl.when(last % NB == slot)
            def _(slot=slot):
                nxt, prv = (slot + 1) % NB, (slot + 2) % NB
                scatter_start(last, slot)
                gather_wait(nxt)
                gather_wait(prv)
                scatter_wait(prv)
                scatter_wait(slot)


def _experts(order, blk_p0, blk_nv, blk_start, n_blk, n_total, h1t, w_gate_up, b_gate, b_up, w_down,
             b_down, n_tokens, rows_per_seq):
    D = D_MODEL
    R = MOE_ROWS
    buf = pltpu.VMEM((R * TOKEN_SUBROWS, 128), F32)
    per_expert = lambda e, *_: (e, 0, 0)
    grid_spec = pltpu.PrefetchScalarGridSpec(
        num_scalar_prefetch=6,
        grid=(N_EXPERTS,),
        in_specs=[
            pl.BlockSpec(memory_space=pl.ANY),
            pl.BlockSpec((1, D, 2 * D_FF), per_expert),
            pl.BlockSpec((1, 1, D_FF), per_expert),
            pl.BlockSpec((1, 1, D_FF), per_expert),
            pl.BlockSpec((1, D_FF, D), per_expert),
            pl.BlockSpec((1, 1, D), per_expert),
        ],
        out_specs=pl.BlockSpec(memory_space=pl.ANY),
        scratch_shapes=[pltpu.VMEM((D, D_FF), BF16), pltpu.VMEM((D, D_FF), BF16), pltpu.VMEM((D_FF, D), BF16),
                        buf, buf, buf, buf, buf, buf, pltpu.VMEM((FRONT_PAD * TOKEN_SUBROWS, 128), F32),
                        pltpu.SemaphoreType.DMA((3,)), pltpu.SemaphoreType.DMA((3,)), pltpu.SemaphoreType.DMA],
    )
    out_rows = (TOP_K * n_tokens + 3 * R) * TOKEN_SUBROWS
    return pl.pallas_call(
        functools.partial(_expert_body, n_tokens=n_tokens, rows_per_seq=rows_per_seq),
        grid_spec=grid_spec,
        out_shape=jax.ShapeDtypeStruct((out_rows, 128), F32),
        compiler_params=pltpu.CompilerParams(
            dimension_semantics=("arbitrary",), vmem_limit_bytes=EXPERT_VMEM_LIMIT),
        name="experts",
    )(order, blk_p0, blk_nv, blk_start, n_blk, n_total, h1t, w_gate_up, b_gate, b_up, w_down, b_down)


def _combine_body(h1_ref, y0_ref, y1_ref, y2_ref, y3_ref, route_ref, g_ref, b_ref, o_ref):
    route = route_ref[...]
    y = jnp.zeros((ROW_BLOCK, D_MODEL), F32)
    for k, yk_ref in enumerate((y0_ref, y1_ref, y2_ref, y3_ref)):
        gate = lax.bitcast_convert_type(route[:, 2 * TOP_K + k:2 * TOP_K + k + 1], F32)
        y = y + gate * _load_token_tiled(yk_ref, ROW_BLOCK)
    h1 = _load_token_tiled(h1_ref, ROW_BLOCK)
    o_ref[0] = _layer_norm(DEEPNORM_ALPHA * h1 + y, g_ref[...], b_ref[...])


def _combine(h1t, yk, route, ln_g, ln_b, batch, seq_len):
    nb = seq_len // ROW_BLOCK
    blocks_per_seq = nb + 1
    blocks_per_plane = batch * blocks_per_seq
    tile_rows = ROW_BLOCK * TOKEN_SUBROWS

    def token_block(b, j):
        return b * blocks_per_seq + j + 1

    def plane(k):
        return pl.BlockSpec((tile_rows, 128), lambda b, j: (k * blocks_per_plane + token_block(b, j), 0))

    tiles = pl.BlockSpec((tile_rows, 128), lambda b, j: (token_block(b, j), 0))
    vec = pl.BlockSpec((1, D_MODEL), lambda b, j: (0, 0))
    return pl.pallas_call(
        _combine_body,
        grid=(batch, nb),
        in_specs=[tiles, plane(0), plane(1), plane(2), plane(3),
                  pl.BlockSpec((ROW_BLOCK, 128), lambda b, j: (token_block(b, j), 0)), vec, vec],
        out_specs=pl.BlockSpec((1, ROW_BLOCK, D_MODEL), lambda b, j: (b, j, 0)),
        out_shape=jax.ShapeDtypeStruct((batch, seq_len, D_MODEL), F32),
        compiler_params=pltpu.CompilerParams(
            dimension_semantics=("parallel", "parallel"), vmem_limit_bytes=VMEM_LIMIT),
        name="combine",
    )(h1t, yk, yk, yk, yk, route, ln_g, ln_b)


def kernel(x, meta_tokens, ln_emb_g, ln_emb_b, w_in, gla_wa2, gla_ba, gla_norm_g, diff_lambda_q1, diff_lambda_k1, diff_lambda_q2, diff_lambda_k2, diff_norm_g, w_out, ln1_g, ln1_b, router_w, router_b, w_gate_up, b_gate_up, w_down, b_down, ln2_g, ln2_b):
    B, S, D = x.shape
    LP = S + ROW_BLOCK
    NP = B * LP
    row = lambda v: v.reshape(1, -1)

    w = w_in[0]
    w_main = jnp.concatenate([w[:, :1536], w[:, 1552:]], axis=1).astype(BF16)
    w_ga = w[:, 1536:1552].astype(BF16)
    meta_pad = jnp.pad(meta_tokens, ((FRONT_PAD, 0), (0, 0)))

    h0, gq, gk, gv, gr, la, dq, dk, dv = _inproj(
        x, meta_pad, row(ln_emb_g), row(ln_emb_b), w_main, w_ga,
        gla_wa2[0].astype(BF16), row(gla_ba[0]))

    og = _gla(gq, gk, gv, gr, la, row(gla_norm_g[0]))
    od = _diff_attention(dq, dk, dv, row(diff_lambda_q1[0]), row(diff_lambda_k1[0]),
                         row(diff_lambda_q2[0]), row(diff_lambda_k2[0]), row(diff_norm_g[0]))

    wo = w_out[0].astype(BF16)
    h1t, route, counts = _outproj_route(
        og.reshape(NP, 512), od.reshape(NP, 512), h0.reshape(NP, D), wo[:512], wo[512:],
        row(ln1_g[0]), row(ln1_b[0]), router_w[0], row(router_b[0]), LP)

    R = MOE_ROWS
    i32 = jnp.int32
    counts = counts[0].astype(i32)
    n_blk = (counts + R - 1) // R
    blk_end = jnp.cumsum(n_blk)
    blk_start = blk_end - n_blk
    grp_start = jnp.cumsum(counts) - counts
    n_total = blk_end[-1:]
    n_assign_max = B * (S + N_META) * TOP_K
    max_blocks = (n_assign_max + N_EXPERTS * (R - 1)) // R
    g = jnp.minimum(jnp.arange(max_blocks, dtype=i32), n_total[0] - 1)
    is_e = (jnp.minimum(jnp.sum(g[:, None] >= blk_end[None, :], axis=1), N_EXPERTS - 1)[:, None]
            == jnp.arange(N_EXPERTS)[None, :])
    pick = lambda v: jnp.sum(jnp.where(is_e, v[None, :], 0), axis=1)
    local = (g - pick(blk_start)) * R
    zero = jnp.zeros((1,), i32)
    blk_p0 = jnp.concatenate([zero, (pick(grp_start) + local).astype(i32)])
    blk_nv = jnp.concatenate([zero, jnp.clip(pick(counts) - local, 0, R).astype(i32)])

    tok_valid = (jnp.arange(NP, dtype=i32) % LP) >= FRONT_PAD
    flat_e = jnp.where(tok_valid[:, None], route[:, :TOP_K], N_EXPERTS).reshape(-1)
    id_bits = (NP * TOP_K - 1).bit_length()
    keyed = jnp.sort(flat_e * (1 << id_bits) + jnp.arange(NP * TOP_K, dtype=i32))
    order = jnp.pad(keyed & ((1 << id_bits) - 1), (0, R))

    bgu = b_gate_up[0].reshape(N_EXPERTS, D_FF, 2)
    yk = _experts(order, blk_p0, blk_nv, blk_start.astype(i32), n_blk.astype(i32), n_total.astype(i32),
                  h1t, w_gate_up[0],
                  bgu[:, :, 0].reshape(N_EXPERTS, 1, D_FF), bgu[:, :, 1].reshape(N_EXPERTS, 1, D_FF),
                  w_down[0], b_down[0].reshape(N_EXPERTS, 1, D), NP, LP)

    return _combine(h1t, yk, route, row(ln2_g[0]), row(ln2_b[0]), B, S)
```

```python
import functools
import math

import jax
import jax.numpy as jnp
from jax import lax
from jax.experimental import pallas as pl
from jax.experimental.pallas import tpu as pltpu

F32 = jnp.float32
BF16 = jnp.bfloat16

D_MODEL = 1024
N_META = 16
ROW_BLOCK = 128
FRONT_PAD = ROW_BLOCK - N_META
CHUNK = 64

GLA_HEADS = 4
GLA_DK = 64
GLA_DV = 128
GLA_RANK = 16
GLA_TAU = 16.0
DIFF_HEADS = 4
DIFF_DH = 64
DIFF_DV = 128

N_EXPERTS = 32
TOP_K = 4
D_FF = 1024
SWIGLU_LIMIT = 7.0
SWIGLU_ALPHA = 1.702
MOE_ROWS = 256

DEEPNORM_ALPHA = 2.0 ** 0.25
LAMBDA_INIT = 0.8 - 0.6 * math.exp(0.0)
LN_EPS = 1e-5
NEG_INF = -1e30
LOG2E = math.log2(math.e)

ROUTE_TILE = 512
VMEM_LIMIT = 48 * 1024 * 1024
EXPERT_VMEM_LIMIT = 56 * 1024 * 1024

_GQ, _GK, _GV, _GR, _DQ, _DK, _DV = 0, 256, 512, 1024, 1536, 2048, 2560
_MAIN_WIDTH = 3072


def _layer_norm(x, g, b):
    mu = jnp.mean(x, axis=-1, keepdims=True)
    xc = x - mu
    var = jnp.mean(xc * xc, axis=-1, keepdims=True)
    return xc * lax.rsqrt(var + LN_EPS) * g + b


def _split3_bf16(x):
    hi = x.astype(BF16)
    r1 = x - hi.astype(F32)
    mid = r1.astype(BF16)
    lo = (r1 - mid.astype(F32)).astype(BF16)
    return hi, mid, lo


TOKEN_SUBROWS = D_MODEL // 128


def _store_token_tiled(ref, val, first_token=0):
    n = val.shape[0]
    for s in range(TOKEN_SUBROWS):
        ref[pl.ds(first_token * TOKEN_SUBROWS + s, n, stride=TOKEN_SUBROWS), :] = val[:, s * 128:(s + 1) * 128]


def _load_token_tiled(ref, n, first_row=0):
    return jnp.concatenate(
        [ref[pl.ds(first_row + s, n, stride=TOKEN_SUBROWS), :] for s in range(TOKEN_SUBROWS)], axis=1)


def _inproj_body(x_ref, meta_ref, g_ref, b_ref, w_ref, wga_ref, wa2_ref, ba_ref,
                 h0_ref, gq_ref, gk_ref, gv_ref, gr_ref, la_ref, dq_ref, dk_ref, dv_ref):
    j = pl.program_id(1)
    rows = lax.broadcasted_iota(jnp.int32, (ROW_BLOCK, 1), 0)
    valid = jnp.logical_or(j > 0, rows >= FRONT_PAD)
    xin = jnp.where(j > 0, x_ref[0], meta_ref[...])
    h = jnp.where(valid, _layer_norm(xin, g_ref[...], b_ref[...]), 0.0)
    h0_ref[0] = h
    hb = h.astype(BF16)

    def proj(off, width):
        return jnp.dot(hb, w_ref[:, off:off + width], preferred_element_type=F32)

    gq_ref[0] = (proj(_GQ, 256) * GLA_DK ** -0.5).astype(BF16)
    gk_ref[0] = proj(_GK, 256)
    gv_ref[0] = proj(_GV, 512).astype(BF16)
    gr_ref[0] = proj(_GR, 512).astype(BF16)
    dq_ref[0] = (proj(_DQ, 512) * (DIFF_DH ** -0.5 * LOG2E)).astype(BF16)
    dk_ref[0] = proj(_DK, 512).astype(BF16)
    dv_ref[0] = proj(_DV, 512).astype(BF16)

    a_lr = jnp.dot(hb, wga_ref[...], preferred_element_type=F32)
    z = jnp.dot(a_lr.astype(BF16), wa2_ref[...], preferred_element_type=F32) + ba_ref[...]
    log_sig = jnp.minimum(z, 0.0) - jnp.log1p(jnp.exp(-jnp.abs(z)))
    la_ref[0] = jnp.where(valid, log_sig / GLA_TAU, 0.0)


def _inproj(x, meta_pad, ln_g, ln_b, w_main, w_ga, wa2, ba):
    B, S, D = x.shape
    nb = S // ROW_BLOCK + 1
    LP = nb * ROW_BLOCK

    def row_spec(width):
        return pl.BlockSpec((1, ROW_BLOCK, width), lambda b, j: (b, j, 0))

    def full(shape):
        return pl.BlockSpec(shape, lambda b, j: (0,) * len(shape))

    out_shapes = [
        jax.ShapeDtypeStruct((B, LP, D), F32),
        jax.ShapeDtypeStruct((B, LP, 256), BF16),
        jax.ShapeDtypeStruct((B, LP, 256), F32),
        jax.ShapeDtypeStruct((B, LP, 512), BF16),
        jax.ShapeDtypeStruct((B, LP, 512), BF16),
        jax.ShapeDtypeStruct((B, LP, 256), F32),
        jax.ShapeDtypeStruct((B, LP, 512), BF16),
        jax.ShapeDtypeStruct((B, LP, 512), BF16),
        jax.ShapeDtypeStruct((B, LP, 512), BF16),
    ]
    return pl.pallas_call(
        _inproj_body,
        grid=(B, nb),
        in_specs=[
            pl.BlockSpec((1, ROW_BLOCK, D), lambda b, j: (b, jnp.maximum(j - 1, 0), 0)),
            full((ROW_BLOCK, D)), full((1, D)), full((1, D)),
            full((D, _MAIN_WIDTH)), full((D, GLA_RANK)), full((GLA_RANK, 256)), full((1, 256)),
        ],
        out_specs=[row_spec(s.shape[-1]) for s in out_shapes],
        out_shape=out_shapes,
        compiler_params=pltpu.CompilerParams(
            dimension_semantics=("parallel", "arbitrary"), vmem_limit_bytes=VMEM_LIMIT),
        name="inproj",
    )(x, meta_pad, ln_g, ln_b, w_main, w_ga, wa2, ba)


def _gla_body(q_ref, k_ref, v_ref, r_ref, la_ref, g_ref, o_ref):
    n_groups = q_ref.shape[1] // ROW_BLOCK
    ri = lax.broadcasted_iota(jnp.int32, (ROW_BLOCK, ROW_BLOCK), 0)
    ci = lax.broadcasted_iota(jnp.int32, (ROW_BLOCK, ROW_BLOCK), 1)
    later = jnp.logical_and(ri // CHUNK == ci // CHUNK, ci > ri).astype(BF16)
    sr = lax.broadcasted_iota(jnp.int32, (GLA_HEADS * GLA_DV, GLA_HEADS * GLA_DK), 0)
    sc = lax.broadcasted_iota(jnp.int32, (GLA_HEADS * GLA_DV, GLA_HEADS * GLA_DK), 1)
    same_head = sr // GLA_DV == sc // GLA_DK
    gain = g_ref[...]
    st = jnp.zeros((GLA_HEADS * GLA_DV, GLA_HEADS * GLA_DK), F32)

    for grp in range(n_groups):
        g0 = grp * ROW_BLOCK
        la = la_ref[0, g0:g0 + ROW_BLOCK, :]
        hi, mid, lo = _split3_bf16(la)
        suffix = (jnp.dot(later, hi, preferred_element_type=F32)
                  + jnp.dot(later, mid, preferred_element_type=F32)
                  + jnp.dot(later, lo, preferred_element_type=F32))
        kdec = (k_ref[0, g0:g0 + ROW_BLOCK, :] * jnp.exp(suffix)).astype(BF16)
        for half in range(ROW_BLOCK // CHUNK):
            f = half * CHUNK
            r0 = g0 + f
            tot = suffix[f:f + 1, :] + la[f:f + 1, :]
            upd = lax.dot_general(v_ref[0, r0:r0 + CHUNK, :], kdec[f:f + CHUNK, :],
                                  (((0,), (0,)), ((), ())), preferred_element_type=F32)
            st = st * jnp.exp(tot) + jnp.where(same_head, upd, 0.0)
            o = lax.dot_general(q_ref[0, r0:r0 + CHUNK, :], st.astype(BF16),
                                (((1,), (1,)), ((), ())), preferred_element_type=F32)
            r = r_ref[0, r0:r0 + CHUNK, :].astype(F32)
            for h in range(GLA_HEADS):
                oh = o[:, h * GLA_DV:(h + 1) * GLA_DV]
                rh = r[:, h * GLA_DV:(h + 1) * GLA_DV]
                ms = jnp.mean(oh * oh, axis=-1, keepdims=True)
                out = oh * lax.rsqrt(ms + LN_EPS) * gain * (rh * jax.nn.sigmoid(rh))
                o_ref[0, r0:r0 + CHUNK, h * GLA_DV:(h + 1) * GLA_DV] = out.astype(BF16)


def _gla(gq, gk, gv, gr, la, norm_g):
    B, LP, _ = gq.shape

    def seq(width):
        return pl.BlockSpec((1, LP, width), lambda b: (b, 0, 0))

    return pl.pallas_call(
        _gla_body,
        grid=(B,),
        in_specs=[seq(256), seq(256), seq(512), seq(512), seq(256),
                  pl.BlockSpec((1, GLA_DV), lambda b: (0, 0))],
        out_specs=seq(512),
        out_shape=jax.ShapeDtypeStruct((B, LP, 512), BF16),
        compiler_params=pltpu.CompilerParams(
            dimension_semantics=("parallel",), vmem_limit_bytes=VMEM_LIMIT),
        name="gla",
    )(gq, gk, gv, gr, la, norm_g)


ATT_BLOCK = 256


def _diff_body(q_ref, k_ref, v_ref, lq1_ref, lk1_ref, lq2_ref, lk2_ref, g_ref, o_ref,
               dmask_ref):
    h = pl.program_id(1)
    slope = jnp.where(h == 0, 2.0 ** -2, jnp.where(h == 1, 2.0 ** -4, jnp.where(h == 2, 2.0 ** -6, 2.0 ** -8)))
    slope = slope.astype(F32) * LOG2E
    lam = (jnp.exp(jnp.sum(lq1_ref[...] * lk1_ref[...], axis=-1, keepdims=True))
           - jnp.exp(jnp.sum(lq2_ref[...] * lk2_ref[...], axis=-1, keepdims=True)) + LAMBDA_INIT)
    gain = g_ref[...] * (1.0 - LAMBDA_INIT)
    n_qblocks = (q_ref.shape[1] - ROW_BLOCK) // ATT_BLOCK
    nt = (((1,), (1,)), ((), ()))

    def split_q(q):
        lane = lax.broadcasted_iota(jnp.int32, q.shape, 1)
        zero = jnp.zeros_like(q)
        return jnp.where(lane < DIFF_DH, q, zero), jnp.where(lane >= DIFF_DH, q, zero)

    def with_ones(v):
        return jnp.concatenate([v, jnp.ones_like(v)], axis=1)

    def softmax_av(qz, kk, vext, add_bias):
        s = add_bias(lax.dot_general(qz, kk, nt, preferred_element_type=F32))
        m = jnp.max(s, axis=-1, keepdims=True)
        p = jnp.exp2(s - m).astype(BF16)
        return jnp.dot(p, vext, preferred_element_type=F32)

    def finish(a1, a2):
        o = a1[:, :DIFF_DV] / a1[:, DIFF_DV:] - lam * (a2[:, :DIFF_DV] / a2[:, DIFF_DV:])
        ms = jnp.mean(o * o, axis=-1, keepdims=True)
        return (o * lax.rsqrt(ms + LN_EPS) * gain).astype(BF16)

    r = lax.broadcasted_iota(jnp.int32, (ROW_BLOCK, ROW_BLOCK), 0)
    c = lax.broadcasted_iota(jnp.int32, (ROW_BLOCK, ROW_BLOCK), 1)
    ok = jnp.logical_and(c // CHUNK <= r // CHUNK, c >= FRONT_PAD)
    bias_lead = jnp.where(ok, -slope * jnp.abs(r - c).astype(F32), NEG_INF)
    q1z, q2z = split_q(q_ref[0, 0:ROW_BLOCK, :])
    k_lead = k_ref[0, 0:ROW_BLOCK, :]
    v_lead = with_ones(v_ref[0, 0:ROW_BLOCK, :])
    o_ref[0, 0:ROW_BLOCK, :] = finish(softmax_av(q1z, k_lead, v_lead, lambda s: s + bias_lead),
                                      softmax_av(q2z, k_lead, v_lead, lambda s: s + bias_lead))

    r = lax.broadcasted_iota(jnp.int32, (ATT_BLOCK, ATT_BLOCK), 0)
    c = lax.broadcasted_iota(jnp.int32, (ATT_BLOCK, ATT_BLOCK), 1)
    rel = jnp.where(c <= r, c, 2 * r - c).astype(F32)
    dmask_ref[...] = jnp.where(c // CHUNK <= r // CHUNK, slope * rel, NEG_INF)

    for jq in range(n_qblocks):
        qbase = ROW_BLOCK + jq * ATT_BLOCK
        n_keys = qbase + ATT_BLOCK
        q1z, q2z = split_q(q_ref[0, qbase:qbase + ATT_BLOCK, :])
        kk = k_ref[0, 0:n_keys, :]
        vext = with_ones(v_ref[0, 0:n_keys, :])
        col = lax.broadcasted_iota(jnp.int32, (1, qbase), 1)
        col_bias = jnp.where(col >= FRONT_PAD, slope * (col - qbase).astype(F32), NEG_INF)

        def add_bias(s, col_bias=col_bias, qbase=qbase):
            return jnp.concatenate([s[:, :qbase] + col_bias, s[:, qbase:] + dmask_ref[...]], axis=1)

        o_ref[0, qbase:qbase + ATT_BLOCK, :] = finish(softmax_av(q1z, kk, vext, add_bias),
                                                      softmax_av(q2z, kk, vext, add_bias))


def _diff_attention(dq, dk, dv, lq1, lk1, lq2, lk2, norm_g):
    B, LP, W = dq.shape
    small = pl.BlockSpec((1, DIFF_DH), lambda b, h: (0, 0))
    seq = pl.BlockSpec((1, LP, 2 * DIFF_DH), lambda b, h: (b, 0, h))
    return pl.pallas_call(
        _diff_body,
        grid=(B, DIFF_HEADS),
        in_specs=[seq, seq, seq, small, small, small, small,
                  pl.BlockSpec((1, DIFF_DV), lambda b, h: (0, 0))],
        out_specs=seq,
        out_shape=jax.ShapeDtypeStruct((B, LP, W), BF16),
        scratch_shapes=[pltpu.VMEM((ATT_BLOCK, ATT_BLOCK), F32)],
        compiler_params=pltpu.CompilerParams(
            dimension_semantics=("parallel", "parallel"), vmem_limit_bytes=VMEM_LIMIT),
        name="diff_attn",
    )(dq, dk, dv, lq1, lk1, lq2, lk2, norm_g)


def _route_body(og_ref, od_ref, h0_ref, wog_ref, wod_ref, g_ref, b_ref, rw_ref, rb_ref,
                h1_ref, route_ref, cnt_ref, carry_ref, *, rows_per_seq):
    t = pl.program_id(0)

    @pl.when(t == 0)
    def _():
        carry_ref[...] = jnp.zeros_like(carry_ref)

    mix = (jnp.dot(og_ref[...], wog_ref[...], preferred_element_type=F32)
           + jnp.dot(od_ref[...], wod_ref[...], preferred_element_type=F32))
    h1 = _layer_norm(DEEPNORM_ALPHA * h0_ref[...] + mix, g_ref[...], b_ref[...])
    _store_token_tiled(h1_ref, h1)

    hh, hm, _ = _split3_bf16(h1)
    wh, wm, _ = _split3_bf16(rw_ref[...])
    head = jnp.dot(hh, jnp.concatenate([wh, wm], axis=1), preferred_element_type=F32)
    logits = (rb_ref[...] + head[:, :N_EXPERTS] + head[:, N_EXPERTS:]
              + jnp.dot(hm, wh, preferred_element_type=F32))

    T = logits.shape[0]
    lane = lax.broadcasted_iota(jnp.int32, (T, N_EXPERTS), 1)
    grow = t * T + lax.broadcasted_iota(jnp.int32, (T, 1), 0)
    valid = (grow % rows_per_seq) >= FRONT_PAD

    work = logits
    top_v, top_i = [], []
    for _ in range(TOP_K):
        mx = jnp.max(work, axis=-1, keepdims=True)
        idx = jnp.min(jnp.where(work == mx, lane, N_EXPERTS), axis=-1, keepdims=True)
        top_v.append(mx)
        top_i.append(idx)
        work = jnp.where(lane == idx, -jnp.inf, work)
    ex = [jnp.exp(v - top_v[0]) for v in top_v]
    den = ex[0] + ex[1] + ex[2] + ex[3]
    gates = [e / den for e in ex]

    onehot = jnp.zeros((T, N_EXPERTS), F32)
    for idx in top_i:
        onehot = onehot + (lane == idx).astype(F32)
    onehot = jnp.where(valid, onehot, 0.0)

    carry_ref[...] = carry_ref[...] + jnp.sum(onehot, axis=0, keepdims=True)
    cnt_ref[...] = carry_ref[...]

    lane_o = lax.broadcasted_iota(jnp.int32, (T, 128), 1)
    packed = jnp.zeros((T, 128), jnp.int32)
    for k in range(TOP_K):
        packed = jnp.where(lane_o == k, top_i[k], packed)
        packed = jnp.where(lane_o == 2 * TOP_K + k, lax.bitcast_convert_type(gates[k], jnp.int32), packed)
    route_ref[...] = packed


def _outproj_route(og, od, h0, wo_g, wo_d, ln_g, ln_b, rw, rb, rows_per_seq):
    NP, D = h0.shape
    T = ROUTE_TILE

    def rows(width):
        return pl.BlockSpec((T, width), lambda t: (t, 0))

    def full(shape):
        return pl.BlockSpec(shape, lambda t: (0,) * len(shape))

    return pl.pallas_call(
        functools.partial(_route_body, rows_per_seq=rows_per_seq),
        grid=(NP // T,),
        in_specs=[rows(512), rows(512), rows(D), full((512, D)), full((512, D)),
                  full((1, D)), full((1, D)), full((D, N_EXPERTS)), full((1, N_EXPERTS))],
        out_specs=[pl.BlockSpec((T * TOKEN_SUBROWS, 128), lambda t: (t, 0)), rows(128), full((1, N_EXPERTS))],
        out_shape=[jax.ShapeDtypeStruct((NP * TOKEN_SUBROWS, 128), F32),
                   jax.ShapeDtypeStruct((NP, 128), jnp.int32),
                   jax.ShapeDtypeStruct((1, N_EXPERTS), F32)],
        scratch_shapes=[pltpu.VMEM((1, N_EXPERTS), F32)],
        compiler_params=pltpu.CompilerParams(
            dimension_semantics=("arbitrary",), vmem_limit_bytes=VMEM_LIMIT),
        name="outproj_route",
    )(og, od, h0, wo_g, wo_d, ln_g, ln_b, rw, rb)


def _expert_body(order_ref, p0_ref, bstart_ref, nblk_ref, ntot_ref,
                 h1t_hbm, wgu_ref, bg_ref, bu_ref, wd_ref, bd_ref, yk_hbm,
                 wg_s, wu_s, wd_s, xb0, xb1, xb2, yb0, yb1, yb2, zbuf, gsem, ssem, zsem,
                 *, n_tokens, rows_per_seq):
    e = pl.program_id(0)
    n_total = ntot_ref[0]
    R = MOE_ROWS
    TS = TOKEN_SUBROWS
    xb = (xb0, xb1, xb2)
    yb = (yb0, yb1, yb2)
    NB = len(xb)

    def gather_start(b, slot):
        p0 = p0_ref[b + 1]
        for r in range(R):
            tok = order_ref[p0 + r] >> 2
            pltpu.make_async_copy(
                h1t_hbm.at[pl.ds(pl.multiple_of(tok * TS, TS), TS), :],
                xb[slot].at[pl.ds(r * TS, TS), :], gsem.at[slot]).start(priority=r % 2)

    def gather_wait(slot):
        pltpu.make_async_copy(h1t_hbm.at[pl.ds(0, R * TS), :], xb[slot], gsem.at[slot]).wait()

    def scatter_start(b, slot):
        p0 = p0_ref[b + 1]
        for r in range(R):
            row = order_ref[p0 + r] * TS
            pltpu.make_async_copy(
                yb[slot].at[pl.ds(r * TS, TS), :],
                yk_hbm.at[pl.ds(pl.multiple_of(row, TS), TS), :], ssem.at[slot]).start(priority=r % 2)

    def scatter_wait(slot):
        pltpu.make_async_copy(yb[slot], yk_hbm.at[pl.ds(0, R * TS), :], ssem.at[slot]).wait()

    @pl.when(e == 0)
    def _():
        zbuf[...] = jnp.zeros_like(zbuf)
        yb2[...] = jnp.zeros_like(yb2)
        lead = FRONT_PAD * TS
        fills = [((s * rows_per_seq * TOP_K + k * FRONT_PAD) * TS, lead)
                 for s in range(n_tokens // rows_per_seq) for k in range(TOP_K)]
        copies = [pltpu.make_async_copy(zbuf.at[pl.ds(0, n), :], yk_hbm.at[pl.ds(o, n), :], zsem)
                  for o, n in fills]
        for cp in copies:
            cp.start()
        for cp in copies:
            cp.wait()
        gather_start(0, 0)
        gather_start(jnp.minimum(1, n_total - 1), 1)

    @pl.when(nblk_ref[e] > 0)
    def _():
        r = lax.broadcasted_iota(jnp.int32, (256, 256), 0)
        c = lax.broadcasted_iota(jnp.int32, (256, 256), 1)
        perm = (r == jnp.where(c < 128, 2 * c, 2 * (c - 128) + 1)).astype(BF16)
        for tt in range(2 * D_FF // 256):
            wt = wgu_ref[0, :, tt * 256:(tt + 1) * 256].astype(BF16)
            sp = jnp.dot(wt, perm, preferred_element_type=F32)
            wg_s[:, tt * 128:(tt + 1) * 128] = sp[:, :128].astype(BF16)
            wu_s[:, tt * 128:(tt + 1) * 128] = sp[:, 128:].astype(BF16)
        wd_s[...] = wd_ref[0].astype(BF16)

    def run_block(b, slot):
        nxt, prv = (slot + 1) % NB, (slot + 2) % NB

        @pl.when(b >= 1)
        def _():
            scatter_wait(nxt)

        gather_wait(slot)
        gather_start(jnp.minimum(b + 2, n_total - 1), prv)
        scatter_start(b - 1, prv)
        x = _load_token_tiled(xb[slot], R).astype(BF16)
        gt = jnp.dot(x, wg_s[...], preferred_element_type=F32) + bg_ref[0]
        up = jnp.dot(x, wu_s[...], preferred_element_type=F32) + bu_ref[0]
        gt = jnp.minimum(gt, SWIGLU_LIMIT)
        up = jnp.clip(up, -SWIGLU_LIMIT, SWIGLU_LIMIT)
        act = (up + 1.0) * (gt * jax.nn.sigmoid(SWIGLU_ALPHA * gt))
        y = jnp.dot(act.astype(BF16), wd_s[...], preferred_element_type=F32) + bd_ref[0]
        _store_token_tiled(yb[slot], y)

    def block(b, carry):
        for slot in range(NB):
            pl.when(b % NB == slot)(functools.partial(run_block, b, slot))
        return carry

    b0 = bstart_ref[e]
    lax.fori_loop(b0, b0 + nblk_ref[e], block, 0)

    @pl.when(e == pl.num_programs(0) - 1)
    def _():
        last = n_total - 1
        for slot in range(NB):
            @pl.when(last % NB == slot)
            def _(slot=slot):
                nxt, prv = (slot + 1) % NB, (slot + 2) % NB
                scatter_start(last, slot)
                gather_wait(nxt)
                gather_wait(prv)
                scatter_wait(prv)
                scatter_wait(slot)


def _experts(order, blk_p0, blk_start, n_blk, n_total, h1t, w_gate_up, b_gate, b_up, w_down,
             b_down, n_tokens, rows_per_seq):
    D = D_MODEL
    R = MOE_ROWS
    buf = pltpu.VMEM((R * TOKEN_SUBROWS, 128), F32)
    per_expert = lambda e, *_: (e, 0, 0)
    grid_spec = pltpu.PrefetchScalarGridSpec(
        num_scalar_prefetch=5,
        grid=(N_EXPERTS,),
        in_specs=[
            pl.BlockSpec(memory_space=pl.ANY),
            pl.BlockSpec((1, D, 2 * D_FF), per_expert),
            pl.BlockSpec((1, 1, D_FF), per_expert),
            pl.BlockSpec((1, 1, D_FF), per_expert),
            pl.BlockSpec((1, D_FF, D), per_expert),
            pl.BlockSpec((1, 1, D), per_expert),
        ],
        out_specs=pl.BlockSpec(memory_space=pl.ANY),
        scratch_shapes=[pltpu.VMEM((D, D_FF), BF16), pltpu.VMEM((D, D_FF), BF16), pltpu.VMEM((D_FF, D), BF16),
                        buf, buf, buf, buf, buf, buf, pltpu.VMEM((FRONT_PAD * TOKEN_SUBROWS, 128), F32),
                        pltpu.SemaphoreType.DMA((3,)), pltpu.SemaphoreType.DMA((3,)), pltpu.SemaphoreType.DMA],
    )
    out_rows = TOP_K * n_tokens * TOKEN_SUBROWS
    return pl.pallas_call(
        functools.partial(_expert_body, n_tokens=n_tokens, rows_per_seq=rows_per_seq),
        grid_spec=grid_spec,
        out_shape=jax.ShapeDtypeStruct((out_rows, 128), F32),
        compiler_params=pltpu.CompilerParams(
            dimension_semantics=("arbitrary",), vmem_limit_bytes=EXPERT_VMEM_LIMIT),
        name="experts",
    )(order, blk_p0, blk_start, n_blk, n_total, h1t, w_gate_up, b_gate, b_up, w_down, b_down)


def _combine_body(h1_ref, y0_ref, y1_ref, y2_ref, y3_ref, route_ref, g_ref, b_ref, o_ref):
    route = route_ref[...]
    y = jnp.zeros((ROW_BLOCK, D_MODEL), F32)
    for k, yk_ref in enumerate((y0_ref, y1_ref, y2_ref, y3_ref)):
        gate = lax.bitcast_convert_type(route[:, 2 * TOP_K + k:2 * TOP_K + k + 1], F32)
        y = y + gate * _load_token_tiled(yk_ref.reshape(ROW_BLOCK * TOKEN_SUBROWS, 128), ROW_BLOCK)
    h1 = _load_token_tiled(h1_ref, ROW_BLOCK)
    o_ref[0] = _layer_norm(DEEPNORM_ALPHA * h1 + y, g_ref[...], b_ref[...])


def _combine(h1t, yk, route, ln_g, ln_b, batch, seq_len):
    nb = seq_len // ROW_BLOCK
    blocks_per_seq = nb + 1
    tile_rows = ROW_BLOCK * TOKEN_SUBROWS
    yk = yk.reshape(-1, TOP_K, TOKEN_SUBROWS, 128)

    def token_block(b, j):
        return b * blocks_per_seq + j + 1

    def plane(k):
        return pl.BlockSpec((ROW_BLOCK, None, TOKEN_SUBROWS, 128), lambda b, j: (token_block(b, j), k, 0, 0))

    tiles = pl.BlockSpec((tile_rows, 128), lambda b, j: (token_block(b, j), 0))
    vec = pl.BlockSpec((1, D_MODEL), lambda b, j: (0, 0))
    return pl.pallas_call(
        _combine_body,
        grid=(batch, nb),
        in_specs=[tiles, plane(0), plane(1), plane(2), plane(3),
                  pl.BlockSpec((ROW_BLOCK, 128), lambda b, j: (token_block(b, j), 0)), vec, vec],
        out_specs=pl.BlockSpec((1, ROW_BLOCK, D_MODEL), lambda b, j: (b, j, 0)),
        out_shape=jax.ShapeDtypeStruct((batch, seq_len, D_MODEL), F32),
        compiler_params=pltpu.CompilerParams(
            dimension_semantics=("parallel", "parallel"), vmem_limit_bytes=VMEM_LIMIT),
        name="combine",
    )(h1t, yk, yk, yk, yk, route, ln_g, ln_b)


def kernel(x, meta_tokens, ln_emb_g, ln_emb_b, w_in, gla_wa2, gla_ba, gla_norm_g, diff_lambda_q1, diff_lambda_k1, diff_lambda_q2, diff_lambda_k2, diff_norm_g, w_out, ln1_g, ln1_b, router_w, router_b, w_gate_up, b_gate_up, w_down, b_down, ln2_g, ln2_b):
    B, S, D = x.shape
    LP = S + ROW_BLOCK
    NP = B * LP
    row = lambda v: v.reshape(1, -1)

    w = w_in[0]
    w_main = jnp.concatenate([w[:, :1536], w[:, 1552:]], axis=1).astype(BF16)
    w_ga = w[:, 1536:1552].astype(BF16)
    meta_pad = jnp.pad(meta_tokens, ((FRONT_PAD, 0), (0, 0)))

    h0, gq, gk, gv, gr, la, dq, dk, dv = _inproj(
        x, meta_pad, row(ln_emb_g), row(ln_emb_b), w_main, w_ga,
        gla_wa2[0].astype(BF16), row(gla_ba[0]))

    og = _gla(gq, gk, gv, gr, la, row(gla_norm_g[0]))
    od = _diff_attention(dq, dk, dv, row(diff_lambda_q1[0]), row(diff_lambda_k1[0]),
                         row(diff_lambda_q2[0]), row(diff_lambda_k2[0]), row(diff_norm_g[0]))

    wo = w_out[0].astype(BF16)
    h1t, route, counts = _outproj_route(
        og.reshape(NP, 512), od.reshape(NP, 512), h0.reshape(NP, D), wo[:512], wo[512:],
        row(ln1_g[0]), row(ln1_b[0]), router_w[0], row(router_b[0]), LP)

    R = MOE_ROWS
    i32 = jnp.int32
    counts = counts[0].astype(i32)
    n_blk = (counts + R - 1) // R
    blk_end = jnp.cumsum(n_blk)
    blk_start = blk_end - n_blk
    grp_start = jnp.cumsum(counts) - counts
    n_total = blk_end[-1:]
    n_assign_max = B * (S + N_META) * TOP_K
    max_blocks = (n_assign_max + N_EXPERTS * (R - 1)) // R
    g = jnp.minimum(jnp.arange(max_blocks, dtype=i32), n_total[0] - 1)
    is_e = (jnp.minimum(jnp.sum(g[:, None] >= blk_end[None, :], axis=1), N_EXPERTS - 1)[:, None]
            == jnp.arange(N_EXPERTS)[None, :])
    pick = lambda v: jnp.sum(jnp.where(is_e, v[None, :], 0), axis=1)
    local = (g - pick(blk_start)) * R
    blk_p0 = jnp.concatenate([jnp.full((1,), NP * TOP_K, i32), (pick(grp_start) + local).astype(i32)])

    tok_valid = (jnp.arange(NP, dtype=i32) % LP) >= FRONT_PAD
    flat_e = jnp.where(tok_valid[:, None], route[:, :TOP_K], N_EXPERTS).reshape(-1)
    id_bits = (NP * TOP_K - 1).bit_length()
    keyed = jnp.sort(flat_e * (1 << id_bits) + jnp.arange(NP * TOP_K, dtype=i32))
    assert R <= FRONT_PAD * TOP_K
    order = jnp.concatenate([keyed & ((1 << id_bits) - 1), jnp.arange(R, dtype=i32)])

    bgu = b_gate_up[0].reshape(N_EXPERTS, D_FF, 2)
    yk = _experts(order, blk_p0, blk_start.astype(i32), n_blk.astype(i32), n_total.astype(i32),
                  h1t, w_gate_up[0],
                  bgu[:, :, 0].reshape(N_EXPERTS, 1, D_FF), bgu[:, :, 1].reshape(N_EXPERTS, 1, D_FF),
                  w_down[0], b_down[0].reshape(N_EXPERTS, 1, D), NP, LP)

    return _combine(h1t, yk, route, row(ln2_g[0]), row(ln2_b[0]), B, S)
```

```python
import functools
import math

import jax
import jax.numpy as jnp
from jax import lax
from jax.experimental import pallas as pl
from jax.experimental.pallas import tpu as pltpu

F32 = jnp.float32
BF16 = jnp.bfloat16

D_MODEL = 1024
N_META = 16
ROW_BLOCK = 128
FRONT_PAD = ROW_BLOCK - N_META
CHUNK = 64

GLA_HEADS = 4
GLA_DK = 64
GLA_DV = 128
GLA_RANK = 16
GLA_TAU = 16.0
DIFF_HEADS = 4
DIFF_DH = 64
DIFF_DV = 128

N_EXPERTS = 32
TOP_K = 4
D_FF = 1024
SWIGLU_LIMIT = 7.0
SWIGLU_ALPHA = 1.702
MOE_ROWS = 256

DEEPNORM_ALPHA = 2.0 ** 0.25
LAMBDA_INIT = 0.8 - 0.6 * math.exp(0.0)
LN_EPS = 1e-5
NEG_INF = -1e30
LOG2E = math.log2(math.e)

ROUTE_TILE = 512
VMEM_LIMIT = 48 * 1024 * 1024
EXPERT_VMEM_LIMIT = 56 * 1024 * 1024

_GQ, _GK, _GV, _GR, _DQ, _DK, _DV = 0, 256, 512, 1024, 1536, 2048, 2560
_MAIN_WIDTH = 3072


def _layer_norm(x, g, b):
    mu = jnp.mean(x, axis=-1, keepdims=True)
    xc = x - mu
    var = jnp.mean(xc * xc, axis=-1, keepdims=True)
    return xc * lax.rsqrt(var + LN_EPS) * g + b


def _split3_bf16(x):
    hi = x.astype(BF16)
    r1 = x - hi.astype(F32)
    mid = r1.astype(BF16)
    lo = (r1 - mid.astype(F32)).astype(BF16)
    return hi, mid, lo


TOKEN_SUBROWS = D_MODEL // 128


def _store_token_tiled(ref, val, first_token=0):
    n = val.shape[0]
    for s in range(TOKEN_SUBROWS):
        ref[pl.ds(first_token * TOKEN_SUBROWS + s, n, stride=TOKEN_SUBROWS), :] = val[:, s * 128:(s + 1) * 128]


def _load_token_tiled(ref, n, first_row=0):
    return jnp.concatenate(
        [ref[pl.ds(first_row + s, n, stride=TOKEN_SUBROWS), :] for s in range(TOKEN_SUBROWS)], axis=1)


def _inproj_body(x_ref, meta_ref, g_ref, b_ref, w_ref, wga_ref, wa2_ref, ba_ref,
                 h0_ref, gq_ref, gk_ref, gv_ref, gr_ref, la_ref, dq_ref, dk_ref, dv_ref, hs_ref):
    j = pl.program_id(1)
    rows = lax.broadcasted_iota(jnp.int32, (ROW_BLOCK, 1), 0)
    valid = jnp.logical_or(j > 0, rows >= FRONT_PAD)

    @pl.when(j == 0)
    def _():
        lead = _layer_norm(meta_ref[...], g_ref[...], b_ref[...])
        hs_ref[0] = jnp.where(rows >= FRONT_PAD, lead, 0.0)

    h = hs_ref[j % 2]
    hs_ref[(j + 1) % 2] = _layer_norm(x_ref[0], g_ref[...], b_ref[...])
    h0_ref[0] = h
    hb = h.astype(BF16)

    def proj(off, width):
        return jnp.dot(hb, w_ref[:, off:off + width], preferred_element_type=F32)

    gq_ref[0] = (proj(_GQ, 256) * GLA_DK ** -0.5).astype(BF16)
    gk_ref[0] = proj(_GK, 256)
    gv_ref[0] = proj(_GV, 512).astype(BF16)
    gr_ref[0] = proj(_GR, 512).astype(BF16)
    dq_ref[0] = (proj(_DQ, 512) * (DIFF_DH ** -0.5 * LOG2E)).astype(BF16)
    dk_ref[0] = proj(_DK, 512).astype(BF16)
    dv_ref[0] = proj(_DV, 512).astype(BF16)

    a_lr = jnp.dot(hb, wga_ref[...], preferred_element_type=F32)
    z = jnp.dot(a_lr.astype(BF16), wa2_ref[...], preferred_element_type=F32) + ba_ref[...]
    log_sig = jnp.minimum(z, 0.0) - jnp.log1p(jnp.exp(-jnp.abs(z)))
    la_ref[0] = jnp.where(valid, log_sig / GLA_TAU, 0.0)


def _inproj(x, meta_pad, ln_g, ln_b, w_main, w_ga, wa2, ba):
    B, S, D = x.shape
    nb = S // ROW_BLOCK + 1
    LP = nb * ROW_BLOCK

    def row_spec(width):
        return pl.BlockSpec((1, ROW_BLOCK, width), lambda b, j: (b, j, 0))

    def full(shape):
        return pl.BlockSpec(shape, lambda b, j: (0,) * len(shape))

    out_shapes = [
        jax.ShapeDtypeStruct((B, LP, D), F32),
        jax.ShapeDtypeStruct((B, LP, 256), BF16),
        jax.ShapeDtypeStruct((B, LP, 256), F32),
        jax.ShapeDtypeStruct((B, LP, 512), BF16),
        jax.ShapeDtypeStruct((B, LP, 512), BF16),
        jax.ShapeDtypeStruct((B, LP, 256), F32),
        jax.ShapeDtypeStruct((B, LP, 512), BF16),
        jax.ShapeDtypeStruct((B, LP, 512), BF16),
        jax.ShapeDtypeStruct((B, LP, 512), BF16),
    ]
    return pl.pallas_call(
        _inproj_body,
        grid=(B, nb),
        in_specs=[
            pl.BlockSpec((1, ROW_BLOCK, D), lambda b, j: (b, jnp.minimum(j, nb - 2), 0)),
            full((ROW_BLOCK, D)), full((1, D)), full((1, D)),
            full((D, _MAIN_WIDTH)), full((D, GLA_RANK)), full((GLA_RANK, 256)), full((1, 256)),
        ],
        out_specs=[row_spec(s.shape[-1]) for s in out_shapes],
        out_shape=out_shapes,
        scratch_shapes=[pltpu.VMEM((2, ROW_BLOCK, D), F32)],
        compiler_params=pltpu.CompilerParams(
            dimension_semantics=("parallel", "arbitrary"), vmem_limit_bytes=VMEM_LIMIT),
        name="inproj",
    )(x, meta_pad, ln_g, ln_b, w_main, w_ga, wa2, ba)


def _gla_body(q_ref, k_ref, v_ref, r_ref, la_ref, g_ref, o_ref):
    n_groups = q_ref.shape[1] // ROW_BLOCK
    ri = lax.broadcasted_iota(jnp.int32, (ROW_BLOCK, ROW_BLOCK), 0)
    ci = lax.broadcasted_iota(jnp.int32, (ROW_BLOCK, ROW_BLOCK), 1)
    later = jnp.logical_and(ri // CHUNK == ci // CHUNK, ci > ri).astype(BF16)
    sr = lax.broadcasted_iota(jnp.int32, (GLA_HEADS * GLA_DV, GLA_HEADS * GLA_DK), 0)
    sc = lax.broadcasted_iota(jnp.int32, (GLA_HEADS * GLA_DV, GLA_HEADS * GLA_DK), 1)
    same_head = sr // GLA_DV == sc // GLA_DK
    gain = g_ref[...]
    st = jnp.zeros((GLA_HEADS * GLA_DV, GLA_HEADS * GLA_DK), F32)

    for grp in range(n_groups):
        g0 = grp * ROW_BLOCK
        la = la_ref[0, g0:g0 + ROW_BLOCK, :]
        hi, mid, lo = _split3_bf16(la)
        suffix = (jnp.dot(later, hi, preferred_element_type=F32)
                  + jnp.dot(later, mid, preferred_element_type=F32)
                  + jnp.dot(later, lo, preferred_element_type=F32))
        kdec = (k_ref[0, g0:g0 + ROW_BLOCK, :] * jnp.exp(suffix)).astype(BF16)
        for half in range(ROW_BLOCK // CHUNK):
            f = half * CHUNK
            r0 = g0 + f
            tot = suffix[f:f + 1, :] + la[f:f + 1, :]
            upd = lax.dot_general(v_ref[0, r0:r0 + CHUNK, :], kdec[f:f + CHUNK, :],
                                  (((0,), (0,)), ((), ())), preferred_element_type=F32)
            st = st * jnp.exp(tot) + jnp.where(same_head, upd, 0.0)
            o = lax.dot_general(q_ref[0, r0:r0 + CHUNK, :], st.astype(BF16),
                                (((1,), (1,)), ((), ())), preferred_element_type=F32)
            r = r_ref[0, r0:r0 + CHUNK, :].astype(F32)
            for h in range(GLA_HEADS):
                oh = o[:, h * GLA_DV:(h + 1) * GLA_DV]
                rh = r[:, h * GLA_DV:(h + 1) * GLA_DV]
                ms = jnp.mean(oh * oh, axis=-1, keepdims=True)
                out = oh * lax.rsqrt(ms + LN_EPS) * gain * (rh * jax.nn.sigmoid(rh))
                o_ref[0, r0:r0 + CHUNK, h * GLA_DV:(h + 1) * GLA_DV] = out.astype(BF16)


def _gla(gq, gk, gv, gr, la, norm_g):
    B, LP, _ = gq.shape

    def seq(width):
        return pl.BlockSpec((1, LP, width), lambda b: (b, 0, 0))

    return pl.pallas_call(
        _gla_body,
        grid=(B,),
        in_specs=[seq(256), seq(256), seq(512), seq(512), seq(256),
                  pl.BlockSpec((1, GLA_DV), lambda b: (0, 0))],
        out_specs=seq(512),
        out_shape=jax.ShapeDtypeStruct((B, LP, 512), BF16),
        compiler_params=pltpu.CompilerParams(
            dimension_semantics=("parallel",), vmem_limit_bytes=VMEM_LIMIT),
        name="gla",
    )(gq, gk, gv, gr, la, norm_g)


ATT_BLOCK = 256


def _diff_body(q_ref, k_ref, v_ref, lq1_ref, lk1_ref, lq2_ref, lk2_ref, g_ref, o_ref,
               dmask_ref):
    h = pl.program_id(1)
    slope = jnp.where(h == 0, 2.0 ** -2, jnp.where(h == 1, 2.0 ** -4, jnp.where(h == 2, 2.0 ** -6, 2.0 ** -8)))
    slope = slope.astype(F32) * LOG2E
    lam = (jnp.exp(jnp.sum(lq1_ref[...] * lk1_ref[...], axis=-1, keepdims=True))
           - jnp.exp(jnp.sum(lq2_ref[...] * lk2_ref[...], axis=-1, keepdims=True)) + LAMBDA_INIT)
    gain = g_ref[...] * (1.0 - LAMBDA_INIT)
    n_qblocks = (q_ref.shape[1] - ROW_BLOCK) // ATT_BLOCK
    nt = (((1,), (1,)), ((), ()))

    def split_q(q):
        lane = lax.broadcasted_iota(jnp.int32, q.shape, 1)
        zero = jnp.zeros_like(q)
        return jnp.where(lane < DIFF_DH, q, zero), jnp.where(lane >= DIFF_DH, q, zero)

    def with_ones(v):
        return jnp.concatenate([v, jnp.ones_like(v)], axis=1)

    def softmax_av(qz, kk, vext, add_bias):
        s = add_bias(lax.dot_general(qz, kk, nt, preferred_element_type=F32))
        m = jnp.max(s, axis=-1, keepdims=True)
        p = jnp.exp2(s - m).astype(BF16)
        return jnp.dot(p, vext, preferred_element_type=F32)

    def finish(a1, a2):
        o = a1[:, :DIFF_DV] / a1[:, DIFF_DV:] - lam * (a2[:, :DIFF_DV] / a2[:, DIFF_DV:])
        ms = jnp.mean(o * o, axis=-1, keepdims=True)
        return (o * lax.rsqrt(ms + LN_EPS) * gain).astype(BF16)

    r = lax.broadcasted_iota(jnp.int32, (ROW_BLOCK, ROW_BLOCK), 0)
    c = lax.broadcasted_iota(jnp.int32, (ROW_BLOCK, ROW_BLOCK), 1)
    ok = jnp.logical_and(c // CHUNK <= r // CHUNK, c >= FRONT_PAD)
    bias_lead = jnp.where(ok, -slope * jnp.abs(r - c).astype(F32), NEG_INF)
    q1z, q2z = split_q(q_ref[0, 0:ROW_BLOCK, :])
    k_lead = k_ref[0, 0:ROW_BLOCK, :]
    v_lead = with_ones(v_ref[0, 0:ROW_BLOCK, :])
    o_ref[0, 0:ROW_BLOCK, :] = finish(softmax_av(q1z, k_lead, v_lead, lambda s: s + bias_lead),
                                      softmax_av(q2z, k_lead, v_lead, lambda s: s + bias_lead))

    r = lax.broadcasted_iota(jnp.int32, (ATT_BLOCK, ATT_BLOCK), 0)
    c = lax.broadcasted_iota(jnp.int32, (ATT_BLOCK, ATT_BLOCK), 1)
    rel = jnp.where(c <= r, c, 2 * r - c).astype(F32)
    dmask_ref[...] = jnp.where(c // CHUNK <= r // CHUNK, slope * rel, NEG_INF)

    for jq in range(n_qblocks):
        qbase = ROW_BLOCK + jq * ATT_BLOCK
        n_keys = qbase + ATT_BLOCK
        q1z, q2z = split_q(q_ref[0, qbase:qbase + ATT_BLOCK, :])
        kk = k_ref[0, 0:n_keys, :]
        vext = with_ones(v_ref[0, 0:n_keys, :])
        col = lax.broadcasted_iota(jnp.int32, (1, qbase), 1)
        col_bias = jnp.where(col >= FRONT_PAD, slope * (col - qbase).astype(F32), NEG_INF)

        def add_bias(s, col_bias=col_bias, qbase=qbase):
            return jnp.concatenate([s[:, :qbase] + col_bias, s[:, qbase:] + dmask_ref[...]], axis=1)

        o_ref[0, qbase:qbase + ATT_BLOCK, :] = finish(softmax_av(q1z, kk, vext, add_bias),
                                                      softmax_av(q2z, kk, vext, add_bias))


def _diff_attention(dq, dk, dv, lq1, lk1, lq2, lk2, norm_g):
    B, LP, W = dq.shape
    small = pl.BlockSpec((1, DIFF_DH), lambda b, h: (0, 0))
    seq = pl.BlockSpec((1, LP, 2 * DIFF_DH), lambda b, h: (b, 0, h))
    return pl.pallas_call(
        _diff_body,
        grid=(B, DIFF_HEADS),
        in_specs=[seq, seq, seq, small, small, small, small,
                  pl.BlockSpec((1, DIFF_DV), lambda b, h: (0, 0))],
        out_specs=seq,
        out_shape=jax.ShapeDtypeStruct((B, LP, W), BF16),
        scratch_shapes=[pltpu.VMEM((ATT_BLOCK, ATT_BLOCK), F32)],
        compiler_params=pltpu.CompilerParams(
            dimension_semantics=("parallel", "parallel"), vmem_limit_bytes=VMEM_LIMIT),
        name="diff_attn",
    )(dq, dk, dv, lq1, lk1, lq2, lk2, norm_g)


def _route_body(og_ref, od_ref, h0_ref, wog_ref, wod_ref, g_ref, b_ref, rw_ref, rb_ref,
                h1_ref, route_ref, cnt_ref, carry_ref, *, rows_per_seq):
    t = pl.program_id(0)

    @pl.when(t == 0)
    def _():
        carry_ref[...] = jnp.zeros_like(carry_ref)

    mix = (jnp.dot(og_ref[...], wog_ref[...], preferred_element_type=F32)
           + jnp.dot(od_ref[...], wod_ref[...], preferred_element_type=F32))
    h1 = _layer_norm(DEEPNORM_ALPHA * h0_ref[...] + mix, g_ref[...], b_ref[...])
    _store_token_tiled(h1_ref, h1)

    hh, hm, _ = _split3_bf16(h1)
    wh, wm, _ = _split3_bf16(rw_ref[...])
    head = jnp.dot(hh, jnp.concatenate([wh, wm], axis=1), preferred_element_type=F32)
    logits = (rb_ref[...] + head[:, :N_EXPERTS] + head[:, N_EXPERTS:]
              + jnp.dot(hm, wh, preferred_element_type=F32))

    T = logits.shape[0]
    lane = lax.broadcasted_iota(jnp.int32, (T, N_EXPERTS), 1)
    grow = t * T + lax.broadcasted_iota(jnp.int32, (T, 1), 0)
    valid = (grow % rows_per_seq) >= FRONT_PAD

    work = logits
    top_v, top_i = [], []
    for _ in range(TOP_K):
        mx = jnp.max(work, axis=-1, keepdims=True)
        idx = jnp.min(jnp.where(work == mx, lane, N_EXPERTS), axis=-1, keepdims=True)
        top_v.append(mx)
        top_i.append(idx)
        work = jnp.where(lane == idx, -jnp.inf, work)
    ex = [jnp.exp(v - top_v[0]) for v in top_v]
    den = ex[0] + ex[1] + ex[2] + ex[3]
    gates = [e / den for e in ex]

    onehot = jnp.zeros((T, N_EXPERTS), F32)
    for idx in top_i:
        onehot = onehot + (lane == idx).astype(F32)
    onehot = jnp.where(valid, onehot, 0.0)

    carry_ref[...] = carry_ref[...] + jnp.sum(onehot, axis=0, keepdims=True)
    cnt_ref[...] = carry_ref[...]

    lane_o = lax.broadcasted_iota(jnp.int32, (T, 128), 1)
    packed = jnp.zeros((T, 128), jnp.int32)
    for k in range(TOP_K):
        packed = jnp.where(lane_o == k, top_i[k], packed)
        packed = jnp.where(lane_o == 2 * TOP_K + k, lax.bitcast_convert_type(gates[k], jnp.int32), packed)
    route_ref[...] = packed


def _outproj_route(og, od, h0, wo_g, wo_d, ln_g, ln_b, rw, rb, rows_per_seq):
    NP, D = h0.shape
    T = ROUTE_TILE

    def rows(width):
        return pl.BlockSpec((T, width), lambda t: (t, 0))

    def full(shape):
        return pl.BlockSpec(shape, lambda t: (0,) * len(shape))

    return pl.pallas_call(
        functools.partial(_route_body, rows_per_seq=rows_per_seq),
        grid=(NP // T,),
        in_specs=[rows(512), rows(512), rows(D), full((512, D)), full((512, D)),
                  full((1, D)), full((1, D)), full((D, N_EXPERTS)), full((1, N_EXPERTS))],
        out_specs=[pl.BlockSpec((T * TOKEN_SUBROWS, 128), lambda t: (t, 0)), rows(128), full((1, N_EXPERTS))],
        out_shape=[jax.ShapeDtypeStruct((NP * TOKEN_SUBROWS, 128), F32),
                   jax.ShapeDtypeStruct((NP, 128), jnp.int32),
                   jax.ShapeDtypeStruct((1, N_EXPERTS), F32)],
        scratch_shapes=[pltpu.VMEM((1, N_EXPERTS), F32)],
        compiler_params=pltpu.CompilerParams(
            dimension_semantics=("arbitrary",), vmem_limit_bytes=VMEM_LIMIT),
        name="outproj_route",
    )(og, od, h0, wo_g, wo_d, ln_g, ln_b, rw, rb)


def _expert_body(order_ref, p0_ref, bstart_ref, nblk_ref, ntot_ref,
                 h1t_hbm, wgu_ref, bg_ref, bu_ref, wd_ref, bd_ref, yk_hbm,
                 wg_s, wu_s, wd_s, xb0, xb1, xb2, yb0, yb1, yb2, zbuf, gsem, ssem, zsem,
                 *, n_tokens, rows_per_seq):
    e = pl.program_id(0)
    n_total = ntot_ref[0]
    R = MOE_ROWS
    TS = TOKEN_SUBROWS
    xb = (xb0, xb1, xb2)
    yb = (yb0, yb1, yb2)
    NB = len(xb)

    def gather_start(b, slot):
        p0 = p0_ref[b + 1]
        for r in range(R):
            tok = order_ref[p0 + r] >> 2
            pltpu.make_async_copy(
                h1t_hbm.at[pl.ds(pl.multiple_of(tok * TS, TS), TS), :],
                xb[slot].at[pl.ds(r * TS, TS), :], gsem.at[slot]).start(priority=r % 2)

    def gather_wait(slot):
        pltpu.make_async_copy(h1t_hbm.at[pl.ds(0, R * TS), :], xb[slot], gsem.at[slot]).wait()

    def scatter_start(b, slot):
        p0 = p0_ref[b + 1]
        for r in range(R):
            row = order_ref[p0 + r] * TS
            pltpu.make_async_copy(
                yb[slot].at[pl.ds(r * TS, TS), :],
                yk_hbm.at[pl.ds(pl.multiple_of(row, TS), TS), :], ssem.at[slot]).start(priority=r % 2)

    def scatter_wait(slot):
        pltpu.make_async_copy(yb[slot], yk_hbm.at[pl.ds(0, R * TS), :], ssem.at[slot]).wait()

    @pl.when(e == 0)
    def _():
        zbuf[...] = jnp.zeros_like(zbuf)
        yb2[...] = jnp.zeros_like(yb2)
        lead = FRONT_PAD * TS
        fills = [((s * rows_per_seq * TOP_K + k * FRONT_PAD) * TS, lead)
                 for s in range(n_tokens // rows_per_seq) for k in range(TOP_K)]
        copies = [pltpu.make_async_copy(zbuf.at[pl.ds(0, n), :], yk_hbm.at[pl.ds(o, n), :], zsem)
                  for o, n in fills]
        for cp in copies:
            cp.start()
        for cp in copies:
            cp.wait()
        gather_start(0, 0)
        gather_start(jnp.minimum(1, n_total - 1), 1)

    @pl.when(nblk_ref[e] > 0)
    def _():
        r = lax.broadcasted_iota(jnp.int32, (256, 256), 0)
        c = lax.broadcasted_iota(jnp.int32, (256, 256), 1)
        perm = (r == jnp.where(c < 128, 2 * c, 2 * (c - 128) + 1)).astype(BF16)
        for tt in range(2 * D_FF // 256):
            wt = wgu_ref[0, :, tt * 256:(tt + 1) * 256].astype(BF16)
            sp = jnp.dot(wt, perm, preferred_element_type=F32)
            wg_s[:, tt * 128:(tt + 1) * 128] = sp[:, :128].astype(BF16)
            wu_s[:, tt * 128:(tt + 1) * 128] = sp[:, 128:].astype(BF16)
        wd_s[...] = wd_ref[0].astype(BF16)

    def run_block(b, slot):
        nxt, prv = (slot + 1) % NB, (slot + 2) % NB

        @pl.when(b >= 1)
        def _():
            scatter_wait(nxt)

        gather_wait(slot)
        gather_start(jnp.minimum(b + 2, n_total - 1), prv)
        scatter_start(b - 1, prv)
        x = _load_token_tiled(xb[slot], R).astype(BF16)
        gt = jnp.dot(x, wg_s[...], preferred_element_type=F32) + bg_ref[0]
        up = jnp.dot(x, wu_s[...], preferred_element_type=F32) + bu_ref[0]
        gt = jnp.minimum(gt, SWIGLU_LIMIT)
        up = jnp.clip(up, -SWIGLU_LIMIT, SWIGLU_LIMIT)
        act = (up + 1.0) * (gt * jax.nn.sigmoid(SWIGLU_ALPHA * gt))
        y = jnp.dot(act.astype(BF16), wd_s[...], preferred_element_type=F32) + bd_ref[0]
        _store_token_tiled(yb[slot], y)

    def block(b, carry):
        for slot in range(NB):
            pl.when(b % NB == slot)(functools.partial(run_block, b, slot))
        return carry

    b0 = bstart_ref[e]
    lax.fori_loop(b0, b0 + nblk_ref[e], block, 0)

    @pl.when(e == pl.num_programs(0) - 1)
    def _():
        last = n_total - 1
        for slot in range(NB):
            @pl.when(last % NB == slot)
            def _(slot=slot):
                nxt, prv = (slot + 1) % NB, (slot + 2) % NB
                scatter_start(last, slot)
                gather_wait(nxt)
                gather_wait(prv)
                scatter_wait(prv)
                scatter_wait(slot)


def _experts(order, blk_p0, blk_start, n_blk, n_total, h1t, w_gate_up, b_gate, b_up, w_down,
             b_down, n_tokens, rows_per_seq):
    D = D_MODEL
    R = MOE_ROWS
    buf = pltpu.VMEM((R * TOKEN_SUBROWS, 128), F32)
    per_expert = lambda e, *_: (e, 0, 0)
    grid_spec = pltpu.PrefetchScalarGridSpec(
        num_scalar_prefetch=5,
        grid=(N_EXPERTS,),
        in_specs=[
            pl.BlockSpec(memory_space=pl.ANY),
            pl.BlockSpec((1, D, 2 * D_FF), per_expert),
            pl.BlockSpec((1, 1, D_FF), per_expert),
            pl.BlockSpec((1, 1, D_FF), per_expert),
            pl.BlockSpec((1, D_FF, D), per_expert),
            pl.BlockSpec((1, 1, D), per_expert),
        ],
        out_specs=pl.BlockSpec(memory_space=pl.ANY),
        scratch_shapes=[pltpu.VMEM((D, D_FF), BF16), pltpu.VMEM((D, D_FF), BF16), pltpu.VMEM((D_FF, D), BF16),
                        buf, buf, buf, buf, buf, buf, pltpu.VMEM((FRONT_PAD * TOKEN_SUBROWS, 128), F32),
                        pltpu.SemaphoreType.DMA((3,)), pltpu.SemaphoreType.DMA((3,)), pltpu.SemaphoreType.DMA],
    )
    out_rows = TOP_K * n_tokens * TOKEN_SUBROWS
    return pl.pallas_call(
        functools.partial(_expert_body, n_tokens=n_tokens, rows_per_seq=rows_per_seq),
        grid_spec=grid_spec,
        out_shape=jax.ShapeDtypeStruct((out_rows, 128), F32),
        compiler_params=pltpu.CompilerParams(
            dimension_semantics=("arbitrary",), vmem_limit_bytes=EXPERT_VMEM_LIMIT),
        name="experts",
    )(order, blk_p0, blk_start, n_blk, n_total, h1t, w_gate_up, b_gate, b_up, w_down, b_down)


def _combine_body(h1_ref, y0_ref, y1_ref, y2_ref, y3_ref, route_ref, g_ref, b_ref, o_ref):
    route = route_ref[...]
    y = jnp.zeros((ROW_BLOCK, D_MODEL), F32)
    for k, yk_ref in enumerate((y0_ref, y1_ref, y2_ref, y3_ref)):
        gate = lax.bitcast_convert_type(route[:, 2 * TOP_K + k:2 * TOP_K + k + 1], F32)
        y = y + gate * _load_token_tiled(yk_ref.reshape(ROW_BLOCK * TOKEN_SUBROWS, 128), ROW_BLOCK)
    h1 = _load_token_tiled(h1_ref, ROW_BLOCK)
    o_ref[0] = _layer_norm(DEEPNORM_ALPHA * h1 + y, g_ref[...], b_ref[...])


def _combine(h1t, yk, route, ln_g, ln_b, batch, seq_len):
    nb = seq_len // ROW_BLOCK
    blocks_per_seq = nb + 1
    tile_rows = ROW_BLOCK * TOKEN_SUBROWS
    yk = yk.reshape(-1, TOP_K, TOKEN_SUBROWS, 128)

    def token_block(b, j):
        return b * blocks_per_seq + j + 1

    def plane(k):
        return pl.BlockSpec((ROW_BLOCK, None, TOKEN_SUBROWS, 128), lambda b, j: (token_block(b, j), k, 0, 0))

    tiles = pl.BlockSpec((tile_rows, 128), lambda b, j: (token_block(b, j), 0))
    vec = pl.BlockSpec((1, D_MODEL), lambda b, j: (0, 0))
    return pl.pallas_call(
        _combine_body,
        grid=(batch, nb),
        in_specs=[tiles, plane(0), plane(1), plane(2), plane(3),
                  pl.BlockSpec((ROW_BLOCK, 128), lambda b, j: (token_block(b, j), 0)), vec, vec],
        out_specs=pl.BlockSpec((1, ROW_BLOCK, D_MODEL), lambda b, j: (b, j, 0)),
        out_shape=jax.ShapeDtypeStruct((batch, seq_len, D_MODEL), F32),
        compiler_params=pltpu.CompilerParams(
            dimension_semantics=("parallel", "parallel"), vmem_limit_bytes=VMEM_LIMIT),
        name="combine",
    )(h1t, yk, yk, yk, yk, route, ln_g, ln_b)


def kernel(x, meta_tokens, ln_emb_g, ln_emb_b, w_in, gla_wa2, gla_ba, gla_norm_g, diff_lambda_q1, diff_lambda_k1, diff_lambda_q2, diff_lambda_k2, diff_norm_g, w_out, ln1_g, ln1_b, router_w, router_b, w_gate_up, b_gate_up, w_down, b_down, ln2_g, ln2_b):
    B, S, D = x.shape
    LP = S + ROW_BLOCK
    NP = B * LP
    row = lambda v: v.reshape(1, -1)

    w = w_in[0]
    w_main = jnp.concatenate([w[:, :1536], w[:, 1552:]], axis=1).astype(BF16)
    w_ga = w[:, 1536:1552].astype(BF16)
    meta_pad = jnp.pad(meta_tokens, ((FRONT_PAD, 0), (0, 0)))

    h0, gq, gk, gv, gr, la, dq, dk, dv = _inproj(
        x, meta_pad, row(ln_emb_g), row(ln_emb_b), w_main, w_ga,
        gla_wa2[0].astype(BF16), row(gla_ba[0]))

    og = _gla(gq, gk, gv, gr, la, row(gla_norm_g[0]))
    od = _diff_attention(dq, dk, dv, row(diff_lambda_q1[0]), row(diff_lambda_k1[0]),
                         row(diff_lambda_q2[0]), row(diff_lambda_k2[0]), row(diff_norm_g[0]))

    wo = w_out[0].astype(BF16)
    h1t, route, counts = _outproj_route(
        og.reshape(NP, 512), od.reshape(NP, 512), h0.reshape(NP, D), wo[:512], wo[512:],
        row(ln1_g[0]), row(ln1_b[0]), router_w[0], row(router_b[0]), LP)

    R = MOE_ROWS
    i32 = jnp.int32
    counts = counts[0].astype(i32)
    n_blk = (counts + R - 1) // R
    blk_end = jnp.cumsum(n_blk)
    blk_start = blk_end - n_blk
    grp_start = jnp.cumsum(counts) - counts
    n_total = blk_end[-1:]
    n_assign_max = B * (S + N_META) * TOP_K
    max_blocks = (n_assign_max + N_EXPERTS * (R - 1)) // R
    g = jnp.minimum(jnp.arange(max_blocks, dtype=i32), n_total[0] - 1)
    is_e = (jnp.minimum(jnp.sum(g[:, None] >= blk_end[None, :], axis=1), N_EXPERTS - 1)[:, None]
            == jnp.arange(N_EXPERTS)[None, :])
    pick = lambda v: jnp.sum(jnp.where(is_e, v[None, :], 0), axis=1)
    local = (g - pick(blk_start)) * R
    blk_p0 = jnp.concatenate([jnp.full((1,), NP * TOP_K, i32), (pick(grp_start) + local).astype(i32)])

    tok_valid = (jnp.arange(NP, dtype=i32) % LP) >= FRONT_PAD
    flat_e = jnp.where(tok_valid[:, None], route[:, :TOP_K], N_EXPERTS).reshape(-1)
    id_bits = (NP * TOP_K - 1).bit_length()
    keyed = jnp.sort(flat_e * (1 << id_bits) + jnp.arange(NP * TOP_K, dtype=i32))
    assert R <= FRONT_PAD * TOP_K
    order = jnp.concatenate([keyed & ((1 << id_bits) - 1), jnp.arange(R, dtype=i32)])

    bgu = b_gate_up[0].reshape(N_EXPERTS, D_FF, 2)
    yk = _experts(order, blk_p0, blk_start.astype(i32), n_blk.astype(i32), n_total.astype(i32),
                  h1t, w_gate_up[0],
                  bgu[:, :, 0].reshape(N_EXPERTS, 1, D_FF), bgu[:, :, 1].reshape(N_EXPERTS, 1, D_FF),
                  w_down[0], b_down[0].reshape(N_EXPERTS, 1, D), NP, LP)

    return _combine(h1t, yk, route, row(ln2_g[0]), row(ln2_b[0]), B, S)
```

```python
import functools
import math

import jax
import jax.numpy as jnp
from jax import lax
from jax.experimental import pallas as pl
from jax.experimental.pallas import tpu as pltpu

F32 = jnp.float32
BF16 = jnp.bfloat16

D_MODEL = 1024
N_META = 16
ROW_BLOCK = 128
FRONT_PAD = ROW_BLOCK - N_META
CHUNK = 64

GLA_HEADS = 4
GLA_DK = 64
GLA_DV = 128
GLA_RANK = 16
GLA_TAU = 16.0
DIFF_HEADS = 4
DIFF_DH = 64
DIFF_DV = 128

N_EXPERTS = 32
TOP_K = 4
D_FF = 1024
SWIGLU_LIMIT = 7.0
SWIGLU_ALPHA = 1.702
MOE_ROWS = 256

DEEPNORM_ALPHA = 2.0 ** 0.25
LAMBDA_INIT = 0.8 - 0.6 * math.exp(0.0)
LN_EPS = 1e-5
NEG_INF = -1e30
LOG2E = math.log2(math.e)

ROUTE_TILE = 512
VMEM_LIMIT = 48 * 1024 * 1024
EXPERT_VMEM_LIMIT = 56 * 1024 * 1024

_GQ, _GK, _GV, _GR, _DQ, _DK, _DV = 0, 256, 512, 1024, 1536, 2048, 2560
_MAIN_WIDTH = 3072


def _layer_norm(x, g, b):
    mu = jnp.mean(x, axis=-1, keepdims=True)
    xc = x - mu
    var = jnp.mean(xc * xc, axis=-1, keepdims=True)
    return xc * lax.rsqrt(var + LN_EPS) * g + b


def _split3_bf16(x):
    hi = x.astype(BF16)
    r1 = x - hi.astype(F32)
    mid = r1.astype(BF16)
    lo = (r1 - mid.astype(F32)).astype(BF16)
    return hi, mid, lo


TOKEN_SUBROWS = D_MODEL // 128


def _store_token_tiled(ref, val, first_token=0):
    n = val.shape[0]
    for s in range(TOKEN_SUBROWS):
        ref[pl.ds(first_token * TOKEN_SUBROWS + s, n, stride=TOKEN_SUBROWS), :] = val[:, s * 128:(s + 1) * 128]


def _load_token_tiled(ref, n, first_row=0):
    return jnp.concatenate(
        [ref[pl.ds(first_row + s, n, stride=TOKEN_SUBROWS), :] for s in range(TOKEN_SUBROWS)], axis=1)


def _inproj_body(x_ref, meta_ref, g_ref, b_ref, w_ref, wga_ref, wa2_ref, ba_ref,
                 h0_ref, gq_ref, gk_ref, gv_ref, gr_ref, la_ref, dq_ref, dk_ref, dv_ref, hs_ref):
    j = pl.program_id(1)
    rows = lax.broadcasted_iota(jnp.int32, (ROW_BLOCK, 1), 0)
    valid = jnp.logical_or(j > 0, rows >= FRONT_PAD)

    @pl.when(j == 0)
    def _():
        lead = _layer_norm(meta_ref[...], g_ref[...], b_ref[...])
        hs_ref[0] = jnp.where(rows >= FRONT_PAD, lead, 0.0)

    h = hs_ref[j % 2]
    hs_ref[(j + 1) % 2] = _layer_norm(x_ref[0], g_ref[...], b_ref[...])
    h0_ref[0] = h
    hb = h.astype(BF16)

    def proj(off, width):
        return jnp.dot(hb, w_ref[:, off:off + width], preferred_element_type=F32)

    gq_ref[0] = (proj(_GQ, 256) * GLA_DK ** -0.5).astype(BF16)
    gk_ref[0] = proj(_GK, 256)
    gv_ref[0] = proj(_GV, 512).astype(BF16)
    gr_ref[0] = proj(_GR, 512).astype(BF16)
    dq_ref[0] = (proj(_DQ, 512) * (DIFF_DH ** -0.5 * LOG2E)).astype(BF16)
    dk_ref[0] = proj(_DK, 512).astype(BF16)
    dv_ref[0] = proj(_DV, 512).astype(BF16)

    a_lr = jnp.dot(hb, wga_ref[...], preferred_element_type=F32)
    z = jnp.dot(a_lr.astype(BF16), wa2_ref[...], preferred_element_type=F32) + ba_ref[...]
    log_sig = jnp.minimum(z, 0.0) - jnp.log1p(jnp.exp(-jnp.abs(z)))
    la_ref[0] = jnp.where(valid, log_sig / GLA_TAU, 0.0)


def _inproj(x, meta_pad, ln_g, ln_b, w_main, w_ga, wa2, ba):
    B, S, D = x.shape
    nb = S // ROW_BLOCK + 1
    LP = nb * ROW_BLOCK

    def row_spec(width):
        return pl.BlockSpec((1, ROW_BLOCK, width), lambda b, j: (b, j, 0))

    def full(shape):
        return pl.BlockSpec(shape, lambda b, j: (0,) * len(shape))

    out_shapes = [
        jax.ShapeDtypeStruct((B, LP, D), F32),
        jax.ShapeDtypeStruct((B, LP, 256), BF16),
        jax.ShapeDtypeStruct((B, LP, 256), F32),
        jax.ShapeDtypeStruct((B, LP, 512), BF16),
        jax.ShapeDtypeStruct((B, LP, 512), BF16),
        jax.ShapeDtypeStruct((B, LP, 256), F32),
        jax.ShapeDtypeStruct((B, LP, 512), BF16),
        jax.ShapeDtypeStruct((B, LP, 512), BF16),
        jax.ShapeDtypeStruct((B, LP, 512), BF16),
    ]
    return pl.pallas_call(
        _inproj_body,
        grid=(B, nb),
        in_specs=[
            pl.BlockSpec((1, ROW_BLOCK, D), lambda b, j: (b, jnp.minimum(j, nb - 2), 0)),
            full((ROW_BLOCK, D)), full((1, D)), full((1, D)),
            full((D, _MAIN_WIDTH)), full((D, GLA_RANK)), full((GLA_RANK, 256)), full((1, 256)),
        ],
        out_specs=[row_spec(s.shape[-1]) for s in out_shapes],
        out_shape=out_shapes,
        scratch_shapes=[pltpu.VMEM((2, ROW_BLOCK, D), F32)],
        compiler_params=pltpu.CompilerParams(
            dimension_semantics=("parallel", "arbitrary"), vmem_limit_bytes=VMEM_LIMIT),
        name="inproj",
    )(x, meta_pad, ln_g, ln_b, w_main, w_ga, wa2, ba)


def _gla_body(q_ref, k_ref, v_ref, r_ref, la_ref, g_ref, o_ref):
    n_groups = q_ref.shape[1] // ROW_BLOCK
    ri = lax.broadcasted_iota(jnp.int32, (ROW_BLOCK, ROW_BLOCK), 0)
    ci = lax.broadcasted_iota(jnp.int32, (ROW_BLOCK, ROW_BLOCK), 1)
    later = jnp.logical_and(ri // CHUNK == ci // CHUNK, ci > ri).astype(BF16)
    sr = lax.broadcasted_iota(jnp.int32, (GLA_HEADS * GLA_DV, GLA_HEADS * GLA_DK), 0)
    sc = lax.broadcasted_iota(jnp.int32, (GLA_HEADS * GLA_DV, GLA_HEADS * GLA_DK), 1)
    same_head = sr // GLA_DV == sc // GLA_DK
    gain = g_ref[...]
    st = jnp.zeros((GLA_HEADS * GLA_DV, GLA_HEADS * GLA_DK), F32)

    for grp in range(n_groups):
        g0 = grp * ROW_BLOCK
        la = la_ref[0, g0:g0 + ROW_BLOCK, :]
        hi, mid, lo = _split3_bf16(la)
        suffix = (jnp.dot(later, hi, preferred_element_type=F32)
                  + jnp.dot(later, mid, preferred_element_type=F32)
                  + jnp.dot(later, lo, preferred_element_type=F32))
        kdec = (k_ref[0, g0:g0 + ROW_BLOCK, :] * jnp.exp(suffix)).astype(BF16)
        for half in range(ROW_BLOCK // CHUNK):
            f = half * CHUNK
            r0 = g0 + f
            tot = suffix[f:f + 1, :] + la[f:f + 1, :]
            upd = lax.dot_general(v_ref[0, r0:r0 + CHUNK, :], kdec[f:f + CHUNK, :],
                                  (((0,), (0,)), ((), ())), preferred_element_type=F32)
            st = st * jnp.exp(tot) + jnp.where(same_head, upd, 0.0)
            o = lax.dot_general(q_ref[0, r0:r0 + CHUNK, :], st.astype(BF16),
                                (((1,), (1,)), ((), ())), preferred_element_type=F32)
            r = r_ref[0, r0:r0 + CHUNK, :].astype(F32)
            for h in range(GLA_HEADS):
                oh = o[:, h * GLA_DV:(h + 1) * GLA_DV]
                rh = r[:, h * GLA_DV:(h + 1) * GLA_DV]
                ms = jnp.mean(oh * oh, axis=-1, keepdims=True)
                out = oh * lax.rsqrt(ms + LN_EPS) * gain * (rh * jax.nn.sigmoid(rh))
                o_ref[0, r0:r0 + CHUNK, h * GLA_DV:(h + 1) * GLA_DV] = out.astype(BF16)


def _gla(gq, gk, gv, gr, la, norm_g):
    B, LP, _ = gq.shape

    def seq(width):
        return pl.BlockSpec((1, LP, width), lambda b: (b, 0, 0))

    return pl.pallas_call(
        _gla_body,
        grid=(B,),
        in_specs=[seq(256), seq(256), seq(512), seq(512), seq(256),
                  pl.BlockSpec((1, GLA_DV), lambda b: (0, 0))],
        out_specs=seq(512),
        out_shape=jax.ShapeDtypeStruct((B, LP, 512), BF16),
        compiler_params=pltpu.CompilerParams(
            dimension_semantics=("parallel",), vmem_limit_bytes=VMEM_LIMIT),
        name="gla",
    )(gq, gk, gv, gr, la, norm_g)


ATT_BLOCK = 256


def _diff_body(q_ref, k_ref, v_ref, lq1_ref, lk1_ref, lq2_ref, lk2_ref, g_ref, o_ref,
               dmask_ref):
    h = pl.program_id(1)
    slope = jnp.where(h == 0, 2.0 ** -2, jnp.where(h == 1, 2.0 ** -4, jnp.where(h == 2, 2.0 ** -6, 2.0 ** -8)))
    slope = slope.astype(F32) * LOG2E
    lam = (jnp.exp(jnp.sum(lq1_ref[...] * lk1_ref[...], axis=-1, keepdims=True))
           - jnp.exp(jnp.sum(lq2_ref[...] * lk2_ref[...], axis=-1, keepdims=True)) + LAMBDA_INIT)
    gain = g_ref[...] * (1.0 - LAMBDA_INIT)
    n_qblocks = (q_ref.shape[1] - ROW_BLOCK) // ATT_BLOCK
    nt = (((1,), (1,)), ((), ()))

    def split_q(q):
        lane = lax.broadcasted_iota(jnp.int32, q.shape, 1)
        zero = jnp.zeros_like(q)
        return jnp.where(lane < DIFF_DH, q, zero), jnp.where(lane >= DIFF_DH, q, zero)

    def with_ones(v):
        return jnp.concatenate([v, jnp.ones_like(v)], axis=1)

    def softmax_av(qz, kk, vext, add_bias):
        s = add_bias(lax.dot_general(qz, kk, nt, preferred_element_type=F32))
        m = jnp.max(s, axis=-1, keepdims=True)
        p = jnp.exp2(s - m).astype(BF16)
        return jnp.dot(p, vext, preferred_element_type=F32)

    def finish(a1, a2):
        o = a1[:, :DIFF_DV] / a1[:, DIFF_DV:] - lam * (a2[:, :DIFF_DV] / a2[:, DIFF_DV:])
        ms = jnp.mean(o * o, axis=-1, keepdims=True)
        return (o * lax.rsqrt(ms + LN_EPS) * gain).astype(BF16)

    r = lax.broadcasted_iota(jnp.int32, (ROW_BLOCK, ROW_BLOCK), 0)
    c = lax.broadcasted_iota(jnp.int32, (ROW_BLOCK, ROW_BLOCK), 1)
    ok = jnp.logical_and(c // CHUNK <= r // CHUNK, c >= FRONT_PAD)
    bias_lead = jnp.where(ok, -slope * jnp.abs(r - c).astype(F32), NEG_INF)
    q1z, q2z = split_q(q_ref[0, 0:ROW_BLOCK, :])
    k_lead = k_ref[0, 0:ROW_BLOCK, :]
    v_lead = with_ones(v_ref[0, 0:ROW_BLOCK, :])
    o_ref[0, 0:ROW_BLOCK, :] = finish(softmax_av(q1z, k_lead, v_lead, lambda s: s + bias_lead),
                                      softmax_av(q2z, k_lead, v_lead, lambda s: s + bias_lead))

    r = lax.broadcasted_iota(jnp.int32, (ATT_BLOCK, ATT_BLOCK), 0)
    c = lax.broadcasted_iota(jnp.int32, (ATT_BLOCK, ATT_BLOCK), 1)
    rel = jnp.where(c <= r, c, 2 * r - c).astype(F32)
    dmask_ref[...] = jnp.where(c // CHUNK <= r // CHUNK, slope * rel, NEG_INF)

    for jq in range(n_qblocks):
        qbase = ROW_BLOCK + jq * ATT_BLOCK
        n_keys = qbase + ATT_BLOCK
        q1z, q2z = split_q(q_ref[0, qbase:qbase + ATT_BLOCK, :])
        kk = k_ref[0, 0:n_keys, :]
        vext = with_ones(v_ref[0, 0:n_keys, :])
        col = lax.broadcasted_iota(jnp.int32, (1, qbase), 1)
        col_bias = jnp.where(col >= FRONT_PAD, slope * (col - qbase).astype(F32), NEG_INF)

        def add_bias(s, col_bias=col_bias, qbase=qbase):
            return jnp.concatenate([s[:, :qbase] + col_bias, s[:, qbase:] + dmask_ref[...]], axis=1)

        o_ref[0, qbase:qbase + ATT_BLOCK, :] = finish(softmax_av(q1z, kk, vext, add_bias),
                                                      softmax_av(q2z, kk, vext, add_bias))


def _diff_attention(dq, dk, dv, lq1, lk1, lq2, lk2, norm_g):
    B, LP, W = dq.shape
    small = pl.BlockSpec((1, DIFF_DH), lambda b, h: (0, 0))
    seq = pl.BlockSpec((1, LP, 2 * DIFF_DH), lambda b, h: (b, 0, h))
    return pl.pallas_call(
        _diff_body,
        grid=(B, DIFF_HEADS),
        in_specs=[seq, seq, seq, small, small, small, small,
                  pl.BlockSpec((1, DIFF_DV), lambda b, h: (0, 0))],
        out_specs=seq,
        out_shape=jax.ShapeDtypeStruct((B, LP, W), BF16),
        scratch_shapes=[pltpu.VMEM((ATT_BLOCK, ATT_BLOCK), F32)],
        compiler_params=pltpu.CompilerParams(
            dimension_semantics=("parallel", "parallel"), vmem_limit_bytes=VMEM_LIMIT),
        name="diff_attn",
    )(dq, dk, dv, lq1, lk1, lq2, lk2, norm_g)


def _route_body(og_ref, od_ref, h0_ref, wog_ref, wod_ref, g_ref, b_ref, rw_ref, rb_ref,
                h1_ref, route_ref, cnt_ref, carry_ref, *, rows_per_seq):
    t = pl.program_id(0)

    @pl.when(t == 0)
    def _():
        carry_ref[...] = jnp.zeros_like(carry_ref)

    mix = (jnp.dot(og_ref[...], wog_ref[...], preferred_element_type=F32)
           + jnp.dot(od_ref[...], wod_ref[...], preferred_element_type=F32))
    h1 = _layer_norm(DEEPNORM_ALPHA * h0_ref[...] + mix, g_ref[...], b_ref[...])
    _store_token_tiled(h1_ref, h1)

    hh, hm, _ = _split3_bf16(h1)
    wh, wm, _ = _split3_bf16(rw_ref[...])
    head = jnp.dot(hh, jnp.concatenate([wh, wm], axis=1), preferred_element_type=F32)
    logits = (rb_ref[...] + head[:, :N_EXPERTS] + head[:, N_EXPERTS:]
              + jnp.dot(hm, wh, preferred_element_type=F32))

    T = logits.shape[0]
    lane = lax.broadcasted_iota(jnp.int32, (T, N_EXPERTS), 1)
    grow = t * T + lax.broadcasted_iota(jnp.int32, (T, 1), 0)
    valid = (grow % rows_per_seq) >= FRONT_PAD

    work = logits
    top_v, top_i = [], []
    for _ in range(TOP_K):
        mx = jnp.max(work, axis=-1, keepdims=True)
        idx = jnp.min(jnp.where(work == mx, lane, N_EXPERTS), axis=-1, keepdims=True)
        top_v.append(mx)
        top_i.append(idx)
        work = jnp.where(lane == idx, -jnp.inf, work)
    ex = [jnp.exp(v - top_v[0]) for v in top_v]
    den = ex[0] + ex[1] + ex[2] + ex[3]
    gates = [e / den for e in ex]

    onehot = jnp.zeros((T, N_EXPERTS), F32)
    for idx in top_i:
        onehot = onehot + (lane == idx).astype(F32)
    onehot = jnp.where(valid, onehot, 0.0)

    carry_ref[...] = carry_ref[...] + jnp.sum(onehot, axis=0, keepdims=True)
    cnt_ref[...] = carry_ref[...]

    lane_o = lax.broadcasted_iota(jnp.int32, (T, 128), 1)
    packed = jnp.zeros((T, 128), jnp.int32)
    for k in range(TOP_K):
        packed = jnp.where(lane_o == k, top_i[k], packed)
        packed = jnp.where(lane_o == 2 * TOP_K + k, lax.bitcast_convert_type(gates[k], jnp.int32), packed)
    route_ref[...] = packed


def _outproj_route(og, od, h0, wo_g, wo_d, ln_g, ln_b, rw, rb, rows_per_seq):
    NP, D = h0.shape
    T = ROUTE_TILE

    def rows(width):
        return pl.BlockSpec((T, width), lambda t: (t, 0))

    def full(shape):
        return pl.BlockSpec(shape, lambda t: (0,) * len(shape))

    return pl.pallas_call(
        functools.partial(_route_body, rows_per_seq=rows_per_seq),
        grid=(NP // T,),
        in_specs=[rows(512), rows(512), rows(D), full((512, D)), full((512, D)),
                  full((1, D)), full((1, D)), full((D, N_EXPERTS)), full((1, N_EXPERTS))],
        out_specs=[pl.BlockSpec((T * TOKEN_SUBROWS, 128), lambda t: (t, 0)), rows(128), full((1, N_EXPERTS))],
        out_shape=[jax.ShapeDtypeStruct((NP * TOKEN_SUBROWS, 128), F32),
                   jax.ShapeDtypeStruct((NP, 128), jnp.int32),
                   jax.ShapeDtypeStruct((1, N_EXPERTS), F32)],
        scratch_shapes=[pltpu.VMEM((1, N_EXPERTS), F32)],
        compiler_params=pltpu.CompilerParams(
            dimension_semantics=("arbitrary",), vmem_limit_bytes=VMEM_LIMIT),
        name="outproj_route",
    )(og, od, h0, wo_g, wo_d, ln_g, ln_b, rw, rb)


def _expert_body(order_ref, p0_ref, bstart_ref, nblk_ref, ntot_ref,
                 h1t_hbm, wgu_ref, bg_ref, bu_ref, wd_ref, bd_ref, yk_hbm,
                 wg_s, wu_s, wd_s, xb0, xb1, xb2, yb0, yb1, yb2, zbuf, gsem, ssem, zsem,
                 *, n_tokens, rows_per_seq):
    e = pl.program_id(0)
    n_total = ntot_ref[0]
    R = MOE_ROWS
    TS = TOKEN_SUBROWS
    xb = (xb0, xb1, xb2)
    yb = (yb0, yb1, yb2)
    NB = len(xb)

    def gather_start(b, slot):
        p0 = p0_ref[b]
        for r in range(R):
            tok = order_ref[p0 + r] >> 2
            pltpu.make_async_copy(
                h1t_hbm.at[pl.ds(pl.multiple_of(tok * TS, TS), TS), :],
                xb[slot].at[pl.ds(r * TS, TS), :], gsem.at[slot]).start(priority=r % 2)

    def gather_wait(slot):
        pltpu.make_async_copy(h1t_hbm.at[pl.ds(0, R * TS), :], xb[slot], gsem.at[slot]).wait()

    def scatter_start(b, slot):
        p0 = p0_ref[b]
        for r in range(R):
            row = order_ref[p0 + r] * TS
            pltpu.make_async_copy(
                yb[slot].at[pl.ds(r * TS, TS), :],
                yk_hbm.at[pl.ds(pl.multiple_of(row, TS), TS), :], ssem.at[slot]).start(priority=r % 2)

    def scatter_wait(slot):
        pltpu.make_async_copy(yb[slot], yk_hbm.at[pl.ds(0, R * TS), :], ssem.at[slot]).wait()

    @pl.when(e == 0)
    def _():
        zbuf[...] = jnp.zeros_like(zbuf)
        lead = FRONT_PAD * TS
        fills = [((s * rows_per_seq * TOP_K + k * FRONT_PAD) * TS, lead)
                 for s in range(n_tokens // rows_per_seq) for k in range(TOP_K)]
        copies = [pltpu.make_async_copy(zbuf.at[pl.ds(0, n), :], yk_hbm.at[pl.ds(o, n), :], zsem)
                  for o, n in fills]
        for cp in copies:
            cp.start()
        for cp in copies:
            cp.wait()
        gather_start(0, 0)
        gather_start(jnp.minimum(1, n_total - 1), 1)

    @pl.when(nblk_ref[e] > 0)
    def _():
        r = lax.broadcasted_iota(jnp.int32, (256, 256), 0)
        c = lax.broadcasted_iota(jnp.int32, (256, 256), 1)
        perm = (r == jnp.where(c < 128, 2 * c, 2 * (c - 128) + 1)).astype(BF16)
        for tt in range(2 * D_FF // 256):
            wt = wgu_ref[0, :, tt * 256:(tt + 1) * 256].astype(BF16)
            sp = jnp.dot(wt, perm, preferred_element_type=F32)
            wg_s[:, tt * 128:(tt + 1) * 128] = sp[:, :128].astype(BF16)
            wu_s[:, tt * 128:(tt + 1) * 128] = sp[:, 128:].astype(BF16)
        wd_s[...] = wd_ref[0].astype(BF16)

    def run_block(b, slot):
        prv = (slot + 2) % NB
        gather_wait(slot)
        gather_start(jnp.minimum(b + 2, n_total - 1), prv)
        x = _load_token_tiled(xb[slot], R).astype(BF16)
        gt = jnp.dot(x, wg_s[...], preferred_element_type=F32) + bg_ref[0]
        up = jnp.dot(x, wu_s[...], preferred_element_type=F32) + bu_ref[0]
        gt = jnp.minimum(gt, SWIGLU_LIMIT)
        up = jnp.clip(up, -SWIGLU_LIMIT, SWIGLU_LIMIT)
        act = (up + 1.0) * (gt * jax.nn.sigmoid(SWIGLU_ALPHA * gt))
        y = jnp.dot(act.astype(BF16), wd_s[...], preferred_element_type=F32) + bd_ref[0]
        _store_token_tiled(yb[slot], y)

        @pl.when(b >= 1)
        def _():
            scatter_wait(prv)
        scatter_start(b, slot)

    def block(b, carry):
        for slot in range(NB):
            pl.when(b % NB == slot)(functools.partial(run_block, b, slot))
        return carry

    b0 = bstart_ref[e]
    lax.fori_loop(b0, b0 + nblk_ref[e], block, 0)

    @pl.when(e == pl.num_programs(0) - 1)
    def _():
        last = n_total - 1
        for slot in range(NB):
            @pl.when(last % NB == slot)
            def _(slot=slot):
                gather_wait((slot + 1) % NB)
                gather_wait((slot + 2) % NB)
                scatter_wait(slot)


def _experts(order, blk_p0, blk_start, n_blk, n_total, h1t, w_gate_up, b_gate, b_up, w_down,
             b_down, n_tokens, rows_per_seq):
    D = D_MODEL
    R = MOE_ROWS
    buf = pltpu.VMEM((R * TOKEN_SUBROWS, 128), F32)
    per_expert = lambda e, *_: (e, 0, 0)
    grid_spec = pltpu.PrefetchScalarGridSpec(
        num_scalar_prefetch=5,
        grid=(N_EXPERTS,),
        in_specs=[
            pl.BlockSpec(memory_space=pl.ANY),
            pl.BlockSpec((1, D, 2 * D_FF), per_expert),
            pl.BlockSpec((1, 1, D_FF), per_expert),
            pl.BlockSpec((1, 1, D_FF), per_expert),
            pl.BlockSpec((1, D_FF, D), per_expert),
            pl.BlockSpec((1, 1, D), per_expert),
        ],
        out_specs=pl.BlockSpec(memory_space=pl.ANY),
        scratch_shapes=[pltpu.VMEM((D, D_FF), BF16), pltpu.VMEM((D, D_FF), BF16), pltpu.VMEM((D_FF, D), BF16),
                        buf, buf, buf, buf, buf, buf, pltpu.VMEM((FRONT_PAD * TOKEN_SUBROWS, 128), F32),
                        pltpu.SemaphoreType.DMA((3,)), pltpu.SemaphoreType.DMA((3,)), pltpu.SemaphoreType.DMA],
    )
    out_rows = TOP_K * n_tokens * TOKEN_SUBROWS
    return pl.pallas_call(
        functools.partial(_expert_body, n_tokens=n_tokens, rows_per_seq=rows_per_seq),
        grid_spec=grid_spec,
        out_shape=jax.ShapeDtypeStruct((out_rows, 128), F32),
        compiler_params=pltpu.CompilerParams(
            dimension_semantics=("arbitrary",), vmem_limit_bytes=EXPERT_VMEM_LIMIT),
        name="experts",
    )(order, blk_p0, blk_start, n_blk, n_total, h1t, w_gate_up, b_gate, b_up, w_down, b_down)


def _combine_body(h1_ref, y0_ref, y1_ref, y2_ref, y3_ref, route_ref, g_ref, b_ref, o_ref):
    route = route_ref[...]
    y = jnp.zeros((ROW_BLOCK, D_MODEL), F32)
    for k, yk_ref in enumerate((y0_ref, y1_ref, y2_ref, y3_ref)):
        gate = lax.bitcast_convert_type(route[:, 2 * TOP_K + k:2 * TOP_K + k + 1], F32)
        y = y + gate * _load_token_tiled(yk_ref.reshape(ROW_BLOCK * TOKEN_SUBROWS, 128), ROW_BLOCK)
    h1 = _load_token_tiled(h1_ref, ROW_BLOCK)
    o_ref[0] = _layer_norm(DEEPNORM_ALPHA * h1 + y, g_ref[...], b_ref[...])


def _combine(h1t, yk, route, ln_g, ln_b, batch, seq_len):
    nb = seq_len // ROW_BLOCK
    blocks_per_seq = nb + 1
    tile_rows = ROW_BLOCK * TOKEN_SUBROWS
    yk = yk.reshape(-1, TOP_K, TOKEN_SUBROWS, 128)

    def token_block(b, j):
        return b * blocks_per_seq + j + 1

    def plane(k):
        return pl.BlockSpec((ROW_BLOCK, None, TOKEN_SUBROWS, 128), lambda b, j: (token_block(b, j), k, 0, 0))

    tiles = pl.BlockSpec((tile_rows, 128), lambda b, j: (token_block(b, j), 0))
    vec = pl.BlockSpec((1, D_MODEL), lambda b, j: (0, 0))
    return pl.pallas_call(
        _combine_body,
        grid=(batch, nb),
        in_specs=[tiles, plane(0), plane(1), plane(2), plane(3),
                  pl.BlockSpec((ROW_BLOCK, 128), lambda b, j: (token_block(b, j), 0)), vec, vec],
        out_specs=pl.BlockSpec((1, ROW_BLOCK, D_MODEL), lambda b, j: (b, j, 0)),
        out_shape=jax.ShapeDtypeStruct((batch, seq_len, D_MODEL), F32),
        compiler_params=pltpu.CompilerParams(
            dimension_semantics=("parallel", "parallel"), vmem_limit_bytes=VMEM_LIMIT),
        name="combine",
    )(h1t, yk, yk, yk, yk, route, ln_g, ln_b)


def kernel(x, meta_tokens, ln_emb_g, ln_emb_b, w_in, gla_wa2, gla_ba, gla_norm_g, diff_lambda_q1, diff_lambda_k1, diff_lambda_q2, diff_lambda_k2, diff_norm_g, w_out, ln1_g, ln1_b, router_w, router_b, w_gate_up, b_gate_up, w_down, b_down, ln2_g, ln2_b):
    B, S, D = x.shape
    LP = S + ROW_BLOCK
    NP = B * LP
    row = lambda v: v.reshape(1, -1)

    w = w_in[0]
    w_main = jnp.concatenate([w[:, :1536], w[:, 1552:]], axis=1).astype(BF16)
    w_ga = w[:, 1536:1552].astype(BF16)
    meta_pad = jnp.pad(meta_tokens, ((FRONT_PAD, 0), (0, 0)))

    h0, gq, gk, gv, gr, la, dq, dk, dv = _inproj(
        x, meta_pad, row(ln_emb_g), row(ln_emb_b), w_main, w_ga,
        gla_wa2[0].astype(BF16), row(gla_ba[0]))

    og = _gla(gq, gk, gv, gr, la, row(gla_norm_g[0]))
    od = _diff_attention(dq, dk, dv, row(diff_lambda_q1[0]), row(diff_lambda_k1[0]),
                         row(diff_lambda_q2[0]), row(diff_lambda_k2[0]), row(diff_norm_g[0]))

    wo = w_out[0].astype(BF16)
    h1t, route, counts = _outproj_route(
        og.reshape(NP, 512), od.reshape(NP, 512), h0.reshape(NP, D), wo[:512], wo[512:],
        row(ln1_g[0]), row(ln1_b[0]), router_w[0], row(router_b[0]), LP)

    R = MOE_ROWS
    i32 = jnp.int32
    counts = counts[0].astype(i32)
    n_blk = (counts + R - 1) // R
    blk_end = jnp.cumsum(n_blk)
    blk_start = blk_end - n_blk
    grp_start = jnp.cumsum(counts) - counts
    n_total = blk_end[-1:]
    n_assign_max = B * (S + N_META) * TOP_K
    max_blocks = (n_assign_max + N_EXPERTS * (R - 1)) // R
    g = jnp.minimum(jnp.arange(max_blocks, dtype=i32), n_total[0] - 1)
    is_e = (jnp.minimum(jnp.sum(g[:, None] >= blk_end[None, :], axis=1), N_EXPERTS - 1)[:, None]
            == jnp.arange(N_EXPERTS)[None, :])
    pick = lambda v: jnp.sum(jnp.where(is_e, v[None, :], 0), axis=1)
    local = (g - pick(blk_start)) * R
    blk_p0 = (pick(grp_start) + local).astype(i32)

    tok_valid = (jnp.arange(NP, dtype=i32) % LP) >= FRONT_PAD
    flat_e = jnp.where(tok_valid[:, None], route[:, :TOP_K], N_EXPERTS).reshape(-1)
    id_bits = (NP * TOP_K - 1).bit_length()
    keyed = jnp.sort(flat_e * (1 << id_bits) + jnp.arange(NP * TOP_K, dtype=i32))
    assert R <= B * FRONT_PAD * TOP_K
    order = keyed & ((1 << id_bits) - 1)

    bgu = b_gate_up[0].reshape(N_EXPERTS, D_FF, 2)
    yk = _experts(order, blk_p0, blk_start.astype(i32), n_blk.astype(i32), n_total.astype(i32),
                  h1t, w_gate_up[0],
                  bgu[:, :, 0].reshape(N_EXPERTS, 1, D_FF), bgu[:, :, 1].reshape(N_EXPERTS, 1, D_FF),
                  w_down[0], b_down[0].reshape(N_EXPERTS, 1, D), NP, LP)

    return _combine(h1t, yk, route, row(ln2_g[0]), row(ln2_b[0]), B, S)
```

```python
import functools
import math

import jax
import jax.numpy as jnp
from jax import lax
from jax.experimental import pallas as pl
from jax.experimental.pallas import tpu as pltpu

F32 = jnp.float32
BF16 = jnp.bfloat16

D_MODEL = 1024
N_META = 16
ROW_BLOCK = 128
FRONT_PAD = ROW_BLOCK - N_META
CHUNK = 64

GLA_HEADS = 4
GLA_DK = 64
GLA_DV = 128
GLA_RANK = 16
GLA_TAU = 16.0
DIFF_HEADS = 4
DIFF_DH = 64
DIFF_DV = 128

N_EXPERTS = 32
TOP_K = 4
D_FF = 1024
SWIGLU_LIMIT = 7.0
SWIGLU_ALPHA = 1.702
MOE_ROWS = 128

DEEPNORM_ALPHA = 2.0 ** 0.25
LAMBDA_INIT = 0.8 - 0.6 * math.exp(0.0)
LN_EPS = 1e-5
NEG_INF = -1e30
LOG2E = math.log2(math.e)

ROUTE_TILE = 512
VMEM_LIMIT = 48 * 1024 * 1024
EXPERT_VMEM_LIMIT = 56 * 1024 * 1024

_GQ, _GK, _GV, _GR, _DQ, _DK, _DV = 0, 256, 512, 1024, 1536, 2048, 2560
_MAIN_WIDTH = 3072


def _layer_norm(x, g, b):
    mu = jnp.mean(x, axis=-1, keepdims=True)
    xc = x - mu
    var = jnp.mean(xc * xc, axis=-1, keepdims=True)
    return xc * lax.rsqrt(var + LN_EPS) * g + b


def _split3_bf16(x):
    hi = x.astype(BF16)
    r1 = x - hi.astype(F32)
    mid = r1.astype(BF16)
    lo = (r1 - mid.astype(F32)).astype(BF16)
    return hi, mid, lo


TOKEN_SUBROWS = D_MODEL // 128


def _store_token_tiled(ref, val, first_token=0):
    n = val.shape[0]
    for s in range(TOKEN_SUBROWS):
        ref[pl.ds(first_token * TOKEN_SUBROWS + s, n, stride=TOKEN_SUBROWS), :] = val[:, s * 128:(s + 1) * 128]


def _load_token_tiled(ref, n, first_row=0):
    return jnp.concatenate(
        [ref[pl.ds(first_row + s, n, stride=TOKEN_SUBROWS), :] for s in range(TOKEN_SUBROWS)], axis=1)


def _inproj_body(x_ref, meta_ref, g_ref, b_ref, w_ref, wga_ref, wa2_ref, ba_ref,
                 h0_ref, gq_ref, gk_ref, gv_ref, gr_ref, la_ref, dq_ref, dk_ref, dv_ref, hs_ref):
    j = pl.program_id(1)
    rows = lax.broadcasted_iota(jnp.int32, (ROW_BLOCK, 1), 0)
    valid = jnp.logical_or(j > 0, rows >= FRONT_PAD)

    @pl.when(j == 0)
    def _():
        lead = _layer_norm(meta_ref[...], g_ref[...], b_ref[...])
        hs_ref[0] = jnp.where(rows >= FRONT_PAD, lead, 0.0)

    h = hs_ref[j % 2]
    hs_ref[(j + 1) % 2] = _layer_norm(x_ref[0], g_ref[...], b_ref[...])
    h0_ref[0] = h
    hb = h.astype(BF16)

    def proj(off, width):
        return jnp.dot(hb, w_ref[:, off:off + width], preferred_element_type=F32)

    gq_ref[0] = (proj(_GQ, 256) * GLA_DK ** -0.5).astype(BF16)
    gk_ref[0] = proj(_GK, 256)
    gv_ref[0] = proj(_GV, 512).astype(BF16)
    gr_ref[0] = proj(_GR, 512).astype(BF16)
    dq_ref[0] = (proj(_DQ, 512) * (DIFF_DH ** -0.5 * LOG2E)).astype(BF16)
    dk_ref[0] = proj(_DK, 512).astype(BF16)
    dv_ref[0] = proj(_DV, 512).astype(BF16)

    a_lr = jnp.dot(hb, wga_ref[...], preferred_element_type=F32)
    z = jnp.dot(a_lr.astype(BF16), wa2_ref[...], preferred_element_type=F32) + ba_ref[...]
    log_sig = jnp.minimum(z, 0.0) - jnp.log1p(jnp.exp(-jnp.abs(z)))
    la_ref[0] = jnp.where(valid, log_sig / GLA_TAU, 0.0)


def _inproj(x, meta_pad, ln_g, ln_b, w_main, w_ga, wa2, ba):
    B, S, D = x.shape
    nb = S // ROW_BLOCK + 1
    LP = nb * ROW_BLOCK

    def row_spec(width):
        return pl.BlockSpec((1, ROW_BLOCK, width), lambda b, j: (b, j, 0))

    def full(shape):
        return pl.BlockSpec(shape, lambda b, j: (0,) * len(shape))

    out_shapes = [
        jax.ShapeDtypeStruct((B, LP, D), F32),
        jax.ShapeDtypeStruct((B, LP, 256), BF16),
        jax.ShapeDtypeStruct((B, LP, 256), F32),
        jax.ShapeDtypeStruct((B, LP, 512), BF16),
        jax.ShapeDtypeStruct((B, LP, 512), BF16),
        jax.ShapeDtypeStruct((B, LP, 256), F32),
        jax.ShapeDtypeStruct((B, LP, 512), BF16),
        jax.ShapeDtypeStruct((B, LP, 512), BF16),
        jax.ShapeDtypeStruct((B, LP, 512), BF16),
    ]
    return pl.pallas_call(
        _inproj_body,
        grid=(B, nb),
        in_specs=[
            pl.BlockSpec((1, ROW_BLOCK, D), lambda b, j: (b, jnp.minimum(j, nb - 2), 0)),
            full((ROW_BLOCK, D)), full((1, D)), full((1, D)),
            full((D, _MAIN_WIDTH)), full((D, GLA_RANK)), full((GLA_RANK, 256)), full((1, 256)),
        ],
        out_specs=[row_spec(s.shape[-1]) for s in out_shapes],
        out_shape=out_shapes,
        scratch_shapes=[pltpu.VMEM((2, ROW_BLOCK, D), F32)],
        compiler_params=pltpu.CompilerParams(
            dimension_semantics=("parallel", "arbitrary"), vmem_limit_bytes=VMEM_LIMIT),
        name="inproj",
    )(x, meta_pad, ln_g, ln_b, w_main, w_ga, wa2, ba)


def _gla_body(q_ref, k_ref, v_ref, r_ref, la_ref, g_ref, o_ref):
    n_groups = q_ref.shape[1] // ROW_BLOCK
    ri = lax.broadcasted_iota(jnp.int32, (ROW_BLOCK, ROW_BLOCK), 0)
    ci = lax.broadcasted_iota(jnp.int32, (ROW_BLOCK, ROW_BLOCK), 1)
    later = jnp.logical_and(ri // CHUNK == ci // CHUNK, ci > ri).astype(BF16)
    sr = lax.broadcasted_iota(jnp.int32, (GLA_HEADS * GLA_DV, GLA_HEADS * GLA_DK), 0)
    sc = lax.broadcasted_iota(jnp.int32, (GLA_HEADS * GLA_DV, GLA_HEADS * GLA_DK), 1)
    same_head = sr // GLA_DV == sc // GLA_DK
    gain = g_ref[...]
    st = jnp.zeros((GLA_HEADS * GLA_DV, GLA_HEADS * GLA_DK), F32)

    for grp in range(n_groups):
        g0 = grp * ROW_BLOCK
        la = la_ref[0, g0:g0 + ROW_BLOCK, :]
        hi, mid, lo = _split3_bf16(la)
        suffix = (jnp.dot(later, hi, preferred_element_type=F32)
                  + jnp.dot(later, mid, preferred_element_type=F32)
                  + jnp.dot(later, lo, preferred_element_type=F32))
        kdec = (k_ref[0, g0:g0 + ROW_BLOCK, :] * jnp.exp(suffix)).astype(BF16)
        for half in range(ROW_BLOCK // CHUNK):
            f = half * CHUNK
            r0 = g0 + f
            tot = suffix[f:f + 1, :] + la[f:f + 1, :]
            upd = lax.dot_general(v_ref[0, r0:r0 + CHUNK, :], kdec[f:f + CHUNK, :],
                                  (((0,), (0,)), ((), ())), preferred_element_type=F32)
            st = st * jnp.exp(tot) + jnp.where(same_head, upd, 0.0)
            o = lax.dot_general(q_ref[0, r0:r0 + CHUNK, :], st.astype(BF16),
                                (((1,), (1,)), ((), ())), preferred_element_type=F32)
            r = r_ref[0, r0:r0 + CHUNK, :].astype(F32)
            for h in range(GLA_HEADS):
                oh = o[:, h * GLA_DV:(h + 1) * GLA_DV]
                rh = r[:, h * GLA_DV:(h + 1) * GLA_DV]
                ms = jnp.mean(oh * oh, axis=-1, keepdims=True)
                out = oh * lax.rsqrt(ms + LN_EPS) * gain * (rh * jax.nn.sigmoid(rh))
                o_ref[0, r0:r0 + CHUNK, h * GLA_DV:(h + 1) * GLA_DV] = out.astype(BF16)


def _gla(gq, gk, gv, gr, la, norm_g):
    B, LP, _ = gq.shape

    def seq(width):
        return pl.BlockSpec((1, LP, width), lambda b: (b, 0, 0))

    return pl.pallas_call(
        _gla_body,
        grid=(B,),
        in_specs=[seq(256), seq(256), seq(512), seq(512), seq(256),
                  pl.BlockSpec((1, GLA_DV), lambda b: (0, 0))],
        out_specs=seq(512),
        out_shape=jax.ShapeDtypeStruct((B, LP, 512), BF16),
        compiler_params=pltpu.CompilerParams(
            dimension_semantics=("parallel",), vmem_limit_bytes=VMEM_LIMIT),
        name="gla",
    )(gq, gk, gv, gr, la, norm_g)


ATT_BLOCK = 256


def _diff_body(q_ref, k_ref, v_ref, lq1_ref, lk1_ref, lq2_ref, lk2_ref, g_ref, o_ref,
               dmask_ref):
    h = pl.program_id(1)
    slope = jnp.where(h == 0, 2.0 ** -2, jnp.where(h == 1, 2.0 ** -4, jnp.where(h == 2, 2.0 ** -6, 2.0 ** -8)))
    slope = slope.astype(F32) * LOG2E
    lam = (jnp.exp(jnp.sum(lq1_ref[...] * lk1_ref[...], axis=-1, keepdims=True))
           - jnp.exp(jnp.sum(lq2_ref[...] * lk2_ref[...], axis=-1, keepdims=True)) + LAMBDA_INIT)
    gain = g_ref[...] * (1.0 - LAMBDA_INIT)
    n_qblocks = (q_ref.shape[1] - ROW_BLOCK) // ATT_BLOCK
    nt = (((1,), (1,)), ((), ()))

    def split_q(q):
        lane = lax.broadcasted_iota(jnp.int32, q.shape, 1)
        zero = jnp.zeros_like(q)
        return jnp.where(lane < DIFF_DH, q, zero), jnp.where(lane >= DIFF_DH, q, zero)

    def with_ones(v):
        return jnp.concatenate([v, jnp.ones_like(v)], axis=1)

    def softmax_av(qz, kk, vext, add_bias):
        s = add_bias(lax.dot_general(qz, kk, nt, preferred_element_type=F32))
        m = jnp.max(s, axis=-1, keepdims=True)
        p = jnp.exp2(s - m).astype(BF16)
        return jnp.dot(p, vext, preferred_element_type=F32)

    def finish(a1, a2):
        o = a1[:, :DIFF_DV] / a1[:, DIFF_DV:] - lam * (a2[:, :DIFF_DV] / a2[:, DIFF_DV:])
        ms = jnp.mean(o * o, axis=-1, keepdims=True)
        return (o * lax.rsqrt(ms + LN_EPS) * gain).astype(BF16)

    r = lax.broadcasted_iota(jnp.int32, (ROW_BLOCK, ROW_BLOCK), 0)
    c = lax.broadcasted_iota(jnp.int32, (ROW_BLOCK, ROW_BLOCK), 1)
    ok = jnp.logical_and(c // CHUNK <= r // CHUNK, c >= FRONT_PAD)
    bias_lead = jnp.where(ok, -slope * jnp.abs(r - c).astype(F32), NEG_INF)
    q1z, q2z = split_q(q_ref[0, 0:ROW_BLOCK, :])
    k_lead = k_ref[0, 0:ROW_BLOCK, :]
    v_lead = with_ones(v_ref[0, 0:ROW_BLOCK, :])
    o_ref[0, 0:ROW_BLOCK, :] = finish(softmax_av(q1z, k_lead, v_lead, lambda s: s + bias_lead),
                                      softmax_av(q2z, k_lead, v_lead, lambda s: s + bias_lead))

    r = lax.broadcasted_iota(jnp.int32, (ATT_BLOCK, ATT_BLOCK), 0)
    c = lax.broadcasted_iota(jnp.int32, (ATT_BLOCK, ATT_BLOCK), 1)
    rel = jnp.where(c <= r, c, 2 * r - c).astype(F32)
    dmask_ref[...] = jnp.where(c // CHUNK <= r // CHUNK, slope * rel, NEG_INF)

    for jq in range(n_qblocks):
        qbase = ROW_BLOCK + jq * ATT_BLOCK
        n_keys = qbase + ATT_BLOCK
        q1z, q2z = split_q(q_ref[0, qbase:qbase + ATT_BLOCK, :])
        kk = k_ref[0, 0:n_keys, :]
        vext = with_ones(v_ref[0, 0:n_keys, :])
        col = lax.broadcasted_iota(jnp.int32, (1, qbase), 1)
        col_bias = jnp.where(col >= FRONT_PAD, slope * (col - qbase).astype(F32), NEG_INF)

        def add_bias(s, col_bias=col_bias, qbase=qbase):
            return jnp.concatenate([s[:, :qbase] + col_bias, s[:, qbase:] + dmask_ref[...]], axis=1)

        o_ref[0, qbase:qbase + ATT_BLOCK, :] = finish(softmax_av(q1z, kk, vext, add_bias),
                                                      softmax_av(q2z, kk, vext, add_bias))


def _diff_attention(dq, dk, dv, lq1, lk1, lq2, lk2, norm_g):
    B, LP, W = dq.shape
    small = pl.BlockSpec((1, DIFF_DH), lambda b, h: (0, 0))
    seq = pl.BlockSpec((1, LP, 2 * DIFF_DH), lambda b, h: (b, 0, h))
    return pl.pallas_call(
        _diff_body,
        grid=(B, DIFF_HEADS),
        in_specs=[seq, seq, seq, small, small, small, small,
                  pl.BlockSpec((1, DIFF_DV), lambda b, h: (0, 0))],
        out_specs=seq,
        out_shape=jax.ShapeDtypeStruct((B, LP, W), BF16),
        scratch_shapes=[pltpu.VMEM((ATT_BLOCK, ATT_BLOCK), F32)],
        compiler_params=pltpu.CompilerParams(
            dimension_semantics=("parallel", "parallel"), vmem_limit_bytes=VMEM_LIMIT),
        name="diff_attn",
    )(dq, dk, dv, lq1, lk1, lq2, lk2, norm_g)


def _route_body(og_ref, od_ref, h0_ref, wog_ref, wod_ref, g_ref, b_ref, rw_ref, rb_ref,
                h1_ref, route_ref, cnt_ref, carry_ref, *, rows_per_seq):
    t = pl.program_id(0)

    @pl.when(t == 0)
    def _():
        carry_ref[...] = jnp.zeros_like(carry_ref)

    mix = (jnp.dot(og_ref[...], wog_ref[...], preferred_element_type=F32)
           + jnp.dot(od_ref[...], wod_ref[...], preferred_element_type=F32))
    h1 = _layer_norm(DEEPNORM_ALPHA * h0_ref[...] + mix, g_ref[...], b_ref[...])
    _store_token_tiled(h1_ref, h1)

    hh, hm, _ = _split3_bf16(h1)
    wh, wm, _ = _split3_bf16(rw_ref[...])
    head = jnp.dot(hh, jnp.concatenate([wh, wm], axis=1), preferred_element_type=F32)
    logits = (rb_ref[...] + head[:, :N_EXPERTS] + head[:, N_EXPERTS:]
              + jnp.dot(hm, wh, preferred_element_type=F32))

    T = logits.shape[0]
    lane = lax.broadcasted_iota(jnp.int32, (T, N_EXPERTS), 1)
    grow = t * T + lax.broadcasted_iota(jnp.int32, (T, 1), 0)
    valid = (grow % rows_per_seq) >= FRONT_PAD

    work = logits
    top_v, top_i = [], []
    for _ in range(TOP_K):
        mx = jnp.max(work, axis=-1, keepdims=True)
        idx = jnp.min(jnp.where(work == mx, lane, N_EXPERTS), axis=-1, keepdims=True)
        top_v.append(mx)
        top_i.append(idx)
        work = jnp.where(lane == idx, -jnp.inf, work)
    ex = [jnp.exp(v - top_v[0]) for v in top_v]
    den = ex[0] + ex[1] + ex[2] + ex[3]
    gates = [e / den for e in ex]

    onehot = jnp.zeros((T, N_EXPERTS), F32)
    for idx in top_i:
        onehot = onehot + (lane == idx).astype(F32)
    onehot = jnp.where(valid, onehot, 0.0)

    carry_ref[...] = carry_ref[...] + jnp.sum(onehot, axis=0, keepdims=True)
    cnt_ref[...] = carry_ref[...]

    lane_o = lax.broadcasted_iota(jnp.int32, (T, 128), 1)
    packed = jnp.zeros((T, 128), jnp.int32)
    for k in range(TOP_K):
        packed = jnp.where(lane_o == k, top_i[k], packed)
        packed = jnp.where(lane_o == 2 * TOP_K + k, lax.bitcast_convert_type(gates[k], jnp.int32), packed)
    route_ref[...] = packed


def _outproj_route(og, od, h0, wo_g, wo_d, ln_g, ln_b, rw, rb, rows_per_seq):
    NP, D = h0.shape
    T = ROUTE_TILE

    def rows(width):
        return pl.BlockSpec((T, width), lambda t: (t, 0))

    def full(shape):
        return pl.BlockSpec(shape, lambda t: (0,) * len(shape))

    return pl.pallas_call(
        functools.partial(_route_body, rows_per_seq=rows_per_seq),
        grid=(NP // T,),
        in_specs=[rows(512), rows(512), rows(D), full((512, D)), full((512, D)),
                  full((1, D)), full((1, D)), full((D, N_EXPERTS)), full((1, N_EXPERTS))],
        out_specs=[pl.BlockSpec((T * TOKEN_SUBROWS, 128), lambda t: (t, 0)), rows(128), full((1, N_EXPERTS))],
        out_shape=[jax.ShapeDtypeStruct((NP * TOKEN_SUBROWS, 128), F32),
                   jax.ShapeDtypeStruct((NP, 128), jnp.int32),
                   jax.ShapeDtypeStruct((1, N_EXPERTS), F32)],
        scratch_shapes=[pltpu.VMEM((1, N_EXPERTS), F32)],
        compiler_params=pltpu.CompilerParams(
            dimension_semantics=("arbitrary",), vmem_limit_bytes=VMEM_LIMIT),
        name="outproj_route",
    )(og, od, h0, wo_g, wo_d, ln_g, ln_b, rw, rb)


def _expert_body(order_ref, p0_ref, bstart_ref, nblk_ref, ntot_ref,
                 h1t_hbm, wgu_ref, bg_ref, bu_ref, wd_ref, bd_ref, yk_hbm,
                 wg_s, wu_s, wd_s, xb0, xb1, xb2, yb0, yb1, yb2, zbuf, gsem, ssem, zsem,
                 *, n_tokens, rows_per_seq):
    e = pl.program_id(0)
    n_total = ntot_ref[0]
    R = MOE_ROWS
    TS = TOKEN_SUBROWS
    xb = (xb0, xb1, xb2)
    yb = (yb0, yb1, yb2)
    NB = len(xb)

    def gather_start(b, slot):
        p0 = p0_ref[b]
        for r in range(R):
            tok = order_ref[p0 + r] >> 2
            pltpu.make_async_copy(
                h1t_hbm.at[pl.ds(pl.multiple_of(tok * TS, TS), TS), :],
                xb[slot].at[pl.ds(r * TS, TS), :], gsem.at[slot]).start(priority=r % 2)

    def gather_wait(slot):
        pltpu.make_async_copy(h1t_hbm.at[pl.ds(0, R * TS), :], xb[slot], gsem.at[slot]).wait()

    def scatter_start(b, slot):
        p0 = p0_ref[b]
        for r in range(R):
            row = order_ref[p0 + r] * TS
            pltpu.make_async_copy(
                yb[slot].at[pl.ds(r * TS, TS), :],
                yk_hbm.at[pl.ds(pl.multiple_of(row, TS), TS), :], ssem.at[slot]).start(priority=r % 2)

    def scatter_wait(slot):
        pltpu.make_async_copy(yb[slot], yk_hbm.at[pl.ds(0, R * TS), :], ssem.at[slot]).wait()

    @pl.when(e == 0)
    def _():
        zbuf[...] = jnp.zeros_like(zbuf)
        lead = FRONT_PAD * TS
        fills = [((s * rows_per_seq * TOP_K + k * FRONT_PAD) * TS, lead)
                 for s in range(n_tokens // rows_per_seq) for k in range(TOP_K)]
        copies = [pltpu.make_async_copy(zbuf.at[pl.ds(0, n), :], yk_hbm.at[pl.ds(o, n), :], zsem)
                  for o, n in fills]
        for cp in copies:
            cp.start()
        for cp in copies:
            cp.wait()
        gather_start(0, 0)
        gather_start(jnp.minimum(1, n_total - 1), 1)

    @pl.when(nblk_ref[e] > 0)
    def _():
        r = lax.broadcasted_iota(jnp.int32, (256, 256), 0)
        c = lax.broadcasted_iota(jnp.int32, (256, 256), 1)
        perm = (r == jnp.where(c < 128, 2 * c, 2 * (c - 128) + 1)).astype(BF16)
        for tt in range(2 * D_FF // 256):
            wt = wgu_ref[0, :, tt * 256:(tt + 1) * 256].astype(BF16)
            sp = jnp.dot(wt, perm, preferred_element_type=F32)
            wg_s[:, tt * 128:(tt + 1) * 128] = sp[:, :128].astype(BF16)
            wu_s[:, tt * 128:(tt + 1) * 128] = sp[:, 128:].astype(BF16)
        wd_s[...] = wd_ref[0].astype(BF16)

    def run_block(b, slot):
        prv = (slot + 2) % NB
        gather_wait(slot)
        gather_start(jnp.minimum(b + 2, n_total - 1), prv)
        x = _load_token_tiled(xb[slot], R).astype(BF16)
        gt = jnp.dot(x, wg_s[...], preferred_element_type=F32) + bg_ref[0]
        up = jnp.dot(x, wu_s[...], preferred_element_type=F32) + bu_ref[0]
        gt = jnp.minimum(gt, SWIGLU_LIMIT)
        up = jnp.clip(up, -SWIGLU_LIMIT, SWIGLU_LIMIT)
        act = (up + 1.0) * (gt * jax.nn.sigmoid(SWIGLU_ALPHA * gt))
        y = jnp.dot(act.astype(BF16), wd_s[...], preferred_element_type=F32) + bd_ref[0]
        _store_token_tiled(yb[slot], y)

        @pl.when(b >= 1)
        def _():
            scatter_wait(prv)
        scatter_start(b, slot)

    def block(b, carry):
        for slot in range(NB):
            pl.when(b % NB == slot)(functools.partial(run_block, b, slot))
        return carry

    b0 = bstart_ref[e]
    lax.fori_loop(b0, b0 + nblk_ref[e], block, 0)

    @pl.when(e == pl.num_programs(0) - 1)
    def _():
        last = n_total - 1
        for slot in range(NB):
            @pl.when(last % NB == slot)
            def _(slot=slot):
                gather_wait((slot + 1) % NB)
                gather_wait((slot + 2) % NB)
                scatter_wait(slot)


def _experts(order, blk_p0, blk_start, n_blk, n_total, h1t, w_gate_up, b_gate, b_up, w_down,
             b_down, n_tokens, rows_per_seq):
    D = D_MODEL
    R = MOE_ROWS
    buf = pltpu.VMEM((R * TOKEN_SUBROWS, 128), F32)
    per_expert = lambda e, *_: (e, 0, 0)
    grid_spec = pltpu.PrefetchScalarGridSpec(
        num_scalar_prefetch=5,
        grid=(N_EXPERTS,),
        in_specs=[
            pl.BlockSpec(memory_space=pl.ANY),
            pl.BlockSpec((1, D, 2 * D_FF), per_expert),
            pl.BlockSpec((1, 1, D_FF), per_expert),
            pl.BlockSpec((1, 1, D_FF), per_expert),
            pl.BlockSpec((1, D_FF, D), per_expert),
            pl.BlockSpec((1, 1, D), per_expert),
        ],
        out_specs=pl.BlockSpec(memory_space=pl.ANY),
        scratch_shapes=[pltpu.VMEM((D, D_FF), BF16), pltpu.VMEM((D, D_FF), BF16), pltpu.VMEM((D_FF, D), BF16),
                        buf, buf, buf, buf, buf, buf, pltpu.VMEM((FRONT_PAD * TOKEN_SUBROWS, 128), F32),
                        pltpu.SemaphoreType.DMA((3,)), pltpu.SemaphoreType.DMA((3,)), pltpu.SemaphoreType.DMA],
    )
    out_rows = TOP_K * n_tokens * TOKEN_SUBROWS
    return pl.pallas_call(
        functools.partial(_expert_body, n_tokens=n_tokens, rows_per_seq=rows_per_seq),
        grid_spec=grid_spec,
        out_shape=jax.ShapeDtypeStruct((out_rows, 128), F32),
        compiler_params=pltpu.CompilerParams(
            dimension_semantics=("arbitrary",), vmem_limit_bytes=EXPERT_VMEM_LIMIT),
        name="experts",
    )(order, blk_p0, blk_start, n_blk, n_total, h1t, w_gate_up, b_gate, b_up, w_down, b_down)


def _combine_body(h1_ref, y0_ref, y1_ref, y2_ref, y3_ref, route_ref, g_ref, b_ref, o_ref):
    route = route_ref[...]
    y = jnp.zeros((ROW_BLOCK, D_MODEL), F32)
    for k, yk_ref in enumerate((y0_ref, y1_ref, y2_ref, y3_ref)):
        gate = lax.bitcast_convert_type(route[:, 2 * TOP_K + k:2 * TOP_K + k + 1], F32)
        y = y + gate * _load_token_tiled(yk_ref.reshape(ROW_BLOCK * TOKEN_SUBROWS, 128), ROW_BLOCK)
    h1 = _load_token_tiled(h1_ref, ROW_BLOCK)
    o_ref[0] = _layer_norm(DEEPNORM_ALPHA * h1 + y, g_ref[...], b_ref[...])


def _combine(h1t, yk, route, ln_g, ln_b, batch, seq_len):
    nb = seq_len // ROW_BLOCK
    blocks_per_seq = nb + 1
    tile_rows = ROW_BLOCK * TOKEN_SUBROWS
    yk = yk.reshape(-1, TOP_K, TOKEN_SUBROWS, 128)

    def token_block(b, j):
        return b * blocks_per_seq + j + 1

    def plane(k):
        return pl.BlockSpec((ROW_BLOCK, None, TOKEN_SUBROWS, 128), lambda b, j: (token_block(b, j), k, 0, 0))

    tiles = pl.BlockSpec((tile_rows, 128), lambda b, j: (token_block(b, j), 0))
    vec = pl.BlockSpec((1, D_MODEL), lambda b, j: (0, 0))
    return pl.pallas_call(
        _combine_body,
        grid=(batch, nb),
        in_specs=[tiles, plane(0), plane(1), plane(2), plane(3),
                  pl.BlockSpec((ROW_BLOCK, 128), lambda b, j: (token_block(b, j), 0)), vec, vec],
        out_specs=pl.BlockSpec((1, ROW_BLOCK, D_MODEL), lambda b, j: (b, j, 0)),
        out_shape=jax.ShapeDtypeStruct((batch, seq_len, D_MODEL), F32),
        compiler_params=pltpu.CompilerParams(
            dimension_semantics=("parallel", "parallel"), vmem_limit_bytes=VMEM_LIMIT),
        name="combine",
    )(h1t, yk, yk, yk, yk, route, ln_g, ln_b)


def kernel(x, meta_tokens, ln_emb_g, ln_emb_b, w_in, gla_wa2, gla_ba, gla_norm_g, diff_lambda_q1, diff_lambda_k1, diff_lambda_q2, diff_lambda_k2, diff_norm_g, w_out, ln1_g, ln1_b, router_w, router_b, w_gate_up, b_gate_up, w_down, b_down, ln2_g, ln2_b):
    B, S, D = x.shape
    LP = S + ROW_BLOCK
    NP = B * LP
    row = lambda v: v.reshape(1, -1)

    w = w_in[0]
    w_main = jnp.concatenate([w[:, :1536], w[:, 1552:]], axis=1).astype(BF16)
    w_ga = w[:, 1536:1552].astype(BF16)
    meta_pad = jnp.pad(meta_tokens, ((FRONT_PAD, 0), (0, 0)))

    h0, gq, gk, gv, gr, la, dq, dk, dv = _inproj(
        x, meta_pad, row(ln_emb_g), row(ln_emb_b), w_main, w_ga,
        gla_wa2[0].astype(BF16), row(gla_ba[0]))

    og = _gla(gq, gk, gv, gr, la, row(gla_norm_g[0]))
    od = _diff_attention(dq, dk, dv, row(diff_lambda_q1[0]), row(diff_lambda_k1[0]),
                         row(diff_lambda_q2[0]), row(diff_lambda_k2[0]), row(diff_norm_g[0]))

    wo = w_out[0].astype(BF16)
    h1t, route, counts = _outproj_route(
        og.reshape(NP, 512), od.reshape(NP, 512), h0.reshape(NP, D), wo[:512], wo[512:],
        row(ln1_g[0]), row(ln1_b[0]), router_w[0], row(router_b[0]), LP)

    R = MOE_ROWS
    i32 = jnp.int32
    counts = counts[0].astype(i32)
    n_blk = (counts + R - 1) // R
    blk_end = jnp.cumsum(n_blk)
    blk_start = blk_end - n_blk
    grp_start = jnp.cumsum(counts) - counts
    n_total = blk_end[-1:]
    n_assign_max = B * (S + N_META) * TOP_K
    max_blocks = (n_assign_max + N_EXPERTS * (R - 1)) // R
    g = jnp.minimum(jnp.arange(max_blocks, dtype=i32), n_total[0] - 1)
    is_e = (jnp.minimum(jnp.sum(g[:, None] >= blk_end[None, :], axis=1), N_EXPERTS - 1)[:, None]
            == jnp.arange(N_EXPERTS)[None, :])
    pick = lambda v: jnp.sum(jnp.where(is_e, v[None, :], 0), axis=1)
    local = (g - pick(blk_start)) * R
    blk_p0 = (pick(grp_start) + local).astype(i32)

    tok_valid = (jnp.arange(NP, dtype=i32) % LP) >= FRONT_PAD
    flat_e = jnp.where(tok_valid[:, None], route[:, :TOP_K], N_EXPERTS).reshape(-1)
    id_bits = (NP * TOP_K - 1).bit_length()
    keyed = jnp.sort(flat_e * (1 << id_bits) + jnp.arange(NP * TOP_K, dtype=i32))
    assert R <= B * FRONT_PAD * TOP_K
    order = keyed & ((1 << id_bits) - 1)

    bgu = b_gate_up[0].reshape(N_EXPERTS, D_FF, 2)
    yk = _experts(order, blk_p0, blk_start.astype(i32), n_blk.astype(i32), n_total.astype(i32),
                  h1t, w_gate_up[0],
                  bgu[:, :, 0].reshape(N_EXPERTS, 1, D_FF), bgu[:, :, 1].reshape(N_EXPERTS, 1, D_FF),
                  w_down[0], b_down[0].reshape(N_EXPERTS, 1, D), NP, LP)

    return _combine(h1t, yk, route, row(ln2_g[0]), row(ln2_b[0]), B, S)
```

```python
import functools
import math

import jax
import jax.numpy as jnp
from jax import lax
from jax.experimental import pallas as pl
from jax.experimental.pallas import tpu as pltpu

F32 = jnp.float32
BF16 = jnp.bfloat16

D_MODEL = 1024
N_META = 16
ROW_BLOCK = 128
FRONT_PAD = ROW_BLOCK - N_META
CHUNK = 64

GLA_HEADS = 4
GLA_DK = 64
GLA_DV = 128
GLA_RANK = 16
GLA_TAU = 16.0
DIFF_HEADS = 4
DIFF_DH = 64
DIFF_DV = 128

N_EXPERTS = 32
TOP_K = 4
D_FF = 1024
SWIGLU_LIMIT = 7.0
SWIGLU_ALPHA = 1.702
MOE_ROWS = 256

DEEPNORM_ALPHA = 2.0 ** 0.25
LAMBDA_INIT = 0.8 - 0.6 * math.exp(0.0)
LN_EPS = 1e-5
NEG_INF = -1e30
LOG2E = math.log2(math.e)

ROUTE_TILE = 512
VMEM_LIMIT = 48 * 1024 * 1024
EXPERT_VMEM_LIMIT = 56 * 1024 * 1024

_GQ, _GK, _GV, _GR, _DQ, _DK, _DV = 0, 256, 512, 1024, 1536, 2048, 2560
_MAIN_WIDTH = 3072


def _layer_norm(x, g, b):
    mu = jnp.mean(x, axis=-1, keepdims=True)
    xc = x - mu
    var = jnp.mean(xc * xc, axis=-1, keepdims=True)
    return xc * lax.rsqrt(var + LN_EPS) * g + b


def _split3_bf16(x):
    hi = x.astype(BF16)
    r1 = x - hi.astype(F32)
    mid = r1.astype(BF16)
    lo = (r1 - mid.astype(F32)).astype(BF16)
    return hi, mid, lo


TOKEN_SUBROWS = D_MODEL // 128


def _store_token_tiled(ref, val, first_token=0):
    n = val.shape[0]
    for s in range(TOKEN_SUBROWS):
        ref[pl.ds(first_token * TOKEN_SUBROWS + s, n, stride=TOKEN_SUBROWS), :] = val[:, s * 128:(s + 1) * 128]


def _load_token_tiled(ref, n, first_row=0):
    return jnp.concatenate(
        [ref[pl.ds(first_row + s, n, stride=TOKEN_SUBROWS), :] for s in range(TOKEN_SUBROWS)], axis=1)


def _inproj_body(xa_ref, xb_ref, meta_ref, g_ref, b_ref, w_ref, wga_ref, wa2_ref, ba_ref,
                 h0_ref, gq_ref, gk_ref, gv_ref, gr_ref, la_ref, dq_ref, dk_ref, dv_ref):
    j = pl.program_id(1)
    rows = lax.broadcasted_iota(jnp.int32, (2 * ROW_BLOCK, 1), 0)
    valid = jnp.logical_or(j > 0, rows >= FRONT_PAD)
    xin = jnp.concatenate([jnp.where(j > 0, xa_ref[0], meta_ref[...]), xb_ref[0]], axis=0)
    h = jnp.where(valid, _layer_norm(xin, g_ref[...], b_ref[...]), 0.0)
    h0_ref[0] = h
    hb = h.astype(BF16)

    def proj(off, width):
        return jnp.dot(hb, w_ref[:, off:off + width], preferred_element_type=F32)

    gq_ref[0] = (proj(_GQ, 256) * GLA_DK ** -0.5).astype(BF16)
    gk_ref[0] = proj(_GK, 256)
    gv_ref[0] = proj(_GV, 512).astype(BF16)
    gr_ref[0] = proj(_GR, 512).astype(BF16)
    dq_ref[0] = (proj(_DQ, 512) * (DIFF_DH ** -0.5 * LOG2E)).astype(BF16)
    dk_ref[0] = proj(_DK, 512).astype(BF16)
    dv_ref[0] = proj(_DV, 512).astype(BF16)

    a_lr = jnp.dot(hb, wga_ref[...], preferred_element_type=F32)
    z = jnp.dot(a_lr.astype(BF16), wa2_ref[...], preferred_element_type=F32) + ba_ref[...]
    log_sig = jnp.minimum(z, 0.0) - jnp.log1p(jnp.exp(-jnp.abs(z)))
    la_ref[0] = jnp.where(valid, log_sig / GLA_TAU, 0.0)


def _inproj(x, meta_pad, ln_g, ln_b, w_main, w_ga, wa2, ba):
    B, S, D = x.shape
    nb = S // ROW_BLOCK + 1
    LP = nb * ROW_BLOCK

    n_frame_blocks = S // ROW_BLOCK
    n_steps = pl.cdiv(LP, 2 * ROW_BLOCK)

    def row_spec(width):
        return pl.BlockSpec((1, 2 * ROW_BLOCK, width), lambda b, j: (b, j, 0))

    def frames(shift):
        return pl.BlockSpec((1, ROW_BLOCK, D),
                            lambda b, j: (b, jnp.clip(2 * j + shift, 0, n_frame_blocks - 1), 0))

    def full(shape):
        return pl.BlockSpec(shape, lambda b, j: (0,) * len(shape))

    out_shapes = [
        jax.ShapeDtypeStruct((B, LP, D), F32),
        jax.ShapeDtypeStruct((B, LP, 256), BF16),
        jax.ShapeDtypeStruct((B, LP, 256), F32),
        jax.ShapeDtypeStruct((B, LP, 512), BF16),
        jax.ShapeDtypeStruct((B, LP, 512), BF16),
        jax.ShapeDtypeStruct((B, LP, 256), F32),
        jax.ShapeDtypeStruct((B, LP, 512), BF16),
        jax.ShapeDtypeStruct((B, LP, 512), BF16),
        jax.ShapeDtypeStruct((B, LP, 512), BF16),
    ]
    return pl.pallas_call(
        _inproj_body,
        grid=(B, n_steps),
        in_specs=[
            frames(-1), frames(0),
            full((ROW_BLOCK, D)), full((1, D)), full((1, D)),
            full((D, _MAIN_WIDTH)), full((D, GLA_RANK)), full((GLA_RANK, 256)), full((1, 256)),
        ],
        out_specs=[row_spec(s.shape[-1]) for s in out_shapes],
        out_shape=out_shapes,
        compiler_params=pltpu.CompilerParams(
            dimension_semantics=("parallel", "parallel"), vmem_limit_bytes=VMEM_LIMIT),
        name="inproj",
    )(x, x, meta_pad, ln_g, ln_b, w_main, w_ga, wa2, ba)


def _gla_body(q_ref, k_ref, v_ref, r_ref, la_ref, g_ref, o_ref):
    n_groups = q_ref.shape[1] // ROW_BLOCK
    ri = lax.broadcasted_iota(jnp.int32, (ROW_BLOCK, ROW_BLOCK), 0)
    ci = lax.broadcasted_iota(jnp.int32, (ROW_BLOCK, ROW_BLOCK), 1)
    later = jnp.logical_and(ri // CHUNK == ci // CHUNK, ci > ri).astype(BF16)
    sr = lax.broadcasted_iota(jnp.int32, (GLA_HEADS * GLA_DV, GLA_HEADS * GLA_DK), 0)
    sc = lax.broadcasted_iota(jnp.int32, (GLA_HEADS * GLA_DV, GLA_HEADS * GLA_DK), 1)
    same_head = sr // GLA_DV == sc // GLA_DK
    gain = g_ref[...]
    st = jnp.zeros((GLA_HEADS * GLA_DV, GLA_HEADS * GLA_DK), F32)

    for grp in range(n_groups):
        g0 = grp * ROW_BLOCK
        la = la_ref[0, g0:g0 + ROW_BLOCK, :]
        hi, mid, lo = _split3_bf16(la)
        suffix = (jnp.dot(later, hi, preferred_element_type=F32)
                  + jnp.dot(later, mid, preferred_element_type=F32)
                  + jnp.dot(later, lo, preferred_element_type=F32))
        kdec = (k_ref[0, g0:g0 + ROW_BLOCK, :] * jnp.exp(suffix)).astype(BF16)
        for half in range(ROW_BLOCK // CHUNK):
            f = half * CHUNK
            r0 = g0 + f
            tot = suffix[f:f + 1, :] + la[f:f + 1, :]
            upd = lax.dot_general(v_ref[0, r0:r0 + CHUNK, :], kdec[f:f + CHUNK, :],
                                  (((0,), (0,)), ((), ())), preferred_element_type=F32)
            st = st * jnp.exp(tot) + jnp.where(same_head, upd, 0.0)
            o = lax.dot_general(q_ref[0, r0:r0 + CHUNK, :], st.astype(BF16),
                                (((1,), (1,)), ((), ())), preferred_element_type=F32)
            r = r_ref[0, r0:r0 + CHUNK, :].astype(F32)
            for h in range(GLA_HEADS):
                oh = o[:, h * GLA_DV:(h + 1) * GLA_DV]
                rh = r[:, h * GLA_DV:(h + 1) * GLA_DV]
                ms = jnp.mean(oh * oh, axis=-1, keepdims=True)
                out = oh * lax.rsqrt(ms + LN_EPS) * gain * (rh * jax.nn.sigmoid(rh))
                o_ref[0, r0:r0 + CHUNK, h * GLA_DV:(h + 1) * GLA_DV] = out.astype(BF16)


def _gla(gq, gk, gv, gr, la, norm_g):
    B, LP, _ = gq.shape

    def seq(width):
        return pl.BlockSpec((1, LP, width), lambda b: (b, 0, 0))

    return pl.pallas_call(
        _gla_body,
        grid=(B,),
        in_specs=[seq(256), seq(256), seq(512), seq(512), seq(256),
                  pl.BlockSpec((1, GLA_DV), lambda b: (0, 0))],
        out_specs=seq(512),
        out_shape=jax.ShapeDtypeStruct((B, LP, 512), BF16),
        compiler_params=pltpu.CompilerParams(
            dimension_semantics=("parallel",), vmem_limit_bytes=VMEM_LIMIT),
        name="gla",
    )(gq, gk, gv, gr, la, norm_g)


ATT_BLOCK = 256


def _diff_body(q_ref, k_ref, v_ref, lq1_ref, lk1_ref, lq2_ref, lk2_ref, g_ref, o_ref,
               dmask_ref):
    h = pl.program_id(1)
    slope = jnp.where(h == 0, 2.0 ** -2, jnp.where(h == 1, 2.0 ** -4, jnp.where(h == 2, 2.0 ** -6, 2.0 ** -8)))
    slope = slope.astype(F32) * LOG2E
    lam = (jnp.exp(jnp.sum(lq1_ref[...] * lk1_ref[...], axis=-1, keepdims=True))
           - jnp.exp(jnp.sum(lq2_ref[...] * lk2_ref[...], axis=-1, keepdims=True)) + LAMBDA_INIT)
    gain = g_ref[...] * (1.0 - LAMBDA_INIT)
    n_qblocks = (q_ref.shape[1] - ROW_BLOCK) // ATT_BLOCK
    nt = (((1,), (1,)), ((), ()))

    def split_q(q):
        lane = lax.broadcasted_iota(jnp.int32, q.shape, 1)
        zero = jnp.zeros_like(q)
        return jnp.where(lane < DIFF_DH, q, zero), jnp.where(lane >= DIFF_DH, q, zero)

    def with_ones(v):
        return jnp.concatenate([v, jnp.ones_like(v)], axis=1)

    def softmax_av(qz, kk, vext, add_bias):
        s = add_bias(lax.dot_general(qz, kk, nt, preferred_element_type=F32))
        m = jnp.max(s, axis=-1, keepdims=True)
        p = jnp.exp2(s - m).astype(BF16)
        return jnp.dot(p, vext, preferred_element_type=F32)

    def finish(a1, a2):
        o = a1[:, :DIFF_DV] / a1[:, DIFF_DV:] - lam * (a2[:, :DIFF_DV] / a2[:, DIFF_DV:])
        ms = jnp.mean(o * o, axis=-1, keepdims=True)
        return (o * lax.rsqrt(ms + LN_EPS) * gain).astype(BF16)

    r = lax.broadcasted_iota(jnp.int32, (ROW_BLOCK, ROW_BLOCK), 0)
    c = lax.broadcasted_iota(jnp.int32, (ROW_BLOCK, ROW_BLOCK), 1)
    ok = jnp.logical_and(c // CHUNK <= r // CHUNK, c >= FRONT_PAD)
    bias_lead = jnp.where(ok, -slope * jnp.abs(r - c).astype(F32), NEG_INF)
    q1z, q2z = split_q(q_ref[0, 0:ROW_BLOCK, :])
    k_lead = k_ref[0, 0:ROW_BLOCK, :]
    v_lead = with_ones(v_ref[0, 0:ROW_BLOCK, :])
    o_ref[0, 0:ROW_BLOCK, :] = finish(softmax_av(q1z, k_lead, v_lead, lambda s: s + bias_lead),
                                      softmax_av(q2z, k_lead, v_lead, lambda s: s + bias_lead))

    r = lax.broadcasted_iota(jnp.int32, (ATT_BLOCK, ATT_BLOCK), 0)
    c = lax.broadcasted_iota(jnp.int32, (ATT_BLOCK, ATT_BLOCK), 1)
    rel = jnp.where(c <= r, c, 2 * r - c).astype(F32)
    dmask_ref[...] = jnp.where(c // CHUNK <= r // CHUNK, slope * rel, NEG_INF)

    for jq in range(n_qblocks):
        qbase = ROW_BLOCK + jq * ATT_BLOCK
        n_keys = qbase + ATT_BLOCK
        q1z, q2z = split_q(q_ref[0, qbase:qbase + ATT_BLOCK, :])
        kk = k_ref[0, 0:n_keys, :]
        vext = with_ones(v_ref[0, 0:n_keys, :])
        col = lax.broadcasted_iota(jnp.int32, (1, qbase), 1)
        col_bias = jnp.where(col >= FRONT_PAD, slope * (col - qbase).astype(F32), NEG_INF)

        def add_bias(s, col_bias=col_bias, qbase=qbase):
            return jnp.concatenate([s[:, :qbase] + col_bias, s[:, qbase:] + dmask_ref[...]], axis=1)

        o_ref[0, qbase:qbase + ATT_BLOCK, :] = finish(softmax_av(q1z, kk, vext, add_bias),
                                                      softmax_av(q2z, kk, vext, add_bias))


def _diff_attention(dq, dk, dv, lq1, lk1, lq2, lk2, norm_g):
    B, LP, W = dq.shape
    small = pl.BlockSpec((1, DIFF_DH), lambda b, h: (0, 0))
    seq = pl.BlockSpec((1, LP, 2 * DIFF_DH), lambda b, h: (b, 0, h))
    return pl.pallas_call(
        _diff_body,
        grid=(B, DIFF_HEADS),
        in_specs=[seq, seq, seq, small, small, small, small,
                  pl.BlockSpec((1, DIFF_DV), lambda b, h: (0, 0))],
        out_specs=seq,
        out_shape=jax.ShapeDtypeStruct((B, LP, W), BF16),
        scratch_shapes=[pltpu.VMEM((ATT_BLOCK, ATT_BLOCK), F32)],
        compiler_params=pltpu.CompilerParams(
            dimension_semantics=("parallel", "parallel"), vmem_limit_bytes=VMEM_LIMIT),
        name="diff_attn",
    )(dq, dk, dv, lq1, lk1, lq2, lk2, norm_g)


def _route_body(og_ref, od_ref, h0_ref, wog_ref, wod_ref, g_ref, b_ref, rw_ref, rb_ref,
                h1_ref, route_ref, cnt_ref, carry_ref, *, rows_per_seq):
    t = pl.program_id(0)

    @pl.when(t == 0)
    def _():
        carry_ref[...] = jnp.zeros_like(carry_ref)

    mix = (jnp.dot(og_ref[...], wog_ref[...], preferred_element_type=F32)
           + jnp.dot(od_ref[...], wod_ref[...], preferred_element_type=F32))
    h1 = _layer_norm(DEEPNORM_ALPHA * h0_ref[...] + mix, g_ref[...], b_ref[...])
    _store_token_tiled(h1_ref, h1)

    hh, hm, _ = _split3_bf16(h1)
    wh, wm, _ = _split3_bf16(rw_ref[...])
    head = jnp.dot(hh, jnp.concatenate([wh, wm], axis=1), preferred_element_type=F32)
    logits = (rb_ref[...] + head[:, :N_EXPERTS] + head[:, N_EXPERTS:]
              + jnp.dot(hm, wh, preferred_element_type=F32))

    T = logits.shape[0]
    lane = lax.broadcasted_iota(jnp.int32, (T, N_EXPERTS), 1)
    grow = t * T + lax.broadcasted_iota(jnp.int32, (T, 1), 0)
    valid = (grow % rows_per_seq) >= FRONT_PAD

    work = logits
    top_v, top_i = [], []
    for _ in range(TOP_K):
        mx = jnp.max(work, axis=-1, keepdims=True)
        idx = jnp.min(jnp.where(work == mx, lane, N_EXPERTS), axis=-1, keepdims=True)
        top_v.append(mx)
        top_i.append(idx)
        work = jnp.where(lane == idx, -jnp.inf, work)
    ex = [jnp.exp(v - top_v[0]) for v in top_v]
    den = ex[0] + ex[1] + ex[2] + ex[3]
    gates = [e / den for e in ex]

    onehot = jnp.zeros((T, N_EXPERTS), F32)
    for idx in top_i:
        onehot = onehot + (lane == idx).astype(F32)
    onehot = jnp.where(valid, onehot, 0.0)

    carry_ref[...] = carry_ref[...] + jnp.sum(onehot, axis=0, keepdims=True)
    cnt_ref[...] = carry_ref[...]

    lane_o = lax.broadcasted_iota(jnp.int32, (T, 128), 1)
    packed = jnp.zeros((T, 128), jnp.int32)
    for k in range(TOP_K):
        packed = jnp.where(lane_o == k, top_i[k], packed)
        packed = jnp.where(lane_o == 2 * TOP_K + k, lax.bitcast_convert_type(gates[k], jnp.int32), packed)
    route_ref[...] = packed


def _outproj_route(og, od, h0, wo_g, wo_d, ln_g, ln_b, rw, rb, rows_per_seq):
    NP, D = h0.shape
    T = ROUTE_TILE

    def rows(width):
        return pl.BlockSpec((T, width), lambda t: (t, 0))

    def full(shape):
        return pl.BlockSpec(shape, lambda t: (0,) * len(shape))

    return pl.pallas_call(
        functools.partial(_route_body, rows_per_seq=rows_per_seq),
        grid=(NP // T,),
        in_specs=[rows(512), rows(512), rows(D), full((512, D)), full((512, D)),
                  full((1, D)), full((1, D)), full((D, N_EXPERTS)), full((1, N_EXPERTS))],
        out_specs=[pl.BlockSpec((T * TOKEN_SUBROWS, 128), lambda t: (t, 0)), rows(128), full((1, N_EXPERTS))],
        out_shape=[jax.ShapeDtypeStruct((NP * TOKEN_SUBROWS, 128), F32),
                   jax.ShapeDtypeStruct((NP, 128), jnp.int32),
                   jax.ShapeDtypeStruct((1, N_EXPERTS), F32)],
        scratch_shapes=[pltpu.VMEM((1, N_EXPERTS), F32)],
        compiler_params=pltpu.CompilerParams(
            dimension_semantics=("arbitrary",), vmem_limit_bytes=VMEM_LIMIT),
        name="outproj_route",
    )(og, od, h0, wo_g, wo_d, ln_g, ln_b, rw, rb)


def _expert_body(order_ref, p0_ref, bstart_ref, nblk_ref, ntot_ref,
                 h1t_hbm, wgu_ref, bg_ref, bu_ref, wd_ref, bd_ref, yk_hbm,
                 wg_s, wu_s, wd_s, xb0, xb1, xb2, yb0, yb1, yb2, zbuf, gsem, ssem, zsem,
                 *, n_tokens, rows_per_seq):
    e = pl.program_id(0)
    n_total = ntot_ref[0]
    R = MOE_ROWS
    TS = TOKEN_SUBROWS
    xb = (xb0, xb1, xb2)
    yb = (yb0, yb1, yb2)
    NB = len(xb)

    def gather_start(b, slot):
        p0 = p0_ref[b]
        for r in range(R):
            tok = order_ref[p0 + r] >> 2
            pltpu.make_async_copy(
                h1t_hbm.at[pl.ds(pl.multiple_of(tok * TS, TS), TS), :],
                xb[slot].at[pl.ds(r * TS, TS), :], gsem.at[slot]).start(priority=r % 2)

    def gather_wait(slot):
        pltpu.make_async_copy(h1t_hbm.at[pl.ds(0, R * TS), :], xb[slot], gsem.at[slot]).wait()

    def scatter_start(b, slot):
        p0 = p0_ref[b]
        for r in range(R):
            row = order_ref[p0 + r] * TS
            pltpu.make_async_copy(
                yb[slot].at[pl.ds(r * TS, TS), :],
                yk_hbm.at[pl.ds(pl.multiple_of(row, TS), TS), :], ssem.at[slot]).start(priority=r % 2)

    def scatter_wait(slot):
        pltpu.make_async_copy(yb[slot], yk_hbm.at[pl.ds(0, R * TS), :], ssem.at[slot]).wait()

    @pl.when(e == 0)
    def _():
        zbuf[...] = jnp.zeros_like(zbuf)
        lead = FRONT_PAD * TS
        fills = [((s * rows_per_seq * TOP_K + k * FRONT_PAD) * TS, lead)
                 for s in range(n_tokens // rows_per_seq) for k in range(TOP_K)]
        copies = [pltpu.make_async_copy(zbuf.at[pl.ds(0, n), :], yk_hbm.at[pl.ds(o, n), :], zsem)
                  for o, n in fills]
        for cp in copies:
            cp.start()
        for cp in copies:
            cp.wait()
        gather_start(0, 0)
        gather_start(jnp.minimum(1, n_total - 1), 1)

    @pl.when(nblk_ref[e] > 0)
    def _():
        r = lax.broadcasted_iota(jnp.int32, (256, 256), 0)
        c = lax.broadcasted_iota(jnp.int32, (256, 256), 1)
        perm = (r == jnp.where(c < 128, 2 * c, 2 * (c - 128) + 1)).astype(BF16)
        for tt in range(2 * D_FF // 256):
            wt = wgu_ref[0, :, tt * 256:(tt + 1) * 256].astype(BF16)
            sp = jnp.dot(wt, perm, preferred_element_type=F32)
            wg_s[:, tt * 128:(tt + 1) * 128] = sp[:, :128].astype(BF16)
            wu_s[:, tt * 128:(tt + 1) * 128] = sp[:, 128:].astype(BF16)
        wd_s[...] = wd_ref[0].astype(BF16)

    def run_block(b, slot):
        prv = (slot + 2) % NB
        gather_wait(slot)
        gather_start(jnp.minimum(b + 2, n_total - 1), prv)
        x = _load_token_tiled(xb[slot], R).astype(BF16)
        gt = jnp.dot(x, wg_s[...], preferred_element_type=F32) + bg_ref[0]
        up = jnp.dot(x, wu_s[...], preferred_element_type=F32) + bu_ref[0]
        gt = jnp.minimum(gt, SWIGLU_LIMIT)
        up = jnp.clip(up, -SWIGLU_LIMIT, SWIGLU_LIMIT)
        act = (up + 1.0) * (gt * jax.nn.sigmoid(SWIGLU_ALPHA * gt))
        y = jnp.dot(act.astype(BF16), wd_s[...], preferred_element_type=F32) + bd_ref[0]
        _store_token_tiled(yb[slot], y)

        @pl.when(b >= 1)
        def _():
            scatter_wait(prv)
        scatter_start(b, slot)

    def block(b, carry):
        for slot in range(NB):
            pl.when(b % NB == slot)(functools.partial(run_block, b, slot))
        return carry

    b0 = bstart_ref[e]
    lax.fori_loop(b0, b0 + nblk_ref[e], block, 0)

    @pl.when(e == pl.num_programs(0) - 1)
    def _():
        last = n_total - 1
        for slot in range(NB):
            @pl.when(last % NB == slot)
            def _(slot=slot):
                gather_wait((slot + 1) % NB)
                gather_wait((slot + 2) % NB)
                scatter_wait(slot)


def _experts(order, blk_p0, blk_start, n_blk, n_total, h1t, w_gate_up, b_gate, b_up, w_down,
             b_down, n_tokens, rows_per_seq):
    D = D_MODEL
    R = MOE_ROWS
    buf = pltpu.VMEM((R * TOKEN_SUBROWS, 128), F32)
    per_expert = lambda e, *_: (e, 0, 0)
    grid_spec = pltpu.PrefetchScalarGridSpec(
        num_scalar_prefetch=5,
        grid=(N_EXPERTS,),
        in_specs=[
            pl.BlockSpec(memory_space=pl.ANY),
            pl.BlockSpec((1, D, 2 * D_FF), per_expert),
            pl.BlockSpec((1, 1, D_FF), per_expert),
            pl.BlockSpec((1, 1, D_FF), per_expert),
            pl.BlockSpec((1, D_FF, D), per_expert),
            pl.BlockSpec((1, 1, D), per_expert),
        ],
        out_specs=pl.BlockSpec(memory_space=pl.ANY),
        scratch_shapes=[pltpu.VMEM((D, D_FF), BF16), pltpu.VMEM((D, D_FF), BF16), pltpu.VMEM((D_FF, D), BF16),
                        buf, buf, buf, buf, buf, buf, pltpu.VMEM((FRONT_PAD * TOKEN_SUBROWS, 128), F32),
                        pltpu.SemaphoreType.DMA((3,)), pltpu.SemaphoreType.DMA((3,)), pltpu.SemaphoreType.DMA],
    )
    out_rows = TOP_K * n_tokens * TOKEN_SUBROWS
    return pl.pallas_call(
        functools.partial(_expert_body, n_tokens=n_tokens, rows_per_seq=rows_per_seq),
        grid_spec=grid_spec,
        out_shape=jax.ShapeDtypeStruct((out_rows, 128), F32),
        compiler_params=pltpu.CompilerParams(
            dimension_semantics=("arbitrary",), vmem_limit_bytes=EXPERT_VMEM_LIMIT),
        name="experts",
    )(order, blk_p0, blk_start, n_blk, n_total, h1t, w_gate_up, b_gate, b_up, w_down, b_down)


def _combine_body(h1_ref, y0_ref, y1_ref, y2_ref, y3_ref, route_ref, g_ref, b_ref, o_ref):
    route = route_ref[...]
    y = jnp.zeros((ROW_BLOCK, D_MODEL), F32)
    for k, yk_ref in enumerate((y0_ref, y1_ref, y2_ref, y3_ref)):
        gate = lax.bitcast_convert_type(route[:, 2 * TOP_K + k:2 * TOP_K + k + 1], F32)
        y = y + gate * _load_token_tiled(yk_ref.reshape(ROW_BLOCK * TOKEN_SUBROWS, 128), ROW_BLOCK)
    h1 = _load_token_tiled(h1_ref, ROW_BLOCK)
    o_ref[0] = _layer_norm(DEEPNORM_ALPHA * h1 + y, g_ref[...], b_ref[...])


def _combine(h1t, yk, route, ln_g, ln_b, batch, seq_len):
    nb = seq_len // ROW_BLOCK
    blocks_per_seq = nb + 1
    tile_rows = ROW_BLOCK * TOKEN_SUBROWS
    yk = yk.reshape(-1, TOP_K, TOKEN_SUBROWS, 128)

    def token_block(b, j):
        return b * blocks_per_seq + j + 1

    def plane(k):
        return pl.BlockSpec((ROW_BLOCK, None, TOKEN_SUBROWS, 128), lambda b, j: (token_block(b, j), k, 0, 0))

    tiles = pl.BlockSpec((tile_rows, 128), lambda b, j: (token_block(b, j), 0))
    vec = pl.BlockSpec((1, D_MODEL), lambda b, j: (0, 0))
    return pl.pallas_call(
        _combine_body,
        grid=(batch, nb),
        in_specs=[tiles, plane(0), plane(1), plane(2), plane(3),
                  pl.BlockSpec((ROW_BLOCK, 128), lambda b, j: (token_block(b, j), 0)), vec, vec],
        out_specs=pl.BlockSpec((1, ROW_BLOCK, D_MODEL), lambda b, j: (b, j, 0)),
        out_shape=jax.ShapeDtypeStruct((batch, seq_len, D_MODEL), F32),
        compiler_params=pltpu.CompilerParams(
            dimension_semantics=("parallel", "parallel"), vmem_limit_bytes=VMEM_LIMIT),
        name="combine",
    )(h1t, yk, yk, yk, yk, route, ln_g, ln_b)


def kernel(x, meta_tokens, ln_emb_g, ln_emb_b, w_in, gla_wa2, gla_ba, gla_norm_g, diff_lambda_q1, diff_lambda_k1, diff_lambda_q2, diff_lambda_k2, diff_norm_g, w_out, ln1_g, ln1_b, router_w, router_b, w_gate_up, b_gate_up, w_down, b_down, ln2_g, ln2_b):
    B, S, D = x.shape
    LP = S + ROW_BLOCK
    NP = B * LP
    row = lambda v: v.reshape(1, -1)

    w = w_in[0]
    w_main = jnp.concatenate([w[:, :1536], w[:, 1552:]], axis=1).astype(BF16)
    w_ga = w[:, 1536:1552].astype(BF16)
    meta_pad = jnp.pad(meta_tokens, ((FRONT_PAD, 0), (0, 0)))

    h0, gq, gk, gv, gr, la, dq, dk, dv = _inproj(
        x, meta_pad, row(ln_emb_g), row(ln_emb_b), w_main, w_ga,
        gla_wa2[0].astype(BF16), row(gla_ba[0]))

    og = _gla(gq, gk, gv, gr, la, row(gla_norm_g[0]))
    od = _diff_attention(dq, dk, dv, row(diff_lambda_q1[0]), row(diff_lambda_k1[0]),
                         row(diff_lambda_q2[0]), row(diff_lambda_k2[0]), row(diff_norm_g[0]))

    wo = w_out[0].astype(BF16)
    h1t, route, counts = _outproj_route(
        og.reshape(NP, 512), od.reshape(NP, 512), h0.reshape(NP, D), wo[:512], wo[512:],
        row(ln1_g[0]), row(ln1_b[0]), router_w[0], row(router_b[0]), LP)

    R = MOE_ROWS
    i32 = jnp.int32
    counts = counts[0].astype(i32)
    n_blk = (counts + R - 1) // R
    blk_end = jnp.cumsum(n_blk)
    blk_start = blk_end - n_blk
    grp_start = jnp.cumsum(counts) - counts
    n_total = blk_end[-1:]
    n_assign_max = B * (S + N_META) * TOP_K
    max_blocks = (n_assign_max + N_EXPERTS * (R - 1)) // R
    g = jnp.minimum(jnp.arange(max_blocks, dtype=i32), n_total[0] - 1)
    is_e = (jnp.minimum(jnp.sum(g[:, None] >= blk_end[None, :], axis=1), N_EXPERTS - 1)[:, None]
            == jnp.arange(N_EXPERTS)[None, :])
    pick = lambda v: jnp.sum(jnp.where(is_e, v[None, :], 0), axis=1)
    local = (g - pick(blk_start)) * R
    blk_p0 = (pick(grp_start) + local).astype(i32)

    tok_valid = (jnp.arange(NP, dtype=i32) % LP) >= FRONT_PAD
    flat_e = jnp.where(tok_valid[:, None], route[:, :TOP_K], N_EXPERTS).reshape(-1)
    id_bits = (NP * TOP_K - 1).bit_length()
    keyed = jnp.sort(flat_e * (1 << id_bits) + jnp.arange(NP * TOP_K, dtype=i32))
    assert R <= B * FRONT_PAD * TOP_K
    order = keyed & ((1 << id_bits) - 1)

    bgu = b_gate_up[0].reshape(N_EXPERTS, D_FF, 2)
    yk = _experts(order, blk_p0, blk_start.astype(i32), n_blk.astype(i32), n_total.astype(i32),
                  h1t, w_gate_up[0],
                  bgu[:, :, 0].reshape(N_EXPERTS, 1, D_FF), bgu[:, :, 1].reshape(N_EXPERTS, 1, D_FF),
                  w_down[0], b_down[0].reshape(N_EXPERTS, 1, D), NP, LP)

    return _combine(h1t, yk, route, row(ln2_g[0]), row(ln2_b[0]), B, S)
```

```python
import functools
import math

import jax
import jax.numpy as jnp
from jax import lax
from jax.experimental import pallas as pl
from jax.experimental.pallas import tpu as pltpu

F32 = jnp.float32
BF16 = jnp.bfloat16

D_MODEL = 1024
N_META = 16
ROW_BLOCK = 128
FRONT_PAD = ROW_BLOCK - N_META
CHUNK = 64

GLA_HEADS = 4
GLA_DK = 64
GLA_DV = 128
GLA_RANK = 16
GLA_TAU = 16.0
DIFF_HEADS = 4
DIFF_DH = 64
DIFF_DV = 128

N_EXPERTS = 32
TOP_K = 4
D_FF = 1024
SWIGLU_LIMIT = 7.0
SWIGLU_ALPHA = 1.702
MOE_ROWS = 256

DEEPNORM_ALPHA = 2.0 ** 0.25
LAMBDA_INIT = 0.8 - 0.6 * math.exp(0.0)
LN_EPS = 1e-5
NEG_INF = -1e30
LOG2E = math.log2(math.e)

ROUTE_TILE = 512
VMEM_LIMIT = 48 * 1024 * 1024
EXPERT_VMEM_LIMIT = 56 * 1024 * 1024

_GQ, _GK, _GV, _GR, _DQ, _DK, _DV = 0, 256, 512, 1024, 1536, 2048, 2560
_MAIN_WIDTH = 3072


def _layer_norm(x, g, b):
    mu = jnp.mean(x, axis=-1, keepdims=True)
    xc = x - mu
    var = jnp.mean(xc * xc, axis=-1, keepdims=True)
    return xc * lax.rsqrt(var + LN_EPS) * g + b


def _split3_bf16(x):
    hi = x.astype(BF16)
    r1 = x - hi.astype(F32)
    mid = r1.astype(BF16)
    lo = (r1 - mid.astype(F32)).astype(BF16)
    return hi, mid, lo


TOKEN_SUBROWS = D_MODEL // 128


def _store_token_tiled(ref, val, first_token=0):
    n = val.shape[0]
    for s in range(TOKEN_SUBROWS):
        ref[pl.ds(first_token * TOKEN_SUBROWS + s, n, stride=TOKEN_SUBROWS), :] = val[:, s * 128:(s + 1) * 128]


def _load_token_tiled(ref, n, first_row=0):
    return jnp.concatenate(
        [ref[pl.ds(first_row + s, n, stride=TOKEN_SUBROWS), :] for s in range(TOKEN_SUBROWS)], axis=1)


def _inproj_body(xa_ref, xb_ref, meta_ref, g_ref, b_ref, w_ref, wga_ref, wa2_ref, ba_ref,
                 h0_ref, gq_ref, gk_ref, gv_ref, gr_ref, la_ref, dq_ref, dk_ref, dv_ref):
    j = pl.program_id(1)
    rows = lax.broadcasted_iota(jnp.int32, (2 * ROW_BLOCK, 1), 0)
    valid = jnp.logical_or(j > 0, rows >= FRONT_PAD)
    xin = jnp.concatenate([jnp.where(j > 0, xa_ref[0], meta_ref[...]), xb_ref[0]], axis=0)
    h = jnp.where(valid, _layer_norm(xin, g_ref[...], b_ref[...]), 0.0)
    h0_ref[0] = h
    hb = h.astype(BF16)

    def proj(off, width):
        return jnp.dot(hb, w_ref[:, off:off + width], preferred_element_type=F32)

    gq_ref[0] = (proj(_GQ, 256) * GLA_DK ** -0.5).astype(BF16)
    gk_ref[0] = proj(_GK, 256)
    gv_ref[0] = proj(_GV, 512).astype(BF16)
    gr_ref[0] = proj(_GR, 512).astype(BF16)
    dq_ref[0] = (proj(_DQ, 512) * (DIFF_DH ** -0.5 * LOG2E)).astype(BF16)
    dk_ref[0] = proj(_DK, 512).astype(BF16)
    dv_ref[0] = proj(_DV, 512).astype(BF16)

    a_lr = jnp.dot(hb, wga_ref[...], preferred_element_type=F32)
    z = jnp.dot(a_lr.astype(BF16), wa2_ref[...], preferred_element_type=F32) + ba_ref[...]
    log_sig = jnp.minimum(z, 0.0) - jnp.log1p(jnp.exp(-jnp.abs(z)))
    la_ref[0] = jnp.where(valid, log_sig / GLA_TAU, 0.0)


def _inproj(x, meta_pad, ln_g, ln_b, w_main, w_ga, wa2, ba):
    B, S, D = x.shape
    nb = S // ROW_BLOCK + 1
    LP = nb * ROW_BLOCK

    n_frame_blocks = S // ROW_BLOCK
    n_steps = pl.cdiv(LP, 2 * ROW_BLOCK)

    def row_spec(width):
        return pl.BlockSpec((1, 2 * ROW_BLOCK, width), lambda b, j: (b, j, 0))

    def frames(shift):
        return pl.BlockSpec((1, ROW_BLOCK, D),
                            lambda b, j: (b, jnp.clip(2 * j + shift, 0, n_frame_blocks - 1), 0))

    def full(shape):
        return pl.BlockSpec(shape, lambda b, j: (0,) * len(shape))

    out_shapes = [
        jax.ShapeDtypeStruct((B, LP, D), F32),
        jax.ShapeDtypeStruct((B, LP, 256), BF16),
        jax.ShapeDtypeStruct((B, LP, 256), F32),
        jax.ShapeDtypeStruct((B, LP, 512), BF16),
        jax.ShapeDtypeStruct((B, LP, 512), BF16),
        jax.ShapeDtypeStruct((B, LP, 256), F32),
        jax.ShapeDtypeStruct((B, LP, 512), BF16),
        jax.ShapeDtypeStruct((B, LP, 512), BF16),
        jax.ShapeDtypeStruct((B, LP, 512), BF16),
    ]
    return pl.pallas_call(
        _inproj_body,
        grid=(B, n_steps),
        in_specs=[
            frames(-1), frames(0),
            full((ROW_BLOCK, D)), full((1, D)), full((1, D)),
            full((D, _MAIN_WIDTH)), full((D, GLA_RANK)), full((GLA_RANK, 256)), full((1, 256)),
        ],
        out_specs=[row_spec(s.shape[-1]) for s in out_shapes],
        out_shape=out_shapes,
        compiler_params=pltpu.CompilerParams(
            dimension_semantics=("parallel", "parallel"), vmem_limit_bytes=VMEM_LIMIT),
        name="inproj",
    )(x, x, meta_pad, ln_g, ln_b, w_main, w_ga, wa2, ba)


def _gla_body(q_ref, k_ref, v_ref, r_ref, la_ref, g_ref, o_ref):
    n_groups = q_ref.shape[1] // ROW_BLOCK
    ri = lax.broadcasted_iota(jnp.int32, (ROW_BLOCK, ROW_BLOCK), 0)
    ci = lax.broadcasted_iota(jnp.int32, (ROW_BLOCK, ROW_BLOCK), 1)
    later = jnp.logical_and(ri // CHUNK == ci // CHUNK, ci > ri).astype(BF16)
    sr = lax.broadcasted_iota(jnp.int32, (GLA_HEADS * GLA_DV, GLA_HEADS * GLA_DK), 0)
    sc = lax.broadcasted_iota(jnp.int32, (GLA_HEADS * GLA_DV, GLA_HEADS * GLA_DK), 1)
    same_head = sr // GLA_DV == sc // GLA_DK
    gain = g_ref[...]
    st = jnp.zeros((GLA_HEADS * GLA_DV, GLA_HEADS * GLA_DK), F32)

    for grp in range(n_groups):
        g0 = grp * ROW_BLOCK
        la = la_ref[0, g0:g0 + ROW_BLOCK, :]
        hi, mid, lo = _split3_bf16(la)
        suffix = (jnp.dot(later, hi, preferred_element_type=F32)
                  + jnp.dot(later, mid, preferred_element_type=F32)
                  + jnp.dot(later, lo, preferred_element_type=F32))
        kdec = (k_ref[0, g0:g0 + ROW_BLOCK, :] * jnp.exp(suffix)).astype(BF16)
        for half in range(ROW_BLOCK // CHUNK):
            f = half * CHUNK
            r0 = g0 + f
            tot = suffix[f:f + 1, :] + la[f:f + 1, :]
            upd = lax.dot_general(v_ref[0, r0:r0 + CHUNK, :], kdec[f:f + CHUNK, :],
                                  (((0,), (0,)), ((), ())), preferred_element_type=F32)
            st = st * jnp.exp(tot) + jnp.where(same_head, upd, 0.0)
            o = lax.dot_general(q_ref[0, r0:r0 + CHUNK, :], st.astype(BF16),
                                (((1,), (1,)), ((), ())), preferred_element_type=F32)
            r = r_ref[0, r0:r0 + CHUNK, :].astype(F32)
            for h in range(GLA_HEADS):
                oh = o[:, h * GLA_DV:(h + 1) * GLA_DV]
                rh = r[:, h * GLA_DV:(h + 1) * GLA_DV]
                ms = jnp.mean(oh * oh, axis=-1, keepdims=True)
                out = oh * lax.rsqrt(ms + LN_EPS) * gain * (rh * jax.nn.sigmoid(rh))
                o_ref[0, r0:r0 + CHUNK, h * GLA_DV:(h + 1) * GLA_DV] = out.astype(BF16)


def _gla(gq, gk, gv, gr, la, norm_g):
    B, LP, _ = gq.shape

    def seq(width):
        return pl.BlockSpec((1, LP, width), lambda b: (b, 0, 0))

    return pl.pallas_call(
        _gla_body,
        grid=(B,),
        in_specs=[seq(256), seq(256), seq(512), seq(512), seq(256),
                  pl.BlockSpec((1, GLA_DV), lambda b: (0, 0))],
        out_specs=seq(512),
        out_shape=jax.ShapeDtypeStruct((B, LP, 512), BF16),
        compiler_params=pltpu.CompilerParams(
            dimension_semantics=("parallel",), vmem_limit_bytes=VMEM_LIMIT),
        name="gla",
    )(gq, gk, gv, gr, la, norm_g)


ATT_BLOCK = 256


def _diff_body(q_ref, k_ref, v_ref, lq1_ref, lk1_ref, lq2_ref, lk2_ref, g_ref, o_ref,
               dmask_ref):
    h = pl.program_id(1)
    slope = jnp.where(h == 0, 2.0 ** -2, jnp.where(h == 1, 2.0 ** -4, jnp.where(h == 2, 2.0 ** -6, 2.0 ** -8)))
    slope = slope.astype(F32) * LOG2E
    lam = (jnp.exp(jnp.sum(lq1_ref[...] * lk1_ref[...], axis=-1, keepdims=True))
           - jnp.exp(jnp.sum(lq2_ref[...] * lk2_ref[...], axis=-1, keepdims=True)) + LAMBDA_INIT)
    gain = g_ref[...] * (1.0 - LAMBDA_INIT)
    n_qblocks = (q_ref.shape[1] - ROW_BLOCK) // ATT_BLOCK
    nt = (((1,), (1,)), ((), ()))

    def split_q(q):
        lane = lax.broadcasted_iota(jnp.int32, q.shape, 1)
        zero = jnp.zeros_like(q)
        return jnp.where(lane < DIFF_DH, q, zero), jnp.where(lane >= DIFF_DH, q, zero)

    def with_ones(v):
        return jnp.concatenate([v, jnp.ones_like(v)], axis=1)

    def softmax_av(qz, kk, vext, add_bias):
        s = add_bias(lax.dot_general(qz, kk, nt, preferred_element_type=F32))
        m = jnp.max(s, axis=-1, keepdims=True)
        p = jnp.exp2(s - m).astype(BF16)
        return jnp.dot(p, vext, preferred_element_type=F32)

    def finish(a1, a2):
        o = a1[:, :DIFF_DV] / a1[:, DIFF_DV:] - lam * (a2[:, :DIFF_DV] / a2[:, DIFF_DV:])
        ms = jnp.mean(o * o, axis=-1, keepdims=True)
        return (o * lax.rsqrt(ms + LN_EPS) * gain).astype(BF16)

    r = lax.broadcasted_iota(jnp.int32, (ROW_BLOCK, ROW_BLOCK), 0)
    c = lax.broadcasted_iota(jnp.int32, (ROW_BLOCK, ROW_BLOCK), 1)
    ok = jnp.logical_and(c // CHUNK <= r // CHUNK, c >= FRONT_PAD)
    bias_lead = jnp.where(ok, -slope * jnp.abs(r - c).astype(F32), NEG_INF)
    q1z, q2z = split_q(q_ref[0, 0:ROW_BLOCK, :])
    k_lead = k_ref[0, 0:ROW_BLOCK, :]
    v_lead = with_ones(v_ref[0, 0:ROW_BLOCK, :])
    o_ref[0, 0:ROW_BLOCK, :] = finish(softmax_av(q1z, k_lead, v_lead, lambda s: s + bias_lead),
                                      softmax_av(q2z, k_lead, v_lead, lambda s: s + bias_lead))

    r = lax.broadcasted_iota(jnp.int32, (ATT_BLOCK, ATT_BLOCK), 0)
    c = lax.broadcasted_iota(jnp.int32, (ATT_BLOCK, ATT_BLOCK), 1)
    rel = jnp.where(c <= r, c, 2 * r - c).astype(F32)
    dmask_ref[...] = jnp.where(c // CHUNK <= r // CHUNK, slope * rel, NEG_INF)

    for jq in range(n_qblocks):
        qbase = ROW_BLOCK + jq * ATT_BLOCK
        n_keys = qbase + ATT_BLOCK
        q1z, q2z = split_q(q_ref[0, qbase:qbase + ATT_BLOCK, :])
        kk = k_ref[0, 0:n_keys, :]
        vext = with_ones(v_ref[0, 0:n_keys, :])
        col = lax.broadcasted_iota(jnp.int32, (1, qbase), 1)
        col_bias = jnp.where(col >= FRONT_PAD, slope * (col - qbase).astype(F32), NEG_INF)

        def add_bias(s, col_bias=col_bias, qbase=qbase):
            return jnp.concatenate([s[:, :qbase] + col_bias, s[:, qbase:] + dmask_ref[...]], axis=1)

        o_ref[0, qbase:qbase + ATT_BLOCK, :] = finish(softmax_av(q1z, kk, vext, add_bias),
                                                      softmax_av(q2z, kk, vext, add_bias))


def _diff_attention(dq, dk, dv, lq1, lk1, lq2, lk2, norm_g):
    B, LP, W = dq.shape
    small = pl.BlockSpec((1, DIFF_DH), lambda b, h: (0, 0))
    seq = pl.BlockSpec((1, LP, 2 * DIFF_DH), lambda b, h: (b, 0, h))
    return pl.pallas_call(
        _diff_body,
        grid=(B, DIFF_HEADS),
        in_specs=[seq, seq, seq, small, small, small, small,
                  pl.BlockSpec((1, DIFF_DV), lambda b, h: (0, 0))],
        out_specs=seq,
        out_shape=jax.ShapeDtypeStruct((B, LP, W), BF16),
        scratch_shapes=[pltpu.VMEM((ATT_BLOCK, ATT_BLOCK), F32)],
        compiler_params=pltpu.CompilerParams(
            dimension_semantics=("parallel", "parallel"), vmem_limit_bytes=VMEM_LIMIT),
        name="diff_attn",
    )(dq, dk, dv, lq1, lk1, lq2, lk2, norm_g)


def _route_body(og_ref, od_ref, h0_ref, wog_ref, wod_ref, g_ref, b_ref, rw_ref, rb_ref,
                h1_ref, route_ref, keys_ref, cnt_ref, carry_ref, *, rows_per_seq, id_bits):
    t = pl.program_id(0)

    @pl.when(t == 0)
    def _():
        carry_ref[...] = jnp.zeros_like(carry_ref)

    mix = (jnp.dot(og_ref[...], wog_ref[...], preferred_element_type=F32)
           + jnp.dot(od_ref[...], wod_ref[...], preferred_element_type=F32))
    h1 = _layer_norm(DEEPNORM_ALPHA * h0_ref[...] + mix, g_ref[...], b_ref[...])
    _store_token_tiled(h1_ref, h1)

    hh, hm, _ = _split3_bf16(h1)
    wh, wm, _ = _split3_bf16(rw_ref[...])
    head = jnp.dot(hh, jnp.concatenate([wh, wm], axis=1), preferred_element_type=F32)
    logits = (rb_ref[...] + head[:, :N_EXPERTS] + head[:, N_EXPERTS:]
              + jnp.dot(hm, wh, preferred_element_type=F32))

    T = logits.shape[0]
    lane = lax.broadcasted_iota(jnp.int32, (T, N_EXPERTS), 1)
    grow = t * T + lax.broadcasted_iota(jnp.int32, (T, 1), 0)
    valid = (grow % rows_per_seq) >= FRONT_PAD

    work = logits
    top_v, top_i = [], []
    for _ in range(TOP_K):
        mx = jnp.max(work, axis=-1, keepdims=True)
        idx = jnp.min(jnp.where(work == mx, lane, N_EXPERTS), axis=-1, keepdims=True)
        top_v.append(mx)
        top_i.append(idx)
        work = jnp.where(lane == idx, -jnp.inf, work)
    ex = [jnp.exp(v - top_v[0]) for v in top_v]
    den = ex[0] + ex[1] + ex[2] + ex[3]
    gates = [e / den for e in ex]

    onehot = jnp.zeros((T, N_EXPERTS), F32)
    for idx in top_i:
        onehot = onehot + (lane == idx).astype(F32)
    onehot = jnp.where(valid, onehot, 0.0)

    carry_ref[...] = carry_ref[...] + jnp.sum(onehot, axis=0, keepdims=True)
    cnt_ref[...] = carry_ref[...]

    lane_o = lax.broadcasted_iota(jnp.int32, (T, 128), 1)
    packed = jnp.zeros((T, 128), jnp.int32)
    for k in range(TOP_K):
        key = jnp.where(valid, top_i[k], N_EXPERTS) * (1 << id_bits) + (grow * TOP_K + k)
        packed = jnp.where(lane_o == k, top_i[k], packed)
        packed = jnp.where(lane_o == TOP_K + k, key, packed)
        packed = jnp.where(lane_o == 2 * TOP_K + k, lax.bitcast_convert_type(gates[k], jnp.int32), packed)
    route_ref[...] = packed
    by_lane = packed.T
    keys_ref[...] = jnp.concatenate(
        [by_lane[TOP_K + k:TOP_K + k + 1, q * 128:(q + 1) * 128] for k in range(TOP_K) for q in range(T // 128)],
        axis=0)


def _outproj_route(og, od, h0, wo_g, wo_d, ln_g, ln_b, rw, rb, rows_per_seq):
    NP, D = h0.shape
    T = ROUTE_TILE

    def rows(width):
        return pl.BlockSpec((T, width), lambda t: (t, 0))

    def full(shape):
        return pl.BlockSpec(shape, lambda t: (0,) * len(shape))

    key_rows = T * TOP_K // 128
    return pl.pallas_call(
        functools.partial(_route_body, rows_per_seq=rows_per_seq, id_bits=(NP * TOP_K - 1).bit_length()),
        grid=(NP // T,),
        in_specs=[rows(512), rows(512), rows(D), full((512, D)), full((512, D)),
                  full((1, D)), full((1, D)), full((D, N_EXPERTS)), full((1, N_EXPERTS))],
        out_specs=[pl.BlockSpec((T * TOKEN_SUBROWS, 128), lambda t: (t, 0)), rows(128),
                   pl.BlockSpec((key_rows, 128), lambda t: (t, 0)), full((1, N_EXPERTS))],
        out_shape=[jax.ShapeDtypeStruct((NP * TOKEN_SUBROWS, 128), F32),
                   jax.ShapeDtypeStruct((NP, 128), jnp.int32),
                   jax.ShapeDtypeStruct((NP * TOP_K // 128, 128), jnp.int32),
                   jax.ShapeDtypeStruct((1, N_EXPERTS), F32)],
        scratch_shapes=[pltpu.VMEM((1, N_EXPERTS), F32)],
        compiler_params=pltpu.CompilerParams(
            dimension_semantics=("arbitrary",), vmem_limit_bytes=VMEM_LIMIT),
        name="outproj_route",
    )(og, od, h0, wo_g, wo_d, ln_g, ln_b, rw, rb)


def _expert_body(order_ref, p0_ref, bstart_ref, nblk_ref, ntot_ref,
                 h1t_hbm, wgu_ref, bg_ref, bu_ref, wd_ref, bd_ref, yk_hbm,
                 wg_s, wu_s, wd_s, xb0, xb1, xb2, yb0, yb1, yb2, zbuf, gsem, ssem, zsem,
                 *, n_tokens, rows_per_seq):
    e = pl.program_id(0)
    n_total = ntot_ref[0]
    R = MOE_ROWS
    TS = TOKEN_SUBROWS
    xb = (xb0, xb1, xb2)
    yb = (yb0, yb1, yb2)
    NB = len(xb)

    def gather_start(b, slot):
        p0 = p0_ref[b]
        for r in range(R):
            tok = order_ref[p0 + r] >> 2
            pltpu.make_async_copy(
                h1t_hbm.at[pl.ds(pl.multiple_of(tok * TS, TS), TS), :],
                xb[slot].at[pl.ds(r * TS, TS), :], gsem.at[slot]).start(priority=r % 2)

    def gather_wait(slot):
        pltpu.make_async_copy(h1t_hbm.at[pl.ds(0, R * TS), :], xb[slot], gsem.at[slot]).wait()

    def scatter_start(b, slot):
        p0 = p0_ref[b]
        for r in range(R):
            row = order_ref[p0 + r] * TS
            pltpu.make_async_copy(
                yb[slot].at[pl.ds(r * TS, TS), :],
                yk_hbm.at[pl.ds(pl.multiple_of(row, TS), TS), :], ssem.at[slot]).start(priority=r % 2)

    def scatter_wait(slot):
        pltpu.make_async_copy(yb[slot], yk_hbm.at[pl.ds(0, R * TS), :], ssem.at[slot]).wait()

    @pl.when(e == 0)
    def _():
        zbuf[...] = jnp.zeros_like(zbuf)
        lead = FRONT_PAD * TS
        fills = [((s * rows_per_seq * TOP_K + k * FRONT_PAD) * TS, lead)
                 for s in range(n_tokens // rows_per_seq) for k in range(TOP_K)]
        copies = [pltpu.make_async_copy(zbuf.at[pl.ds(0, n), :], yk_hbm.at[pl.ds(o, n), :], zsem)
                  for o, n in fills]
        for cp in copies:
            cp.start()
        for cp in copies:
            cp.wait()
        gather_start(0, 0)
        gather_start(jnp.minimum(1, n_total - 1), 1)

    @pl.when(nblk_ref[e] > 0)
    def _():
        r = lax.broadcasted_iota(jnp.int32, (256, 256), 0)
        c = lax.broadcasted_iota(jnp.int32, (256, 256), 1)
        perm = (r == jnp.where(c < 128, 2 * c, 2 * (c - 128) + 1)).astype(BF16)
        for tt in range(2 * D_FF // 256):
            wt = wgu_ref[0, :, tt * 256:(tt + 1) * 256].astype(BF16)
            sp = jnp.dot(wt, perm, preferred_element_type=F32)
            wg_s[:, tt * 128:(tt + 1) * 128] = sp[:, :128].astype(BF16)
            wu_s[:, tt * 128:(tt + 1) * 128] = sp[:, 128:].astype(BF16)
        wd_s[...] = wd_ref[0].astype(BF16)

    def run_block(b, slot):
        prv = (slot + 2) % NB
        gather_wait(slot)
        gather_start(jnp.minimum(b + 2, n_total - 1), prv)
        x = _load_token_tiled(xb[slot], R).astype(BF16)
        gt = jnp.dot(x, wg_s[...], preferred_element_type=F32) + bg_ref[0]
        up = jnp.dot(x, wu_s[...], preferred_element_type=F32) + bu_ref[0]
        gt = jnp.minimum(gt, SWIGLU_LIMIT)
        up = jnp.clip(up, -SWIGLU_LIMIT, SWIGLU_LIMIT)
        act = (up + 1.0) * (gt * jax.nn.sigmoid(SWIGLU_ALPHA * gt))
        y = jnp.dot(act.astype(BF16), wd_s[...], preferred_element_type=F32) + bd_ref[0]
        _store_token_tiled(yb[slot], y)

        @pl.when(b >= 1)
        def _():
            scatter_wait(prv)
        scatter_start(b, slot)

    def block(b, carry):
        for slot in range(NB):
            pl.when(b % NB == slot)(functools.partial(run_block, b, slot))
        return carry

    b0 = bstart_ref[e]
    lax.fori_loop(b0, b0 + nblk_ref[e], block, 0)

    @pl.when(e == pl.num_programs(0) - 1)
    def _():
        last = n_total - 1
        for slot in range(NB):
            @pl.when(last % NB == slot)
            def _(slot=slot):
                gather_wait((slot + 1) % NB)
                gather_wait((slot + 2) % NB)
                scatter_wait(slot)


def _experts(order, blk_p0, blk_start, n_blk, n_total, h1t, w_gate_up, b_gate, b_up, w_down,
             b_down, n_tokens, rows_per_seq):
    D = D_MODEL
    R = MOE_ROWS
    buf = pltpu.VMEM((R * TOKEN_SUBROWS, 128), F32)
    per_expert = lambda e, *_: (e, 0, 0)
    grid_spec = pltpu.PrefetchScalarGridSpec(
        num_scalar_prefetch=5,
        grid=(N_EXPERTS,),
        in_specs=[
            pl.BlockSpec(memory_space=pl.ANY),
            pl.BlockSpec((1, D, 2 * D_FF), per_expert),
            pl.BlockSpec((1, 1, D_FF), per_expert),
            pl.BlockSpec((1, 1, D_FF), per_expert),
            pl.BlockSpec((1, D_FF, D), per_expert),
            pl.BlockSpec((1, 1, D), per_expert),
        ],
        out_specs=pl.BlockSpec(memory_space=pl.ANY),
        scratch_shapes=[pltpu.VMEM((D, D_FF), BF16), pltpu.VMEM((D, D_FF), BF16), pltpu.VMEM((D_FF, D), BF16),
                        buf, buf, buf, buf, buf, buf, pltpu.VMEM((FRONT_PAD * TOKEN_SUBROWS, 128), F32),
                        pltpu.SemaphoreType.DMA((3,)), pltpu.SemaphoreType.DMA((3,)), pltpu.SemaphoreType.DMA],
    )
    out_rows = TOP_K * n_tokens * TOKEN_SUBROWS
    return pl.pallas_call(
        functools.partial(_expert_body, n_tokens=n_tokens, rows_per_seq=rows_per_seq),
        grid_spec=grid_spec,
        out_shape=jax.ShapeDtypeStruct((out_rows, 128), F32),
        compiler_params=pltpu.CompilerParams(
            dimension_semantics=("arbitrary",), vmem_limit_bytes=EXPERT_VMEM_LIMIT),
        name="experts",
    )(order, blk_p0, blk_start, n_blk, n_total, h1t, w_gate_up, b_gate, b_up, w_down, b_down)


COMBINE_BLOCKS = 2


def _combine_body(*refs):
    per_block = 2 + TOP_K
    g_ref, b_ref, o_ref = refs[COMBINE_BLOCKS * per_block:]
    for i in range(COMBINE_BLOCKS):
        h1_ref, *y_refs, route_ref = refs[i * per_block:(i + 1) * per_block]
        route = route_ref[...]
        y = jnp.zeros((ROW_BLOCK, D_MODEL), F32)
        for k, yk_ref in enumerate(y_refs):
            gate = lax.bitcast_convert_type(route[:, 2 * TOP_K + k:2 * TOP_K + k + 1], F32)
            y = y + gate * _load_token_tiled(yk_ref.reshape(ROW_BLOCK * TOKEN_SUBROWS, 128), ROW_BLOCK)
        h1 = _load_token_tiled(h1_ref, ROW_BLOCK)
        o_ref[0, i * ROW_BLOCK:(i + 1) * ROW_BLOCK, :] = _layer_norm(
            DEEPNORM_ALPHA * h1 + y, g_ref[...], b_ref[...])


def _combine(h1t, yk, route, ln_g, ln_b, batch, seq_len):
    nb = seq_len // ROW_BLOCK
    blocks_per_seq = nb + 1
    tile_rows = ROW_BLOCK * TOKEN_SUBROWS
    yk = yk.reshape(-1, TOP_K, TOKEN_SUBROWS, 128)

    def token_block(i):
        return lambda b, j: b * blocks_per_seq + COMBINE_BLOCKS * j + i + 1

    in_specs, operands = [], []
    for i in range(COMBINE_BLOCKS):
        tb = token_block(i)
        in_specs.append(pl.BlockSpec((tile_rows, 128), lambda b, j, tb=tb: (tb(b, j), 0)))
        in_specs += [pl.BlockSpec((ROW_BLOCK, None, TOKEN_SUBROWS, 128), lambda b, j, tb=tb, k=k: (tb(b, j), k, 0, 0))
                     for k in range(TOP_K)]
        in_specs.append(pl.BlockSpec((ROW_BLOCK, 128), lambda b, j, tb=tb: (tb(b, j), 0)))
        operands += [h1t] + [yk] * TOP_K + [route]
    vec = pl.BlockSpec((1, D_MODEL), lambda b, j: (0, 0))
    return pl.pallas_call(
        _combine_body,
        grid=(batch, nb // COMBINE_BLOCKS),
        in_specs=in_specs + [vec, vec],
        out_specs=pl.BlockSpec((1, COMBINE_BLOCKS * ROW_BLOCK, D_MODEL), lambda b, j: (b, j, 0)),
        out_shape=jax.ShapeDtypeStruct((batch, seq_len, D_MODEL), F32),
        compiler_params=pltpu.CompilerParams(
            dimension_semantics=("parallel", "parallel"), vmem_limit_bytes=VMEM_LIMIT),
        name="combine",
    )(*operands, ln_g, ln_b)


def kernel(x, meta_tokens, ln_emb_g, ln_emb_b, w_in, gla_wa2, gla_ba, gla_norm_g, diff_lambda_q1, diff_lambda_k1, diff_lambda_q2, diff_lambda_k2, diff_norm_g, w_out, ln1_g, ln1_b, router_w, router_b, w_gate_up, b_gate_up, w_down, b_down, ln2_g, ln2_b):
    B, S, D = x.shape
    LP = S + ROW_BLOCK
    NP = B * LP
    row = lambda v: v.reshape(1, -1)

    w = w_in[0]
    w_main = jnp.concatenate([w[:, :1536], w[:, 1552:]], axis=1).astype(BF16)
    w_ga = w[:, 1536:1552].astype(BF16)
    meta_pad = jnp.pad(meta_tokens, ((FRONT_PAD, 0), (0, 0)))

    h0, gq, gk, gv, gr, la, dq, dk, dv = _inproj(
        x, meta_pad, row(ln_emb_g), row(ln_emb_b), w_main, w_ga,
        gla_wa2[0].astype(BF16), row(gla_ba[0]))

    og = _gla(gq, gk, gv, gr, la, row(gla_norm_g[0]))
    od = _diff_attention(dq, dk, dv, row(diff_lambda_q1[0]), row(diff_lambda_k1[0]),
                         row(diff_lambda_q2[0]), row(diff_lambda_k2[0]), row(diff_norm_g[0]))

    wo = w_out[0].astype(BF16)
    h1t, route, keys, counts = _outproj_route(
        og.reshape(NP, 512), od.reshape(NP, 512), h0.reshape(NP, D), wo[:512], wo[512:],
        row(ln1_g[0]), row(ln1_b[0]), router_w[0], row(router_b[0]), LP)

    R = MOE_ROWS
    i32 = jnp.int32
    counts = counts[0].astype(i32)
    n_blk = (counts + R - 1) // R
    blk_end = jnp.cumsum(n_blk)
    blk_start = blk_end - n_blk
    grp_start = jnp.cumsum(counts) - counts
    n_total = blk_end[-1:]
    n_assign_max = B * (S + N_META) * TOP_K
    max_blocks = (n_assign_max + N_EXPERTS * (R - 1)) // R
    g = jnp.minimum(jnp.arange(max_blocks, dtype=i32), n_total[0] - 1)
    is_e = (jnp.minimum(jnp.sum(g[:, None] >= blk_end[None, :], axis=1), N_EXPERTS - 1)[:, None]
            == jnp.arange(N_EXPERTS)[None, :])
    pick = lambda v: jnp.sum(jnp.where(is_e, v[None, :], 0), axis=1)
    local = (g - pick(blk_start)) * R
    blk_p0 = (pick(grp_start) + local).astype(i32)

    assert R <= B * FRONT_PAD * TOP_K
    order = jnp.sort(keys.reshape(-1)) & ((1 << (NP * TOP_K - 1).bit_length()) - 1)

    bgu = b_gate_up[0].reshape(N_EXPERTS, D_FF, 2)
    yk = _experts(order, blk_p0, blk_start.astype(i32), n_blk.astype(i32), n_total.astype(i32),
                  h1t, w_gate_up[0],
                  bgu[:, :, 0].reshape(N_EXPERTS, 1, D_FF), bgu[:, :, 1].reshape(N_EXPERTS, 1, D_FF),
                  w_down[0], b_down[0].reshape(N_EXPERTS, 1, D), NP, LP)

    return _combine(h1t, yk, route, row(ln2_g[0]), row(ln2_b[0]), B, S)
```

```python
import functools
import math

import jax
import jax.numpy as jnp
from jax import lax
from jax.experimental import pallas as pl
from jax.experimental.pallas import tpu as pltpu

F32 = jnp.float32
BF16 = jnp.bfloat16

D_MODEL = 1024
N_META = 16
ROW_BLOCK = 128
FRONT_PAD = ROW_BLOCK - N_META
CHUNK = 64

GLA_HEADS = 4
GLA_DK = 64
GLA_DV = 128
GLA_RANK = 16
GLA_TAU = 16.0
DIFF_HEADS = 4
DIFF_DH = 64
DIFF_DV = 128

N_EXPERTS = 32
TOP_K = 4
D_FF = 1024
SWIGLU_LIMIT = 7.0
SWIGLU_ALPHA = 1.702
MOE_ROWS = 256

DEEPNORM_ALPHA = 2.0 ** 0.25
LAMBDA_INIT = 0.8 - 0.6 * math.exp(0.0)
LN_EPS = 1e-5
NEG_INF = -1e30
LOG2E = math.log2(math.e)

MXU_TILE = 256
ROUTE_TILE = 512
VMEM_LIMIT = 48 * 1024 * 1024
EXPERT_VMEM_LIMIT = 56 * 1024 * 1024

_GQ, _GK, _GV, _GR, _DQ, _DK, _DV = 0, 256, 512, 1024, 1536, 2048, 2560
_MAIN_WIDTH = 3072


def _layer_norm(x, g, b):
    mu = jnp.mean(x, axis=-1, keepdims=True)
    xc = x - mu
    var = jnp.mean(xc * xc, axis=-1, keepdims=True)
    return xc * lax.rsqrt(var + LN_EPS) * g + b


def _split3_bf16(x):
    hi = x.astype(BF16)
    r1 = x - hi.astype(F32)
    mid = r1.astype(BF16)
    lo = (r1 - mid.astype(F32)).astype(BF16)
    return hi, mid, lo


TOKEN_SUBROWS = D_MODEL // 128


def _store_token_tiled(ref, val, first_token=0):
    n = val.shape[0]
    for s in range(TOKEN_SUBROWS):
        ref[pl.ds(first_token * TOKEN_SUBROWS + s, n, stride=TOKEN_SUBROWS), :] = val[:, s * 128:(s + 1) * 128]


def _load_token_tiled(ref, n, first_row=0):
    return jnp.concatenate(
        [ref[pl.ds(first_row + s, n, stride=TOKEN_SUBROWS), :] for s in range(TOKEN_SUBROWS)], axis=1)


def _inproj_body(xa_ref, xb_ref, meta_ref, g_ref, b_ref, w_ref, wga_ref, wa2_ref, ba_ref,
                 h0_ref, gq_ref, gk_ref, gv_ref, gr_ref, la_ref, dq_ref, dk_ref, dv_ref):
    j = pl.program_id(1)
    rows = lax.broadcasted_iota(jnp.int32, (2 * ROW_BLOCK, 1), 0)
    valid = jnp.logical_or(j > 0, rows >= FRONT_PAD)
    xin = jnp.concatenate([jnp.where(j > 0, xa_ref[0], meta_ref[...]), xb_ref[0]], axis=0)
    h = jnp.where(valid, _layer_norm(xin, g_ref[...], b_ref[...]), 0.0)
    h0_ref[0] = h
    hb = h.astype(BF16)

    def proj(off, width):
        return jnp.dot(hb, w_ref[:, off:off + width], preferred_element_type=F32)

    gq_ref[0] = (proj(_GQ, 256) * GLA_DK ** -0.5).astype(BF16)
    gk_ref[0] = proj(_GK, 256)
    gv_ref[0] = proj(_GV, 512).astype(BF16)
    gr_ref[0] = proj(_GR, 512).astype(BF16)
    dq_ref[0] = (proj(_DQ, 512) * (DIFF_DH ** -0.5 * LOG2E)).astype(BF16)
    dk_ref[0] = proj(_DK, 512).astype(BF16)
    dv_ref[0] = proj(_DV, 512).astype(BF16)

    a_lr = jnp.dot(hb, wga_ref[...], preferred_element_type=F32)
    z = jnp.dot(a_lr.astype(BF16), wa2_ref[...], preferred_element_type=F32) + ba_ref[...]
    log_sig = jnp.minimum(z, 0.0) - jnp.log1p(jnp.exp(-jnp.abs(z)))
    la_ref[0] = jnp.where(valid, log_sig / GLA_TAU, 0.0)


def _inproj(x, meta_pad, ln_g, ln_b, w_main, w_ga, wa2, ba):
    B, S, D = x.shape
    nb = S // ROW_BLOCK + 1
    LP = nb * ROW_BLOCK

    n_frame_blocks = S // ROW_BLOCK
    n_steps = pl.cdiv(LP, 2 * ROW_BLOCK)

    def row_spec(width):
        return pl.BlockSpec((1, 2 * ROW_BLOCK, width), lambda b, j: (b, j, 0))

    def frames(shift):
        return pl.BlockSpec((1, ROW_BLOCK, D),
                            lambda b, j: (b, jnp.clip(2 * j + shift, 0, n_frame_blocks - 1), 0))

    def full(shape):
        return pl.BlockSpec(shape, lambda b, j: (0,) * len(shape))

    out_shapes = [
        jax.ShapeDtypeStruct((B, LP, D), F32),
        jax.ShapeDtypeStruct((B, LP, 256), BF16),
        jax.ShapeDtypeStruct((B, LP, 256), F32),
        jax.ShapeDtypeStruct((B, LP, 512), BF16),
        jax.ShapeDtypeStruct((B, LP, 512), BF16),
        jax.ShapeDtypeStruct((B, LP, 256), F32),
        jax.ShapeDtypeStruct((B, LP, 512), BF16),
        jax.ShapeDtypeStruct((B, LP, 512), BF16),
        jax.ShapeDtypeStruct((B, LP, 512), BF16),
    ]
    return pl.pallas_call(
        _inproj_body,
        grid=(B, n_steps),
        in_specs=[
            frames(-1), frames(0),
            full((ROW_BLOCK, D)), full((1, D)), full((1, D)),
            full((D, _MAIN_WIDTH)), full((D, GLA_RANK)), full((GLA_RANK, 256)), full((1, 256)),
        ],
        out_specs=[row_spec(s.shape[-1]) for s in out_shapes],
        out_shape=out_shapes,
        compiler_params=pltpu.CompilerParams(
            dimension_semantics=("parallel", "parallel"), vmem_limit_bytes=VMEM_LIMIT),
        name="inproj",
    )(x, x, meta_pad, ln_g, ln_b, w_main, w_ga, wa2, ba)


def _gla_body(q_ref, k_ref, v_ref, r_ref, la_ref, g_ref, o_ref):
    n_groups = q_ref.shape[1] // ROW_BLOCK
    ri = lax.broadcasted_iota(jnp.int32, (ROW_BLOCK, ROW_BLOCK), 0)
    ci = lax.broadcasted_iota(jnp.int32, (ROW_BLOCK, ROW_BLOCK), 1)
    later = jnp.logical_and(ri // CHUNK == ci // CHUNK, ci > ri).astype(BF16)
    sr = lax.broadcasted_iota(jnp.int32, (GLA_HEADS * GLA_DV, GLA_HEADS * GLA_DK), 0)
    sc = lax.broadcasted_iota(jnp.int32, (GLA_HEADS * GLA_DV, GLA_HEADS * GLA_DK), 1)
    same_head = sr // GLA_DV == sc // GLA_DK
    gain = g_ref[...]
    st = jnp.zeros((GLA_HEADS * GLA_DV, GLA_HEADS * GLA_DK), F32)

    for grp in range(n_groups):
        g0 = grp * ROW_BLOCK
        la = la_ref[0, g0:g0 + ROW_BLOCK, :]
        hi, mid, lo = _split3_bf16(la)
        suffix = (jnp.dot(later, hi, preferred_element_type=F32)
                  + jnp.dot(later, mid, preferred_element_type=F32)
                  + jnp.dot(later, lo, preferred_element_type=F32))
        kdec = (k_ref[0, g0:g0 + ROW_BLOCK, :] * jnp.exp(suffix)).astype(BF16)
        for half in range(ROW_BLOCK // CHUNK):
            f = half * CHUNK
            r0 = g0 + f
            tot = suffix[f:f + 1, :] + la[f:f + 1, :]
            upd = lax.dot_general(v_ref[0, r0:r0 + CHUNK, :], kdec[f:f + CHUNK, :],
                                  (((0,), (0,)), ((), ())), preferred_element_type=F32)
            st = st * jnp.exp(tot) + jnp.where(same_head, upd, 0.0)
            o = lax.dot_general(q_ref[0, r0:r0 + CHUNK, :], st.astype(BF16),
                                (((1,), (1,)), ((), ())), preferred_element_type=F32)
            r = r_ref[0, r0:r0 + CHUNK, :].astype(F32)
            for h in range(GLA_HEADS):
                oh = o[:, h * GLA_DV:(h + 1) * GLA_DV]
                rh = r[:, h * GLA_DV:(h + 1) * GLA_DV]
                ms = jnp.mean(oh * oh, axis=-1, keepdims=True)
                out = oh * lax.rsqrt(ms + LN_EPS) * gain * (rh * jax.nn.sigmoid(rh))
                o_ref[0, r0:r0 + CHUNK, h * GLA_DV:(h + 1) * GLA_DV] = out.astype(BF16)


def _gla(gq, gk, gv, gr, la, norm_g):
    B, LP, _ = gq.shape

    def seq(width):
        return pl.BlockSpec((1, LP, width), lambda b: (b, 0, 0))

    return pl.pallas_call(
        _gla_body,
        grid=(B,),
        in_specs=[seq(256), seq(256), seq(512), seq(512), seq(256),
                  pl.BlockSpec((1, GLA_DV), lambda b: (0, 0))],
        out_specs=seq(512),
        out_shape=jax.ShapeDtypeStruct((B, LP, 512), BF16),
        compiler_params=pltpu.CompilerParams(
            dimension_semantics=("parallel",), vmem_limit_bytes=VMEM_LIMIT),
        name="gla",
    )(gq, gk, gv, gr, la, norm_g)


ATT_BLOCK = 256


def _diff_body(q_ref, k_ref, v_ref, lq1_ref, lk1_ref, lq2_ref, lk2_ref, g_ref, o_ref,
               dmask_ref):
    h = pl.program_id(1)
    slope = jnp.where(h == 0, 2.0 ** -2, jnp.where(h == 1, 2.0 ** -4, jnp.where(h == 2, 2.0 ** -6, 2.0 ** -8)))
    slope = slope.astype(F32) * LOG2E
    lam = (jnp.exp(jnp.sum(lq1_ref[...] * lk1_ref[...], axis=-1, keepdims=True))
           - jnp.exp(jnp.sum(lq2_ref[...] * lk2_ref[...], axis=-1, keepdims=True)) + LAMBDA_INIT)
    gain = g_ref[...] * (1.0 - LAMBDA_INIT)
    n_qblocks = (q_ref.shape[1] - ROW_BLOCK) // ATT_BLOCK
    nt = (((1,), (1,)), ((), ()))

    def split_q(q):
        lane = lax.broadcasted_iota(jnp.int32, q.shape, 1)
        zero = jnp.zeros_like(q)
        return jnp.where(lane < DIFF_DH, q, zero), jnp.where(lane >= DIFF_DH, q, zero)

    def with_ones(v):
        return jnp.concatenate([v, jnp.ones_like(v)], axis=1)

    def softmax_av(qz, kk, vext, add_bias):
        s = add_bias(lax.dot_general(qz, kk, nt, preferred_element_type=F32))
        m = jnp.max(s, axis=-1, keepdims=True)
        p = jnp.exp2(s - m).astype(BF16)
        return jnp.dot(p, vext, preferred_element_type=F32)

    def finish(a1, a2):
        o = a1[:, :DIFF_DV] / a1[:, DIFF_DV:] - lam * (a2[:, :DIFF_DV] / a2[:, DIFF_DV:])
        ms = jnp.mean(o * o, axis=-1, keepdims=True)
        return (o * lax.rsqrt(ms + LN_EPS) * gain).astype(BF16)

    r = lax.broadcasted_iota(jnp.int32, (ROW_BLOCK, ROW_BLOCK), 0)
    c = lax.broadcasted_iota(jnp.int32, (ROW_BLOCK, ROW_BLOCK), 1)
    ok = jnp.logical_and(c // CHUNK <= r // CHUNK, c >= FRONT_PAD)
    bias_lead = jnp.where(ok, -slope * jnp.abs(r - c).astype(F32), NEG_INF)
    q1z, q2z = split_q(q_ref[0, 0:ROW_BLOCK, :])
    k_lead = k_ref[0, 0:ROW_BLOCK, :]
    v_lead = with_ones(v_ref[0, 0:ROW_BLOCK, :])
    o_ref[0, 0:ROW_BLOCK, :] = finish(softmax_av(q1z, k_lead, v_lead, lambda s: s + bias_lead),
                                      softmax_av(q2z, k_lead, v_lead, lambda s: s + bias_lead))

    r = lax.broadcasted_iota(jnp.int32, (ATT_BLOCK, ATT_BLOCK), 0)
    c = lax.broadcasted_iota(jnp.int32, (ATT_BLOCK, ATT_BLOCK), 1)
    rel = jnp.where(c <= r, c, 2 * r - c).astype(F32)
    dmask_ref[...] = jnp.where(c // CHUNK <= r // CHUNK, slope * rel, NEG_INF)

    for jq in range(n_qblocks):
        qbase = ROW_BLOCK + jq * ATT_BLOCK
        n_keys = qbase + ATT_BLOCK
        q1z, q2z = split_q(q_ref[0, qbase:qbase + ATT_BLOCK, :])
        kk = k_ref[0, 0:n_keys, :]
        vext = with_ones(v_ref[0, 0:n_keys, :])
        col = lax.broadcasted_iota(jnp.int32, (1, qbase), 1)
        col_bias = jnp.where(col >= FRONT_PAD, slope * (col - qbase).astype(F32), NEG_INF)

        def add_bias(s, col_bias=col_bias, qbase=qbase):
            return jnp.concatenate([s[:, :qbase] + col_bias, s[:, qbase:] + dmask_ref[...]], axis=1)

        o_ref[0, qbase:qbase + ATT_BLOCK, :] = finish(softmax_av(q1z, kk, vext, add_bias),
                                                      softmax_av(q2z, kk, vext, add_bias))


def _diff_attention(dq, dk, dv, lq1, lk1, lq2, lk2, norm_g):
    B, LP, W = dq.shape
    small = pl.BlockSpec((1, DIFF_DH), lambda b, h: (0, 0))
    seq = pl.BlockSpec((1, LP, 2 * DIFF_DH), lambda b, h: (b, 0, h))
    return pl.pallas_call(
        _diff_body,
        grid=(B, DIFF_HEADS),
        in_specs=[seq, seq, seq, small, small, small, small,
                  pl.BlockSpec((1, DIFF_DV), lambda b, h: (0, 0))],
        out_specs=seq,
        out_shape=jax.ShapeDtypeStruct((B, LP, W), BF16),
        scratch_shapes=[pltpu.VMEM((ATT_BLOCK, ATT_BLOCK), F32)],
        compiler_params=pltpu.CompilerParams(
            dimension_semantics=("parallel", "parallel"), vmem_limit_bytes=VMEM_LIMIT),
        name="diff_attn",
    )(dq, dk, dv, lq1, lk1, lq2, lk2, norm_g)


def _route_body(og_ref, od_ref, h0_ref, wog_ref, wod_ref, g_ref, b_ref, rw_ref, rb_ref,
                h1_ref, route_ref, keys_ref, cnt_ref, carry_ref, *, rows_per_seq, id_bits):
    t = pl.program_id(0)

    @pl.when(t == 0)
    def _():
        carry_ref[...] = jnp.zeros_like(carry_ref)

    mix = (jnp.dot(og_ref[...], wog_ref[...], preferred_element_type=F32)
           + jnp.dot(od_ref[...], wod_ref[...], preferred_element_type=F32))
    h1 = _layer_norm(DEEPNORM_ALPHA * h0_ref[...] + mix, g_ref[...], b_ref[...])
    _store_token_tiled(h1_ref, h1)

    hh, hm, _ = _split3_bf16(h1)
    wh, wm, _ = _split3_bf16(rw_ref[...])
    head = jnp.dot(hh, jnp.concatenate([wh, wm], axis=1), preferred_element_type=F32)
    logits = (rb_ref[...] + head[:, :N_EXPERTS] + head[:, N_EXPERTS:]
              + jnp.dot(hm, wh, preferred_element_type=F32))

    T = logits.shape[0]
    lane = lax.broadcasted_iota(jnp.int32, (T, N_EXPERTS), 1)
    grow = t * T + lax.broadcasted_iota(jnp.int32, (T, 1), 0)
    valid = (grow % rows_per_seq) >= FRONT_PAD

    work = logits
    top_v, top_i = [], []
    for _ in range(TOP_K):
        mx = jnp.max(work, axis=-1, keepdims=True)
        idx = jnp.min(jnp.where(work == mx, lane, N_EXPERTS), axis=-1, keepdims=True)
        top_v.append(mx)
        top_i.append(idx)
        work = jnp.where(lane == idx, -jnp.inf, work)
    ex = [jnp.exp(v - top_v[0]) for v in top_v]
    den = ex[0] + ex[1] + ex[2] + ex[3]
    gates = [e / den for e in ex]

    onehot = jnp.zeros((T, N_EXPERTS), F32)
    for idx in top_i:
        onehot = onehot + (lane == idx).astype(F32)
    onehot = jnp.where(valid, onehot, 0.0)

    carry_ref[...] = carry_ref[...] + jnp.sum(onehot, axis=0, keepdims=True)
    cnt_ref[...] = carry_ref[...]

    lane_o = lax.broadcasted_iota(jnp.int32, (T, 128), 1)
    packed = jnp.zeros((T, 128), jnp.int32)
    for k in range(TOP_K):
        key = jnp.where(valid, top_i[k], N_EXPERTS) * (1 << id_bits) + (grow * TOP_K + k)
        packed = jnp.where(lane_o == k, top_i[k], packed)
        packed = jnp.where(lane_o == TOP_K + k, key, packed)
        packed = jnp.where(lane_o == 2 * TOP_K + k, lax.bitcast_convert_type(gates[k], jnp.int32), packed)
    route_ref[...] = packed
    by_lane = packed.T
    keys_ref[...] = jnp.concatenate(
        [by_lane[TOP_K + k:TOP_K + k + 1, q * 128:(q + 1) * 128] for k in range(TOP_K) for q in range(T // 128)],
        axis=0)


def _outproj_route(og, od, h0, wo_g, wo_d, ln_g, ln_b, rw, rb, rows_per_seq):
    NP, D = h0.shape
    T = ROUTE_TILE

    def rows(width):
        return pl.BlockSpec((T, width), lambda t: (t, 0))

    def full(shape):
        return pl.BlockSpec(shape, lambda t: (0,) * len(shape))

    key_rows = T * TOP_K // 128
    return pl.pallas_call(
        functools.partial(_route_body, rows_per_seq=rows_per_seq, id_bits=(NP * TOP_K - 1).bit_length()),
        grid=(NP // T,),
        in_specs=[rows(512), rows(512), rows(D), full((512, D)), full((512, D)),
                  full((1, D)), full((1, D)), full((D, N_EXPERTS)), full((1, N_EXPERTS))],
        out_specs=[pl.BlockSpec((T * TOKEN_SUBROWS, 128), lambda t: (t, 0)), rows(128),
                   pl.BlockSpec((key_rows, 128), lambda t: (t, 0)), full((1, N_EXPERTS))],
        out_shape=[jax.ShapeDtypeStruct((NP * TOKEN_SUBROWS, 128), F32),
                   jax.ShapeDtypeStruct((NP, 128), jnp.int32),
                   jax.ShapeDtypeStruct((NP * TOP_K // 128, 128), jnp.int32),
                   jax.ShapeDtypeStruct((1, N_EXPERTS), F32)],
        scratch_shapes=[pltpu.VMEM((1, N_EXPERTS), F32)],
        compiler_params=pltpu.CompilerParams(
            dimension_semantics=("arbitrary",), vmem_limit_bytes=VMEM_LIMIT),
        name="outproj_route",
    )(og, od, h0, wo_g, wo_d, ln_g, ln_b, rw, rb)


def _expert_body(order_ref, p0_ref, bstart_ref, nblk_ref, ntot_ref,
                 h1t_hbm, wgu_ref, bg_ref, bu_ref, wd_ref, bd_ref, yk_hbm,
                 wg_s, wu_s, wd_s, xb0, xb1, xb2, yb0, yb1, yb2, zbuf, gsem, ssem, zsem,
                 *, n_tokens, rows_per_seq):
    e = pl.program_id(0)
    n_total = ntot_ref[0]
    R = MOE_ROWS
    TS = TOKEN_SUBROWS
    xb = (xb0, xb1, xb2)
    yb = (yb0, yb1, yb2)
    NB = len(xb)

    def gather_start(b, slot):
        p0 = p0_ref[b]
        for r in range(R):
            tok = order_ref[p0 + r] >> 2
            pltpu.make_async_copy(
                h1t_hbm.at[pl.ds(pl.multiple_of(tok * TS, TS), TS), :],
                xb[slot].at[pl.ds(r * TS, TS), :], gsem.at[slot]).start(priority=r % 2)

    def gather_wait(slot):
        pltpu.make_async_copy(h1t_hbm.at[pl.ds(0, R * TS), :], xb[slot], gsem.at[slot]).wait()

    def scatter_start(b, slot):
        p0 = p0_ref[b]
        for r in range(R):
            row = order_ref[p0 + r] * TS
            pltpu.make_async_copy(
                yb[slot].at[pl.ds(r * TS, TS), :],
                yk_hbm.at[pl.ds(pl.multiple_of(row, TS), TS), :], ssem.at[slot]).start(priority=r % 2)

    def scatter_wait(slot):
        pltpu.make_async_copy(yb[slot], yk_hbm.at[pl.ds(0, R * TS), :], ssem.at[slot]).wait()

    @pl.when(e == 0)
    def _():
        zbuf[...] = jnp.zeros_like(zbuf)
        lead = FRONT_PAD * TS
        fills = [((s * rows_per_seq * TOP_K + k * FRONT_PAD) * TS, lead)
                 for s in range(n_tokens // rows_per_seq) for k in range(TOP_K)]
        copies = [pltpu.make_async_copy(zbuf.at[pl.ds(0, n), :], yk_hbm.at[pl.ds(o, n), :], zsem)
                  for o, n in fills]
        for cp in copies:
            cp.start()
        for cp in copies:
            cp.wait()
        gather_start(0, 0)
        gather_start(jnp.minimum(1, n_total - 1), 1)

    @pl.when(nblk_ref[e] > 0)
    def _():
        W, H = MXU_TILE, MXU_TILE // 2
        r = lax.broadcasted_iota(jnp.int32, (W, W), 0)
        c = lax.broadcasted_iota(jnp.int32, (W, W), 1)
        perm = (r == jnp.where(c < H, 2 * c, 2 * (c - H) + 1)).astype(BF16)
        for tt in range(2 * D_FF // W):
            wt = wgu_ref[0, :, tt * W:(tt + 1) * W].astype(BF16)
            sp = jnp.dot(wt, perm, preferred_element_type=F32)
            wg_s[:, tt * H:(tt + 1) * H] = sp[:, :H].astype(BF16)
            wu_s[:, tt * H:(tt + 1) * H] = sp[:, H:].astype(BF16)
        wd_s[...] = wd_ref[0].astype(BF16)

    def run_block(b, slot):
        prv = (slot + 2) % NB
        gather_wait(slot)
        gather_start(jnp.minimum(b + 2, n_total - 1), prv)
        x = _load_token_tiled(xb[slot], R).astype(BF16)
        gt = jnp.dot(x, wg_s[...], preferred_element_type=F32) + bg_ref[0]
        up = jnp.dot(x, wu_s[...], preferred_element_type=F32) + bu_ref[0]
        gt = jnp.minimum(gt, SWIGLU_LIMIT)
        up = jnp.clip(up, -SWIGLU_LIMIT, SWIGLU_LIMIT)
        act = (up + 1.0) * (gt * jax.nn.sigmoid(SWIGLU_ALPHA * gt))
        y = jnp.dot(act.astype(BF16), wd_s[...], preferred_element_type=F32) + bd_ref[0]
        _store_token_tiled(yb[slot], y)

        @pl.when(b >= 1)
        def _():
            scatter_wait(prv)
        scatter_start(b, slot)

    def block(b, carry):
        for slot in range(NB):
            pl.when(b % NB == slot)(functools.partial(run_block, b, slot))
        return carry

    b0 = bstart_ref[e]
    lax.fori_loop(b0, b0 + nblk_ref[e], block, 0)

    @pl.when(e == pl.num_programs(0) - 1)
    def _():
        last = n_total - 1
        for slot in range(NB):
            @pl.when(last % NB == slot)
            def _(slot=slot):
                gather_wait((slot + 1) % NB)
                gather_wait((slot + 2) % NB)
                scatter_wait(slot)


def _experts(order, blk_p0, blk_start, n_blk, n_total, h1t, w_gate_up, b_gate, b_up, w_down,
             b_down, n_tokens, rows_per_seq):
    D = D_MODEL
    R = MOE_ROWS
    buf = pltpu.VMEM((R * TOKEN_SUBROWS, 128), F32)
    per_expert = lambda e, *_: (e, 0, 0)
    grid_spec = pltpu.PrefetchScalarGridSpec(
        num_scalar_prefetch=5,
        grid=(N_EXPERTS,),
        in_specs=[
            pl.BlockSpec(memory_space=pl.ANY),
            pl.BlockSpec((1, D, 2 * D_FF), per_expert),
            pl.BlockSpec((1, 1, D_FF), per_expert),
            pl.BlockSpec((1, 1, D_FF), per_expert),
            pl.BlockSpec((1, D_FF, D), per_expert),
            pl.BlockSpec((1, 1, D), per_expert),
        ],
        out_specs=pl.BlockSpec(memory_space=pl.ANY),
        scratch_shapes=[pltpu.VMEM((D, D_FF), BF16), pltpu.VMEM((D, D_FF), BF16), pltpu.VMEM((D_FF, D), BF16),
                        buf, buf, buf, buf, buf, buf, pltpu.VMEM((FRONT_PAD * TOKEN_SUBROWS, 128), F32),
                        pltpu.SemaphoreType.DMA((3,)), pltpu.SemaphoreType.DMA((3,)), pltpu.SemaphoreType.DMA],
    )
    out_rows = TOP_K * n_tokens * TOKEN_SUBROWS
    return pl.pallas_call(
        functools.partial(_expert_body, n_tokens=n_tokens, rows_per_seq=rows_per_seq),
        grid_spec=grid_spec,
        out_shape=jax.ShapeDtypeStruct((out_rows, 128), F32),
        compiler_params=pltpu.CompilerParams(
            dimension_semantics=("arbitrary",), vmem_limit_bytes=EXPERT_VMEM_LIMIT),
        name="experts",
    )(order, blk_p0, blk_start, n_blk, n_total, h1t, w_gate_up, b_gate, b_up, w_down, b_down)


COMBINE_BLOCKS = 4


def _combine_body(*refs):
    per_block = 2 + TOP_K
    g_ref, b_ref, o_ref = refs[COMBINE_BLOCKS * per_block:]
    for i in range(COMBINE_BLOCKS):
        h1_ref, *y_refs, route_ref = refs[i * per_block:(i + 1) * per_block]
        route = route_ref[...]
        y = jnp.zeros((ROW_BLOCK, D_MODEL), F32)
        for k, yk_ref in enumerate(y_refs):
            gate = lax.bitcast_convert_type(route[:, 2 * TOP_K + k:2 * TOP_K + k + 1], F32)
            y = y + gate * _load_token_tiled(yk_ref.reshape(ROW_BLOCK * TOKEN_SUBROWS, 128), ROW_BLOCK)
        h1 = _load_token_tiled(h1_ref, ROW_BLOCK)
        o_ref[0, i * ROW_BLOCK:(i + 1) * ROW_BLOCK, :] = _layer_norm(
            DEEPNORM_ALPHA * h1 + y, g_ref[...], b_ref[...])


def _combine(h1t, yk, route, ln_g, ln_b, batch, seq_len):
    nb = seq_len // ROW_BLOCK
    blocks_per_seq = nb + 1
    tile_rows = ROW_BLOCK * TOKEN_SUBROWS
    yk = yk.reshape(-1, TOP_K, TOKEN_SUBROWS, 128)

    def token_block(i):
        return lambda b, j: b * blocks_per_seq + COMBINE_BLOCKS * j + i + 1

    in_specs, operands = [], []
    for i in range(COMBINE_BLOCKS):
        tb = token_block(i)
        in_specs.append(pl.BlockSpec((tile_rows, 128), lambda b, j, tb=tb: (tb(b, j), 0)))
        in_specs += [pl.BlockSpec((ROW_BLOCK, None, TOKEN_SUBROWS, 128), lambda b, j, tb=tb, k=k: (tb(b, j), k, 0, 0))
                     for k in range(TOP_K)]
        in_specs.append(pl.BlockSpec((ROW_BLOCK, 128), lambda b, j, tb=tb: (tb(b, j), 0)))
        operands += [h1t] + [yk] * TOP_K + [route]
    vec = pl.BlockSpec((1, D_MODEL), lambda b, j: (0, 0))
    return pl.pallas_call(
        _combine_body,
        grid=(batch, nb // COMBINE_BLOCKS),
        in_specs=in_specs + [vec, vec],
        out_specs=pl.BlockSpec((1, COMBINE_BLOCKS * ROW_BLOCK, D_MODEL), lambda b, j: (b, j, 0)),
        out_shape=jax.ShapeDtypeStruct((batch, seq_len, D_MODEL), F32),
        compiler_params=pltpu.CompilerParams(
            dimension_semantics=("parallel", "parallel"), vmem_limit_bytes=VMEM_LIMIT),
        name="combine",
    )(*operands, ln_g, ln_b)


def kernel(x, meta_tokens, ln_emb_g, ln_emb_b, w_in, gla_wa2, gla_ba, gla_norm_g, diff_lambda_q1, diff_lambda_k1, diff_lambda_q2, diff_lambda_k2, diff_norm_g, w_out, ln1_g, ln1_b, router_w, router_b, w_gate_up, b_gate_up, w_down, b_down, ln2_g, ln2_b):
    B, S, D = x.shape
    LP = S + ROW_BLOCK
    NP = B * LP
    row = lambda v: v.reshape(1, -1)

    w = w_in[0]
    w_main = jnp.concatenate([w[:, :1536], w[:, 1552:]], axis=1).astype(BF16)
    w_ga = w[:, 1536:1552].astype(BF16)
    meta_pad = jnp.pad(meta_tokens, ((FRONT_PAD, 0), (0, 0)))

    h0, gq, gk, gv, gr, la, dq, dk, dv = _inproj(
        x, meta_pad, row(ln_emb_g), row(ln_emb_b), w_main, w_ga,
        gla_wa2[0].astype(BF16), row(gla_ba[0]))

    og = _gla(gq, gk, gv, gr, la, row(gla_norm_g[0]))
    od = _diff_attention(dq, dk, dv, row(diff_lambda_q1[0]), row(diff_lambda_k1[0]),
                         row(diff_lambda_q2[0]), row(diff_lambda_k2[0]), row(diff_norm_g[0]))

    wo = w_out[0].astype(BF16)
    h1t, route, keys, counts = _outproj_route(
        og.reshape(NP, 512), od.reshape(NP, 512), h0.reshape(NP, D), wo[:512], wo[512:],
        row(ln1_g[0]), row(ln1_b[0]), router_w[0], row(router_b[0]), LP)

    R = MOE_ROWS
    i32 = jnp.int32
    counts = counts[0].astype(i32)
    n_blk = (counts + R - 1) // R
    blk_end = jnp.cumsum(n_blk)
    blk_start = blk_end - n_blk
    grp_start = jnp.cumsum(counts) - counts
    n_total = blk_end[-1:]
    n_assign_max = B * (S + N_META) * TOP_K
    max_blocks = (n_assign_max + N_EXPERTS * (R - 1)) // R
    g = jnp.minimum(jnp.arange(max_blocks, dtype=i32), n_total[0] - 1)
    is_e = (jnp.minimum(jnp.sum(g[:, None] >= blk_end[None, :], axis=1), N_EXPERTS - 1)[:, None]
            == jnp.arange(N_EXPERTS)[None, :])
    pick = lambda v: jnp.sum(jnp.where(is_e, v[None, :], 0), axis=1)
    local = (g - pick(blk_start)) * R
    blk_p0 = (pick(grp_start) + local).astype(i32)

    assert R <= B * FRONT_PAD * TOP_K
    order = jnp.sort(keys.reshape(-1)) & ((1 << (NP * TOP_K - 1).bit_length()) - 1)

    bgu = b_gate_up[0].reshape(N_EXPERTS, D_FF, 2)
    yk = _experts(order, blk_p0, blk_start.astype(i32), n_blk.astype(i32), n_total.astype(i32),
                  h1t, w_gate_up[0],
                  bgu[:, :, 0].reshape(N_EXPERTS, 1, D_FF), bgu[:, :, 1].reshape(N_EXPERTS, 1, D_FF),
                  w_down[0], b_down[0].reshape(N_EXPERTS, 1, D), NP, LP)

    return _combine(h1t, yk, route, row(ln2_g[0]), row(ln2_b[0]), B, S)
```

```python
import functools
import math

import jax
import jax.numpy as jnp
from jax import lax
from jax.experimental import pallas as pl
from jax.experimental.pallas import tpu as pltpu

F32 = jnp.float32
BF16 = jnp.bfloat16

D_MODEL = 1024
N_META = 16
ROW_BLOCK = 128
FRONT_PAD = ROW_BLOCK - N_META
CHUNK = 64

GLA_HEADS = 4
GLA_DK = 64
GLA_DV = 128
GLA_RANK = 16
GLA_TAU = 16.0
DIFF_HEADS = 4
DIFF_DH = 64
DIFF_DV = 128

N_EXPERTS = 32
TOP_K = 4
D_FF = 1024
SWIGLU_LIMIT = 7.0
SWIGLU_ALPHA = 1.702
MOE_ROWS = 256

DEEPNORM_ALPHA = 2.0 ** 0.25
LAMBDA_INIT = 0.8 - 0.6 * math.exp(0.0)
LN_EPS = 1e-5
NEG_INF = -1e30
LOG2E = math.log2(math.e)

MXU_TILE = 256
ROUTE_TILE = 512
VMEM_LIMIT = 48 * 1024 * 1024
EXPERT_VMEM_LIMIT = 56 * 1024 * 1024

_GQ, _GK, _GV, _GR, _DQ, _DK, _DV = 0, 256, 512, 1024, 1536, 2048, 2560
_MAIN_WIDTH = 3072


def _layer_norm(x, g, b):
    mu = jnp.mean(x, axis=-1, keepdims=True)
    xc = x - mu
    var = jnp.mean(xc * xc, axis=-1, keepdims=True)
    return xc * lax.rsqrt(var + LN_EPS) * g + b


def _split3_bf16(x):
    hi = x.astype(BF16)
    r1 = x - hi.astype(F32)
    mid = r1.astype(BF16)
    lo = (r1 - mid.astype(F32)).astype(BF16)
    return hi, mid, lo


TOKEN_SUBROWS = D_MODEL // 128


def _store_token_tiled(ref, val, first_token=0):
    n = val.shape[0]
    for s in range(TOKEN_SUBROWS):
        ref[pl.ds(first_token * TOKEN_SUBROWS + s, n, stride=TOKEN_SUBROWS), :] = val[:, s * 128:(s + 1) * 128]


def _load_token_tiled(ref, n, first_row=0):
    return jnp.concatenate(
        [ref[pl.ds(first_row + s, n, stride=TOKEN_SUBROWS), :] for s in range(TOKEN_SUBROWS)], axis=1)


def _inproj_body(xa_ref, xb_ref, meta_ref, g_ref, b_ref, w_ref, wga_ref, wa2_ref, ba_ref,
                 h0_ref, gq_ref, gk_ref, gv_ref, gr_ref, la_ref, dq_ref, dk_ref, dv_ref):
    j = pl.program_id(1)
    rows = lax.broadcasted_iota(jnp.int32, (2 * ROW_BLOCK, 1), 0)
    valid = jnp.logical_or(j > 0, rows >= FRONT_PAD)
    xin = jnp.concatenate([jnp.where(j > 0, xa_ref[0], meta_ref[...]), xb_ref[0]], axis=0)
    h = jnp.where(valid, _layer_norm(xin, g_ref[...], b_ref[...]), 0.0)
    h0_ref[0] = h
    hb = h.astype(BF16)

    def proj(off, width):
        return jnp.dot(hb, w_ref[:, off:off + width], preferred_element_type=F32)

    gq_ref[0] = (proj(_GQ, 256) * GLA_DK ** -0.5).astype(BF16)
    gk_ref[0] = proj(_GK, 256)
    gv_ref[0] = proj(_GV, 512).astype(BF16)
    gr_ref[0] = proj(_GR, 512).astype(BF16)
    dq_ref[0] = (proj(_DQ, 512) * (DIFF_DH ** -0.5 * LOG2E)).astype(BF16)
    dk_ref[0] = proj(_DK, 512).astype(BF16)
    dv_ref[0] = proj(_DV, 512).astype(BF16)

    a_lr = jnp.dot(hb, wga_ref[...], preferred_element_type=F32)
    z = jnp.dot(a_lr.astype(BF16), wa2_ref[...], preferred_element_type=F32) + ba_ref[...]
    log_sig = jnp.minimum(z, 0.0) - jnp.log1p(jnp.exp(-jnp.abs(z)))
    la_ref[0] = jnp.where(valid, log_sig / GLA_TAU, 0.0)


def _inproj(x, meta_pad, ln_g, ln_b, w_main, w_ga, wa2, ba):
    B, S, D = x.shape
    nb = S // ROW_BLOCK + 1
    LP = nb * ROW_BLOCK

    n_frame_blocks = S // ROW_BLOCK
    n_steps = pl.cdiv(LP, 2 * ROW_BLOCK)

    def row_spec(width):
        return pl.BlockSpec((1, 2 * ROW_BLOCK, width), lambda b, j: (b, j, 0))

    def frames(shift):
        return pl.BlockSpec((1, ROW_BLOCK, D),
                            lambda b, j: (b, jnp.clip(2 * j + shift, 0, n_frame_blocks - 1), 0))

    def full(shape):
        return pl.BlockSpec(shape, lambda b, j: (0,) * len(shape))

    out_shapes = [
        jax.ShapeDtypeStruct((B, LP, D), F32),
        jax.ShapeDtypeStruct((B, LP, 256), BF16),
        jax.ShapeDtypeStruct((B, LP, 256), F32),
        jax.ShapeDtypeStruct((B, LP, 512), BF16),
        jax.ShapeDtypeStruct((B, LP, 512), BF16),
        jax.ShapeDtypeStruct((B, LP, 256), F32),
        jax.ShapeDtypeStruct((B, LP, 512), BF16),
        jax.ShapeDtypeStruct((B, LP, 512), BF16),
        jax.ShapeDtypeStruct((B, LP, 512), BF16),
    ]
    return pl.pallas_call(
        _inproj_body,
        grid=(B, n_steps),
        in_specs=[
            frames(-1), frames(0),
            full((ROW_BLOCK, D)), full((1, D)), full((1, D)),
            full((D, _MAIN_WIDTH)), full((D, GLA_RANK)), full((GLA_RANK, 256)), full((1, 256)),
        ],
        out_specs=[row_spec(s.shape[-1]) for s in out_shapes],
        out_shape=out_shapes,
        compiler_params=pltpu.CompilerParams(
            dimension_semantics=("parallel", "parallel"), vmem_limit_bytes=VMEM_LIMIT),
        name="inproj",
    )(x, x, meta_pad, ln_g, ln_b, w_main, w_ga, wa2, ba)


def _gla_body(q_ref, k_ref, v_ref, r_ref, la_ref, g_ref, o_ref):
    n_groups = q_ref.shape[1] // ROW_BLOCK
    ri = lax.broadcasted_iota(jnp.int32, (ROW_BLOCK, ROW_BLOCK), 0)
    ci = lax.broadcasted_iota(jnp.int32, (ROW_BLOCK, ROW_BLOCK), 1)
    later = jnp.logical_and(ri // CHUNK == ci // CHUNK, ci > ri).astype(BF16)
    sr = lax.broadcasted_iota(jnp.int32, (GLA_HEADS * GLA_DV, GLA_HEADS * GLA_DK), 0)
    sc = lax.broadcasted_iota(jnp.int32, (GLA_HEADS * GLA_DV, GLA_HEADS * GLA_DK), 1)
    same_head = sr // GLA_DV == sc // GLA_DK
    gain = g_ref[...]
    st = jnp.zeros((GLA_HEADS * GLA_DV, GLA_HEADS * GLA_DK), F32)

    for grp in range(n_groups):
        g0 = grp * ROW_BLOCK
        la = la_ref[0, g0:g0 + ROW_BLOCK, :]
        hi, mid, lo = _split3_bf16(la)
        suffix = (jnp.dot(later, hi, preferred_element_type=F32)
                  + jnp.dot(later, mid, preferred_element_type=F32)
                  + jnp.dot(later, lo, preferred_element_type=F32))
        kdec = (k_ref[0, g0:g0 + ROW_BLOCK, :] * jnp.exp(suffix)).astype(BF16)
        for half in range(ROW_BLOCK // CHUNK):
            f = half * CHUNK
            r0 = g0 + f
            tot = suffix[f:f + 1, :] + la[f:f + 1, :]
            upd = lax.dot_general(v_ref[0, r0:r0 + CHUNK, :], kdec[f:f + CHUNK, :],
                                  (((0,), (0,)), ((), ())), preferred_element_type=F32)
            st = st * jnp.exp(tot) + jnp.where(same_head, upd, 0.0)
            o = lax.dot_general(q_ref[0, r0:r0 + CHUNK, :], st.astype(BF16),
                                (((1,), (1,)), ((), ())), preferred_element_type=F32)
            r = r_ref[0, r0:r0 + CHUNK, :].astype(F32)
            for h in range(GLA_HEADS):
                oh = o[:, h * GLA_DV:(h + 1) * GLA_DV]
                rh = r[:, h * GLA_DV:(h + 1) * GLA_DV]
                ms = jnp.mean(oh * oh, axis=-1, keepdims=True)
                out = oh * lax.rsqrt(ms + LN_EPS) * gain * (rh * jax.nn.sigmoid(rh))
                o_ref[0, r0:r0 + CHUNK, h * GLA_DV:(h + 1) * GLA_DV] = out.astype(BF16)


def _gla(gq, gk, gv, gr, la, norm_g):
    B, LP, _ = gq.shape

    def seq(width):
        return pl.BlockSpec((1, LP, width), lambda b: (b, 0, 0))

    return pl.pallas_call(
        _gla_body,
        grid=(B,),
        in_specs=[seq(256), seq(256), seq(512), seq(512), seq(256),
                  pl.BlockSpec((1, GLA_DV), lambda b: (0, 0))],
        out_specs=seq(512),
        out_shape=jax.ShapeDtypeStruct((B, LP, 512), BF16),
        compiler_params=pltpu.CompilerParams(
            dimension_semantics=("parallel",), vmem_limit_bytes=VMEM_LIMIT),
        name="gla",
    )(gq, gk, gv, gr, la, norm_g)


ATT_BLOCK = 256


def _diff_body(q_ref, k_ref, v_ref, lq1_ref, lk1_ref, lq2_ref, lk2_ref, g_ref, o_ref,
               dmask_ref):
    h = pl.program_id(1)
    slope = jnp.where(h == 0, 2.0 ** -2, jnp.where(h == 1, 2.0 ** -4, jnp.where(h == 2, 2.0 ** -6, 2.0 ** -8)))
    slope = slope.astype(F32) * LOG2E
    lam = (jnp.exp(jnp.sum(lq1_ref[...] * lk1_ref[...], axis=-1, keepdims=True))
           - jnp.exp(jnp.sum(lq2_ref[...] * lk2_ref[...], axis=-1, keepdims=True)) + LAMBDA_INIT)
    gain = g_ref[...] * (1.0 - LAMBDA_INIT)
    n_qblocks = (q_ref.shape[1] - ROW_BLOCK) // ATT_BLOCK
    nt = (((1,), (1,)), ((), ()))

    def split_q(q):
        lane = lax.broadcasted_iota(jnp.int32, q.shape, 1)
        zero = jnp.zeros_like(q)
        return jnp.where(lane < DIFF_DH, q, zero), jnp.where(lane >= DIFF_DH, q, zero)

    def with_ones(v):
        return jnp.concatenate([v, jnp.ones_like(v)], axis=1)

    def softmax_av(qz, kk, vext, add_bias):
        s = add_bias(lax.dot_general(qz, kk, nt, preferred_element_type=F32))
        m = jnp.max(s, axis=-1, keepdims=True)
        p = jnp.exp2(s - m).astype(BF16)
        return jnp.dot(p, vext, preferred_element_type=F32)

    def finish(a1, a2):
        o = a1[:, :DIFF_DV] / a1[:, DIFF_DV:] - lam * (a2[:, :DIFF_DV] / a2[:, DIFF_DV:])
        ms = jnp.mean(o * o, axis=-1, keepdims=True)
        return (o * lax.rsqrt(ms + LN_EPS) * gain).astype(BF16)

    r = lax.broadcasted_iota(jnp.int32, (ROW_BLOCK, ROW_BLOCK), 0)
    c = lax.broadcasted_iota(jnp.int32, (ROW_BLOCK, ROW_BLOCK), 1)
    ok = jnp.logical_and(c // CHUNK <= r // CHUNK, c >= FRONT_PAD)
    bias_lead = jnp.where(ok, -slope * jnp.abs(r - c).astype(F32), NEG_INF)
    q1z, q2z = split_q(q_ref[0, 0:ROW_BLOCK, :])
    k_lead = k_ref[0, 0:ROW_BLOCK, :]
    v_lead = with_ones(v_ref[0, 0:ROW_BLOCK, :])
    o_ref[0, 0:ROW_BLOCK, :] = finish(softmax_av(q1z, k_lead, v_lead, lambda s: s + bias_lead),
                                      softmax_av(q2z, k_lead, v_lead, lambda s: s + bias_lead))

    r = lax.broadcasted_iota(jnp.int32, (ATT_BLOCK, ATT_BLOCK), 0)
    c = lax.broadcasted_iota(jnp.int32, (ATT_BLOCK, ATT_BLOCK), 1)
    rel = jnp.where(c <= r, c, 2 * r - c).astype(F32)
    dmask_ref[...] = jnp.where(c // CHUNK <= r // CHUNK, slope * rel, NEG_INF)

    for jq in range(n_qblocks):
        qbase = ROW_BLOCK + jq * ATT_BLOCK
        n_keys = qbase + ATT_BLOCK
        q1z, q2z = split_q(q_ref[0, qbase:qbase + ATT_BLOCK, :])
        kk = k_ref[0, 0:n_keys, :]
        vext = with_ones(v_ref[0, 0:n_keys, :])
        col = lax.broadcasted_iota(jnp.int32, (1, qbase), 1)
        col_bias = jnp.where(col >= FRONT_PAD, slope * (col - qbase).astype(F32), NEG_INF)

        def add_bias(s, col_bias=col_bias, qbase=qbase):
            return jnp.concatenate([s[:, :qbase] + col_bias, s[:, qbase:] + dmask_ref[...]], axis=1)

        o_ref[0, qbase:qbase + ATT_BLOCK, :] = finish(softmax_av(q1z, kk, vext, add_bias),
                                                      softmax_av(q2z, kk, vext, add_bias))


def _diff_attention(dq, dk, dv, lq1, lk1, lq2, lk2, norm_g):
    B, LP, W = dq.shape
    small = pl.BlockSpec((1, DIFF_DH), lambda b, h: (0, 0))
    seq = pl.BlockSpec((1, LP, 2 * DIFF_DH), lambda b, h: (b, 0, h))
    return pl.pallas_call(
        _diff_body,
        grid=(B, DIFF_HEADS),
        in_specs=[seq, seq, seq, small, small, small, small,
                  pl.BlockSpec((1, DIFF_DV), lambda b, h: (0, 0))],
        out_specs=seq,
        out_shape=jax.ShapeDtypeStruct((B, LP, W), BF16),
        scratch_shapes=[pltpu.VMEM((ATT_BLOCK, ATT_BLOCK), F32)],
        compiler_params=pltpu.CompilerParams(
            dimension_semantics=("parallel", "parallel"), vmem_limit_bytes=VMEM_LIMIT),
        name="diff_attn",
    )(dq, dk, dv, lq1, lk1, lq2, lk2, norm_g)


def _route_body(og_ref, od_ref, h0_ref, wog_ref, wod_ref, g_ref, b_ref, rw_ref, rb_ref,
                h1_ref, route_ref, keys_ref, cnt_ref, carry_ref, *, rows_per_seq, id_bits):
    t = pl.program_id(0)

    @pl.when(t == 0)
    def _():
        carry_ref[...] = jnp.zeros_like(carry_ref)

    mix = (jnp.dot(og_ref[...], wog_ref[...], preferred_element_type=F32)
           + jnp.dot(od_ref[...], wod_ref[...], preferred_element_type=F32))
    h1 = _layer_norm(DEEPNORM_ALPHA * h0_ref[...] + mix, g_ref[...], b_ref[...])
    _store_token_tiled(h1_ref, h1)

    hh, hm, _ = _split3_bf16(h1)
    wh, wm, _ = _split3_bf16(rw_ref[...])
    head = jnp.dot(hh, jnp.concatenate([wh, wm], axis=1), preferred_element_type=F32)
    logits = rb_ref[...] + head[:, :128] + head[:, 128:] + jnp.dot(hm, wh, preferred_element_type=F32)

    T = logits.shape[0]
    work = logits.T[:N_EXPERTS]
    expert = lax.broadcasted_iota(jnp.int32, (N_EXPERTS, T), 0)
    tok = t * T + lax.broadcasted_iota(jnp.int32, (1, T), 1)
    valid = (tok % rows_per_seq) >= FRONT_PAD
    top_v, top_i = [], []
    for _ in range(TOP_K):
        mx = jnp.max(work, axis=0, keepdims=True)
        idx = jnp.min(jnp.where(work == mx, expert, N_EXPERTS), axis=0, keepdims=True)
        top_v.append(mx)
        top_i.append(idx)
        work = jnp.where(expert == idx, -jnp.inf, work)
    ex = [jnp.exp(v - top_v[0]) for v in top_v]
    den = ex[0] + ex[1] + ex[2] + ex[3]
    gates = [lax.bitcast_convert_type(e / den, jnp.int32) for e in ex]
    keys = [jnp.where(valid, idx, N_EXPERTS) * (1 << id_bits) + (tok * TOP_K + k) for k, idx in enumerate(top_i)]

    onehot = jnp.zeros((N_EXPERTS, T), F32)
    for idx in top_i:
        onehot = onehot + (expert == idx).astype(F32)
    carry_ref[...] = carry_ref[...] + jnp.sum(jnp.where(valid, onehot, 0.0), axis=1, keepdims=True)
    cnt_ref[...] = carry_ref[...]

    by_lane = jnp.concatenate(top_i + keys + gates + [jnp.zeros((128 - 3 * TOP_K, T), jnp.int32)], axis=0)
    route_ref[...] = by_lane.T
    keys_ref[...] = jnp.concatenate([key[:, q * 128:(q + 1) * 128] for key in keys for q in range(T // 128)], axis=0)


def _outproj_route(og, od, h0, wo_g, wo_d, ln_g, ln_b, rw, rb, rows_per_seq):
    NP, D = h0.shape
    T = ROUTE_TILE

    def rows(width):
        return pl.BlockSpec((T, width), lambda t: (t, 0))

    def full(shape):
        return pl.BlockSpec(shape, lambda t: (0,) * len(shape))

    key_rows = T * TOP_K // 128
    return pl.pallas_call(
        functools.partial(_route_body, rows_per_seq=rows_per_seq, id_bits=(NP * TOP_K - 1).bit_length()),
        grid=(NP // T,),
        in_specs=[rows(512), rows(512), rows(D), full((512, D)), full((512, D)),
                  full((1, D)), full((1, D)), full((D, 128)), full((1, 128))],
        out_specs=[pl.BlockSpec((T * TOKEN_SUBROWS, 128), lambda t: (t, 0)), rows(128),
                   pl.BlockSpec((key_rows, 128), lambda t: (t, 0)), full((N_EXPERTS, 1))],
        out_shape=[jax.ShapeDtypeStruct((NP * TOKEN_SUBROWS, 128), F32),
                   jax.ShapeDtypeStruct((NP, 128), jnp.int32),
                   jax.ShapeDtypeStruct((NP * TOP_K // 128, 128), jnp.int32),
                   jax.ShapeDtypeStruct((N_EXPERTS, 1), F32)],
        scratch_shapes=[pltpu.VMEM((N_EXPERTS, 1), F32)],
        compiler_params=pltpu.CompilerParams(
            dimension_semantics=("arbitrary",), vmem_limit_bytes=VMEM_LIMIT),
        name="outproj_route",
    )(og, od, h0, wo_g, wo_d, ln_g, ln_b, rw, rb)


def _expert_body(order_ref, p0_ref, bstart_ref, nblk_ref, ntot_ref,
                 h1t_hbm, wgu_ref, bg_ref, bu_ref, wd_ref, bd_ref, yk_hbm,
                 wg_s, wu_s, wd_s, xb0, xb1, xb2, yb0, yb1, yb2, zbuf, gsem, ssem, zsem,
                 *, n_tokens, rows_per_seq):
    e = pl.program_id(0)
    n_total = ntot_ref[0]
    R = MOE_ROWS
    TS = TOKEN_SUBROWS
    xb = (xb0, xb1, xb2)
    yb = (yb0, yb1, yb2)
    NB = len(xb)

    def gather_start(b, slot):
        p0 = p0_ref[b]
        for r in range(R):
            tok = order_ref[p0 + r] >> 2
            pltpu.make_async_copy(
                h1t_hbm.at[pl.ds(pl.multiple_of(tok * TS, TS), TS), :],
                xb[slot].at[pl.ds(r * TS, TS), :], gsem.at[slot]).start(priority=r % 2)

    def gather_wait(slot):
        pltpu.make_async_copy(h1t_hbm.at[pl.ds(0, R * TS), :], xb[slot], gsem.at[slot]).wait()

    def scatter_start(b, slot):
        p0 = p0_ref[b]
        for r in range(R):
            row = order_ref[p0 + r] * TS
            pltpu.make_async_copy(
                yb[slot].at[pl.ds(r * TS, TS), :],
                yk_hbm.at[pl.ds(pl.multiple_of(row, TS), TS), :], ssem.at[slot]).start(priority=r % 2)

    def scatter_wait(slot):
        pltpu.make_async_copy(yb[slot], yk_hbm.at[pl.ds(0, R * TS), :], ssem.at[slot]).wait()

    @pl.when(e == 0)
    def _():
        zbuf[...] = jnp.zeros_like(zbuf)
        lead = FRONT_PAD * TS
        fills = [((s * rows_per_seq * TOP_K + k * FRONT_PAD) * TS, lead)
                 for s in range(n_tokens // rows_per_seq) for k in range(TOP_K)]
        copies = [pltpu.make_async_copy(zbuf.at[pl.ds(0, n), :], yk_hbm.at[pl.ds(o, n), :], zsem)
                  for o, n in fills]
        for cp in copies:
            cp.start()
        for cp in copies:
            cp.wait()
        gather_start(0, 0)
        gather_start(jnp.minimum(1, n_total - 1), 1)

    @pl.when(nblk_ref[e] > 0)
    def _():
        W, H = MXU_TILE, MXU_TILE // 2
        r = lax.broadcasted_iota(jnp.int32, (W, W), 0)
        c = lax.broadcasted_iota(jnp.int32, (W, W), 1)
        perm = (r == jnp.where(c < H, 2 * c, 2 * (c - H) + 1)).astype(BF16)
        for tt in range(2 * D_FF // W):
            wt = wgu_ref[0, :, tt * W:(tt + 1) * W].astype(BF16)
            sp = jnp.dot(wt, perm, preferred_element_type=F32)
            wg_s[:, tt * H:(tt + 1) * H] = sp[:, :H].astype(BF16)
            wu_s[:, tt * H:(tt + 1) * H] = sp[:, H:].astype(BF16)
        wd_s[...] = wd_ref[0].astype(BF16)

    def run_block(b, slot):
        prv = (slot + 2) % NB
        gather_wait(slot)
        gather_start(jnp.minimum(b + 2, n_total - 1), prv)
        x = _load_token_tiled(xb[slot], R).astype(BF16)
        gt = jnp.dot(x, wg_s[...], preferred_element_type=F32) + bg_ref[0]
        up = jnp.dot(x, wu_s[...], preferred_element_type=F32) + bu_ref[0]
        gt = jnp.minimum(gt, SWIGLU_LIMIT)
        up = jnp.clip(up, -SWIGLU_LIMIT, SWIGLU_LIMIT)
        act = (up + 1.0) * (gt * jax.nn.sigmoid(SWIGLU_ALPHA * gt))
        y = jnp.dot(act.astype(BF16), wd_s[...], preferred_element_type=F32) + bd_ref[0]
        _store_token_tiled(yb[slot], y)

        @pl.when(b >= 1)
        def _():
            scatter_wait(prv)
        scatter_start(b, slot)

    def block(b, carry):
        for slot in range(NB):
            pl.when(b % NB == slot)(functools.partial(run_block, b, slot))
        return carry

    b0 = bstart_ref[e]
    lax.fori_loop(b0, b0 + nblk_ref[e], block, 0)

    @pl.when(e == pl.num_programs(0) - 1)
    def _():
        last = n_total - 1
        for slot in range(NB):
            @pl.when(last % NB == slot)
            def _(slot=slot):
                gather_wait((slot + 1) % NB)
                gather_wait((slot + 2) % NB)
                scatter_wait(slot)


def _experts(order, blk_p0, blk_start, n_blk, n_total, h1t, w_gate_up, b_gate, b_up, w_down,
             b_down, n_tokens, rows_per_seq):
    D = D_MODEL
    R = MOE_ROWS
    buf = pltpu.VMEM((R * TOKEN_SUBROWS, 128), F32)
    per_expert = lambda e, *_: (e, 0, 0)
    grid_spec = pltpu.PrefetchScalarGridSpec(
        num_scalar_prefetch=5,
        grid=(N_EXPERTS,),
        in_specs=[
            pl.BlockSpec(memory_space=pl.ANY),
            pl.BlockSpec((1, D, 2 * D_FF), per_expert),
            pl.BlockSpec((1, 1, D_FF), per_expert),
            pl.BlockSpec((1, 1, D_FF), per_expert),
            pl.BlockSpec((1, D_FF, D), per_expert),
            pl.BlockSpec((1, 1, D), per_expert),
        ],
        out_specs=pl.BlockSpec(memory_space=pl.ANY),
        scratch_shapes=[pltpu.VMEM((D, D_FF), BF16), pltpu.VMEM((D, D_FF), BF16), pltpu.VMEM((D_FF, D), BF16),
                        buf, buf, buf, buf, buf, buf, pltpu.VMEM((FRONT_PAD * TOKEN_SUBROWS, 128), F32),
                        pltpu.SemaphoreType.DMA((3,)), pltpu.SemaphoreType.DMA((3,)), pltpu.SemaphoreType.DMA],
    )
    out_rows = TOP_K * n_tokens * TOKEN_SUBROWS
    return pl.pallas_call(
        functools.partial(_expert_body, n_tokens=n_tokens, rows_per_seq=rows_per_seq),
        grid_spec=grid_spec,
        out_shape=jax.ShapeDtypeStruct((out_rows, 128), F32),
        compiler_params=pltpu.CompilerParams(
            dimension_semantics=("arbitrary",), vmem_limit_bytes=EXPERT_VMEM_LIMIT),
        name="experts",
    )(order, blk_p0, blk_start, n_blk, n_total, h1t, w_gate_up, b_gate, b_up, w_down, b_down)


COMBINE_BLOCKS = 4


def _combine_body(*refs):
    per_block = 2 + TOP_K
    g_ref, b_ref, o_ref = refs[COMBINE_BLOCKS * per_block:]
    for i in range(COMBINE_BLOCKS):
        h1_ref, *y_refs, route_ref = refs[i * per_block:(i + 1) * per_block]
        route = route_ref[...]
        y = jnp.zeros((ROW_BLOCK, D_MODEL), F32)
        for k, yk_ref in enumerate(y_refs):
            gate = lax.bitcast_convert_type(route[:, 2 * TOP_K + k:2 * TOP_K + k + 1], F32)
            y = y + gate * _load_token_tiled(yk_ref.reshape(ROW_BLOCK * TOKEN_SUBROWS, 128), ROW_BLOCK)
        h1 = _load_token_tiled(h1_ref, ROW_BLOCK)
        o_ref[0, i * ROW_BLOCK:(i + 1) * ROW_BLOCK, :] = _layer_norm(
            DEEPNORM_ALPHA * h1 + y, g_ref[...], b_ref[...])


def _combine(h1t, yk, route, ln_g, ln_b, batch, seq_len):
    nb = seq_len // ROW_BLOCK
    blocks_per_seq = nb + 1
    tile_rows = ROW_BLOCK * TOKEN_SUBROWS
    yk = yk.reshape(-1, TOP_K, TOKEN_SUBROWS, 128)

    def token_block(i):
        return lambda b, j: b * blocks_per_seq + COMBINE_BLOCKS * j + i + 1

    in_specs, operands = [], []
    for i in range(COMBINE_BLOCKS):
        tb = token_block(i)
        in_specs.append(pl.BlockSpec((tile_rows, 128), lambda b, j, tb=tb: (tb(b, j), 0)))
        in_specs += [pl.BlockSpec((ROW_BLOCK, None, TOKEN_SUBROWS, 128), lambda b, j, tb=tb, k=k: (tb(b, j), k, 0, 0))
                     for k in range(TOP_K)]
        in_specs.append(pl.BlockSpec((ROW_BLOCK, 128), lambda b, j, tb=tb: (tb(b, j), 0)))
        operands += [h1t] + [yk] * TOP_K + [route]
    vec = pl.BlockSpec((1, D_MODEL), lambda b, j: (0, 0))
    return pl.pallas_call(
        _combine_body,
        grid=(batch, nb // COMBINE_BLOCKS),
        in_specs=in_specs + [vec, vec],
        out_specs=pl.BlockSpec((1, COMBINE_BLOCKS * ROW_BLOCK, D_MODEL), lambda b, j: (b, j, 0)),
        out_shape=jax.ShapeDtypeStruct((batch, seq_len, D_MODEL), F32),
        compiler_params=pltpu.CompilerParams(
            dimension_semantics=("parallel", "parallel"), vmem_limit_bytes=VMEM_LIMIT),
        name="combine",
    )(*operands, ln_g, ln_b)


def kernel(x, meta_tokens, ln_emb_g, ln_emb_b, w_in, gla_wa2, gla_ba, gla_norm_g, diff_lambda_q1, diff_lambda_k1, diff_lambda_q2, diff_lambda_k2, diff_norm_g, w_out, ln1_g, ln1_b, router_w, router_b, w_gate_up, b_gate_up, w_down, b_down, ln2_g, ln2_b):
    B, S, D = x.shape
    LP = S + ROW_BLOCK
    NP = B * LP
    row = lambda v: v.reshape(1, -1)

    w = w_in[0]
    w_main = jnp.concatenate([w[:, :1536], w[:, 1552:]], axis=1).astype(BF16)
    w_ga = w[:, 1536:1552].astype(BF16)
    meta_pad = jnp.pad(meta_tokens, ((FRONT_PAD, 0), (0, 0)))

    h0, gq, gk, gv, gr, la, dq, dk, dv = _inproj(
        x, meta_pad, row(ln_emb_g), row(ln_emb_b), w_main, w_ga,
        gla_wa2[0].astype(BF16), row(gla_ba[0]))

    og = _gla(gq, gk, gv, gr, la, row(gla_norm_g[0]))
    od = _diff_attention(dq, dk, dv, row(diff_lambda_q1[0]), row(diff_lambda_k1[0]),
                         row(diff_lambda_q2[0]), row(diff_lambda_k2[0]), row(diff_norm_g[0]))

    wo = w_out[0].astype(BF16)
    rw_pad = jnp.pad(router_w[0], ((0, 0), (0, 128 - N_EXPERTS)))
    rb_pad = jnp.pad(row(router_b[0]), ((0, 0), (0, 128 - N_EXPERTS)))
    h1t, route, keys, counts = _outproj_route(
        og.reshape(NP, 512), od.reshape(NP, 512), h0.reshape(NP, D), wo[:512], wo[512:],
        row(ln1_g[0]), row(ln1_b[0]), rw_pad, rb_pad, LP)

    R = MOE_ROWS
    i32 = jnp.int32
    counts = counts[:, 0].astype(i32)
    n_blk = (counts + R - 1) // R
    blk_end = jnp.cumsum(n_blk)
    blk_start = blk_end - n_blk
    grp_start = jnp.cumsum(counts) - counts
    n_total = blk_end[-1:]
    n_assign_max = B * (S + N_META) * TOP_K
    max_blocks = (n_assign_max + N_EXPERTS * (R - 1)) // R
    g = jnp.minimum(jnp.arange(max_blocks, dtype=i32), n_total[0] - 1)
    is_e = (jnp.minimum(jnp.sum(g[:, None] >= blk_end[None, :], axis=1), N_EXPERTS - 1)[:, None]
            == jnp.arange(N_EXPERTS)[None, :])
    pick = lambda v: jnp.sum(jnp.where(is_e, v[None, :], 0), axis=1)
    local = (g - pick(blk_start)) * R
    blk_p0 = (pick(grp_start) + local).astype(i32)

    assert R <= B * FRONT_PAD * TOP_K
    order = jnp.sort(keys.reshape(-1)) & ((1 << (NP * TOP_K - 1).bit_length()) - 1)

    bgu = b_gate_up[0].reshape(N_EXPERTS, D_FF, 2)
    yk = _experts(order, blk_p0, blk_start.astype(i32), n_blk.astype(i32), n_total.astype(i32),
                  h1t, w_gate_up[0],
                  bgu[:, :, 0].reshape(N_EXPERTS, 1, D_FF), bgu[:, :, 1].reshape(N_EXPERTS, 1, D_FF),
                  w_down[0], b_down[0].reshape(N_EXPERTS, 1, D), NP, LP)

    return _combine(h1t, yk, route, row(ln2_g[0]), row(ln2_b[0]), B, S)
```

```python
import functools
import math

import jax
import jax.numpy as jnp
from jax import lax
from jax.experimental import pallas as pl
from jax.experimental.pallas import tpu as pltpu

F32 = jnp.float32
BF16 = jnp.bfloat16

D_MODEL = 1024
LANES = 128
N_META = 16
ROW_BLOCK = 128
FRONT_PAD = ROW_BLOCK - N_META
CHUNK = 64

GLA_HEADS = 4
GLA_DK = 64
GLA_DV = 128
GLA_RANK = 16
GLA_TAU = 16.0
DIFF_HEADS = 4
DIFF_DH = 64
DIFF_DV = 128

N_EXPERTS = 32
TOP_K = 4
TOP_K_SHIFT = TOP_K.bit_length() - 1
D_FF = 1024
SWIGLU_LIMIT = 7.0
SWIGLU_ALPHA = 1.702
MOE_ROWS = 256

DEEPNORM_ALPHA = 2.0 ** 0.25
LAMBDA_INIT = 0.8 - 0.6 * math.exp(0.0)
LN_EPS = 1e-5
NEG_INF = -1e30
LOG2E = math.log2(math.e)

MXU_TILE = 256
ROUTE_TILE = 512
VMEM_LIMIT = 48 * 1024 * 1024
EXPERT_VMEM_LIMIT = 56 * 1024 * 1024

_GQ, _GK, _GV, _GR, _DQ, _DK, _DV = 0, 256, 512, 1024, 1536, 2048, 2560
_MAIN_WIDTH = 3072


def _layer_norm(x, g, b):
    mu = jnp.mean(x, axis=-1, keepdims=True)
    xc = x - mu
    var = jnp.mean(xc * xc, axis=-1, keepdims=True)
    return xc * lax.rsqrt(var + LN_EPS) * g + b


def _split3_bf16(x):
    hi = x.astype(BF16)
    r1 = x - hi.astype(F32)
    mid = r1.astype(BF16)
    lo = (r1 - mid.astype(F32)).astype(BF16)
    return hi, mid, lo


TOKEN_SUBROWS = D_MODEL // LANES


def _store_token_tiled(ref, val, first_token=0):
    n = val.shape[0]
    for s in range(TOKEN_SUBROWS):
        ref[pl.ds(first_token * TOKEN_SUBROWS + s, n, stride=TOKEN_SUBROWS), :] = val[:, s * LANES:(s + 1) * LANES]


def _load_token_tiled(ref, n, first_row=0):
    return jnp.concatenate(
        [ref[pl.ds(first_row + s, n, stride=TOKEN_SUBROWS), :] for s in range(TOKEN_SUBROWS)], axis=1)


def _inproj_body(xa_ref, xb_ref, meta_ref, g_ref, b_ref, w_ref, wga_ref, wa2_ref, ba_ref,
                 h0_ref, gq_ref, gk_ref, gv_ref, gr_ref, la_ref, dq_ref, dk_ref, dv_ref):
    j = pl.program_id(1)
    rows = lax.broadcasted_iota(jnp.int32, (2 * ROW_BLOCK, 1), 0)
    valid = jnp.logical_or(j > 0, rows >= FRONT_PAD)
    xin = jnp.concatenate([jnp.where(j > 0, xa_ref[0], meta_ref[...]), xb_ref[0]], axis=0)
    h = jnp.where(valid, _layer_norm(xin, g_ref[...], b_ref[...]), 0.0)
    h0_ref[0] = h
    hb = h.astype(BF16)

    def proj(off, width):
        return jnp.dot(hb, w_ref[:, off:off + width], preferred_element_type=F32)

    gq_ref[0] = (proj(_GQ, 256) * GLA_DK ** -0.5).astype(BF16)
    gk_ref[0] = proj(_GK, 256)
    gv_ref[0] = proj(_GV, 512).astype(BF16)
    gr_ref[0] = proj(_GR, 512).astype(BF16)
    dq_ref[0] = (proj(_DQ, 512) * (DIFF_DH ** -0.5 * LOG2E)).astype(BF16)
    dk_ref[0] = proj(_DK, 512).astype(BF16)
    dv_ref[0] = proj(_DV, 512).astype(BF16)

    a_lr = jnp.dot(hb, wga_ref[...], preferred_element_type=F32)
    z = jnp.dot(a_lr.astype(BF16), wa2_ref[...], preferred_element_type=F32) + ba_ref[...]
    log_sig = jnp.minimum(z, 0.0) - jnp.log1p(jnp.exp(-jnp.abs(z)))
    la_ref[0] = jnp.where(valid, log_sig / GLA_TAU, 0.0)


def _inproj(x, meta_pad, ln_g, ln_b, w_main, w_ga, wa2, ba):
    B, S, D = x.shape
    nb = S // ROW_BLOCK + 1
    LP = nb * ROW_BLOCK

    n_frame_blocks = S // ROW_BLOCK
    n_steps = pl.cdiv(LP, 2 * ROW_BLOCK)

    def row_spec(width):
        return pl.BlockSpec((1, 2 * ROW_BLOCK, width), lambda b, j: (b, j, 0))

    def frames(shift):
        return pl.BlockSpec((1, ROW_BLOCK, D),
                            lambda b, j: (b, jnp.clip(2 * j + shift, 0, n_frame_blocks - 1), 0))

    def full(shape):
        return pl.BlockSpec(shape, lambda b, j: (0,) * len(shape))

    out_shapes = [
        jax.ShapeDtypeStruct((B, LP, D), F32),
        jax.ShapeDtypeStruct((B, LP, 256), BF16),
        jax.ShapeDtypeStruct((B, LP, 256), F32),
        jax.ShapeDtypeStruct((B, LP, 512), BF16),
        jax.ShapeDtypeStruct((B, LP, 512), BF16),
        jax.ShapeDtypeStruct((B, LP, 256), F32),
        jax.ShapeDtypeStruct((B, LP, 512), BF16),
        jax.ShapeDtypeStruct((B, LP, 512), BF16),
        jax.ShapeDtypeStruct((B, LP, 512), BF16),
    ]
    return pl.pallas_call(
        _inproj_body,
        grid=(B, n_steps),
        in_specs=[
            frames(-1), frames(0),
            full((ROW_BLOCK, D)), full((1, D)), full((1, D)),
            full((D, _MAIN_WIDTH)), full((D, GLA_RANK)), full((GLA_RANK, 256)), full((1, 256)),
        ],
        out_specs=[row_spec(s.shape[-1]) for s in out_shapes],
        out_shape=out_shapes,
        compiler_params=pltpu.CompilerParams(
            dimension_semantics=("parallel", "parallel"), vmem_limit_bytes=VMEM_LIMIT),
        name="inproj",
    )(x, x, meta_pad, ln_g, ln_b, w_main, w_ga, wa2, ba)


def _gla_body(q_ref, k_ref, v_ref, r_ref, la_ref, g_ref, o_ref):
    n_groups = q_ref.shape[1] // ROW_BLOCK
    ri = lax.broadcasted_iota(jnp.int32, (ROW_BLOCK, ROW_BLOCK), 0)
    ci = lax.broadcasted_iota(jnp.int32, (ROW_BLOCK, ROW_BLOCK), 1)
    later = jnp.logical_and(ri // CHUNK == ci // CHUNK, ci > ri).astype(BF16)
    sr = lax.broadcasted_iota(jnp.int32, (GLA_HEADS * GLA_DV, GLA_HEADS * GLA_DK), 0)
    sc = lax.broadcasted_iota(jnp.int32, (GLA_HEADS * GLA_DV, GLA_HEADS * GLA_DK), 1)
    same_head = sr // GLA_DV == sc // GLA_DK
    gain = g_ref[...]
    st = jnp.zeros((GLA_HEADS * GLA_DV, GLA_HEADS * GLA_DK), F32)

    for grp in range(n_groups):
        g0 = grp * ROW_BLOCK
        la = la_ref[0, g0:g0 + ROW_BLOCK, :]
        hi, mid, lo = _split3_bf16(la)
        suffix = (jnp.dot(later, hi, preferred_element_type=F32)
                  + jnp.dot(later, mid, preferred_element_type=F32)
                  + jnp.dot(later, lo, preferred_element_type=F32))
        kdec = (k_ref[0, g0:g0 + ROW_BLOCK, :] * jnp.exp(suffix)).astype(BF16)
        for half in range(ROW_BLOCK // CHUNK):
            f = half * CHUNK
            r0 = g0 + f
            tot = suffix[f:f + 1, :] + la[f:f + 1, :]
            upd = lax.dot_general(v_ref[0, r0:r0 + CHUNK, :], kdec[f:f + CHUNK, :],
                                  (((0,), (0,)), ((), ())), preferred_element_type=F32)
            st = st * jnp.exp(tot) + jnp.where(same_head, upd, 0.0)
            o = lax.dot_general(q_ref[0, r0:r0 + CHUNK, :], st.astype(BF16),
                                (((1,), (1,)), ((), ())), preferred_element_type=F32)
            r = r_ref[0, r0:r0 + CHUNK, :].astype(F32)
            for h in range(GLA_HEADS):
                oh = o[:, h * GLA_DV:(h + 1) * GLA_DV]
                rh = r[:, h * GLA_DV:(h + 1) * GLA_DV]
                ms = jnp.mean(oh * oh, axis=-1, keepdims=True)
                out = oh * lax.rsqrt(ms + LN_EPS) * gain * (rh * jax.nn.sigmoid(rh))
                o_ref[0, r0:r0 + CHUNK, h * GLA_DV:(h + 1) * GLA_DV] = out.astype(BF16)


def _gla(gq, gk, gv, gr, la, norm_g):
    B, LP, _ = gq.shape

    def seq(width):
        return pl.BlockSpec((1, LP, width), lambda b: (b, 0, 0))

    return pl.pallas_call(
        _gla_body,
        grid=(B,),
        in_specs=[seq(256), seq(256), seq(512), seq(512), seq(256),
                  pl.BlockSpec((1, GLA_DV), lambda b: (0, 0))],
        out_specs=seq(512),
        out_shape=jax.ShapeDtypeStruct((B, LP, 512), BF16),
        compiler_params=pltpu.CompilerParams(
            dimension_semantics=("parallel",), vmem_limit_bytes=VMEM_LIMIT),
        name="gla",
    )(gq, gk, gv, gr, la, norm_g)


ATT_BLOCK = 256


def _diff_body(q_ref, k_ref, v_ref, lq1_ref, lk1_ref, lq2_ref, lk2_ref, g_ref, o_ref,
               dmask_ref):
    h = pl.program_id(1)
    slope = jnp.where(h == 0, 2.0 ** -2, jnp.where(h == 1, 2.0 ** -4, jnp.where(h == 2, 2.0 ** -6, 2.0 ** -8)))
    slope = slope.astype(F32) * LOG2E
    lam = (jnp.exp(jnp.sum(lq1_ref[...] * lk1_ref[...], axis=-1, keepdims=True))
           - jnp.exp(jnp.sum(lq2_ref[...] * lk2_ref[...], axis=-1, keepdims=True)) + LAMBDA_INIT)
    gain = g_ref[...] * (1.0 - LAMBDA_INIT)
    n_qblocks = (q_ref.shape[1] - ROW_BLOCK) // ATT_BLOCK
    nt = (((1,), (1,)), ((), ()))

    def split_q(q):
        lane = lax.broadcasted_iota(jnp.int32, q.shape, 1)
        zero = jnp.zeros_like(q)
        return jnp.where(lane < DIFF_DH, q, zero), jnp.where(lane >= DIFF_DH, q, zero)

    def with_ones(v):
        return jnp.concatenate([v, jnp.ones_like(v)], axis=1)

    def softmax_av(qz, kk, vext, add_bias):
        s = add_bias(lax.dot_general(qz, kk, nt, preferred_element_type=F32))
        m = jnp.max(s, axis=-1, keepdims=True)
        p = jnp.exp2(s - m).astype(BF16)
        return jnp.dot(p, vext, preferred_element_type=F32)

    def finish(a1, a2):
        o = a1[:, :DIFF_DV] / a1[:, DIFF_DV:] - lam * (a2[:, :DIFF_DV] / a2[:, DIFF_DV:])
        ms = jnp.mean(o * o, axis=-1, keepdims=True)
        return (o * lax.rsqrt(ms + LN_EPS) * gain).astype(BF16)

    r = lax.broadcasted_iota(jnp.int32, (ROW_BLOCK, ROW_BLOCK), 0)
    c = lax.broadcasted_iota(jnp.int32, (ROW_BLOCK, ROW_BLOCK), 1)
    ok = jnp.logical_and(c // CHUNK <= r // CHUNK, c >= FRONT_PAD)
    bias_lead = jnp.where(ok, -slope * jnp.abs(r - c).astype(F32), NEG_INF)
    q1z, q2z = split_q(q_ref[0, 0:ROW_BLOCK, :])
    k_lead = k_ref[0, 0:ROW_BLOCK, :]
    v_lead = with_ones(v_ref[0, 0:ROW_BLOCK, :])
    o_ref[0, 0:ROW_BLOCK, :] = finish(softmax_av(q1z, k_lead, v_lead, lambda s: s + bias_lead),
                                      softmax_av(q2z, k_lead, v_lead, lambda s: s + bias_lead))

    r = lax.broadcasted_iota(jnp.int32, (ATT_BLOCK, ATT_BLOCK), 0)
    c = lax.broadcasted_iota(jnp.int32, (ATT_BLOCK, ATT_BLOCK), 1)
    rel = jnp.where(c <= r, c, 2 * r - c).astype(F32)
    dmask_ref[...] = jnp.where(c // CHUNK <= r // CHUNK, slope * rel, NEG_INF)

    for jq in range(n_qblocks):
        qbase = ROW_BLOCK + jq * ATT_BLOCK
        n_keys = qbase + ATT_BLOCK
        q1z, q2z = split_q(q_ref[0, qbase:qbase + ATT_BLOCK, :])
        kk = k_ref[0, 0:n_keys, :]
        vext = with_ones(v_ref[0, 0:n_keys, :])
        col = lax.broadcasted_iota(jnp.int32, (1, qbase), 1)
        col_bias = jnp.where(col >= FRONT_PAD, slope * (col - qbase).astype(F32), NEG_INF)

        def add_bias(s, col_bias=col_bias, qbase=qbase):
            return jnp.concatenate([s[:, :qbase] + col_bias, s[:, qbase:] + dmask_ref[...]], axis=1)

        o_ref[0, qbase:qbase + ATT_BLOCK, :] = finish(softmax_av(q1z, kk, vext, add_bias),
                                                      softmax_av(q2z, kk, vext, add_bias))


def _diff_attention(dq, dk, dv, lq1, lk1, lq2, lk2, norm_g):
    B, LP, W = dq.shape
    small = pl.BlockSpec((1, DIFF_DH), lambda b, h: (0, 0))
    seq = pl.BlockSpec((1, LP, 2 * DIFF_DH), lambda b, h: (b, 0, h))
    return pl.pallas_call(
        _diff_body,
        grid=(B, DIFF_HEADS),
        in_specs=[seq, seq, seq, small, small, small, small,
                  pl.BlockSpec((1, DIFF_DV), lambda b, h: (0, 0))],
        out_specs=seq,
        out_shape=jax.ShapeDtypeStruct((B, LP, W), BF16),
        scratch_shapes=[pltpu.VMEM((ATT_BLOCK, ATT_BLOCK), F32)],
        compiler_params=pltpu.CompilerParams(
            dimension_semantics=("parallel", "parallel"), vmem_limit_bytes=VMEM_LIMIT),
        name="diff_attn",
    )(dq, dk, dv, lq1, lk1, lq2, lk2, norm_g)


def _route_body(og_ref, od_ref, h0_ref, wog_ref, wod_ref, g_ref, b_ref, rw_ref, rb_ref,
                h1_ref, route_ref, keys_ref, cnt_ref, carry_ref, *, rows_per_seq, id_bits):
    t = pl.program_id(0)

    @pl.when(t == 0)
    def _():
        carry_ref[...] = jnp.zeros_like(carry_ref)

    mix = (jnp.dot(og_ref[...], wog_ref[...], preferred_element_type=F32)
           + jnp.dot(od_ref[...], wod_ref[...], preferred_element_type=F32))
    h1 = _layer_norm(DEEPNORM_ALPHA * h0_ref[...] + mix, g_ref[...], b_ref[...])
    _store_token_tiled(h1_ref, h1)

    hh, hm, _ = _split3_bf16(h1)
    wh, wm, _ = _split3_bf16(rw_ref[...])
    head = jnp.dot(hh, jnp.concatenate([wh, wm], axis=1), preferred_element_type=F32)
    logits = rb_ref[...] + head[:, :LANES] + head[:, LANES:] + jnp.dot(hm, wh, preferred_element_type=F32)

    T = logits.shape[0]
    work = logits.T[:N_EXPERTS]
    expert = lax.broadcasted_iota(jnp.int32, (N_EXPERTS, T), 0)
    tok = t * T + lax.broadcasted_iota(jnp.int32, (1, T), 1)
    valid = (tok % rows_per_seq) >= FRONT_PAD
    top_v, top_i = [], []
    for _ in range(TOP_K):
        mx = jnp.max(work, axis=0, keepdims=True)
        idx = jnp.min(jnp.where(work == mx, expert, N_EXPERTS), axis=0, keepdims=True)
        top_v.append(mx)
        top_i.append(idx)
        work = jnp.where(expert == idx, -jnp.inf, work)
    ex = [jnp.exp(v - top_v[0]) for v in top_v]
    den = ex[0] + ex[1] + ex[2] + ex[3]
    gates = [lax.bitcast_convert_type(e / den, jnp.int32) for e in ex]
    keys = [jnp.where(valid, idx, N_EXPERTS) * (1 << id_bits) + (tok * TOP_K + k) for k, idx in enumerate(top_i)]

    onehot = jnp.zeros((N_EXPERTS, T), F32)
    for idx in top_i:
        onehot = onehot + (expert == idx).astype(F32)
    carry_ref[...] = carry_ref[...] + jnp.sum(jnp.where(valid, onehot, 0.0), axis=1, keepdims=True)
    cnt_ref[...] = carry_ref[...]

    by_lane = jnp.concatenate(top_i + keys + gates + [jnp.zeros((LANES - 3 * TOP_K, T), jnp.int32)], axis=0)
    route_ref[...] = by_lane.T
    keys_ref[...] = jnp.concatenate([key[:, q * LANES:(q + 1) * LANES] for key in keys for q in range(T // LANES)], axis=0)


def _outproj_route(og, od, h0, wo_g, wo_d, ln_g, ln_b, rw, rb, rows_per_seq):
    NP, D = h0.shape
    T = ROUTE_TILE

    def rows(width):
        return pl.BlockSpec((T, width), lambda t: (t, 0))

    def full(shape):
        return pl.BlockSpec(shape, lambda t: (0,) * len(shape))

    key_rows = T * TOP_K // LANES
    return pl.pallas_call(
        functools.partial(_route_body, rows_per_seq=rows_per_seq, id_bits=(NP * TOP_K - 1).bit_length()),
        grid=(NP // T,),
        in_specs=[rows(512), rows(512), rows(D), full((512, D)), full((512, D)),
                  full((1, D)), full((1, D)), full((D, LANES)), full((1, LANES))],
        out_specs=[pl.BlockSpec((T * TOKEN_SUBROWS, LANES), lambda t: (t, 0)), rows(LANES),
                   pl.BlockSpec((key_rows, LANES), lambda t: (t, 0)), full((N_EXPERTS, 1))],
        out_shape=[jax.ShapeDtypeStruct((NP * TOKEN_SUBROWS, LANES), F32),
                   jax.ShapeDtypeStruct((NP, LANES), jnp.int32),
                   jax.ShapeDtypeStruct((NP * TOP_K // LANES, LANES), jnp.int32),
                   jax.ShapeDtypeStruct((N_EXPERTS, 1), F32)],
        scratch_shapes=[pltpu.VMEM((N_EXPERTS, 1), F32)],
        compiler_params=pltpu.CompilerParams(
            dimension_semantics=("arbitrary",), vmem_limit_bytes=VMEM_LIMIT),
        name="outproj_route",
    )(og, od, h0, wo_g, wo_d, ln_g, ln_b, rw, rb)


def _expert_body(order_ref, p0_ref, bstart_ref, nblk_ref, ntot_ref,
                 h1t_hbm, wgu_ref, bg_ref, bu_ref, wd_ref, bd_ref, yk_hbm,
                 wg_s, wu_s, wd_s, xb0, xb1, xb2, yb0, yb1, yb2, zbuf, gsem, ssem, zsem,
                 *, n_tokens, rows_per_seq):
    e = pl.program_id(0)
    n_total = ntot_ref[0]
    R = MOE_ROWS
    TS = TOKEN_SUBROWS
    xb = (xb0, xb1, xb2)
    yb = (yb0, yb1, yb2)
    NB = len(xb)

    def gather_start(b, slot):
        p0 = p0_ref[b]
        for r in range(R):
            tok = order_ref[p0 + r] >> TOP_K_SHIFT
            pltpu.make_async_copy(
                h1t_hbm.at[pl.ds(pl.multiple_of(tok * TS, TS), TS), :],
                xb[slot].at[pl.ds(r * TS, TS), :], gsem.at[slot]).start(priority=r % 2)

    def gather_wait(slot):
        pltpu.make_async_copy(h1t_hbm.at[pl.ds(0, R * TS), :], xb[slot], gsem.at[slot]).wait()

    def scatter_start(b, slot):
        p0 = p0_ref[b]
        for r in range(R):
            row = order_ref[p0 + r] * TS
            pltpu.make_async_copy(
                yb[slot].at[pl.ds(r * TS, TS), :],
                yk_hbm.at[pl.ds(pl.multiple_of(row, TS), TS), :], ssem.at[slot]).start(priority=r % 2)

    def scatter_wait(slot):
        pltpu.make_async_copy(yb[slot], yk_hbm.at[pl.ds(0, R * TS), :], ssem.at[slot]).wait()

    @pl.when(e == 0)
    def _():
        zbuf[...] = jnp.zeros_like(zbuf)
        lead = FRONT_PAD * TS
        fills = [((s * rows_per_seq * TOP_K + k * FRONT_PAD) * TS, lead)
                 for s in range(n_tokens // rows_per_seq) for k in range(TOP_K)]
        copies = [pltpu.make_async_copy(zbuf.at[pl.ds(0, n), :], yk_hbm.at[pl.ds(o, n), :], zsem)
                  for o, n in fills]
        for cp in copies:
            cp.start()
        for cp in copies:
            cp.wait()
        gather_start(0, 0)
        gather_start(jnp.minimum(1, n_total - 1), 1)

    @pl.when(nblk_ref[e] > 0)
    def _():
        W, H = MXU_TILE, MXU_TILE // 2
        r = lax.broadcasted_iota(jnp.int32, (W, W), 0)
        c = lax.broadcasted_iota(jnp.int32, (W, W), 1)
        perm = (r == jnp.where(c < H, 2 * c, 2 * (c - H) + 1)).astype(BF16)
        for tt in range(2 * D_FF // W):
            wt = wgu_ref[0, :, tt * W:(tt + 1) * W].astype(BF16)
            sp = jnp.dot(wt, perm, preferred_element_type=F32)
            wg_s[:, tt * H:(tt + 1) * H] = sp[:, :H].astype(BF16)
            wu_s[:, tt * H:(tt + 1) * H] = sp[:, H:].astype(BF16)
        wd_s[...] = wd_ref[0].astype(BF16)

    def run_block(b, slot):
        prv = (slot + 2) % NB
        gather_wait(slot)
        gather_start(jnp.minimum(b + 2, n_total - 1), prv)
        x = _load_token_tiled(xb[slot], R).astype(BF16)
        gt = jnp.dot(x, wg_s[...], preferred_element_type=F32) + bg_ref[0]
        up = jnp.dot(x, wu_s[...], preferred_element_type=F32) + bu_ref[0]
        gt = jnp.minimum(gt, SWIGLU_LIMIT)
        up = jnp.clip(up, -SWIGLU_LIMIT, SWIGLU_LIMIT)
        act = (up + 1.0) * (gt * jax.nn.sigmoid(SWIGLU_ALPHA * gt))
        y = jnp.dot(act.astype(BF16), wd_s[...], preferred_element_type=F32) + bd_ref[0]
        _store_token_tiled(yb[slot], y)

        @pl.when(b >= 1)
        def _():
            scatter_wait(prv)
        scatter_start(b, slot)

    def block(b, carry):
        for slot in range(NB):
            pl.when(b % NB == slot)(functools.partial(run_block, b, slot))
        return carry

    b0 = bstart_ref[e]
    lax.fori_loop(b0, b0 + nblk_ref[e], block, 0)

    @pl.when(e == pl.num_programs(0) - 1)
    def _():
        last = n_total - 1
        for slot in range(NB):
            @pl.when(last % NB == slot)
            def _(slot=slot):
                gather_wait((slot + 1) % NB)
                gather_wait((slot + 2) % NB)
                scatter_wait(slot)


def _experts(order, blk_p0, blk_start, n_blk, n_total, h1t, w_gate_up, b_gate, b_up, w_down,
             b_down, n_tokens, rows_per_seq):
    D = D_MODEL
    R = MOE_ROWS
    buf = pltpu.VMEM((R * TOKEN_SUBROWS, LANES), F32)
    per_expert = lambda e, *_: (e, 0, 0)
    grid_spec = pltpu.PrefetchScalarGridSpec(
        num_scalar_prefetch=5,
        grid=(N_EXPERTS,),
        in_specs=[
            pl.BlockSpec(memory_space=pl.ANY),
            pl.BlockSpec((1, D, 2 * D_FF), per_expert),
            pl.BlockSpec((1, 1, D_FF), per_expert),
            pl.BlockSpec((1, 1, D_FF), per_expert),
            pl.BlockSpec((1, D_FF, D), per_expert),
            pl.BlockSpec((1, 1, D), per_expert),
        ],
        out_specs=pl.BlockSpec(memory_space=pl.ANY),
        scratch_shapes=[pltpu.VMEM((D, D_FF), BF16), pltpu.VMEM((D, D_FF), BF16), pltpu.VMEM((D_FF, D), BF16),
                        buf, buf, buf, buf, buf, buf, pltpu.VMEM((FRONT_PAD * TOKEN_SUBROWS, LANES), F32),
                        pltpu.SemaphoreType.DMA((3,)), pltpu.SemaphoreType.DMA((3,)), pltpu.SemaphoreType.DMA],
    )
    out_rows = TOP_K * n_tokens * TOKEN_SUBROWS
    return pl.pallas_call(
        functools.partial(_expert_body, n_tokens=n_tokens, rows_per_seq=rows_per_seq),
        grid_spec=grid_spec,
        out_shape=jax.ShapeDtypeStruct((out_rows, LANES), F32),
        compiler_params=pltpu.CompilerParams(
            dimension_semantics=("arbitrary",), vmem_limit_bytes=EXPERT_VMEM_LIMIT),
        name="experts",
    )(order, blk_p0, blk_start, n_blk, n_total, h1t, w_gate_up, b_gate, b_up, w_down, b_down)


COMBINE_BLOCKS = 4


def _combine_body(*refs):
    per_block = 2 + TOP_K
    g_ref, b_ref, o_ref = refs[COMBINE_BLOCKS * per_block:]
    for i in range(COMBINE_BLOCKS):
        h1_ref, *y_refs, route_ref = refs[i * per_block:(i + 1) * per_block]
        route = route_ref[...]
        y = jnp.zeros((ROW_BLOCK, D_MODEL), F32)
        for k, yk_ref in enumerate(y_refs):
            gate = lax.bitcast_convert_type(route[:, 2 * TOP_K + k:2 * TOP_K + k + 1], F32)
            y = y + gate * _load_token_tiled(yk_ref.reshape(ROW_BLOCK * TOKEN_SUBROWS, LANES), ROW_BLOCK)
        h1 = _load_token_tiled(h1_ref, ROW_BLOCK)
        o_ref[0, i * ROW_BLOCK:(i + 1) * ROW_BLOCK, :] = _layer_norm(
            DEEPNORM_ALPHA * h1 + y, g_ref[...], b_ref[...])


def _combine(h1t, yk, route, ln_g, ln_b, batch, seq_len):
    nb = seq_len // ROW_BLOCK
    blocks_per_seq = nb + 1
    tile_rows = ROW_BLOCK * TOKEN_SUBROWS
    yk = yk.reshape(-1, TOP_K, TOKEN_SUBROWS, LANES)

    def token_block(i):
        return lambda b, j: b * blocks_per_seq + COMBINE_BLOCKS * j + i + 1

    in_specs, operands = [], []
    for i in range(COMBINE_BLOCKS):
        tb = token_block(i)
        in_specs.append(pl.BlockSpec((tile_rows, LANES), lambda b, j, tb=tb: (tb(b, j), 0)))
        in_specs += [pl.BlockSpec((ROW_BLOCK, None, TOKEN_SUBROWS, LANES), lambda b, j, tb=tb, k=k: (tb(b, j), k, 0, 0))
                     for k in range(TOP_K)]
        in_specs.append(pl.BlockSpec((ROW_BLOCK, LANES), lambda b, j, tb=tb: (tb(b, j), 0)))
        operands += [h1t] + [yk] * TOP_K + [route]
    vec = pl.BlockSpec((1, D_MODEL), lambda b, j: (0, 0))
    return pl.pallas_call(
        _combine_body,
        grid=(batch, nb // COMBINE_BLOCKS),
        in_specs=in_specs + [vec, vec],
        out_specs=pl.BlockSpec((1, COMBINE_BLOCKS * ROW_BLOCK, D_MODEL), lambda b, j: (b, j, 0)),
        out_shape=jax.ShapeDtypeStruct((batch, seq_len, D_MODEL), F32),
        compiler_params=pltpu.CompilerParams(
            dimension_semantics=("parallel", "parallel"), vmem_limit_bytes=VMEM_LIMIT),
        name="combine",
    )(*operands, ln_g, ln_b)


def kernel(x, meta_tokens, ln_emb_g, ln_emb_b, w_in, gla_wa2, gla_ba, gla_norm_g, diff_lambda_q1, diff_lambda_k1, diff_lambda_q2, diff_lambda_k2, diff_norm_g, w_out, ln1_g, ln1_b, router_w, router_b, w_gate_up, b_gate_up, w_down, b_down, ln2_g, ln2_b):
    B, S, D = x.shape
    LP = S + ROW_BLOCK
    NP = B * LP
    row = lambda v: v.reshape(1, -1)

    w = w_in[0]
    w_main = jnp.concatenate([w[:, :1536], w[:, 1552:]], axis=1).astype(BF16)
    w_ga = w[:, 1536:1552].astype(BF16)
    meta_pad = jnp.pad(meta_tokens, ((FRONT_PAD, 0), (0, 0)))

    h0, gq, gk, gv, gr, la, dq, dk, dv = _inproj(
        x, meta_pad, row(ln_emb_g), row(ln_emb_b), w_main, w_ga,
        gla_wa2[0].astype(BF16), row(gla_ba[0]))

    og = _gla(gq, gk, gv, gr, la, row(gla_norm_g[0]))
    od = _diff_attention(dq, dk, dv, row(diff_lambda_q1[0]), row(diff_lambda_k1[0]),
                         row(diff_lambda_q2[0]), row(diff_lambda_k2[0]), row(diff_norm_g[0]))

    wo = w_out[0].astype(BF16)
    rw_pad = jnp.pad(router_w[0], ((0, 0), (0, LANES - N_EXPERTS)))
    rb_pad = jnp.pad(row(router_b[0]), ((0, 0), (0, LANES - N_EXPERTS)))
    h1t, route, keys, counts = _outproj_route(
        og.reshape(NP, 512), od.reshape(NP, 512), h0.reshape(NP, D), wo[:512], wo[512:],
        row(ln1_g[0]), row(ln1_b[0]), rw_pad, rb_pad, LP)

    R = MOE_ROWS
    i32 = jnp.int32
    counts = counts[:, 0].astype(i32)
    n_blk = (counts + R - 1) // R
    blk_end = jnp.cumsum(n_blk)
    blk_start = blk_end - n_blk
    grp_start = jnp.cumsum(counts) - counts
    n_total = blk_end[-1:]
    n_assign_max = B * (S + N_META) * TOP_K
    max_blocks = (n_assign_max + N_EXPERTS * (R - 1)) // R
    g = jnp.minimum(jnp.arange(max_blocks, dtype=i32), n_total[0] - 1)
    is_e = (jnp.minimum(jnp.sum(g[:, None] >= blk_end[None, :], axis=1), N_EXPERTS - 1)[:, None]
            == jnp.arange(N_EXPERTS)[None, :])
    pick = lambda v: jnp.sum(jnp.where(is_e, v[None, :], 0), axis=1)
    local = (g - pick(blk_start)) * R
    blk_p0 = (pick(grp_start) + local).astype(i32)

    assert R <= B * FRONT_PAD * TOP_K
    order = jnp.sort(keys.reshape(-1)) & ((1 << (NP * TOP_K - 1).bit_length()) - 1)

    bgu = b_gate_up[0].reshape(N_EXPERTS, D_FF, 2)
    yk = _experts(order, blk_p0, blk_start.astype(i32), n_blk.astype(i32), n_total.astype(i32),
                  h1t, w_gate_up[0],
                  bgu[:, :, 0].reshape(N_EXPERTS, 1, D_FF), bgu[:, :, 1].reshape(N_EXPERTS, 1, D_FF),
                  w_down[0], b_down[0].reshape(N_EXPERTS, 1, D), NP, LP)

    return _combine(h1t, yk, route, row(ln2_g[0]), row(ln2_b[0]), B, S)
```

```python
import functools
import math

import jax
import jax.numpy as jnp
from jax import lax
from jax.experimental import pallas as pl
from jax.experimental.pallas import tpu as pltpu

F32 = jnp.float32
BF16 = jnp.bfloat16

D_MODEL = 1024
LANES = 128
N_META = 16
ROW_BLOCK = 128
FRONT_PAD = ROW_BLOCK - N_META
CHUNK = 64

GLA_HEADS = 4
GLA_DK = 64
GLA_DV = 128
GLA_RANK = 16
GLA_TAU = 16.0
DIFF_HEADS = 4
DIFF_DH = 64
DIFF_DV = 128

N_EXPERTS = 32
TOP_K = 4
TOP_K_SHIFT = TOP_K.bit_length() - 1
D_FF = 1024
SWIGLU_LIMIT = 7.0
SWIGLU_ALPHA = 1.702
MOE_ROWS = 256

DEEPNORM_ALPHA = 2.0 ** 0.25
LAMBDA_INIT = 0.8 - 0.6 * math.exp(0.0)
LN_EPS = 1e-5
NEG_INF = -1e30
LOG2E = math.log2(math.e)

MXU_TILE = 256
ROUTE_TILE = 512
VMEM_LIMIT = 48 * 1024 * 1024
EXPERT_VMEM_LIMIT = 56 * 1024 * 1024

_GQ, _GK, _GV, _GR, _DQ, _DK, _DV = 0, 256, 512, 1024, 1536, 2048, 2560
_MAIN_WIDTH = 3072
_A_GV, _A_GR, _A_DQ, _A_DK, _A_DV, _A_GQ, _A_WIDTH = 0, 512, 1024, 1536, 2048, 2560, 2816
_X_H0, _X_GK, _X_LA, _X_WIDTH = 0, 1024, 1280, 1536


def _layer_norm(x, g, b):
    mu = jnp.mean(x, axis=-1, keepdims=True)
    xc = x - mu
    var = jnp.mean(xc * xc, axis=-1, keepdims=True)
    return xc * lax.rsqrt(var + LN_EPS) * g + b


def _split3_bf16(x):
    hi = x.astype(BF16)
    r1 = x - hi.astype(F32)
    mid = r1.astype(BF16)
    lo = (r1 - mid.astype(F32)).astype(BF16)
    return hi, mid, lo


TOKEN_SUBROWS = D_MODEL // LANES


def _store_token_tiled(ref, val, first_token=0):
    n = val.shape[0]
    for s in range(TOKEN_SUBROWS):
        ref[pl.ds(first_token * TOKEN_SUBROWS + s, n, stride=TOKEN_SUBROWS), :] = val[:, s * LANES:(s + 1) * LANES]


def _load_token_tiled(ref, n, first_row=0):
    return jnp.concatenate(
        [ref[pl.ds(first_row + s, n, stride=TOKEN_SUBROWS), :] for s in range(TOKEN_SUBROWS)], axis=1)


def _inproj_body(xa_ref, xb_ref, meta_ref, g_ref, b_ref, w_ref, wga_ref, wa2_ref, ba_ref,
                 act_ref, aux_ref):
    j = pl.program_id(1)
    rows = lax.broadcasted_iota(jnp.int32, (2 * ROW_BLOCK, 1), 0)
    valid = jnp.logical_or(j > 0, rows >= FRONT_PAD)
    xin = jnp.concatenate([jnp.where(j > 0, xa_ref[0], meta_ref[...]), xb_ref[0]], axis=0)
    h = jnp.where(valid, _layer_norm(xin, g_ref[...], b_ref[...]), 0.0)
    aux_ref[0, :, _X_H0:_X_H0 + D_MODEL] = h
    hb = h.astype(BF16)

    def proj(off, width):
        return jnp.dot(hb, w_ref[:, off:off + width], preferred_element_type=F32)

    act_ref[0, :, _A_GQ:_A_GQ + 256] = (proj(_GQ, 256) * GLA_DK ** -0.5).astype(BF16)
    aux_ref[0, :, _X_GK:_X_GK + 256] = proj(_GK, 256)
    act_ref[0, :, _A_GV:_A_GV + 512] = proj(_GV, 512).astype(BF16)
    act_ref[0, :, _A_GR:_A_GR + 512] = proj(_GR, 512).astype(BF16)
    act_ref[0, :, _A_DQ:_A_DQ + 512] = (proj(_DQ, 512) * (DIFF_DH ** -0.5 * LOG2E)).astype(BF16)
    act_ref[0, :, _A_DK:_A_DK + 512] = proj(_DK, 512).astype(BF16)
    act_ref[0, :, _A_DV:_A_DV + 512] = proj(_DV, 512).astype(BF16)

    a_lr = jnp.dot(hb, wga_ref[...], preferred_element_type=F32)
    z = jnp.dot(a_lr.astype(BF16), wa2_ref[...], preferred_element_type=F32) + ba_ref[...]
    log_sig = jnp.minimum(z, 0.0) - jnp.log1p(jnp.exp(-jnp.abs(z)))
    aux_ref[0, :, _X_LA:_X_LA + 256] = jnp.where(valid, log_sig / GLA_TAU, 0.0)


def _inproj(x, meta_pad, ln_g, ln_b, w_main, w_ga, wa2, ba):
    B, S, D = x.shape
    nb = S // ROW_BLOCK + 1
    LP = nb * ROW_BLOCK

    n_frame_blocks = S // ROW_BLOCK
    n_steps = pl.cdiv(LP, 2 * ROW_BLOCK)

    def row_spec(width):
        return pl.BlockSpec((1, 2 * ROW_BLOCK, width), lambda b, j: (b, j, 0))

    def frames(shift):
        return pl.BlockSpec((1, ROW_BLOCK, D),
                            lambda b, j: (b, jnp.clip(2 * j + shift, 0, n_frame_blocks - 1), 0))

    def full(shape):
        return pl.BlockSpec(shape, lambda b, j: (0,) * len(shape))

    out_shapes = [
        jax.ShapeDtypeStruct((B, LP, _A_WIDTH), BF16),
        jax.ShapeDtypeStruct((B, LP, _X_WIDTH), F32),
    ]
    return pl.pallas_call(
        _inproj_body,
        grid=(B, n_steps),
        in_specs=[
            frames(-1), frames(0),
            full((ROW_BLOCK, D)), full((1, D)), full((1, D)),
            full((D, _MAIN_WIDTH)), full((D, GLA_RANK)), full((GLA_RANK, 256)), full((1, 256)),
        ],
        out_specs=[row_spec(s.shape[-1]) for s in out_shapes],
        out_shape=out_shapes,
        compiler_params=pltpu.CompilerParams(
            dimension_semantics=("parallel", "parallel"), vmem_limit_bytes=VMEM_LIMIT),
        name="inproj",
    )(x, x, meta_pad, ln_g, ln_b, w_main, w_ga, wa2, ba)


def _gla_body(q_ref, k_ref, v_ref, r_ref, la_ref, g_ref, o_ref):
    n_groups = q_ref.shape[1] // ROW_BLOCK
    ri = lax.broadcasted_iota(jnp.int32, (ROW_BLOCK, ROW_BLOCK), 0)
    ci = lax.broadcasted_iota(jnp.int32, (ROW_BLOCK, ROW_BLOCK), 1)
    later = jnp.logical_and(ri // CHUNK == ci // CHUNK, ci > ri).astype(BF16)
    sr = lax.broadcasted_iota(jnp.int32, (GLA_HEADS * GLA_DV, GLA_HEADS * GLA_DK), 0)
    sc = lax.broadcasted_iota(jnp.int32, (GLA_HEADS * GLA_DV, GLA_HEADS * GLA_DK), 1)
    same_head = sr // GLA_DV == sc // GLA_DK
    gain = g_ref[...]
    st = jnp.zeros((GLA_HEADS * GLA_DV, GLA_HEADS * GLA_DK), F32)

    for grp in range(n_groups):
        g0 = grp * ROW_BLOCK
        la = la_ref[0, g0:g0 + ROW_BLOCK, :]
        hi, mid, lo = _split3_bf16(la)
        suffix = (jnp.dot(later, hi, preferred_element_type=F32)
                  + jnp.dot(later, mid, preferred_element_type=F32)
                  + jnp.dot(later, lo, preferred_element_type=F32))
        kdec = (k_ref[0, g0:g0 + ROW_BLOCK, :] * jnp.exp(suffix)).astype(BF16)
        for half in range(ROW_BLOCK // CHUNK):
            f = half * CHUNK
            r0 = g0 + f
            tot = suffix[f:f + 1, :] + la[f:f + 1, :]
            upd = lax.dot_general(v_ref[0, r0:r0 + CHUNK, :], kdec[f:f + CHUNK, :],
                                  (((0,), (0,)), ((), ())), preferred_element_type=F32)
            st = st * jnp.exp(tot) + jnp.where(same_head, upd, 0.0)
            o = lax.dot_general(q_ref[0, r0:r0 + CHUNK, :], st.astype(BF16),
                                (((1,), (1,)), ((), ())), preferred_element_type=F32)
            r = r_ref[0, r0:r0 + CHUNK, :].astype(F32)
            for h in range(GLA_HEADS):
                oh = o[:, h * GLA_DV:(h + 1) * GLA_DV]
                rh = r[:, h * GLA_DV:(h + 1) * GLA_DV]
                ms = jnp.mean(oh * oh, axis=-1, keepdims=True)
                out = oh * lax.rsqrt(ms + LN_EPS) * gain * (rh * jax.nn.sigmoid(rh))
                o_ref[0, r0:r0 + CHUNK, h * GLA_DV:(h + 1) * GLA_DV] = out.astype(BF16)


def _gla(act, aux, norm_g):
    B, LP, _ = act.shape

    def seq(width, offset=0):
        return pl.BlockSpec((1, LP, width), lambda b: (b, 0, offset // width))

    return pl.pallas_call(
        _gla_body,
        grid=(B,),
        in_specs=[seq(256, _A_GQ), seq(256, _X_GK), seq(512, _A_GV), seq(512, _A_GR), seq(256, _X_LA),
                  pl.BlockSpec((1, GLA_DV), lambda b: (0, 0))],
        out_specs=seq(512),
        out_shape=jax.ShapeDtypeStruct((B, LP, 512), BF16),
        compiler_params=pltpu.CompilerParams(
            dimension_semantics=("parallel",), vmem_limit_bytes=VMEM_LIMIT),
        name="gla",
    )(act, aux, act, act, aux, norm_g)


ATT_BLOCK = 256


def _diff_body(q_ref, k_ref, v_ref, lq1_ref, lk1_ref, lq2_ref, lk2_ref, g_ref, o_ref,
               dmask_ref):
    h = pl.program_id(1)
    slope = jnp.where(h == 0, 2.0 ** -2, jnp.where(h == 1, 2.0 ** -4, jnp.where(h == 2, 2.0 ** -6, 2.0 ** -8)))
    slope = slope.astype(F32) * LOG2E
    lam = (jnp.exp(jnp.sum(lq1_ref[...] * lk1_ref[...], axis=-1, keepdims=True))
           - jnp.exp(jnp.sum(lq2_ref[...] * lk2_ref[...], axis=-1, keepdims=True)) + LAMBDA_INIT)
    gain = g_ref[...] * (1.0 - LAMBDA_INIT)
    n_qblocks = (q_ref.shape[1] - ROW_BLOCK) // ATT_BLOCK
    nt = (((1,), (1,)), ((), ()))

    def split_q(q):
        lane = lax.broadcasted_iota(jnp.int32, q.shape, 1)
        zero = jnp.zeros_like(q)
        return jnp.where(lane < DIFF_DH, q, zero), jnp.where(lane >= DIFF_DH, q, zero)

    def with_ones(v):
        return jnp.concatenate([v, jnp.ones_like(v)], axis=1)

    def softmax_av(qz, kk, vext, add_bias):
        s = add_bias(lax.dot_general(qz, kk, nt, preferred_element_type=F32))
        m = jnp.max(s, axis=-1, keepdims=True)
        p = jnp.exp2(s - m).astype(BF16)
        return jnp.dot(p, vext, preferred_element_type=F32)

    def finish(a1, a2):
        o = a1[:, :DIFF_DV] / a1[:, DIFF_DV:] - lam * (a2[:, :DIFF_DV] / a2[:, DIFF_DV:])
        ms = jnp.mean(o * o, axis=-1, keepdims=True)
        return (o * lax.rsqrt(ms + LN_EPS) * gain).astype(BF16)

    r = lax.broadcasted_iota(jnp.int32, (ROW_BLOCK, ROW_BLOCK), 0)
    c = lax.broadcasted_iota(jnp.int32, (ROW_BLOCK, ROW_BLOCK), 1)
    ok = jnp.logical_and(c // CHUNK <= r // CHUNK, c >= FRONT_PAD)
    bias_lead = jnp.where(ok, -slope * jnp.abs(r - c).astype(F32), NEG_INF)
    q1z, q2z = split_q(q_ref[0, 0:ROW_BLOCK, :])
    k_lead = k_ref[0, 0:ROW_BLOCK, :]
    v_lead = with_ones(v_ref[0, 0:ROW_BLOCK, :])
    o_ref[0, 0:ROW_BLOCK, :] = finish(softmax_av(q1z, k_lead, v_lead, lambda s: s + bias_lead),
                                      softmax_av(q2z, k_lead, v_lead, lambda s: s + bias_lead))

    r = lax.broadcasted_iota(jnp.int32, (ATT_BLOCK, ATT_BLOCK), 0)
    c = lax.broadcasted_iota(jnp.int32, (ATT_BLOCK, ATT_BLOCK), 1)
    rel = jnp.where(c <= r, c, 2 * r - c).astype(F32)
    dmask_ref[...] = jnp.where(c // CHUNK <= r // CHUNK, slope * rel, NEG_INF)

    for jq in range(n_qblocks):
        qbase = ROW_BLOCK + jq * ATT_BLOCK
        n_keys = qbase + ATT_BLOCK
        q1z, q2z = split_q(q_ref[0, qbase:qbase + ATT_BLOCK, :])
        kk = k_ref[0, 0:n_keys, :]
        vext = with_ones(v_ref[0, 0:n_keys, :])
        col = lax.broadcasted_iota(jnp.int32, (1, qbase), 1)
        col_bias = jnp.where(col >= FRONT_PAD, slope * (col - qbase).astype(F32), NEG_INF)

        def add_bias(s, col_bias=col_bias, qbase=qbase):
            return jnp.concatenate([s[:, :qbase] + col_bias, s[:, qbase:] + dmask_ref[...]], axis=1)

        o_ref[0, qbase:qbase + ATT_BLOCK, :] = finish(softmax_av(q1z, kk, vext, add_bias),
                                                      softmax_av(q2z, kk, vext, add_bias))


def _diff_attention(act, lq1, lk1, lq2, lk2, norm_g):
    B, LP, _ = act.shape
    W = DIFF_HEADS * DIFF_DV
    small = pl.BlockSpec((1, DIFF_DH), lambda b, h: (0, 0))

    def seq(offset=0):
        return pl.BlockSpec((1, LP, 2 * DIFF_DH), lambda b, h: (b, 0, offset // (2 * DIFF_DH) + h))

    return pl.pallas_call(
        _diff_body,
        grid=(B, DIFF_HEADS),
        in_specs=[seq(_A_DQ), seq(_A_DK), seq(_A_DV), small, small, small, small,
                  pl.BlockSpec((1, DIFF_DV), lambda b, h: (0, 0))],
        out_specs=seq(),
        out_shape=jax.ShapeDtypeStruct((B, LP, W), BF16),
        scratch_shapes=[pltpu.VMEM((ATT_BLOCK, ATT_BLOCK), F32)],
        compiler_params=pltpu.CompilerParams(
            dimension_semantics=("parallel", "parallel"), vmem_limit_bytes=VMEM_LIMIT),
        name="diff_attn",
    )(act, act, act, lq1, lk1, lq2, lk2, norm_g)


def _route_body(og_ref, od_ref, h0_ref, wog_ref, wod_ref, g_ref, b_ref, rw_ref, rb_ref,
                h1_ref, route_ref, keys_ref, cnt_ref, carry_ref, *, rows_per_seq, id_bits):
    t = pl.program_id(0)

    @pl.when(t == 0)
    def _():
        carry_ref[...] = jnp.zeros_like(carry_ref)

    mix = (jnp.dot(og_ref[...], wog_ref[...], preferred_element_type=F32)
           + jnp.dot(od_ref[...], wod_ref[...], preferred_element_type=F32))
    h1 = _layer_norm(DEEPNORM_ALPHA * h0_ref[...] + mix, g_ref[...], b_ref[...])
    _store_token_tiled(h1_ref, h1)

    hh, hm, _ = _split3_bf16(h1)
    wh, wm, _ = _split3_bf16(rw_ref[...])
    head = jnp.dot(hh, jnp.concatenate([wh, wm], axis=1), preferred_element_type=F32)
    logits = rb_ref[...] + head[:, :LANES] + head[:, LANES:] + jnp.dot(hm, wh, preferred_element_type=F32)

    T = logits.shape[0]
    work = logits.T[:N_EXPERTS]
    expert = lax.broadcasted_iota(jnp.int32, (N_EXPERTS, T), 0)
    tok = t * T + lax.broadcasted_iota(jnp.int32, (1, T), 1)
    valid = (tok % rows_per_seq) >= FRONT_PAD
    top_v, top_i = [], []
    for _ in range(TOP_K):
        mx = jnp.max(work, axis=0, keepdims=True)
        idx = jnp.min(jnp.where(work == mx, expert, N_EXPERTS), axis=0, keepdims=True)
        top_v.append(mx)
        top_i.append(idx)
        work = jnp.where(expert == idx, -jnp.inf, work)
    ex = [jnp.exp(v - top_v[0]) for v in top_v]
    den = ex[0] + ex[1] + ex[2] + ex[3]
    gates = [lax.bitcast_convert_type(e / den, jnp.int32) for e in ex]
    keys = [jnp.where(valid, idx, N_EXPERTS) * (1 << id_bits) + (tok * TOP_K + k) for k, idx in enumerate(top_i)]

    onehot = jnp.zeros((N_EXPERTS, T), F32)
    for idx in top_i:
        onehot = onehot + (expert == idx).astype(F32)
    carry_ref[...] = carry_ref[...] + jnp.sum(jnp.where(valid, onehot, 0.0), axis=1, keepdims=True)
    cnt_ref[...] = carry_ref[...]

    by_lane = jnp.concatenate(top_i + keys + gates + [jnp.zeros((LANES - 3 * TOP_K, T), jnp.int32)], axis=0)
    route_ref[...] = by_lane.T
    keys_ref[...] = jnp.concatenate([key[:, q * LANES:(q + 1) * LANES] for key in keys for q in range(T // LANES)], axis=0)


def _outproj_route(og, od, aux, wo_g, wo_d, ln_g, ln_b, rw, rb, rows_per_seq):
    NP, D = aux.shape[0], D_MODEL
    assert _X_H0 == 0
    T = ROUTE_TILE

    def rows(width):
        return pl.BlockSpec((T, width), lambda t: (t, 0))

    def full(shape):
        return pl.BlockSpec(shape, lambda t: (0,) * len(shape))

    key_rows = T * TOP_K // LANES
    return pl.pallas_call(
        functools.partial(_route_body, rows_per_seq=rows_per_seq, id_bits=(NP * TOP_K - 1).bit_length()),
        grid=(NP // T,),
        in_specs=[rows(512), rows(512), rows(D), full((512, D)), full((512, D)),
                  full((1, D)), full((1, D)), full((D, LANES)), full((1, LANES))],
        out_specs=[pl.BlockSpec((T * TOKEN_SUBROWS, LANES), lambda t: (t, 0)), rows(LANES),
                   pl.BlockSpec((key_rows, LANES), lambda t: (t, 0)), full((N_EXPERTS, 1))],
        out_shape=[jax.ShapeDtypeStruct((NP * TOKEN_SUBROWS, LANES), F32),
                   jax.ShapeDtypeStruct((NP, LANES), jnp.int32),
                   jax.ShapeDtypeStruct((NP * TOP_K // LANES, LANES), jnp.int32),
                   jax.ShapeDtypeStruct((N_EXPERTS, 1), F32)],
        scratch_shapes=[pltpu.VMEM((N_EXPERTS, 1), F32)],
        compiler_params=pltpu.CompilerParams(
            dimension_semantics=("arbitrary",), vmem_limit_bytes=VMEM_LIMIT),
        name="outproj_route",
    )(og, od, aux, wo_g, wo_d, ln_g, ln_b, rw, rb)


def _expert_body(order_ref, p0_ref, bstart_ref, nblk_ref, ntot_ref,
                 h1t_hbm, wgu_ref, bg_ref, bu_ref, wd_ref, bd_ref, yk_hbm,
                 wg_s, wu_s, wd_s, xb0, xb1, xb2, yb0, yb1, yb2, zbuf, gsem, ssem, zsem,
                 *, n_tokens, rows_per_seq):
    e = pl.program_id(0)
    n_total = ntot_ref[0]
    R = MOE_ROWS
    TS = TOKEN_SUBROWS
    xb = (xb0, xb1, xb2)
    yb = (yb0, yb1, yb2)
    NB = len(xb)

    def gather_start(b, slot):
        p0 = p0_ref[b]
        for r in range(R):
            tok = order_ref[p0 + r] >> TOP_K_SHIFT
            pltpu.make_async_copy(
                h1t_hbm.at[pl.ds(pl.multiple_of(tok * TS, TS), TS), :],
                xb[slot].at[pl.ds(r * TS, TS), :], gsem.at[slot]).start(priority=r % 2)

    def gather_wait(slot):
        pltpu.make_async_copy(h1t_hbm.at[pl.ds(0, R * TS), :], xb[slot], gsem.at[slot]).wait()

    def scatter_start(b, slot):
        p0 = p0_ref[b]
        for r in range(R):
            row = order_ref[p0 + r] * TS
            pltpu.make_async_copy(
                yb[slot].at[pl.ds(r * TS, TS), :],
                yk_hbm.at[pl.ds(pl.multiple_of(row, TS), TS), :], ssem.at[slot]).start(priority=r % 2)

    def scatter_wait(slot):
        pltpu.make_async_copy(yb[slot], yk_hbm.at[pl.ds(0, R * TS), :], ssem.at[slot]).wait()

    @pl.when(e == 0)
    def _():
        zbuf[...] = jnp.zeros_like(zbuf)
        lead = FRONT_PAD * TS
        fills = [((s * rows_per_seq * TOP_K + k * FRONT_PAD) * TS, lead)
                 for s in range(n_tokens // rows_per_seq) for k in range(TOP_K)]
        copies = [pltpu.make_async_copy(zbuf.at[pl.ds(0, n), :], yk_hbm.at[pl.ds(o, n), :], zsem)
                  for o, n in fills]
        for cp in copies:
            cp.start()
        for cp in copies:
            cp.wait()
        gather_start(0, 0)
        gather_start(jnp.minimum(1, n_total - 1), 1)

    @pl.when(nblk_ref[e] > 0)
    def _():
        W, H = MXU_TILE, MXU_TILE // 2
        r = lax.broadcasted_iota(jnp.int32, (W, W), 0)
        c = lax.broadcasted_iota(jnp.int32, (W, W), 1)
        perm = (r == jnp.where(c < H, 2 * c, 2 * (c - H) + 1)).astype(BF16)
        for tt in range(2 * D_FF // W):
            wt = wgu_ref[0, :, tt * W:(tt + 1) * W].astype(BF16)
            sp = jnp.dot(wt, perm, preferred_element_type=F32)
            wg_s[:, tt * H:(tt + 1) * H] = sp[:, :H].astype(BF16)
            wu_s[:, tt * H:(tt + 1) * H] = sp[:, H:].astype(BF16)
        wd_s[...] = wd_ref[0].astype(BF16)

    def run_block(b, slot):
        prv = (slot + 2) % NB
        gather_wait(slot)
        gather_start(jnp.minimum(b + 2, n_total - 1), prv)
        x = _load_token_tiled(xb[slot], R).astype(BF16)
        gt = jnp.dot(x, wg_s[...], preferred_element_type=F32) + bg_ref[0]
        up = jnp.dot(x, wu_s[...], preferred_element_type=F32) + bu_ref[0]
        gt = jnp.minimum(gt, SWIGLU_LIMIT)
        up = jnp.clip(up, -SWIGLU_LIMIT, SWIGLU_LIMIT)
        act = (up + 1.0) * (gt * jax.nn.sigmoid(SWIGLU_ALPHA * gt))
        y = jnp.dot(act.astype(BF16), wd_s[...], preferred_element_type=F32) + bd_ref[0]
        _store_token_tiled(yb[slot], y)

        @pl.when(b >= 1)
        def _():
            scatter_wait(prv)
        scatter_start(b, slot)

    def block(b, carry):
        for slot in range(NB):
            pl.when(b % NB == slot)(functools.partial(run_block, b, slot))
        return carry

    b0 = bstart_ref[e]
    lax.fori_loop(b0, b0 + nblk_ref[e], block, 0)

    @pl.when(e == pl.num_programs(0) - 1)
    def _():
        last = n_total - 1
        for slot in range(NB):
            @pl.when(last % NB == slot)
            def _(slot=slot):
                gather_wait((slot + 1) % NB)
                gather_wait((slot + 2) % NB)
                scatter_wait(slot)


def _experts(order, blk_p0, blk_start, n_blk, n_total, h1t, w_gate_up, b_gate, b_up, w_down,
             b_down, n_tokens, rows_per_seq):
    D = D_MODEL
    R = MOE_ROWS
    buf = pltpu.VMEM((R * TOKEN_SUBROWS, LANES), F32)
    per_expert = lambda e, *_: (e, 0, 0)
    grid_spec = pltpu.PrefetchScalarGridSpec(
        num_scalar_prefetch=5,
        grid=(N_EXPERTS,),
        in_specs=[
            pl.BlockSpec(memory_space=pl.ANY),
            pl.BlockSpec((1, D, 2 * D_FF), per_expert),
            pl.BlockSpec((1, 1, D_FF), per_expert),
            pl.BlockSpec((1, 1, D_FF), per_expert),
            pl.BlockSpec((1, D_FF, D), per_expert),
            pl.BlockSpec((1, 1, D), per_expert),
        ],
        out_specs=pl.BlockSpec(memory_space=pl.ANY),
        scratch_shapes=[pltpu.VMEM((D, D_FF), BF16), pltpu.VMEM((D, D_FF), BF16), pltpu.VMEM((D_FF, D), BF16),
                        buf, buf, buf, buf, buf, buf, pltpu.VMEM((FRONT_PAD * TOKEN_SUBROWS, LANES), F32),
                        pltpu.SemaphoreType.DMA((3,)), pltpu.SemaphoreType.DMA((3,)), pltpu.SemaphoreType.DMA],
    )
    out_rows = TOP_K * n_tokens * TOKEN_SUBROWS
    return pl.pallas_call(
        functools.partial(_expert_body, n_tokens=n_tokens, rows_per_seq=rows_per_seq),
        grid_spec=grid_spec,
        out_shape=jax.ShapeDtypeStruct((out_rows, LANES), F32),
        compiler_params=pltpu.CompilerParams(
            dimension_semantics=("arbitrary",), vmem_limit_bytes=EXPERT_VMEM_LIMIT),
        name="experts",
    )(order, blk_p0, blk_start, n_blk, n_total, h1t, w_gate_up, b_gate, b_up, w_down, b_down)


COMBINE_BLOCKS = 4


def _combine_body(*refs):
    per_block = 2 + TOP_K
    g_ref, b_ref, o_ref = refs[COMBINE_BLOCKS * per_block:]
    for i in range(COMBINE_BLOCKS):
        h1_ref, *y_refs, route_ref = refs[i * per_block:(i + 1) * per_block]
        route = route_ref[...]
        y = jnp.zeros((ROW_BLOCK, D_MODEL), F32)
        for k, yk_ref in enumerate(y_refs):
            gate = lax.bitcast_convert_type(route[:, 2 * TOP_K + k:2 * TOP_K + k + 1], F32)
            y = y + gate * _load_token_tiled(yk_ref.reshape(ROW_BLOCK * TOKEN_SUBROWS, LANES), ROW_BLOCK)
        h1 = _load_token_tiled(h1_ref, ROW_BLOCK)
        o_ref[0, i * ROW_BLOCK:(i + 1) * ROW_BLOCK, :] = _layer_norm(
            DEEPNORM_ALPHA * h1 + y, g_ref[...], b_ref[...])


def _combine(h1t, yk, route, ln_g, ln_b, batch, seq_len):
    nb = seq_len // ROW_BLOCK
    blocks_per_seq = nb + 1
    tile_rows = ROW_BLOCK * TOKEN_SUBROWS
    yk = yk.reshape(-1, TOP_K, TOKEN_SUBROWS, LANES)

    def token_block(i):
        return lambda b, j: b * blocks_per_seq + COMBINE_BLOCKS * j + i + 1

    in_specs, operands = [], []
    for i in range(COMBINE_BLOCKS):
        tb = token_block(i)
        in_specs.append(pl.BlockSpec((tile_rows, LANES), lambda b, j, tb=tb: (tb(b, j), 0)))
        in_specs += [pl.BlockSpec((ROW_BLOCK, None, TOKEN_SUBROWS, LANES), lambda b, j, tb=tb, k=k: (tb(b, j), k, 0, 0))
                     for k in range(TOP_K)]
        in_specs.append(pl.BlockSpec((ROW_BLOCK, LANES), lambda b, j, tb=tb: (tb(b, j), 0)))
        operands += [h1t] + [yk] * TOP_K + [route]
    vec = pl.BlockSpec((1, D_MODEL), lambda b, j: (0, 0))
    return pl.pallas_call(
        _combine_body,
        grid=(batch, nb // COMBINE_BLOCKS),
        in_specs=in_specs + [vec, vec],
        out_specs=pl.BlockSpec((1, COMBINE_BLOCKS * ROW_BLOCK, D_MODEL), lambda b, j: (b, j, 0)),
        out_shape=jax.ShapeDtypeStruct((batch, seq_len, D_MODEL), F32),
        compiler_params=pltpu.CompilerParams(
            dimension_semantics=("parallel", "parallel"), vmem_limit_bytes=VMEM_LIMIT),
        name="combine",
    )(*operands, ln_g, ln_b)


def kernel(x, meta_tokens, ln_emb_g, ln_emb_b, w_in, gla_wa2, gla_ba, gla_norm_g, diff_lambda_q1, diff_lambda_k1, diff_lambda_q2, diff_lambda_k2, diff_norm_g, w_out, ln1_g, ln1_b, router_w, router_b, w_gate_up, b_gate_up, w_down, b_down, ln2_g, ln2_b):
    B, S, D = x.shape
    LP = S + ROW_BLOCK
    NP = B * LP
    row = lambda v: v.reshape(1, -1)

    w = w_in[0]
    w_main = jnp.concatenate([w[:, :1536], w[:, 1552:]], axis=1).astype(BF16)
    w_ga = w[:, 1536:1552].astype(BF16)
    meta_pad = jnp.pad(meta_tokens, ((FRONT_PAD, 0), (0, 0)))

    act, aux = _inproj(
        x, meta_pad, row(ln_emb_g), row(ln_emb_b), w_main, w_ga,
        gla_wa2[0].astype(BF16), row(gla_ba[0]))

    og = _gla(act, aux, row(gla_norm_g[0]))
    od = _diff_attention(act, row(diff_lambda_q1[0]), row(diff_lambda_k1[0]),
                         row(diff_lambda_q2[0]), row(diff_lambda_k2[0]), row(diff_norm_g[0]))

    wo = w_out[0].astype(BF16)
    rw_pad = jnp.pad(router_w[0], ((0, 0), (0, LANES - N_EXPERTS)))
    rb_pad = jnp.pad(row(router_b[0]), ((0, 0), (0, LANES - N_EXPERTS)))
    h1t, route, keys, counts = _outproj_route(
        og.reshape(NP, 512), od.reshape(NP, 512), aux.reshape(NP, _X_WIDTH), wo[:512], wo[512:],
        row(ln1_g[0]), row(ln1_b[0]), rw_pad, rb_pad, LP)

    R = MOE_ROWS
    i32 = jnp.int32
    counts = counts[:, 0].astype(i32)
    n_blk = (counts + R - 1) // R
    blk_end = jnp.cumsum(n_blk)
    blk_start = blk_end - n_blk
    grp_start = jnp.cumsum(counts) - counts
    n_total = blk_end[-1:]
    n_assign_max = B * (S + N_META) * TOP_K
    max_blocks = (n_assign_max + N_EXPERTS * (R - 1)) // R
    g = jnp.minimum(jnp.arange(max_blocks, dtype=i32), n_total[0] - 1)
    is_e = (jnp.minimum(jnp.sum(g[:, None] >= blk_end[None, :], axis=1), N_EXPERTS - 1)[:, None]
            == jnp.arange(N_EXPERTS)[None, :])
    pick = lambda v: jnp.sum(jnp.where(is_e, v[None, :], 0), axis=1)
    local = (g - pick(blk_start)) * R
    blk_p0 = (pick(grp_start) + local).astype(i32)

    assert R <= B * FRONT_PAD * TOP_K
    order = jnp.sort(keys.reshape(-1)) & ((1 << (NP * TOP_K - 1).bit_length()) - 1)

    bgu = b_gate_up[0].reshape(N_EXPERTS, D_FF, 2)
    yk = _experts(order, blk_p0, blk_start.astype(i32), n_blk.astype(i32), n_total.astype(i32),
                  h1t, w_gate_up[0],
                  bgu[:, :, 0].reshape(N_EXPERTS, 1, D_FF), bgu[:, :, 1].reshape(N_EXPERTS, 1, D_FF),
                  w_down[0], b_down[0].reshape(N_EXPERTS, 1, D), NP, LP)

    return _combine(h1t, yk, route, row(ln2_g[0]), row(ln2_b[0]), B, S)
```

```python
import functools
import math

import jax
import jax.numpy as jnp
from jax import lax
from jax.experimental import pallas as pl
from jax.experimental.pallas import tpu as pltpu

F32 = jnp.float32
BF16 = jnp.bfloat16

D_MODEL = 1024
LANES = 128
N_META = 16
ROW_BLOCK = 128
FRONT_PAD = ROW_BLOCK - N_META
CHUNK = 64

GLA_HEADS = 4
GLA_DK = 64
GLA_DV = 128
GLA_RANK = 16
GLA_TAU = 16.0
DIFF_HEADS = 4
DIFF_DH = 64
DIFF_DV = 128

N_EXPERTS = 32
TOP_K = 4
TOP_K_SHIFT = TOP_K.bit_length() - 1
D_FF = 1024
SWIGLU_LIMIT = 7.0
SWIGLU_ALPHA = 1.702
MOE_ROWS = 256

DEEPNORM_ALPHA = 2.0 ** 0.25
LAMBDA_INIT = 0.8 - 0.6 * math.exp(0.0)
LN_EPS = 1e-5
NEG_INF = -1e30
LOG2E = math.log2(math.e)

MXU_TILE = 256
ROUTE_TILE = 1024
VMEM_LIMIT = 48 * 1024 * 1024
EXPERT_VMEM_LIMIT = 56 * 1024 * 1024

_GQ, _GK, _GV, _GR, _DQ, _DK, _DV = 0, 256, 512, 1024, 1536, 2048, 2560
_MAIN_WIDTH = 3072


def _layer_norm(x, g, b):
    mu = jnp.mean(x, axis=-1, keepdims=True)
    xc = x - mu
    var = jnp.mean(xc * xc, axis=-1, keepdims=True)
    return xc * lax.rsqrt(var + LN_EPS) * g + b


def _split3_bf16(x):
    hi = x.astype(BF16)
    r1 = x - hi.astype(F32)
    mid = r1.astype(BF16)
    lo = (r1 - mid.astype(F32)).astype(BF16)
    return hi, mid, lo


TOKEN_SUBROWS = D_MODEL // LANES


def _store_token_tiled(ref, val, first_token=0):
    n = val.shape[0]
    for s in range(TOKEN_SUBROWS):
        ref[pl.ds(first_token * TOKEN_SUBROWS + s, n, stride=TOKEN_SUBROWS), :] = val[:, s * LANES:(s + 1) * LANES]


def _load_token_tiled(ref, n, first_row=0):
    return jnp.concatenate(
        [ref[pl.ds(first_row + s, n, stride=TOKEN_SUBROWS), :] for s in range(TOKEN_SUBROWS)], axis=1)


def _inproj_body(xa_ref, xb_ref, meta_ref, g_ref, b_ref, w_ref, wga_ref, wa2_ref, ba_ref,
                 h0_ref, gq_ref, gk_ref, gv_ref, gr_ref, la_ref, dq_ref, dk_ref, dv_ref):
    j = pl.program_id(1)
    rows = lax.broadcasted_iota(jnp.int32, (2 * ROW_BLOCK, 1), 0)
    valid = jnp.logical_or(j > 0, rows >= FRONT_PAD)
    xin = jnp.concatenate([jnp.where(j > 0, xa_ref[0], meta_ref[...]), xb_ref[0]], axis=0)
    h = jnp.where(valid, _layer_norm(xin, g_ref[...], b_ref[...]), 0.0)
    h0_ref[0] = h
    hb = h.astype(BF16)

    def proj(off, width):
        return jnp.dot(hb, w_ref[:, off:off + width], preferred_element_type=F32)

    gq_ref[0] = (proj(_GQ, 256) * GLA_DK ** -0.5).astype(BF16)
    gk_ref[0] = proj(_GK, 256)
    gv_ref[0] = proj(_GV, 512).astype(BF16)
    gr_ref[0] = proj(_GR, 512).astype(BF16)
    dq_ref[0] = (proj(_DQ, 512) * (DIFF_DH ** -0.5 * LOG2E)).astype(BF16)
    dk_ref[0] = proj(_DK, 512).astype(BF16)
    dv_ref[0] = proj(_DV, 512).astype(BF16)

    a_lr = jnp.dot(hb, wga_ref[...], preferred_element_type=F32)
    z = jnp.dot(a_lr.astype(BF16), wa2_ref[...], preferred_element_type=F32) + ba_ref[...]
    log_sig = jnp.minimum(z, 0.0) - jnp.log1p(jnp.exp(-jnp.abs(z)))
    la_ref[0] = jnp.where(valid, log_sig / GLA_TAU, 0.0)


def _inproj(x, meta_pad, ln_g, ln_b, w_main, w_ga, wa2, ba):
    B, S, D = x.shape
    nb = S // ROW_BLOCK + 1
    LP = nb * ROW_BLOCK

    n_frame_blocks = S // ROW_BLOCK
    n_steps = pl.cdiv(LP, 2 * ROW_BLOCK)

    def row_spec(width):
        return pl.BlockSpec((1, 2 * ROW_BLOCK, width), lambda b, j: (b, j, 0))

    def frames(shift):
        return pl.BlockSpec((1, ROW_BLOCK, D),
                            lambda b, j: (b, jnp.clip(2 * j + shift, 0, n_frame_blocks - 1), 0))

    def full(shape):
        return pl.BlockSpec(shape, lambda b, j: (0,) * len(shape))

    out_shapes = [
        jax.ShapeDtypeStruct((B, LP, D), F32),
        jax.ShapeDtypeStruct((B, LP, 256), BF16),
        jax.ShapeDtypeStruct((B, LP, 256), F32),
        jax.ShapeDtypeStruct((B, LP, 512), BF16),
        jax.ShapeDtypeStruct((B, LP, 512), BF16),
        jax.ShapeDtypeStruct((B, LP, 256), F32),
        jax.ShapeDtypeStruct((B, LP, 512), BF16),
        jax.ShapeDtypeStruct((B, LP, 512), BF16),
        jax.ShapeDtypeStruct((B, LP, 512), BF16),
    ]
    return pl.pallas_call(
        _inproj_body,
        grid=(B, n_steps),
        in_specs=[
            frames(-1), frames(0),
            full((ROW_BLOCK, D)), full((1, D)), full((1, D)),
            full((D, _MAIN_WIDTH)), full((D, GLA_RANK)), full((GLA_RANK, 256)), full((1, 256)),
        ],
        out_specs=[row_spec(s.shape[-1]) for s in out_shapes],
        out_shape=out_shapes,
        compiler_params=pltpu.CompilerParams(
            dimension_semantics=("parallel", "parallel"), vmem_limit_bytes=VMEM_LIMIT),
        name="inproj",
    )(x, x, meta_pad, ln_g, ln_b, w_main, w_ga, wa2, ba)


def _gla_body(q_ref, k_ref, v_ref, r_ref, la_ref, g_ref, o_ref):
    n_groups = q_ref.shape[1] // ROW_BLOCK
    ri = lax.broadcasted_iota(jnp.int32, (ROW_BLOCK, ROW_BLOCK), 0)
    ci = lax.broadcasted_iota(jnp.int32, (ROW_BLOCK, ROW_BLOCK), 1)
    later = jnp.logical_and(ri // CHUNK == ci // CHUNK, ci > ri).astype(BF16)
    sr = lax.broadcasted_iota(jnp.int32, (GLA_HEADS * GLA_DV, GLA_HEADS * GLA_DK), 0)
    sc = lax.broadcasted_iota(jnp.int32, (GLA_HEADS * GLA_DV, GLA_HEADS * GLA_DK), 1)
    same_head = sr // GLA_DV == sc // GLA_DK
    gain = g_ref[...]
    st = jnp.zeros((GLA_HEADS * GLA_DV, GLA_HEADS * GLA_DK), F32)

    for grp in range(n_groups):
        g0 = grp * ROW_BLOCK
        la = la_ref[0, g0:g0 + ROW_BLOCK, :]
        hi, mid, lo = _split3_bf16(la)
        suffix = (jnp.dot(later, hi, preferred_element_type=F32)
                  + jnp.dot(later, mid, preferred_element_type=F32)
                  + jnp.dot(later, lo, preferred_element_type=F32))
        kdec = (k_ref[0, g0:g0 + ROW_BLOCK, :] * jnp.exp(suffix)).astype(BF16)
        for half in range(ROW_BLOCK // CHUNK):
            f = half * CHUNK
            r0 = g0 + f
            tot = suffix[f:f + 1, :] + la[f:f + 1, :]
            upd = lax.dot_general(v_ref[0, r0:r0 + CHUNK, :], kdec[f:f + CHUNK, :],
                                  (((0,), (0,)), ((), ())), preferred_element_type=F32)
            st = st * jnp.exp(tot) + jnp.where(same_head, upd, 0.0)
            o = lax.dot_general(q_ref[0, r0:r0 + CHUNK, :], st.astype(BF16),
                                (((1,), (1,)), ((), ())), preferred_element_type=F32)
            r = r_ref[0, r0:r0 + CHUNK, :].astype(F32)
            for h in range(GLA_HEADS):
                oh = o[:, h * GLA_DV:(h + 1) * GLA_DV]
                rh = r[:, h * GLA_DV:(h + 1) * GLA_DV]
                ms = jnp.mean(oh * oh, axis=-1, keepdims=True)
                out = oh * lax.rsqrt(ms + LN_EPS) * gain * (rh * jax.nn.sigmoid(rh))
                o_ref[0, r0:r0 + CHUNK, h * GLA_DV:(h + 1) * GLA_DV] = out.astype(BF16)


def _gla(gq, gk, gv, gr, la, norm_g):
    B, LP, _ = gq.shape

    def seq(width):
        return pl.BlockSpec((1, LP, width), lambda b: (b, 0, 0))

    return pl.pallas_call(
        _gla_body,
        grid=(B,),
        in_specs=[seq(256), seq(256), seq(512), seq(512), seq(256),
                  pl.BlockSpec((1, GLA_DV), lambda b: (0, 0))],
        out_specs=seq(512),
        out_shape=jax.ShapeDtypeStruct((B, LP, 512), BF16),
        compiler_params=pltpu.CompilerParams(
            dimension_semantics=("parallel",), vmem_limit_bytes=VMEM_LIMIT),
        name="gla",
    )(gq, gk, gv, gr, la, norm_g)


ATT_BLOCK = 256


def _diff_body(q_ref, k_ref, v_ref, lq1_ref, lk1_ref, lq2_ref, lk2_ref, g_ref, o_ref,
               dmask_ref):
    h = pl.program_id(1)
    slope = jnp.where(h == 0, 2.0 ** -2, jnp.where(h == 1, 2.0 ** -4, jnp.where(h == 2, 2.0 ** -6, 2.0 ** -8)))
    slope = slope.astype(F32) * LOG2E
    lam = (jnp.exp(jnp.sum(lq1_ref[...] * lk1_ref[...], axis=-1, keepdims=True))
           - jnp.exp(jnp.sum(lq2_ref[...] * lk2_ref[...], axis=-1, keepdims=True)) + LAMBDA_INIT)
    gain = g_ref[...] * (1.0 - LAMBDA_INIT)
    n_qblocks = (q_ref.shape[1] - ROW_BLOCK) // ATT_BLOCK
    nt = (((1,), (1,)), ((), ()))

    def split_q(q):
        lane = lax.broadcasted_iota(jnp.int32, q.shape, 1)
        zero = jnp.zeros_like(q)
        return jnp.where(lane < DIFF_DH, q, zero), jnp.where(lane >= DIFF_DH, q, zero)

    def with_ones(v):
        return jnp.concatenate([v, jnp.ones_like(v)], axis=1)

    def softmax_av(qz, kk, vext, add_bias):
        s = add_bias(lax.dot_general(qz, kk, nt, preferred_element_type=F32))
        m = jnp.max(s, axis=-1, keepdims=True)
        p = jnp.exp2(s - m).astype(BF16)
        return jnp.dot(p, vext, preferred_element_type=F32)

    def finish(a1, a2):
        o = a1[:, :DIFF_DV] / a1[:, DIFF_DV:] - lam * (a2[:, :DIFF_DV] / a2[:, DIFF_DV:])
        ms = jnp.mean(o * o, axis=-1, keepdims=True)
        return (o * lax.rsqrt(ms + LN_EPS) * gain).astype(BF16)

    r = lax.broadcasted_iota(jnp.int32, (ROW_BLOCK, ROW_BLOCK), 0)
    c = lax.broadcasted_iota(jnp.int32, (ROW_BLOCK, ROW_BLOCK), 1)
    ok = jnp.logical_and(c // CHUNK <= r // CHUNK, c >= FRONT_PAD)
    bias_lead = jnp.where(ok, -slope * jnp.abs(r - c).astype(F32), NEG_INF)
    q1z, q2z = split_q(q_ref[0, 0:ROW_BLOCK, :])
    k_lead = k_ref[0, 0:ROW_BLOCK, :]
    v_lead = with_ones(v_ref[0, 0:ROW_BLOCK, :])
    o_ref[0, 0:ROW_BLOCK, :] = finish(softmax_av(q1z, k_lead, v_lead, lambda s: s + bias_lead),
                                      softmax_av(q2z, k_lead, v_lead, lambda s: s + bias_lead))

    r = lax.broadcasted_iota(jnp.int32, (ATT_BLOCK, ATT_BLOCK), 0)
    c = lax.broadcasted_iota(jnp.int32, (ATT_BLOCK, ATT_BLOCK), 1)
    rel = jnp.where(c <= r, c, 2 * r - c).astype(F32)
    dmask_ref[...] = jnp.where(c // CHUNK <= r // CHUNK, slope * rel, NEG_INF)

    for jq in range(n_qblocks):
        qbase = ROW_BLOCK + jq * ATT_BLOCK
        n_keys = qbase + ATT_BLOCK
        q1z, q2z = split_q(q_ref[0, qbase:qbase + ATT_BLOCK, :])
        kk = k_ref[0, 0:n_keys, :]
        vext = with_ones(v_ref[0, 0:n_keys, :])
        col = lax.broadcasted_iota(jnp.int32, (1, qbase), 1)
        col_bias = jnp.where(col >= FRONT_PAD, slope * (col - qbase).astype(F32), NEG_INF)

        def add_bias(s, col_bias=col_bias, qbase=qbase):
            return jnp.concatenate([s[:, :qbase] + col_bias, s[:, qbase:] + dmask_ref[...]], axis=1)

        o_ref[0, qbase:qbase + ATT_BLOCK, :] = finish(softmax_av(q1z, kk, vext, add_bias),
                                                      softmax_av(q2z, kk, vext, add_bias))


def _diff_attention(dq, dk, dv, lq1, lk1, lq2, lk2, norm_g):
    B, LP, W = dq.shape
    small = pl.BlockSpec((1, DIFF_DH), lambda b, h: (0, 0))
    seq = pl.BlockSpec((1, LP, 2 * DIFF_DH), lambda b, h: (b, 0, h))
    return pl.pallas_call(
        _diff_body,
        grid=(B, DIFF_HEADS),
        in_specs=[seq, seq, seq, small, small, small, small,
                  pl.BlockSpec((1, DIFF_DV), lambda b, h: (0, 0))],
        out_specs=seq,
        out_shape=jax.ShapeDtypeStruct((B, LP, W), BF16),
        scratch_shapes=[pltpu.VMEM((ATT_BLOCK, ATT_BLOCK), F32)],
        compiler_params=pltpu.CompilerParams(
            dimension_semantics=("parallel", "parallel"), vmem_limit_bytes=VMEM_LIMIT),
        name="diff_attn",
    )(dq, dk, dv, lq1, lk1, lq2, lk2, norm_g)


def _route_body(og_ref, od_ref, h0_ref, wog_ref, wod_ref, g_ref, b_ref, rw_ref, rb_ref,
                h1_ref, route_ref, keys_ref, cnt_ref, carry_ref, *, rows_per_seq, id_bits):
    t = pl.program_id(0)

    @pl.when(t == 0)
    def _():
        carry_ref[...] = jnp.zeros_like(carry_ref)

    mix = (jnp.dot(og_ref[...], wog_ref[...], preferred_element_type=F32)
           + jnp.dot(od_ref[...], wod_ref[...], preferred_element_type=F32))
    h1 = _layer_norm(DEEPNORM_ALPHA * h0_ref[...] + mix, g_ref[...], b_ref[...])
    _store_token_tiled(h1_ref, h1)

    hh, hm, _ = _split3_bf16(h1)
    wh, wm, _ = _split3_bf16(rw_ref[...])
    head = jnp.dot(hh, jnp.concatenate([wh, wm], axis=1), preferred_element_type=F32)
    logits = rb_ref[...] + head[:, :LANES] + head[:, LANES:] + jnp.dot(hm, wh, preferred_element_type=F32)

    T = logits.shape[0]
    work = logits.T[:N_EXPERTS]
    expert = lax.broadcasted_iota(jnp.int32, (N_EXPERTS, T), 0)
    tok = t * T + lax.broadcasted_iota(jnp.int32, (1, T), 1)
    valid = (tok % rows_per_seq) >= FRONT_PAD
    top_v, top_i = [], []
    for _ in range(TOP_K):
        mx = jnp.max(work, axis=0, keepdims=True)
        idx = jnp.min(jnp.where(work == mx, expert, N_EXPERTS), axis=0, keepdims=True)
        top_v.append(mx)
        top_i.append(idx)
        work = jnp.where(expert == idx, -jnp.inf, work)
    ex = [jnp.exp(v - top_v[0]) for v in top_v]
    den = ex[0] + ex[1] + ex[2] + ex[3]
    gates = [lax.bitcast_convert_type(e / den, jnp.int32) for e in ex]
    keys = [jnp.where(valid, idx, N_EXPERTS) * (1 << id_bits) + (tok * TOP_K + k) for k, idx in enumerate(top_i)]

    onehot = jnp.zeros((N_EXPERTS, T), F32)
    for idx in top_i:
        onehot = onehot + (expert == idx).astype(F32)
    carry_ref[...] = carry_ref[...] + jnp.sum(jnp.where(valid, onehot, 0.0), axis=1, keepdims=True)
    cnt_ref[...] = carry_ref[...]

    by_lane = jnp.concatenate(top_i + keys + gates + [jnp.zeros((LANES - 3 * TOP_K, T), jnp.int32)], axis=0)
    route_ref[...] = by_lane.T
    keys_ref[...] = jnp.concatenate([key[:, q * LANES:(q + 1) * LANES] for key in keys for q in range(T // LANES)], axis=0)


def _outproj_route(og, od, h0, wo_g, wo_d, ln_g, ln_b, rw, rb, rows_per_seq):
    NP, D = h0.shape
    T = ROUTE_TILE

    def rows(width):
        return pl.BlockSpec((T, width), lambda t: (t, 0))

    def full(shape):
        return pl.BlockSpec(shape, lambda t: (0,) * len(shape))

    key_rows = T * TOP_K // LANES
    return pl.pallas_call(
        functools.partial(_route_body, rows_per_seq=rows_per_seq, id_bits=(NP * TOP_K - 1).bit_length()),
        grid=(NP // T,),
        in_specs=[rows(512), rows(512), rows(D), full((512, D)), full((512, D)),
                  full((1, D)), full((1, D)), full((D, LANES)), full((1, LANES))],
        out_specs=[pl.BlockSpec((T * TOKEN_SUBROWS, LANES), lambda t: (t, 0)), rows(LANES),
                   pl.BlockSpec((key_rows, LANES), lambda t: (t, 0)), full((N_EXPERTS, 1))],
        out_shape=[jax.ShapeDtypeStruct((NP * TOKEN_SUBROWS, LANES), F32),
                   jax.ShapeDtypeStruct((NP, LANES), jnp.int32),
                   jax.ShapeDtypeStruct((NP * TOP_K // LANES, LANES), jnp.int32),
                   jax.ShapeDtypeStruct((N_EXPERTS, 1), F32)],
        scratch_shapes=[pltpu.VMEM((N_EXPERTS, 1), F32)],
        compiler_params=pltpu.CompilerParams(
            dimension_semantics=("arbitrary",), vmem_limit_bytes=VMEM_LIMIT),
        name="outproj_route",
    )(og, od, h0, wo_g, wo_d, ln_g, ln_b, rw, rb)


def _expert_body(order_ref, p0_ref, bstart_ref, nblk_ref, ntot_ref,
                 h1t_hbm, wgu_ref, bg_ref, bu_ref, wd_ref, bd_ref, yk_hbm,
                 wg_s, wu_s, wd_s, xb0, xb1, xb2, yb0, yb1, yb2, zbuf, gsem, ssem, zsem,
                 *, n_tokens, rows_per_seq):
    e = pl.program_id(0)
    n_total = ntot_ref[0]
    R = MOE_ROWS
    TS = TOKEN_SUBROWS
    xb = (xb0, xb1, xb2)
    yb = (yb0, yb1, yb2)
    NB = len(xb)

    def gather_start(b, slot):
        p0 = p0_ref[b]
        for r in range(R):
            tok = order_ref[p0 + r] >> TOP_K_SHIFT
            pltpu.make_async_copy(
                h1t_hbm.at[pl.ds(pl.multiple_of(tok * TS, TS), TS), :],
                xb[slot].at[pl.ds(r * TS, TS), :], gsem.at[slot]).start(priority=r % 2)

    def gather_wait(slot):
        pltpu.make_async_copy(h1t_hbm.at[pl.ds(0, R * TS), :], xb[slot], gsem.at[slot]).wait()

    def scatter_start(b, slot):
        p0 = p0_ref[b]
        for r in range(R):
            row = order_ref[p0 + r] * TS
            pltpu.make_async_copy(
                yb[slot].at[pl.ds(r * TS, TS), :],
                yk_hbm.at[pl.ds(pl.multiple_of(row, TS), TS), :], ssem.at[slot]).start(priority=r % 2)

    def scatter_wait(slot):
        pltpu.make_async_copy(yb[slot], yk_hbm.at[pl.ds(0, R * TS), :], ssem.at[slot]).wait()

    @pl.when(e == 0)
    def _():
        zbuf[...] = jnp.zeros_like(zbuf)
        lead = FRONT_PAD * TS
        fills = [((s * rows_per_seq * TOP_K + k * FRONT_PAD) * TS, lead)
                 for s in range(n_tokens // rows_per_seq) for k in range(TOP_K)]
        copies = [pltpu.make_async_copy(zbuf.at[pl.ds(0, n), :], yk_hbm.at[pl.ds(o, n), :], zsem)
                  for o, n in fills]
        for cp in copies:
            cp.start()
        for cp in copies:
            cp.wait()
        gather_start(0, 0)
        gather_start(jnp.minimum(1, n_total - 1), 1)

    @pl.when(nblk_ref[e] > 0)
    def _():
        W, H = MXU_TILE, MXU_TILE // 2
        r = lax.broadcasted_iota(jnp.int32, (W, W), 0)
        c = lax.broadcasted_iota(jnp.int32, (W, W), 1)
        perm = (r == jnp.where(c < H, 2 * c, 2 * (c - H) + 1)).astype(BF16)
        for tt in range(2 * D_FF // W):
            wt = wgu_ref[0, :, tt * W:(tt + 1) * W].astype(BF16)
            sp = jnp.dot(wt, perm, preferred_element_type=F32)
            wg_s[:, tt * H:(tt + 1) * H] = sp[:, :H].astype(BF16)
            wu_s[:, tt * H:(tt + 1) * H] = sp[:, H:].astype(BF16)
        wd_s[...] = wd_ref[0].astype(BF16)

    def run_block(b, slot):
        prv = (slot + 2) % NB
        gather_wait(slot)
        gather_start(jnp.minimum(b + 2, n_total - 1), prv)
        x = _load_token_tiled(xb[slot], R).astype(BF16)
        gt = jnp.dot(x, wg_s[...], preferred_element_type=F32) + bg_ref[0]
        up = jnp.dot(x, wu_s[...], preferred_element_type=F32) + bu_ref[0]
        gt = jnp.minimum(gt, SWIGLU_LIMIT)
        up = jnp.clip(up, -SWIGLU_LIMIT, SWIGLU_LIMIT)
        act = (up + 1.0) * (gt * jax.nn.sigmoid(SWIGLU_ALPHA * gt))
        y = jnp.dot(act.astype(BF16), wd_s[...], preferred_element_type=F32) + bd_ref[0]
        _store_token_tiled(yb[slot], y)

        @pl.when(b >= 1)
        def _():
            scatter_wait(prv)
        scatter_start(b, slot)

    def block(b, carry):
        for slot in range(NB):
            pl.when(b % NB == slot)(functools.partial(run_block, b, slot))
        return carry

    b0 = bstart_ref[e]
    lax.fori_loop(b0, b0 + nblk_ref[e], block, 0)

    @pl.when(e == pl.num_programs(0) - 1)
    def _():
        last = n_total - 1
        for slot in range(NB):
            @pl.when(last % NB == slot)
            def _(slot=slot):
                gather_wait((slot + 1) % NB)
                gather_wait((slot + 2) % NB)
                scatter_wait(slot)


def _experts(order, blk_p0, blk_start, n_blk, n_total, h1t, w_gate_up, b_gate, b_up, w_down,
             b_down, n_tokens, rows_per_seq):
    D = D_MODEL
    R = MOE_ROWS
    buf = pltpu.VMEM((R * TOKEN_SUBROWS, LANES), F32)
    per_expert = lambda e, *_: (e, 0, 0)
    grid_spec = pltpu.PrefetchScalarGridSpec(
        num_scalar_prefetch=5,
        grid=(N_EXPERTS,),
        in_specs=[
            pl.BlockSpec(memory_space=pl.ANY),
            pl.BlockSpec((1, D, 2 * D_FF), per_expert),
            pl.BlockSpec((1, 1, D_FF), per_expert),
            pl.BlockSpec((1, 1, D_FF), per_expert),
            pl.BlockSpec((1, D_FF, D), per_expert),
            pl.BlockSpec((1, 1, D), per_expert),
        ],
        out_specs=pl.BlockSpec(memory_space=pl.ANY),
        scratch_shapes=[pltpu.VMEM((D, D_FF), BF16), pltpu.VMEM((D, D_FF), BF16), pltpu.VMEM((D_FF, D), BF16),
                        buf, buf, buf, buf, buf, buf, pltpu.VMEM((FRONT_PAD * TOKEN_SUBROWS, LANES), F32),
                        pltpu.SemaphoreType.DMA((3,)), pltpu.SemaphoreType.DMA((3,)), pltpu.SemaphoreType.DMA],
    )
    out_rows = TOP_K * n_tokens * TOKEN_SUBROWS
    return pl.pallas_call(
        functools.partial(_expert_body, n_tokens=n_tokens, rows_per_seq=rows_per_seq),
        grid_spec=grid_spec,
        out_shape=jax.ShapeDtypeStruct((out_rows, LANES), F32),
        compiler_params=pltpu.CompilerParams(
            dimension_semantics=("arbitrary",), vmem_limit_bytes=EXPERT_VMEM_LIMIT),
        name="experts",
    )(order, blk_p0, blk_start, n_blk, n_total, h1t, w_gate_up, b_gate, b_up, w_down, b_down)


COMBINE_BLOCKS = 4


def _combine_body(*refs):
    per_block = 2 + TOP_K
    g_ref, b_ref, o_ref = refs[COMBINE_BLOCKS * per_block:]
    for i in range(COMBINE_BLOCKS):
        h1_ref, *y_refs, route_ref = refs[i * per_block:(i + 1) * per_block]
        route = route_ref[...]
        y = jnp.zeros((ROW_BLOCK, D_MODEL), F32)
        for k, yk_ref in enumerate(y_refs):
            gate = lax.bitcast_convert_type(route[:, 2 * TOP_K + k:2 * TOP_K + k + 1], F32)
            y = y + gate * _load_token_tiled(yk_ref.reshape(ROW_BLOCK * TOKEN_SUBROWS, LANES), ROW_BLOCK)
        h1 = _load_token_tiled(h1_ref, ROW_BLOCK)
        o_ref[0, i * ROW_BLOCK:(i + 1) * ROW_BLOCK, :] = _layer_norm(
            DEEPNORM_ALPHA * h1 + y, g_ref[...], b_ref[...])


def _combine(h1t, yk, route, ln_g, ln_b, batch, seq_len):
    nb = seq_len // ROW_BLOCK
    blocks_per_seq = nb + 1
    tile_rows = ROW_BLOCK * TOKEN_SUBROWS
    yk = yk.reshape(-1, TOP_K, TOKEN_SUBROWS, LANES)

    def token_block(i):
        return lambda b, j: b * blocks_per_seq + COMBINE_BLOCKS * j + i + 1

    in_specs, operands = [], []
    for i in range(COMBINE_BLOCKS):
        tb = token_block(i)
        in_specs.append(pl.BlockSpec((tile_rows, LANES), lambda b, j, tb=tb: (tb(b, j), 0)))
        in_specs += [pl.BlockSpec((ROW_BLOCK, None, TOKEN_SUBROWS, LANES), lambda b, j, tb=tb, k=k: (tb(b, j), k, 0, 0))
                     for k in range(TOP_K)]
        in_specs.append(pl.BlockSpec((ROW_BLOCK, LANES), lambda b, j, tb=tb: (tb(b, j), 0)))
        operands += [h1t] + [yk] * TOP_K + [route]
    vec = pl.BlockSpec((1, D_MODEL), lambda b, j: (0, 0))
    return pl.pallas_call(
        _combine_body,
        grid=(batch, nb // COMBINE_BLOCKS),
        in_specs=in_specs + [vec, vec],
        out_specs=pl.BlockSpec((1, COMBINE_BLOCKS * ROW_BLOCK, D_MODEL), lambda b, j: (b, j, 0)),
        out_shape=jax.ShapeDtypeStruct((batch, seq_len, D_MODEL), F32),
        compiler_params=pltpu.CompilerParams(
            dimension_semantics=("parallel", "parallel"), vmem_limit_bytes=VMEM_LIMIT),
        name="combine",
    )(*operands, ln_g, ln_b)


def kernel(x, meta_tokens, ln_emb_g, ln_emb_b, w_in, gla_wa2, gla_ba, gla_norm_g, diff_lambda_q1, diff_lambda_k1, diff_lambda_q2, diff_lambda_k2, diff_norm_g, w_out, ln1_g, ln1_b, router_w, router_b, w_gate_up, b_gate_up, w_down, b_down, ln2_g, ln2_b):
    B, S, D = x.shape
    LP = S + ROW_BLOCK
    NP = B * LP
    row = lambda v: v.reshape(1, -1)

    w = w_in[0]
    w_main = jnp.concatenate([w[:, :1536], w[:, 1552:]], axis=1).astype(BF16)
    w_ga = w[:, 1536:1552].astype(BF16)
    meta_pad = jnp.pad(meta_tokens, ((FRONT_PAD, 0), (0, 0)))

    h0, gq, gk, gv, gr, la, dq, dk, dv = _inproj(
        x, meta_pad, row(ln_emb_g), row(ln_emb_b), w_main, w_ga,
        gla_wa2[0].astype(BF16), row(gla_ba[0]))

    og = _gla(gq, gk, gv, gr, la, row(gla_norm_g[0]))
    od = _diff_attention(dq, dk, dv, row(diff_lambda_q1[0]), row(diff_lambda_k1[0]),
                         row(diff_lambda_q2[0]), row(diff_lambda_k2[0]), row(diff_norm_g[0]))

    wo = w_out[0].astype(BF16)
    rw_pad = jnp.pad(router_w[0], ((0, 0), (0, LANES - N_EXPERTS)))
    rb_pad = jnp.pad(row(router_b[0]), ((0, 0), (0, LANES - N_EXPERTS)))
    h1t, route, keys, counts = _outproj_route(
        og.reshape(NP, 512), od.reshape(NP, 512), h0.reshape(NP, D), wo[:512], wo[512:],
        row(ln1_g[0]), row(ln1_b[0]), rw_pad, rb_pad, LP)

    R = MOE_ROWS
    i32 = jnp.int32
    counts = counts[:, 0].astype(i32)
    n_blk = (counts + R - 1) // R
    blk_end = jnp.cumsum(n_blk)
    blk_start = blk_end - n_blk
    grp_start = jnp.cumsum(counts) - counts
    n_total = blk_end[-1:]
    n_assign_max = B * (S + N_META) * TOP_K
    max_blocks = (n_assign_max + N_EXPERTS * (R - 1)) // R
    g = jnp.minimum(jnp.arange(max_blocks, dtype=i32), n_total[0] - 1)
    is_e = (jnp.minimum(jnp.sum(g[:, None] >= blk_end[None, :], axis=1), N_EXPERTS - 1)[:, None]
            == jnp.arange(N_EXPERTS)[None, :])
    pick = lambda v: jnp.sum(jnp.where(is_e, v[None, :], 0), axis=1)
    local = (g - pick(blk_start)) * R
    blk_p0 = (pick(grp_start) + local).astype(i32)

    assert R <= B * FRONT_PAD * TOP_K
    order = jnp.sort(keys.reshape(-1)) & ((1 << (NP * TOP_K - 1).bit_length()) - 1)

    bgu = b_gate_up[0].reshape(N_EXPERTS, D_FF, 2)
    yk = _experts(order, blk_p0, blk_start.astype(i32), n_blk.astype(i32), n_total.astype(i32),
                  h1t, w_gate_up[0],
                  bgu[:, :, 0].reshape(N_EXPERTS, 1, D_FF), bgu[:, :, 1].reshape(N_EXPERTS, 1, D_FF),
                  w_down[0], b_down[0].reshape(N_EXPERTS, 1, D), NP, LP)

    return _combine(h1t, yk, route, row(ln2_g[0]), row(ln2_b[0]), B, S)
```

```python
import functools
import math

import jax
import jax.numpy as jnp
from jax import lax
from jax.experimental import pallas as pl
from jax.experimental.pallas import tpu as pltpu

F32 = jnp.float32
BF16 = jnp.bfloat16

D_MODEL = 1024
LANES = 128
N_META = 16
ROW_BLOCK = 128
FRONT_PAD = ROW_BLOCK - N_META
CHUNK = 64

GLA_HEADS = 4
GLA_DK = 64
GLA_DV = 128
GLA_RANK = 16
GLA_TAU = 16.0
DIFF_HEADS = 4
DIFF_DH = 64
DIFF_DV = 128

N_EXPERTS = 32
TOP_K = 4
TOP_K_SHIFT = TOP_K.bit_length() - 1
D_FF = 1024
SWIGLU_LIMIT = 7.0
SWIGLU_ALPHA = 1.702
MOE_ROWS = 256

DEEPNORM_ALPHA = 2.0 ** 0.25
LAMBDA_INIT = 0.8 - 0.6 * math.exp(0.0)
LN_EPS = 1e-5
NEG_INF = -1e30
LOG2E = math.log2(math.e)

MXU_TILE = 256
ROUTE_TILE = 1024
VMEM_LIMIT = 48 * 1024 * 1024
EXPERT_VMEM_LIMIT = 56 * 1024 * 1024

_GQ, _GK, _GV, _GR, _DQ, _DK, _DV = 0, 256, 512, 1024, 1536, 2048, 2560
_MAIN_WIDTH = 3072


def _layer_norm(x, g, b):
    mu = jnp.mean(x, axis=-1, keepdims=True)
    xc = x - mu
    var = jnp.mean(xc * xc, axis=-1, keepdims=True)
    return xc * lax.rsqrt(var + LN_EPS) * g + b


def _split3_bf16(x):
    hi = x.astype(BF16)
    r1 = x - hi.astype(F32)
    mid = r1.astype(BF16)
    lo = (r1 - mid.astype(F32)).astype(BF16)
    return hi, mid, lo


TOKEN_SUBROWS = D_MODEL // LANES


def _store_token_tiled(ref, val, first_token=0):
    n = val.shape[0]
    for s in range(TOKEN_SUBROWS):
        ref[pl.ds(first_token * TOKEN_SUBROWS + s, n, stride=TOKEN_SUBROWS), :] = val[:, s * LANES:(s + 1) * LANES]


def _load_token_tiled(ref, n, first_row=0):
    return jnp.concatenate(
        [ref[pl.ds(first_row + s, n, stride=TOKEN_SUBROWS), :] for s in range(TOKEN_SUBROWS)], axis=1)


def _inproj_body(x0_ref, xa_ref, xb_ref, meta_ref, g_ref, b_ref, w_ref, wga_ref, wa2_ref, ba_ref,
                 h0_ref, gq_ref, gk_ref, gv_ref, gr_ref, la_ref, dq_ref, dk_ref, dv_ref, hs_ref):
    j = pl.program_id(1)
    rows = lax.broadcasted_iota(jnp.int32, (2 * ROW_BLOCK, 1), 0)
    valid = jnp.logical_or(j > 0, rows >= FRONT_PAD)

    @pl.when(j == 0)
    def _():
        lead = jnp.concatenate([meta_ref[...], x0_ref[0]], axis=0)
        hs_ref[0] = jnp.where(rows >= FRONT_PAD, _layer_norm(lead, g_ref[...], b_ref[...]), 0.0)

    h = hs_ref[j % 2]
    ahead = jnp.concatenate([xa_ref[0], xb_ref[0]], axis=0)
    hs_ref[(j + 1) % 2] = _layer_norm(ahead, g_ref[...], b_ref[...])
    h0_ref[0] = h
    hb = h.astype(BF16)

    def proj(off, width):
        return jnp.dot(hb, w_ref[:, off:off + width], preferred_element_type=F32)

    gq_ref[0] = (proj(_GQ, 256) * GLA_DK ** -0.5).astype(BF16)
    gk_ref[0] = proj(_GK, 256)
    gv_ref[0] = proj(_GV, 512).astype(BF16)
    gr_ref[0] = proj(_GR, 512).astype(BF16)
    dq_ref[0] = (proj(_DQ, 512) * (DIFF_DH ** -0.5 * LOG2E)).astype(BF16)
    dk_ref[0] = proj(_DK, 512).astype(BF16)
    dv_ref[0] = proj(_DV, 512).astype(BF16)

    a_lr = jnp.dot(hb, wga_ref[...], preferred_element_type=F32)
    z = jnp.dot(a_lr.astype(BF16), wa2_ref[...], preferred_element_type=F32) + ba_ref[...]
    log_sig = jnp.minimum(z, 0.0) - jnp.log1p(jnp.exp(-jnp.abs(z)))
    la_ref[0] = jnp.where(valid, log_sig / GLA_TAU, 0.0)


def _inproj(x, meta_pad, ln_g, ln_b, w_main, w_ga, wa2, ba):
    B, S, D = x.shape
    nb = S // ROW_BLOCK + 1
    LP = nb * ROW_BLOCK

    n_frame_blocks = S // ROW_BLOCK
    n_steps = pl.cdiv(LP, 2 * ROW_BLOCK)

    def row_spec(width):
        return pl.BlockSpec((1, 2 * ROW_BLOCK, width), lambda b, j: (b, j, 0))

    def frames(shift):
        return pl.BlockSpec((1, ROW_BLOCK, D),
                            lambda b, j: (b, jnp.clip(2 * j + shift, 0, n_frame_blocks - 1), 0))

    def full(shape):
        return pl.BlockSpec(shape, lambda b, j: (0,) * len(shape))

    out_shapes = [
        jax.ShapeDtypeStruct((B, LP, D), F32),
        jax.ShapeDtypeStruct((B, LP, 256), BF16),
        jax.ShapeDtypeStruct((B, LP, 256), F32),
        jax.ShapeDtypeStruct((B, LP, 512), BF16),
        jax.ShapeDtypeStruct((B, LP, 512), BF16),
        jax.ShapeDtypeStruct((B, LP, 256), F32),
        jax.ShapeDtypeStruct((B, LP, 512), BF16),
        jax.ShapeDtypeStruct((B, LP, 512), BF16),
        jax.ShapeDtypeStruct((B, LP, 512), BF16),
    ]
    return pl.pallas_call(
        _inproj_body,
        grid=(B, n_steps),
        in_specs=[
            pl.BlockSpec((1, ROW_BLOCK, D), lambda b, j: (b, 0, 0)),
            frames(1), frames(2),
            full((ROW_BLOCK, D)), full((1, D)), full((1, D)),
            full((D, _MAIN_WIDTH)), full((D, GLA_RANK)), full((GLA_RANK, 256)), full((1, 256)),
        ],
        out_specs=[row_spec(s.shape[-1]) for s in out_shapes],
        out_shape=out_shapes,
        scratch_shapes=[pltpu.VMEM((2, 2 * ROW_BLOCK, D), F32)],
        compiler_params=pltpu.CompilerParams(
            dimension_semantics=("parallel", "arbitrary"), vmem_limit_bytes=VMEM_LIMIT),
        name="inproj",
    )(x, x, x, meta_pad, ln_g, ln_b, w_main, w_ga, wa2, ba)


def _gla_body(q_ref, k_ref, v_ref, r_ref, la_ref, g_ref, o_ref):
    n_groups = q_ref.shape[1] // ROW_BLOCK
    ri = lax.broadcasted_iota(jnp.int32, (ROW_BLOCK, ROW_BLOCK), 0)
    ci = lax.broadcasted_iota(jnp.int32, (ROW_BLOCK, ROW_BLOCK), 1)
    later = jnp.logical_and(ri // CHUNK == ci // CHUNK, ci > ri).astype(BF16)
    sr = lax.broadcasted_iota(jnp.int32, (GLA_HEADS * GLA_DV, GLA_HEADS * GLA_DK), 0)
    sc = lax.broadcasted_iota(jnp.int32, (GLA_HEADS * GLA_DV, GLA_HEADS * GLA_DK), 1)
    same_head = sr // GLA_DV == sc // GLA_DK
    gain = g_ref[...]
    st = jnp.zeros((GLA_HEADS * GLA_DV, GLA_HEADS * GLA_DK), F32)

    for grp in range(n_groups):
        g0 = grp * ROW_BLOCK
        la = la_ref[0, g0:g0 + ROW_BLOCK, :]
        hi, mid, lo = _split3_bf16(la)
        suffix = (jnp.dot(later, hi, preferred_element_type=F32)
                  + jnp.dot(later, mid, preferred_element_type=F32)
                  + jnp.dot(later, lo, preferred_element_type=F32))
        kdec = (k_ref[0, g0:g0 + ROW_BLOCK, :] * jnp.exp(suffix)).astype(BF16)
        for half in range(ROW_BLOCK // CHUNK):
            f = half * CHUNK
            r0 = g0 + f
            tot = suffix[f:f + 1, :] + la[f:f + 1, :]
            upd = lax.dot_general(v_ref[0, r0:r0 + CHUNK, :], kdec[f:f + CHUNK, :],
                                  (((0,), (0,)), ((), ())), preferred_element_type=F32)
            st = st * jnp.exp(tot) + jnp.where(same_head, upd, 0.0)
            o = lax.dot_general(q_ref[0, r0:r0 + CHUNK, :], st.astype(BF16),
                                (((1,), (1,)), ((), ())), preferred_element_type=F32)
            r = r_ref[0, r0:r0 + CHUNK, :].astype(F32)
            for h in range(GLA_HEADS):
                oh = o[:, h * GLA_DV:(h + 1) * GLA_DV]
                rh = r[:, h * GLA_DV:(h + 1) * GLA_DV]
                ms = jnp.mean(oh * oh, axis=-1, keepdims=True)
                out = oh * lax.rsqrt(ms + LN_EPS) * gain * (rh * jax.nn.sigmoid(rh))
                o_ref[0, r0:r0 + CHUNK, h * GLA_DV:(h + 1) * GLA_DV] = out.astype(BF16)


def _gla(gq, gk, gv, gr, la, norm_g):
    B, LP, _ = gq.shape

    def seq(width):
        return pl.BlockSpec((1, LP, width), lambda b: (b, 0, 0))

    return pl.pallas_call(
        _gla_body,
        grid=(B,),
        in_specs=[seq(256), seq(256), seq(512), seq(512), seq(256),
                  pl.BlockSpec((1, GLA_DV), lambda b: (0, 0))],
        out_specs=seq(512),
        out_shape=jax.ShapeDtypeStruct((B, LP, 512), BF16),
        compiler_params=pltpu.CompilerParams(
            dimension_semantics=("parallel",), vmem_limit_bytes=VMEM_LIMIT),
        name="gla",
    )(gq, gk, gv, gr, la, norm_g)


ATT_BLOCK = 256


def _diff_body(q_ref, k_ref, v_ref, lq1_ref, lk1_ref, lq2_ref, lk2_ref, g_ref, o_ref,
               dmask_ref):
    h = pl.program_id(1)
    slope = jnp.where(h == 0, 2.0 ** -2, jnp.where(h == 1, 2.0 ** -4, jnp.where(h == 2, 2.0 ** -6, 2.0 ** -8)))
    slope = slope.astype(F32) * LOG2E
    lam = (jnp.exp(jnp.sum(lq1_ref[...] * lk1_ref[...], axis=-1, keepdims=True))
           - jnp.exp(jnp.sum(lq2_ref[...] * lk2_ref[...], axis=-1, keepdims=True)) + LAMBDA_INIT)
    gain = g_ref[...] * (1.0 - LAMBDA_INIT)
    n_qblocks = (q_ref.shape[1] - ROW_BLOCK) // ATT_BLOCK
    nt = (((1,), (1,)), ((), ()))

    def split_q(q):
        lane = lax.broadcasted_iota(jnp.int32, q.shape, 1)
        zero = jnp.zeros_like(q)
        return jnp.where(lane < DIFF_DH, q, zero), jnp.where(lane >= DIFF_DH, q, zero)

    def with_ones(v):
        return jnp.concatenate([v, jnp.ones_like(v)], axis=1)

    def softmax_av(qz, kk, vext, add_bias):
        s = add_bias(lax.dot_general(qz, kk, nt, preferred_element_type=F32))
        m = jnp.max(s, axis=-1, keepdims=True)
        p = jnp.exp2(s - m).astype(BF16)
        return jnp.dot(p, vext, preferred_element_type=F32)

    def finish(a1, a2):
        o = a1[:, :DIFF_DV] / a1[:, DIFF_DV:] - lam * (a2[:, :DIFF_DV] / a2[:, DIFF_DV:])
        ms = jnp.mean(o * o, axis=-1, keepdims=True)
        return (o * lax.rsqrt(ms + LN_EPS) * gain).astype(BF16)

    r = lax.broadcasted_iota(jnp.int32, (ROW_BLOCK, ROW_BLOCK), 0)
    c = lax.broadcasted_iota(jnp.int32, (ROW_BLOCK, ROW_BLOCK), 1)
    ok = jnp.logical_and(c // CHUNK <= r // CHUNK, c >= FRONT_PAD)
    bias_lead = jnp.where(ok, -slope * jnp.abs(r - c).astype(F32), NEG_INF)
    q1z, q2z = split_q(q_ref[0, 0:ROW_BLOCK, :])
    k_lead = k_ref[0, 0:ROW_BLOCK, :]
    v_lead = with_ones(v_ref[0, 0:ROW_BLOCK, :])
    o_ref[0, 0:ROW_BLOCK, :] = finish(softmax_av(q1z, k_lead, v_lead, lambda s: s + bias_lead),
                                      softmax_av(q2z, k_lead, v_lead, lambda s: s + bias_lead))

    r = lax.broadcasted_iota(jnp.int32, (ATT_BLOCK, ATT_BLOCK), 0)
    c = lax.broadcasted_iota(jnp.int32, (ATT_BLOCK, ATT_BLOCK), 1)
    rel = jnp.where(c <= r, c, 2 * r - c).astype(F32)
    dmask_ref[...] = jnp.where(c // CHUNK <= r // CHUNK, slope * rel, NEG_INF)

    for jq in range(n_qblocks):
        qbase = ROW_BLOCK + jq * ATT_BLOCK
        n_keys = qbase + ATT_BLOCK
        q1z, q2z = split_q(q_ref[0, qbase:qbase + ATT_BLOCK, :])
        kk = k_ref[0, 0:n_keys, :]
        vext = with_ones(v_ref[0, 0:n_keys, :])
        col = lax.broadcasted_iota(jnp.int32, (1, qbase), 1)
        col_bias = jnp.where(col >= FRONT_PAD, slope * (col - qbase).astype(F32), NEG_INF)

        def add_bias(s, col_bias=col_bias, qbase=qbase):
            return jnp.concatenate([s[:, :qbase] + col_bias, s[:, qbase:] + dmask_ref[...]], axis=1)

        o_ref[0, qbase:qbase + ATT_BLOCK, :] = finish(softmax_av(q1z, kk, vext, add_bias),
                                                      softmax_av(q2z, kk, vext, add_bias))


def _diff_attention(dq, dk, dv, lq1, lk1, lq2, lk2, norm_g):
    B, LP, W = dq.shape
    small = pl.BlockSpec((1, DIFF_DH), lambda b, h: (0, 0))
    seq = pl.BlockSpec((1, LP, 2 * DIFF_DH), lambda b, h: (b, 0, h))
    return pl.pallas_call(
        _diff_body,
        grid=(B, DIFF_HEADS),
        in_specs=[seq, seq, seq, small, small, small, small,
                  pl.BlockSpec((1, DIFF_DV), lambda b, h: (0, 0))],
        out_specs=seq,
        out_shape=jax.ShapeDtypeStruct((B, LP, W), BF16),
        scratch_shapes=[pltpu.VMEM((ATT_BLOCK, ATT_BLOCK), F32)],
        compiler_params=pltpu.CompilerParams(
            dimension_semantics=("parallel", "parallel"), vmem_limit_bytes=VMEM_LIMIT),
        name="diff_attn",
    )(dq, dk, dv, lq1, lk1, lq2, lk2, norm_g)


def _route_body(og_ref, od_ref, h0_ref, wog_ref, wod_ref, g_ref, b_ref, rw_ref, rb_ref,
                h1_ref, route_ref, keys_ref, cnt_ref, carry_ref, *, rows_per_seq, id_bits):
    t = pl.program_id(0)

    @pl.when(t == 0)
    def _():
        carry_ref[...] = jnp.zeros_like(carry_ref)

    mix = (jnp.dot(og_ref[...], wog_ref[...], preferred_element_type=F32)
           + jnp.dot(od_ref[...], wod_ref[...], preferred_element_type=F32))
    h1 = _layer_norm(DEEPNORM_ALPHA * h0_ref[...] + mix, g_ref[...], b_ref[...])
    _store_token_tiled(h1_ref, h1)

    hh, hm, _ = _split3_bf16(h1)
    wh, wm, _ = _split3_bf16(rw_ref[...])
    head = jnp.dot(hh, jnp.concatenate([wh, wm], axis=1), preferred_element_type=F32)
    logits = rb_ref[...] + head[:, :LANES] + head[:, LANES:] + jnp.dot(hm, wh, preferred_element_type=F32)

    T = logits.shape[0]
    work = logits.T[:N_EXPERTS]
    expert = lax.broadcasted_iota(jnp.int32, (N_EXPERTS, T), 0)
    tok = t * T + lax.broadcasted_iota(jnp.int32, (1, T), 1)
    valid = (tok % rows_per_seq) >= FRONT_PAD
    top_v, top_i = [], []
    for _ in range(TOP_K):
        mx = jnp.max(work, axis=0, keepdims=True)
        idx = jnp.min(jnp.where(work == mx, expert, N_EXPERTS), axis=0, keepdims=True)
        top_v.append(mx)
        top_i.append(idx)
        work = jnp.where(expert == idx, -jnp.inf, work)
    ex = [jnp.exp(v - top_v[0]) for v in top_v]
    den = ex[0] + ex[1] + ex[2] + ex[3]
    gates = [lax.bitcast_convert_type(e / den, jnp.int32) for e in ex]
    keys = [jnp.where(valid, idx, N_EXPERTS) * (1 << id_bits) + (tok * TOP_K + k) for k, idx in enumerate(top_i)]

    onehot = jnp.zeros((N_EXPERTS, T), F32)
    for idx in top_i:
        onehot = onehot + (expert == idx).astype(F32)
    carry_ref[...] = carry_ref[...] + jnp.sum(jnp.where(valid, onehot, 0.0), axis=1, keepdims=True)
    cnt_ref[...] = carry_ref[...]

    by_lane = jnp.concatenate(top_i + keys + gates + [jnp.zeros((LANES - 3 * TOP_K, T), jnp.int32)], axis=0)
    route_ref[...] = by_lane.T
    keys_ref[...] = jnp.concatenate([key[:, q * LANES:(q + 1) * LANES] for key in keys for q in range(T // LANES)], axis=0)


def _outproj_route(og, od, h0, wo_g, wo_d, ln_g, ln_b, rw, rb, rows_per_seq):
    NP, D = h0.shape
    T = ROUTE_TILE

    def rows(width):
        return pl.BlockSpec((T, width), lambda t: (t, 0))

    def full(shape):
        return pl.BlockSpec(shape, lambda t: (0,) * len(shape))

    key_rows = T * TOP_K // LANES
    return pl.pallas_call(
        functools.partial(_route_body, rows_per_seq=rows_per_seq, id_bits=(NP * TOP_K - 1).bit_length()),
        grid=(NP // T,),
        in_specs=[rows(512), rows(512), rows(D), full((512, D)), full((512, D)),
                  full((1, D)), full((1, D)), full((D, LANES)), full((1, LANES))],
        out_specs=[pl.BlockSpec((T * TOKEN_SUBROWS, LANES), lambda t: (t, 0)), rows(LANES),
                   pl.BlockSpec((key_rows, LANES), lambda t: (t, 0)), full((N_EXPERTS, 1))],
        out_shape=[jax.ShapeDtypeStruct((NP * TOKEN_SUBROWS, LANES), F32),
                   jax.ShapeDtypeStruct((NP, LANES), jnp.int32),
                   jax.ShapeDtypeStruct((NP * TOP_K // LANES, LANES), jnp.int32),
                   jax.ShapeDtypeStruct((N_EXPERTS, 1), F32)],
        scratch_shapes=[pltpu.VMEM((N_EXPERTS, 1), F32)],
        compiler_params=pltpu.CompilerParams(
            dimension_semantics=("arbitrary",), vmem_limit_bytes=VMEM_LIMIT),
        name="outproj_route",
    )(og, od, h0, wo_g, wo_d, ln_g, ln_b, rw, rb)


def _expert_body(order_ref, p0_ref, bstart_ref, nblk_ref, ntot_ref,
                 h1t_hbm, wgu_ref, bg_ref, bu_ref, wd_ref, bd_ref, yk_hbm,
                 wg_s, wu_s, wd_s, xb0, xb1, xb2, yb0, yb1, yb2, zbuf, gsem, ssem, zsem,
                 *, n_tokens, rows_per_seq):
    e = pl.program_id(0)
    n_total = ntot_ref[0]
    R = MOE_ROWS
    TS = TOKEN_SUBROWS
    xb = (xb0, xb1, xb2)
    yb = (yb0, yb1, yb2)
    NB = len(xb)

    def gather_start(b, slot):
        p0 = p0_ref[b]
        for r in range(R):
            tok = order_ref[p0 + r] >> TOP_K_SHIFT
            pltpu.make_async_copy(
                h1t_hbm.at[pl.ds(pl.multiple_of(tok * TS, TS), TS), :],
                xb[slot].at[pl.ds(r * TS, TS), :], gsem.at[slot]).start(priority=r % 2)

    def gather_wait(slot):
        pltpu.make_async_copy(h1t_hbm.at[pl.ds(0, R * TS), :], xb[slot], gsem.at[slot]).wait()

    def scatter_start(b, slot):
        p0 = p0_ref[b]
        for r in range(R):
            row = order_ref[p0 + r] * TS
            pltpu.make_async_copy(
                yb[slot].at[pl.ds(r * TS, TS), :],
                yk_hbm.at[pl.ds(pl.multiple_of(row, TS), TS), :], ssem.at[slot]).start(priority=r % 2)

    def scatter_wait(slot):
        pltpu.make_async_copy(yb[slot], yk_hbm.at[pl.ds(0, R * TS), :], ssem.at[slot]).wait()

    @pl.when(e == 0)
    def _():
        zbuf[...] = jnp.zeros_like(zbuf)
        lead = FRONT_PAD * TS
        fills = [((s * rows_per_seq * TOP_K + k * FRONT_PAD) * TS, lead)
                 for s in range(n_tokens // rows_per_seq) for k in range(TOP_K)]
        copies = [pltpu.make_async_copy(zbuf.at[pl.ds(0, n), :], yk_hbm.at[pl.ds(o, n), :], zsem)
                  for o, n in fills]
        for cp in copies:
            cp.start()
        for cp in copies:
            cp.wait()
        gather_start(0, 0)
        gather_start(jnp.minimum(1, n_total - 1), 1)

    @pl.when(nblk_ref[e] > 0)
    def _():
        W, H = MXU_TILE, MXU_TILE // 2
        r = lax.broadcasted_iota(jnp.int32, (W, W), 0)
        c = lax.broadcasted_iota(jnp.int32, (W, W), 1)
        perm = (r == jnp.where(c < H, 2 * c, 2 * (c - H) + 1)).astype(BF16)
        for tt in range(2 * D_FF // W):
            wt = wgu_ref[0, :, tt * W:(tt + 1) * W].astype(BF16)
            sp = jnp.dot(wt, perm, preferred_element_type=F32)
            wg_s[:, tt * H:(tt + 1) * H] = sp[:, :H].astype(BF16)
            wu_s[:, tt * H:(tt + 1) * H] = sp[:, H:].astype(BF16)
        wd_s[...] = wd_ref[0].astype(BF16)

    def run_block(b, slot):
        prv = (slot + 2) % NB
        gather_wait(slot)
        gather_start(jnp.minimum(b + 2, n_total - 1), prv)
        x = _load_token_tiled(xb[slot], R).astype(BF16)
        gt = jnp.dot(x, wg_s[...], preferred_element_type=F32) + bg_ref[0]
        up = jnp.dot(x, wu_s[...], preferred_element_type=F32) + bu_ref[0]
        gt = jnp.minimum(gt, SWIGLU_LIMIT)
        up = jnp.clip(up, -SWIGLU_LIMIT, SWIGLU_LIMIT)
        act = (up + 1.0) * (gt * jax.nn.sigmoid(SWIGLU_ALPHA * gt))
        y = jnp.dot(act.astype(BF16), wd_s[...], preferred_element_type=F32) + bd_ref[0]
        _store_token_tiled(yb[slot], y)

        @pl.when(b >= 1)
        def _():
            scatter_wait(prv)
        scatter_start(b, slot)

    def block(b, carry):
        for slot in range(NB):
            pl.when(b % NB == slot)(functools.partial(run_block, b, slot))
        return carry

    b0 = bstart_ref[e]
    lax.fori_loop(b0, b0 + nblk_ref[e], block, 0)

    @pl.when(e == pl.num_programs(0) - 1)
    def _():
        last = n_total - 1
        for slot in range(NB):
            @pl.when(last % NB == slot)
            def _(slot=slot):
                gather_wait((slot + 1) % NB)
                gather_wait((slot + 2) % NB)
                scatter_wait(slot)


def _experts(order, blk_p0, blk_start, n_blk, n_total, h1t, w_gate_up, b_gate, b_up, w_down,
             b_down, n_tokens, rows_per_seq):
    D = D_MODEL
    R = MOE_ROWS
    buf = pltpu.VMEM((R * TOKEN_SUBROWS, LANES), F32)
    per_expert = lambda e, *_: (e, 0, 0)
    grid_spec = pltpu.PrefetchScalarGridSpec(
        num_scalar_prefetch=5,
        grid=(N_EXPERTS,),
        in_specs=[
            pl.BlockSpec(memory_space=pl.ANY),
            pl.BlockSpec((1, D, 2 * D_FF), per_expert),
            pl.BlockSpec((1, 1, D_FF), per_expert),
            pl.BlockSpec((1, 1, D_FF), per_expert),
            pl.BlockSpec((1, D_FF, D), per_expert),
            pl.BlockSpec((1, 1, D), per_expert),
        ],
        out_specs=pl.BlockSpec(memory_space=pl.ANY),
        scratch_shapes=[pltpu.VMEM((D, D_FF), BF16), pltpu.VMEM((D, D_FF), BF16), pltpu.VMEM((D_FF, D), BF16),
                        buf, buf, buf, buf, buf, buf, pltpu.VMEM((FRONT_PAD * TOKEN_SUBROWS, LANES), F32),
                        pltpu.SemaphoreType.DMA((3,)), pltpu.SemaphoreType.DMA((3,)), pltpu.SemaphoreType.DMA],
    )
    out_rows = TOP_K * n_tokens * TOKEN_SUBROWS
    return pl.pallas_call(
        functools.partial(_expert_body, n_tokens=n_tokens, rows_per_seq=rows_per_seq),
        grid_spec=grid_spec,
        out_shape=jax.ShapeDtypeStruct((out_rows, LANES), F32),
        compiler_params=pltpu.CompilerParams(
            dimension_semantics=("arbitrary",), vmem_limit_bytes=EXPERT_VMEM_LIMIT),
        name="experts",
    )(order, blk_p0, blk_start, n_blk, n_total, h1t, w_gate_up, b_gate, b_up, w_down, b_down)


COMBINE_BLOCKS = 4


def _combine_body(*refs):
    per_block = 2 + TOP_K
    g_ref, b_ref, o_ref = refs[COMBINE_BLOCKS * per_block:]
    for i in range(COMBINE_BLOCKS):
        h1_ref, *y_refs, route_ref = refs[i * per_block:(i + 1) * per_block]
        route = route_ref[...]
        y = jnp.zeros((ROW_BLOCK, D_MODEL), F32)
        for k, yk_ref in enumerate(y_refs):
            gate = lax.bitcast_convert_type(route[:, 2 * TOP_K + k:2 * TOP_K + k + 1], F32)
            y = y + gate * _load_token_tiled(yk_ref.reshape(ROW_BLOCK * TOKEN_SUBROWS, LANES), ROW_BLOCK)
        h1 = _load_token_tiled(h1_ref, ROW_BLOCK)
        o_ref[0, i * ROW_BLOCK:(i + 1) * ROW_BLOCK, :] = _layer_norm(
            DEEPNORM_ALPHA * h1 + y, g_ref[...], b_ref[...])


def _combine(h1t, yk, route, ln_g, ln_b, batch, seq_len):
    nb = seq_len // ROW_BLOCK
    blocks_per_seq = nb + 1
    tile_rows = ROW_BLOCK * TOKEN_SUBROWS
    yk = yk.reshape(-1, TOP_K, TOKEN_SUBROWS, LANES)

    def token_block(i):
        return lambda b, j: b * blocks_per_seq + COMBINE_BLOCKS * j + i + 1

    in_specs, operands = [], []
    for i in range(COMBINE_BLOCKS):
        tb = token_block(i)
        in_specs.append(pl.BlockSpec((tile_rows, LANES), lambda b, j, tb=tb: (tb(b, j), 0)))
        in_specs += [pl.BlockSpec((ROW_BLOCK, None, TOKEN_SUBROWS, LANES), lambda b, j, tb=tb, k=k: (tb(b, j), k, 0, 0))
                     for k in range(TOP_K)]
        in_specs.append(pl.BlockSpec((ROW_BLOCK, LANES), lambda b, j, tb=tb: (tb(b, j), 0)))
        operands += [h1t] + [yk] * TOP_K + [route]
    vec = pl.BlockSpec((1, D_MODEL), lambda b, j: (0, 0))
    return pl.pallas_call(
        _combine_body,
        grid=(batch, nb // COMBINE_BLOCKS),
        in_specs=in_specs + [vec, vec],
        out_specs=pl.BlockSpec((1, COMBINE_BLOCKS * ROW_BLOCK, D_MODEL), lambda b, j: (b, j, 0)),
        out_shape=jax.ShapeDtypeStruct((batch, seq_len, D_MODEL), F32),
        compiler_params=pltpu.CompilerParams(
            dimension_semantics=("parallel", "parallel"), vmem_limit_bytes=VMEM_LIMIT),
        name="combine",
    )(*operands, ln_g, ln_b)


def kernel(x, meta_tokens, ln_emb_g, ln_emb_b, w_in, gla_wa2, gla_ba, gla_norm_g, diff_lambda_q1, diff_lambda_k1, diff_lambda_q2, diff_lambda_k2, diff_norm_g, w_out, ln1_g, ln1_b, router_w, router_b, w_gate_up, b_gate_up, w_down, b_down, ln2_g, ln2_b):
    B, S, D = x.shape
    LP = S + ROW_BLOCK
    NP = B * LP
    row = lambda v: v.reshape(1, -1)

    w = w_in[0]
    w_main = jnp.concatenate([w[:, :1536], w[:, 1552:]], axis=1).astype(BF16)
    w_ga = w[:, 1536:1552].astype(BF16)
    meta_pad = jnp.pad(meta_tokens, ((FRONT_PAD, 0), (0, 0)))

    h0, gq, gk, gv, gr, la, dq, dk, dv = _inproj(
        x, meta_pad, row(ln_emb_g), row(ln_emb_b), w_main, w_ga,
        gla_wa2[0].astype(BF16), row(gla_ba[0]))

    og = _gla(gq, gk, gv, gr, la, row(gla_norm_g[0]))
    od = _diff_attention(dq, dk, dv, row(diff_lambda_q1[0]), row(diff_lambda_k1[0]),
                         row(diff_lambda_q2[0]), row(diff_lambda_k2[0]), row(diff_norm_g[0]))

    wo = w_out[0].astype(BF16)
    rw_pad = jnp.pad(router_w[0], ((0, 0), (0, LANES - N_EXPERTS)))
    rb_pad = jnp.pad(row(router_b[0]), ((0, 0), (0, LANES - N_EXPERTS)))
    h1t, route, keys, counts = _outproj_route(
        og.reshape(NP, 512), od.reshape(NP, 512), h0.reshape(NP, D), wo[:512], wo[512:],
        row(ln1_g[0]), row(ln1_b[0]), rw_pad, rb_pad, LP)

    R = MOE_ROWS
    i32 = jnp.int32
    counts = counts[:, 0].astype(i32)
    n_blk = (counts + R - 1) // R
    blk_end = jnp.cumsum(n_blk)
    blk_start = blk_end - n_blk
    grp_start = jnp.cumsum(counts) - counts
    n_total = blk_end[-1:]
    n_assign_max = B * (S + N_META) * TOP_K
    max_blocks = (n_assign_max + N_EXPERTS * (R - 1)) // R
    g = jnp.minimum(jnp.arange(max_blocks, dtype=i32), n_total[0] - 1)
    is_e = (jnp.minimum(jnp.sum(g[:, None] >= blk_end[None, :], axis=1), N_EXPERTS - 1)[:, None]
            == jnp.arange(N_EXPERTS)[None, :])
    pick = lambda v: jnp.sum(jnp.where(is_e, v[None, :], 0), axis=1)
    local = (g - pick(blk_start)) * R
    blk_p0 = (pick(grp_start) + local).astype(i32)

    assert R <= B * FRONT_PAD * TOP_K
    order = jnp.sort(keys.reshape(-1)) & ((1 << (NP * TOP_K - 1).bit_length()) - 1)

    bgu = b_gate_up[0].reshape(N_EXPERTS, D_FF, 2)
    yk = _experts(order, blk_p0, blk_start.astype(i32), n_blk.astype(i32), n_total.astype(i32),
                  h1t, w_gate_up[0],
                  bgu[:, :, 0].reshape(N_EXPERTS, 1, D_FF), bgu[:, :, 1].reshape(N_EXPERTS, 1, D_FF),
                  w_down[0], b_down[0].reshape(N_EXPERTS, 1, D), NP, LP)

    return _combine(h1t, yk, route, row(ln2_g[0]), row(ln2_b[0]), B, S)
```

```python
import functools
import math

import jax
import jax.numpy as jnp
from jax import lax
from jax.experimental import pallas as pl
from jax.experimental.pallas import tpu as pltpu

F32 = jnp.float32
BF16 = jnp.bfloat16

D_MODEL = 1024
LANES = 128
N_META = 16
ROW_BLOCK = 128
FRONT_PAD = ROW_BLOCK - N_META
CHUNK = 64

GLA_HEADS = 4
GLA_DK = 64
GLA_DV = 128
GLA_RANK = 16
GLA_TAU = 16.0
DIFF_HEADS = 4
DIFF_DH = 64
DIFF_DV = 128

N_EXPERTS = 32
TOP_K = 4
TOP_K_SHIFT = TOP_K.bit_length() - 1
D_FF = 1024
SWIGLU_LIMIT = 7.0
SWIGLU_ALPHA = 1.702
MOE_ROWS = 256

DEEPNORM_ALPHA = 2.0 ** 0.25
LAMBDA_INIT = 0.8 - 0.6 * math.exp(0.0)
LN_EPS = 1e-5
NEG_INF = -1e30
LOG2E = math.log2(math.e)

MXU_TILE = 256
ROUTE_TILE = 1024
VMEM_LIMIT = 48 * 1024 * 1024
EXPERT_VMEM_LIMIT = 56 * 1024 * 1024

_GQ, _GK, _GV, _GR, _DQ, _DK, _DV = 0, 256, 512, 1024, 1536, 2048, 2560
_MAIN_WIDTH = 3072


def _layer_norm(x, g, b):
    mu = jnp.mean(x, axis=-1, keepdims=True)
    xc = x - mu
    var = jnp.mean(xc * xc, axis=-1, keepdims=True)
    return xc * lax.rsqrt(var + LN_EPS) * g + b


def _split3_bf16(x):
    hi = x.astype(BF16)
    r1 = x - hi.astype(F32)
    mid = r1.astype(BF16)
    lo = (r1 - mid.astype(F32)).astype(BF16)
    return hi, mid, lo


TOKEN_SUBROWS = D_MODEL // LANES


def _store_token_tiled(ref, val, first_token=0):
    n = val.shape[0]
    for s in range(TOKEN_SUBROWS):
        ref[pl.ds(first_token * TOKEN_SUBROWS + s, n, stride=TOKEN_SUBROWS), :] = val[:, s * LANES:(s + 1) * LANES]


def _load_token_tiled(ref, n, first_row=0):
    return jnp.concatenate(
        [ref[pl.ds(first_row + s, n, stride=TOKEN_SUBROWS), :] for s in range(TOKEN_SUBROWS)], axis=1)


def _inproj_body(xa_ref, xb_ref, meta_ref, g_ref, b_ref, w_ref, wga_ref, wa2_ref, ba_ref,
                 h0_ref, gq_ref, gk_ref, gv_ref, gr_ref, la_ref, dq_ref, dk_ref, dv_ref):
    j = pl.program_id(1)
    rows = lax.broadcasted_iota(jnp.int32, (2 * ROW_BLOCK, 1), 0)
    valid = jnp.logical_or(j > 0, rows >= FRONT_PAD)
    xin = jnp.concatenate([jnp.where(j > 0, xa_ref[0], meta_ref[...]), xb_ref[0]], axis=0)
    h = jnp.where(valid, _layer_norm(xin, g_ref[...], b_ref[...]), 0.0)
    h0_ref[0] = h
    hb = h.astype(BF16)

    def proj(off, width):
        return jnp.dot(hb, w_ref[:, off:off + width], preferred_element_type=F32)

    gq_ref[0] = (proj(_GQ, 256) * GLA_DK ** -0.5).astype(BF16)
    gk_ref[0] = proj(_GK, 256)
    gv_ref[0] = proj(_GV, 512).astype(BF16)
    gr_ref[0] = proj(_GR, 512).astype(BF16)
    dq_ref[0] = (proj(_DQ, 512) * (DIFF_DH ** -0.5 * LOG2E)).astype(BF16)
    dk_ref[0] = proj(_DK, 512).astype(BF16)
    dv_ref[0] = proj(_DV, 512).astype(BF16)

    a_lr = jnp.dot(hb, wga_ref[...], preferred_element_type=F32)
    z = jnp.dot(a_lr.astype(BF16), wa2_ref[...], preferred_element_type=F32) + ba_ref[...]
    log_sig = jnp.minimum(z, 0.0) - jnp.log1p(jnp.exp(-jnp.abs(z)))
    la_ref[0] = jnp.where(valid, log_sig / GLA_TAU, 0.0)


def _inproj(x, meta_pad, ln_g, ln_b, w_main, w_ga, wa2, ba):
    B, S, D = x.shape
    nb = S // ROW_BLOCK + 1
    LP = nb * ROW_BLOCK

    n_frame_blocks = S // ROW_BLOCK
    n_steps = pl.cdiv(LP, 2 * ROW_BLOCK)

    def row_spec(width):
        return pl.BlockSpec((1, 2 * ROW_BLOCK, width), lambda b, j: (b, j, 0))

    def frames(shift):
        return pl.BlockSpec((1, ROW_BLOCK, D),
                            lambda b, j: (b, jnp.clip(2 * j + shift, 0, n_frame_blocks - 1), 0))

    def full(shape):
        return pl.BlockSpec(shape, lambda b, j: (0,) * len(shape))

    out_shapes = [
        jax.ShapeDtypeStruct((B, LP, D), F32),
        jax.ShapeDtypeStruct((B, LP, 256), BF16),
        jax.ShapeDtypeStruct((B, LP, 256), F32),
        jax.ShapeDtypeStruct((B, LP, 512), BF16),
        jax.ShapeDtypeStruct((B, LP, 512), BF16),
        jax.ShapeDtypeStruct((B, LP, 256), F32),
        jax.ShapeDtypeStruct((B, LP, 512), BF16),
        jax.ShapeDtypeStruct((B, LP, 512), BF16),
        jax.ShapeDtypeStruct((B, LP, 512), BF16),
    ]
    return pl.pallas_call(
        _inproj_body,
        grid=(B, n_steps),
        in_specs=[
            frames(-1), frames(0),
            full((ROW_BLOCK, D)), full((1, D)), full((1, D)),
            full((D, _MAIN_WIDTH)), full((D, GLA_RANK)), full((GLA_RANK, 256)), full((1, 256)),
        ],
        out_specs=[row_spec(s.shape[-1]) for s in out_shapes],
        out_shape=out_shapes,
        compiler_params=pltpu.CompilerParams(
            dimension_semantics=("parallel", "parallel"), vmem_limit_bytes=VMEM_LIMIT),
        name="inproj",
    )(x, x, meta_pad, ln_g, ln_b, w_main, w_ga, wa2, ba)


def _gla_body(q_ref, k_ref, v_ref, r_ref, la_ref, g_ref, o_ref):
    n_groups = q_ref.shape[1] // ROW_BLOCK
    ri = lax.broadcasted_iota(jnp.int32, (ROW_BLOCK, ROW_BLOCK), 0)
    ci = lax.broadcasted_iota(jnp.int32, (ROW_BLOCK, ROW_BLOCK), 1)
    later = jnp.logical_and(ri // CHUNK == ci // CHUNK, ci > ri).astype(BF16)
    sr = lax.broadcasted_iota(jnp.int32, (GLA_HEADS * GLA_DV, GLA_HEADS * GLA_DK), 0)
    sc = lax.broadcasted_iota(jnp.int32, (GLA_HEADS * GLA_DV, GLA_HEADS * GLA_DK), 1)
    same_head = sr // GLA_DV == sc // GLA_DK
    gain = g_ref[...]
    st = jnp.zeros((GLA_HEADS * GLA_DV, GLA_HEADS * GLA_DK), F32)

    for grp in range(n_groups):
        g0 = grp * ROW_BLOCK
        la = la_ref[0, g0:g0 + ROW_BLOCK, :]
        hi, mid, lo = _split3_bf16(la)
        suffix = (jnp.dot(later, hi, preferred_element_type=F32)
                  + jnp.dot(later, mid, preferred_element_type=F32)
                  + jnp.dot(later, lo, preferred_element_type=F32))
        kdec = (k_ref[0, g0:g0 + ROW_BLOCK, :] * jnp.exp(suffix)).astype(BF16)
        for half in range(ROW_BLOCK // CHUNK):
            f = half * CHUNK
            r0 = g0 + f
            tot = suffix[f:f + 1, :] + la[f:f + 1, :]
            upd = lax.dot_general(v_ref[0, r0:r0 + CHUNK, :], kdec[f:f + CHUNK, :],
                                  (((0,), (0,)), ((), ())), preferred_element_type=F32)
            st = st * jnp.exp(tot) + jnp.where(same_head, upd, 0.0)
            o = lax.dot_general(q_ref[0, r0:r0 + CHUNK, :], st.astype(BF16),
                                (((1,), (1,)), ((), ())), preferred_element_type=F32)
            r = r_ref[0, r0:r0 + CHUNK, :].astype(F32)
            for h in range(GLA_HEADS):
                oh = o[:, h * GLA_DV:(h + 1) * GLA_DV]
                rh = r[:, h * GLA_DV:(h + 1) * GLA_DV]
                ms = jnp.mean(oh * oh, axis=-1, keepdims=True)
                out = oh * lax.rsqrt(ms + LN_EPS) * gain * (rh * jax.nn.sigmoid(rh))
                o_ref[0, r0:r0 + CHUNK, h * GLA_DV:(h + 1) * GLA_DV] = out.astype(BF16)


def _gla(gq, gk, gv, gr, la, norm_g):
    B, LP, _ = gq.shape

    def seq(width):
        return pl.BlockSpec((1, LP, width), lambda b: (b, 0, 0))

    return pl.pallas_call(
        _gla_body,
        grid=(B,),
        in_specs=[seq(256), seq(256), seq(512), seq(512), seq(256),
                  pl.BlockSpec((1, GLA_DV), lambda b: (0, 0))],
        out_specs=seq(512),
        out_shape=jax.ShapeDtypeStruct((B, LP, 512), BF16),
        compiler_params=pltpu.CompilerParams(
            dimension_semantics=("parallel",), vmem_limit_bytes=VMEM_LIMIT),
        name="gla",
    )(gq, gk, gv, gr, la, norm_g)


ATT_BLOCK = 256


def _diff_body(q_ref, k_ref, v_ref, lq1_ref, lk1_ref, lq2_ref, lk2_ref, g_ref, o_ref,
               dmask_ref):
    h = pl.program_id(1)
    slope = jnp.where(h == 0, 2.0 ** -2, jnp.where(h == 1, 2.0 ** -4, jnp.where(h == 2, 2.0 ** -6, 2.0 ** -8)))
    slope = slope.astype(F32) * LOG2E
    lam = (jnp.exp(jnp.sum(lq1_ref[...] * lk1_ref[...], axis=-1, keepdims=True))
           - jnp.exp(jnp.sum(lq2_ref[...] * lk2_ref[...], axis=-1, keepdims=True)) + LAMBDA_INIT)
    gain = g_ref[...] * (1.0 - LAMBDA_INIT)
    n_qblocks = (q_ref.shape[1] - ROW_BLOCK) // ATT_BLOCK
    nt = (((1,), (1,)), ((), ()))

    def split_q(q):
        lane = lax.broadcasted_iota(jnp.int32, q.shape, 1)
        zero = jnp.zeros_like(q)
        return jnp.where(lane < DIFF_DH, q, zero), jnp.where(lane >= DIFF_DH, q, zero)

    def with_ones(v):
        return jnp.concatenate([v, jnp.ones_like(v)], axis=1)

    def softmax_av(qz, kk, vext, add_bias):
        s = add_bias(lax.dot_general(qz, kk, nt, preferred_element_type=F32))
        m = jnp.max(s, axis=-1, keepdims=True)
        p = jnp.exp2(s - m).astype(BF16)
        return jnp.dot(p, vext, preferred_element_type=F32)

    def finish(a1, a2):
        o = a1[:, :DIFF_DV] / a1[:, DIFF_DV:] - lam * (a2[:, :DIFF_DV] / a2[:, DIFF_DV:])
        ms = jnp.mean(o * o, axis=-1, keepdims=True)
        return (o * lax.rsqrt(ms + LN_EPS) * gain).astype(BF16)

    r = lax.broadcasted_iota(jnp.int32, (ROW_BLOCK, ROW_BLOCK), 0)
    c = lax.broadcasted_iota(jnp.int32, (ROW_BLOCK, ROW_BLOCK), 1)
    ok = jnp.logical_and(c // CHUNK <= r // CHUNK, c >= FRONT_PAD)
    bias_lead = jnp.where(ok, -slope * jnp.abs(r - c).astype(F32), NEG_INF)
    q1z, q2z = split_q(q_ref[0, 0:ROW_BLOCK, :])
    k_lead = k_ref[0, 0:ROW_BLOCK, :]
    v_lead = with_ones(v_ref[0, 0:ROW_BLOCK, :])
    o_ref[0, 0:ROW_BLOCK, :] = finish(softmax_av(q1z, k_lead, v_lead, lambda s: s + bias_lead),
                                      softmax_av(q2z, k_lead, v_lead, lambda s: s + bias_lead))

    r = lax.broadcasted_iota(jnp.int32, (ATT_BLOCK, ATT_BLOCK), 0)
    c = lax.broadcasted_iota(jnp.int32, (ATT_BLOCK, ATT_BLOCK), 1)
    rel = jnp.where(c <= r, c, 2 * r - c).astype(F32)
    dmask_ref[...] = jnp.where(c // CHUNK <= r // CHUNK, slope * rel, NEG_INF)

    for jq in range(n_qblocks):
        qbase = ROW_BLOCK + jq * ATT_BLOCK
        n_keys = qbase + ATT_BLOCK
        q1z, q2z = split_q(q_ref[0, qbase:qbase + ATT_BLOCK, :])
        kk = k_ref[0, 0:n_keys, :]
        vext = with_ones(v_ref[0, 0:n_keys, :])
        col = lax.broadcasted_iota(jnp.int32, (1, qbase), 1)
        col_bias = jnp.where(col >= FRONT_PAD, slope * (col - qbase).astype(F32), NEG_INF)

        def add_bias(s, col_bias=col_bias, qbase=qbase):
            return jnp.concatenate([s[:, :qbase] + col_bias, s[:, qbase:] + dmask_ref[...]], axis=1)

        o_ref[0, qbase:qbase + ATT_BLOCK, :] = finish(softmax_av(q1z, kk, vext, add_bias),
                                                      softmax_av(q2z, kk, vext, add_bias))


def _diff_attention(dq, dk, dv, lq1, lk1, lq2, lk2, norm_g):
    B, LP, W = dq.shape
    small = pl.BlockSpec((1, DIFF_DH), lambda b, h: (0, 0))
    seq = pl.BlockSpec((1, LP, 2 * DIFF_DH), lambda b, h: (b, 0, h))
    return pl.pallas_call(
        _diff_body,
        grid=(B, DIFF_HEADS),
        in_specs=[seq, seq, seq, small, small, small, small,
                  pl.BlockSpec((1, DIFF_DV), lambda b, h: (0, 0))],
        out_specs=seq,
        out_shape=jax.ShapeDtypeStruct((B, LP, W), BF16),
        scratch_shapes=[pltpu.VMEM((ATT_BLOCK, ATT_BLOCK), F32)],
        compiler_params=pltpu.CompilerParams(
            dimension_semantics=("parallel", "parallel"), vmem_limit_bytes=VMEM_LIMIT),
        name="diff_attn",
    )(dq, dk, dv, lq1, lk1, lq2, lk2, norm_g)


def _route_body(og_ref, od_ref, h0_ref, wog_ref, wod_ref, g_ref, b_ref, rw_ref, rb_ref,
                h1_ref, route_ref, keys_ref, cnt_ref, carry_ref, *, rows_per_seq, id_bits):
    t = pl.program_id(0)

    @pl.when(t == 0)
    def _():
        carry_ref[...] = jnp.zeros_like(carry_ref)

    mix = (jnp.dot(og_ref[...], wog_ref[...], preferred_element_type=F32)
           + jnp.dot(od_ref[...], wod_ref[...], preferred_element_type=F32))
    h1 = _layer_norm(DEEPNORM_ALPHA * h0_ref[...] + mix, g_ref[...], b_ref[...])
    _store_token_tiled(h1_ref, h1)

    hh, hm, _ = _split3_bf16(h1)
    wh, wm, _ = _split3_bf16(rw_ref[...])
    head = jnp.dot(hh, jnp.concatenate([wh, wm], axis=1), preferred_element_type=F32)
    logits = rb_ref[...] + head[:, :LANES] + head[:, LANES:] + jnp.dot(hm, wh, preferred_element_type=F32)

    T = logits.shape[0]
    work = logits.T[:N_EXPERTS]
    expert = lax.broadcasted_iota(jnp.int32, (N_EXPERTS, T), 0)
    tok = t * T + lax.broadcasted_iota(jnp.int32, (1, T), 1)
    valid = (tok % rows_per_seq) >= FRONT_PAD
    top_v, top_i = [], []
    for _ in range(TOP_K):
        mx = jnp.max(work, axis=0, keepdims=True)
        idx = jnp.min(jnp.where(work == mx, expert, N_EXPERTS), axis=0, keepdims=True)
        top_v.append(mx)
        top_i.append(idx)
        work = jnp.where(expert == idx, -jnp.inf, work)
    ex = [jnp.exp(v - top_v[0]) for v in top_v]
    den = ex[0] + ex[1] + ex[2] + ex[3]
    gates = [lax.bitcast_convert_type(e / den, jnp.int32) for e in ex]
    keys = [jnp.where(valid, idx, N_EXPERTS) * (1 << id_bits) + (tok * TOP_K + k) for k, idx in enumerate(top_i)]

    onehot = jnp.zeros((N_EXPERTS, T), F32)
    for idx in top_i:
        onehot = onehot + (expert == idx).astype(F32)
    carry_ref[...] = carry_ref[...] + jnp.sum(jnp.where(valid, onehot, 0.0), axis=1, keepdims=True)
    cnt_ref[...] = carry_ref[...]

    by_lane = jnp.concatenate(top_i + keys + gates + [jnp.zeros((LANES - 3 * TOP_K, T), jnp.int32)], axis=0)
    route_ref[...] = by_lane.T
    keys_ref[...] = jnp.concatenate([key[:, q * LANES:(q + 1) * LANES] for key in keys for q in range(T // LANES)], axis=0)


def _outproj_route(og, od, h0, wo_g, wo_d, ln_g, ln_b, rw, rb, rows_per_seq):
    NP, D = h0.shape
    T = ROUTE_TILE

    def rows(width):
        return pl.BlockSpec((T, width), lambda t: (t, 0))

    def full(shape):
        return pl.BlockSpec(shape, lambda t: (0,) * len(shape))

    key_rows = T * TOP_K // LANES
    return pl.pallas_call(
        functools.partial(_route_body, rows_per_seq=rows_per_seq, id_bits=(NP * TOP_K - 1).bit_length()),
        grid=(NP // T,),
        in_specs=[rows(512), rows(512), rows(D), full((512, D)), full((512, D)),
                  full((1, D)), full((1, D)), full((D, LANES)), full((1, LANES))],
        out_specs=[pl.BlockSpec((T * TOKEN_SUBROWS, LANES), lambda t: (t, 0)), rows(LANES),
                   pl.BlockSpec((key_rows, LANES), lambda t: (t, 0)), full((N_EXPERTS, 1))],
        out_shape=[jax.ShapeDtypeStruct((NP * TOKEN_SUBROWS, LANES), F32),
                   jax.ShapeDtypeStruct((NP, LANES), jnp.int32),
                   jax.ShapeDtypeStruct((NP * TOP_K // LANES, LANES), jnp.int32),
                   jax.ShapeDtypeStruct((N_EXPERTS, 1), F32)],
        scratch_shapes=[pltpu.VMEM((N_EXPERTS, 1), F32)],
        compiler_params=pltpu.CompilerParams(
            dimension_semantics=("arbitrary",), vmem_limit_bytes=VMEM_LIMIT),
        name="outproj_route",
    )(og, od, h0, wo_g, wo_d, ln_g, ln_b, rw, rb)


def _expert_body(order_ref, p0_ref, bstart_ref, nblk_ref, ntot_ref,
                 h1t_hbm, wgu_ref, bg_ref, bu_ref, wd_ref, bd_ref, yk_hbm,
                 wg_s, wu_s, wd_s, xb0, xb1, xb2, yb0, yb1, yb2, zbuf, gsem, ssem, zsem,
                 *, n_tokens, rows_per_seq):
    e = pl.program_id(0)
    n_total = ntot_ref[0]
    R = MOE_ROWS
    TS = TOKEN_SUBROWS
    xb = (xb0, xb1, xb2)
    yb = (yb0, yb1, yb2)
    NB = len(xb)

    def gather_start(b, slot):
        p0 = p0_ref[b]
        for r in range(R):
            tok = order_ref[p0 + r] >> TOP_K_SHIFT
            pltpu.make_async_copy(
                h1t_hbm.at[pl.ds(pl.multiple_of(tok * TS, TS), TS), :],
                xb[slot].at[pl.ds(r * TS, TS), :], gsem.at[slot]).start(priority=r % 2)

    def gather_wait(slot):
        pltpu.make_async_copy(h1t_hbm.at[pl.ds(0, R * TS), :], xb[slot], gsem.at[slot]).wait()

    def scatter_start(b, slot):
        p0 = p0_ref[b]
        for r in range(R):
            row = order_ref[p0 + r] * TS
            pltpu.make_async_copy(
                yb[slot].at[pl.ds(r * TS, TS), :],
                yk_hbm.at[pl.ds(pl.multiple_of(row, TS), TS), :], ssem.at[slot]).start(priority=r % 2)

    def scatter_wait(slot):
        pltpu.make_async_copy(yb[slot], yk_hbm.at[pl.ds(0, R * TS), :], ssem.at[slot]).wait()

    @pl.when(e == 0)
    def _():
        zbuf[...] = jnp.zeros_like(zbuf)
        lead = FRONT_PAD * TS
        fills = [((s * rows_per_seq * TOP_K + k * FRONT_PAD) * TS, lead)
                 for s in range(n_tokens // rows_per_seq) for k in range(TOP_K)]
        copies = [pltpu.make_async_copy(zbuf.at[pl.ds(0, n), :], yk_hbm.at[pl.ds(o, n), :], zsem)
                  for o, n in fills]
        for cp in copies:
            cp.start()
        for cp in copies:
            cp.wait()
        gather_start(0, 0)
        gather_start(jnp.minimum(1, n_total - 1), 1)

    @pl.when(nblk_ref[e] > 0)
    def _():
        W, H = MXU_TILE, MXU_TILE // 2
        r = lax.broadcasted_iota(jnp.int32, (W, W), 0)
        c = lax.broadcasted_iota(jnp.int32, (W, W), 1)
        perm = (r == jnp.where(c < H, 2 * c, 2 * (c - H) + 1)).astype(BF16)
        for tt in range(2 * D_FF // W):
            wt = wgu_ref[0, :, tt * W:(tt + 1) * W].astype(BF16)
            sp = jnp.dot(wt, perm, preferred_element_type=F32)
            wg_s[:, tt * H:(tt + 1) * H] = sp[:, :H].astype(BF16)
            wu_s[:, tt * H:(tt + 1) * H] = sp[:, H:].astype(BF16)
        wd_s[...] = wd_ref[0].astype(BF16)

    def run_block(b, slot):
        prv = (slot + 2) % NB
        gather_wait(slot)
        gather_start(jnp.minimum(b + 2, n_total - 1), prv)
        x = _load_token_tiled(xb[slot], R).astype(BF16)
        gt = jnp.dot(x, wg_s[...], preferred_element_type=F32) + bg_ref[0]
        up = jnp.dot(x, wu_s[...], preferred_element_type=F32) + bu_ref[0]
        gt = jnp.minimum(gt, SWIGLU_LIMIT)
        up = jnp.clip(up, -SWIGLU_LIMIT, SWIGLU_LIMIT)
        act = (up + 1.0) * (gt * jax.nn.sigmoid(SWIGLU_ALPHA * gt))
        y = jnp.dot(act.astype(BF16), wd_s[...], preferred_element_type=F32) + bd_ref[0]
        _store_token_tiled(yb[slot], y)

        @pl.when(b >= 1)
        def _():
            scatter_wait(prv)
        scatter_start(b, slot)

    def block(b, carry):
        for slot in range(NB):
            pl.when(b % NB == slot)(functools.partial(run_block, b, slot))
        return carry

    b0 = bstart_ref[e]
    lax.fori_loop(b0, b0 + nblk_ref[e], block, 0)

    @pl.when(e == pl.num_programs(0) - 1)
    def _():
        last = n_total - 1
        for slot in range(NB):
            @pl.when(last % NB == slot)
            def _(slot=slot):
                gather_wait((slot + 1) % NB)
                gather_wait((slot + 2) % NB)
                scatter_wait(slot)


def _experts(order, blk_p0, blk_start, n_blk, n_total, h1t, w_gate_up, b_gate, b_up, w_down,
             b_down, n_tokens, rows_per_seq):
    D = D_MODEL
    R = MOE_ROWS
    buf = pltpu.VMEM((R * TOKEN_SUBROWS, LANES), F32)
    per_expert = lambda e, *_: (e, 0, 0)
    grid_spec = pltpu.PrefetchScalarGridSpec(
        num_scalar_prefetch=5,
        grid=(N_EXPERTS,),
        in_specs=[
            pl.BlockSpec(memory_space=pl.ANY),
            pl.BlockSpec((1, D, 2 * D_FF), per_expert),
            pl.BlockSpec((1, 1, D_FF), per_expert),
            pl.BlockSpec((1, 1, D_FF), per_expert),
            pl.BlockSpec((1, D_FF, D), per_expert),
            pl.BlockSpec((1, 1, D), per_expert),
        ],
        out_specs=pl.BlockSpec(memory_space=pl.ANY),
        scratch_shapes=[pltpu.VMEM((D, D_FF), BF16), pltpu.VMEM((D, D_FF), BF16), pltpu.VMEM((D_FF, D), BF16),
                        buf, buf, buf, buf, buf, buf, pltpu.VMEM((FRONT_PAD * TOKEN_SUBROWS, LANES), F32),
                        pltpu.SemaphoreType.DMA((3,)), pltpu.SemaphoreType.DMA((3,)), pltpu.SemaphoreType.DMA],
    )
    out_rows = TOP_K * n_tokens * TOKEN_SUBROWS
    return pl.pallas_call(
        functools.partial(_expert_body, n_tokens=n_tokens, rows_per_seq=rows_per_seq),
        grid_spec=grid_spec,
        out_shape=jax.ShapeDtypeStruct((out_rows, LANES), F32),
        compiler_params=pltpu.CompilerParams(
            dimension_semantics=("arbitrary",), vmem_limit_bytes=EXPERT_VMEM_LIMIT),
        name="experts",
    )(order, blk_p0, blk_start, n_blk, n_total, h1t, w_gate_up, b_gate, b_up, w_down, b_down)


COMBINE_BLOCKS = 4


def _combine_body(*refs):
    per_block = 2 + TOP_K
    g_ref, b_ref, o_ref = refs[COMBINE_BLOCKS * per_block:]
    for i in range(COMBINE_BLOCKS):
        h1_ref, *y_refs, route_ref = refs[i * per_block:(i + 1) * per_block]
        route = route_ref[...]
        y = jnp.zeros((ROW_BLOCK, D_MODEL), F32)
        for k, yk_ref in enumerate(y_refs):
            gate = lax.bitcast_convert_type(route[:, 2 * TOP_K + k:2 * TOP_K + k + 1], F32)
            y = y + gate * _load_token_tiled(yk_ref.reshape(ROW_BLOCK * TOKEN_SUBROWS, LANES), ROW_BLOCK)
        h1 = _load_token_tiled(h1_ref, ROW_BLOCK)
        o_ref[0, i * ROW_BLOCK:(i + 1) * ROW_BLOCK, :] = _layer_norm(
            DEEPNORM_ALPHA * h1 + y, g_ref[...], b_ref[...])


def _combine(h1t, yk, route, ln_g, ln_b, batch, seq_len):
    nb = seq_len // ROW_BLOCK
    blocks_per_seq = nb + 1
    tile_rows = ROW_BLOCK * TOKEN_SUBROWS
    yk = yk.reshape(-1, TOP_K, TOKEN_SUBROWS, LANES)

    def token_block(i):
        return lambda b, j: b * blocks_per_seq + COMBINE_BLOCKS * j + i + 1

    in_specs, operands = [], []
    for i in range(COMBINE_BLOCKS):
        tb = token_block(i)
        in_specs.append(pl.BlockSpec((tile_rows, LANES), lambda b, j, tb=tb: (tb(b, j), 0)))
        in_specs += [pl.BlockSpec((ROW_BLOCK, None, TOKEN_SUBROWS, LANES), lambda b, j, tb=tb, k=k: (tb(b, j), k, 0, 0))
                     for k in range(TOP_K)]
        in_specs.append(pl.BlockSpec((ROW_BLOCK, LANES), lambda b, j, tb=tb: (tb(b, j), 0)))
        operands += [h1t] + [yk] * TOP_K + [route]
    vec = pl.BlockSpec((1, D_MODEL), lambda b, j: (0, 0))
    return pl.pallas_call(
        _combine_body,
        grid=(batch, nb // COMBINE_BLOCKS),
        in_specs=in_specs + [vec, vec],
        out_specs=pl.BlockSpec((1, COMBINE_BLOCKS * ROW_BLOCK, D_MODEL), lambda b, j: (b, j, 0)),
        out_shape=jax.ShapeDtypeStruct((batch, seq_len, D_MODEL), F32),
        compiler_params=pltpu.CompilerParams(
            dimension_semantics=("parallel", "parallel"), vmem_limit_bytes=VMEM_LIMIT),
        name="combine",
    )(*operands, ln_g, ln_b)


def kernel(x, meta_tokens, ln_emb_g, ln_emb_b, w_in, gla_wa2, gla_ba, gla_norm_g, diff_lambda_q1, diff_lambda_k1, diff_lambda_q2, diff_lambda_k2, diff_norm_g, w_out, ln1_g, ln1_b, router_w, router_b, w_gate_up, b_gate_up, w_down, b_down, ln2_g, ln2_b):
    B, S, D = x.shape
    LP = S + ROW_BLOCK
    NP = B * LP
    row = lambda v: v.reshape(1, -1)

    w = w_in[0]
    w_main = jnp.concatenate([w[:, :1536], w[:, 1552:]], axis=1).astype(BF16)
    w_ga = w[:, 1536:1552].astype(BF16)
    meta_pad = jnp.pad(meta_tokens, ((FRONT_PAD, 0), (0, 0)))

    h0, gq, gk, gv, gr, la, dq, dk, dv = _inproj(
        x, meta_pad, row(ln_emb_g), row(ln_emb_b), w_main, w_ga,
        gla_wa2[0].astype(BF16), row(gla_ba[0]))

    og = _gla(gq, gk, gv, gr, la, row(gla_norm_g[0]))
    od = _diff_attention(dq, dk, dv, row(diff_lambda_q1[0]), row(diff_lambda_k1[0]),
                         row(diff_lambda_q2[0]), row(diff_lambda_k2[0]), row(diff_norm_g[0]))

    wo = w_out[0].astype(BF16)
    rw_pad = jnp.pad(router_w[0], ((0, 0), (0, LANES - N_EXPERTS)))
    rb_pad = jnp.pad(row(router_b[0]), ((0, 0), (0, LANES - N_EXPERTS)))
    h1t, route, keys, counts = _outproj_route(
        og.reshape(NP, 512), od.reshape(NP, 512), h0.reshape(NP, D), wo[:512], wo[512:],
        row(ln1_g[0]), row(ln1_b[0]), rw_pad, rb_pad, LP)

    R = MOE_ROWS
    i32 = jnp.int32
    counts = counts[:, 0].astype(i32)
    n_blk = (counts + R - 1) // R
    blk_end = jnp.cumsum(n_blk)
    blk_start = blk_end - n_blk
    grp_start = jnp.cumsum(counts) - counts
    n_total = blk_end[-1:]
    n_assign_max = B * (S + N_META) * TOP_K
    max_blocks = (n_assign_max + N_EXPERTS * (R - 1)) // R
    g = jnp.minimum(jnp.arange(max_blocks, dtype=i32), n_total[0] - 1)
    is_e = (jnp.minimum(jnp.sum(g[:, None] >= blk_end[None, :], axis=1), N_EXPERTS - 1)[:, None]
            == jnp.arange(N_EXPERTS)[None, :])
    pick = lambda v: jnp.sum(jnp.where(is_e, v[None, :], 0), axis=1)
    local = (g - pick(blk_start)) * R
    blk_p0 = (pick(grp_start) + local).astype(i32)

    assert R <= B * FRONT_PAD * TOP_K
    order = jnp.sort(keys.reshape(-1), stable=False) & ((1 << (NP * TOP_K - 1).bit_length()) - 1)

    bgu = b_gate_up[0].reshape(N_EXPERTS, D_FF, 2)
    yk = _experts(order, blk_p0, blk_start.astype(i32), n_blk.astype(i32), n_total.astype(i32),
                  h1t, w_gate_up[0],
                  bgu[:, :, 0].reshape(N_EXPERTS, 1, D_FF), bgu[:, :, 1].reshape(N_EXPERTS, 1, D_FF),
                  w_down[0], b_down[0].reshape(N_EXPERTS, 1, D), NP, LP)

    return _combine(h1t, yk, route, row(ln2_g[0]), row(ln2_b[0]), B, S)
```

```python
import functools
import math

import jax
import jax.numpy as jnp
from jax import lax
from jax.experimental import pallas as pl
from jax.experimental.pallas import tpu as pltpu

F32 = jnp.float32
BF16 = jnp.bfloat16

D_MODEL = 1024
LANES = 128
N_META = 16
ROW_BLOCK = 128
FRONT_PAD = ROW_BLOCK - N_META
CHUNK = 64

GLA_HEADS = 4
GLA_DK = 64
GLA_DV = 128
GLA_RANK = 16
GLA_TAU = 16.0
DIFF_HEADS = 4
DIFF_DH = 64
DIFF_DV = 128

N_EXPERTS = 32
TOP_K = 4
TOP_K_SHIFT = TOP_K.bit_length() - 1
D_FF = 1024
SWIGLU_LIMIT = 7.0
SWIGLU_ALPHA = 1.702
MOE_ROWS = 256

DEEPNORM_ALPHA = 2.0 ** 0.25
LAMBDA_INIT = 0.8 - 0.6 * math.exp(0.0)
LN_EPS = 1e-5
NEG_INF = -1e30
LOG2E = math.log2(math.e)

MXU_TILE = 256
ROUTE_TILE = 1024
VMEM_LIMIT = 48 * 1024 * 1024
EXPERT_VMEM_LIMIT = 56 * 1024 * 1024

_GQ, _GK, _GV, _GR, _DQ, _DK, _DV = 0, 256, 512, 1024, 1536, 2048, 2560
_MAIN_WIDTH = 3072


def _layer_norm(x, g, b):
    mu = jnp.mean(x, axis=-1, keepdims=True)
    xc = x - mu
    var = jnp.mean(xc * xc, axis=-1, keepdims=True)
    return xc * lax.rsqrt(var + LN_EPS) * g + b


def _split3_bf16(x):
    hi = x.astype(BF16)
    r1 = x - hi.astype(F32)
    mid = r1.astype(BF16)
    lo = (r1 - mid.astype(F32)).astype(BF16)
    return hi, mid, lo


TOKEN_SUBROWS = D_MODEL // LANES


def _store_token_tiled(ref, val, first_token=0):
    n = val.shape[0]
    for s in range(TOKEN_SUBROWS):
        ref[pl.ds(first_token * TOKEN_SUBROWS + s, n, stride=TOKEN_SUBROWS), :] = val[:, s * LANES:(s + 1) * LANES]


def _load_token_tiled(ref, n, first_row=0):
    return jnp.concatenate(
        [ref[pl.ds(first_row + s, n, stride=TOKEN_SUBROWS), :] for s in range(TOKEN_SUBROWS)], axis=1)


def _inproj_body(xa_ref, xb_ref, meta_ref, g_ref, b_ref, w_ref, wga_ref, wa2_ref, ba_ref,
                 h0_ref, gq_ref, gk_ref, gv_ref, gr_ref, la_ref, dq_ref, dk_ref, dv_ref):
    j = pl.program_id(1)
    rows = lax.broadcasted_iota(jnp.int32, (2 * ROW_BLOCK, 1), 0)
    valid = jnp.logical_or(j > 0, rows >= FRONT_PAD)
    xin = jnp.concatenate([jnp.where(j > 0, xa_ref[0], meta_ref[...]), xb_ref[0]], axis=0)
    h = jnp.where(valid, _layer_norm(xin, g_ref[...], b_ref[...]), 0.0)
    h0_ref[0] = h
    hb = h.astype(BF16)

    def proj(off, width):
        return jnp.dot(hb, w_ref[:, off:off + width], preferred_element_type=F32)

    gq_ref[0] = (proj(_GQ, 256) * GLA_DK ** -0.5).astype(BF16)
    gk_ref[0] = proj(_GK, 256)
    gv_ref[0] = proj(_GV, 512).astype(BF16)
    gr_ref[0] = proj(_GR, 512).astype(BF16)
    dq_ref[0] = (proj(_DQ, 512) * (DIFF_DH ** -0.5 * LOG2E)).astype(BF16)
    dk_ref[0] = proj(_DK, 512).astype(BF16)
    dv_ref[0] = proj(_DV, 512).astype(BF16)

    a_lr = jnp.dot(hb, wga_ref[...], preferred_element_type=F32)
    z = jnp.dot(a_lr.astype(BF16), wa2_ref[...], preferred_element_type=F32) + ba_ref[...]
    log_sig = jnp.minimum(z, 0.0) - jnp.log1p(jnp.exp(-jnp.abs(z)))
    la_ref[0] = jnp.where(valid, log_sig / GLA_TAU, 0.0)


def _inproj(x, meta_pad, ln_g, ln_b, w_main, w_ga, wa2, ba):
    B, S, D = x.shape
    nb = S // ROW_BLOCK + 1
    LP = nb * ROW_BLOCK

    n_frame_blocks = S // ROW_BLOCK
    n_steps = pl.cdiv(LP, 2 * ROW_BLOCK)

    def row_spec(width):
        return pl.BlockSpec((1, 2 * ROW_BLOCK, width), lambda b, j: (b, j, 0))

    def frames(shift):
        return pl.BlockSpec((1, ROW_BLOCK, D),
                            lambda b, j: (b, jnp.clip(2 * j + shift, 0, n_frame_blocks - 1), 0))

    def full(shape):
        return pl.BlockSpec(shape, lambda b, j: (0,) * len(shape))

    out_shapes = [
        jax.ShapeDtypeStruct((B, LP, D), F32),
        jax.ShapeDtypeStruct((B, LP, 256), BF16),
        jax.ShapeDtypeStruct((B, LP, 256), F32),
        jax.ShapeDtypeStruct((B, LP, 512), BF16),
        jax.ShapeDtypeStruct((B, LP, 512), BF16),
        jax.ShapeDtypeStruct((B, LP, 256), F32),
        jax.ShapeDtypeStruct((B, LP, 512), BF16),
        jax.ShapeDtypeStruct((B, LP, 512), BF16),
        jax.ShapeDtypeStruct((B, LP, 512), BF16),
    ]
    return pl.pallas_call(
        _inproj_body,
        grid=(B, n_steps),
        in_specs=[
            frames(-1), frames(0),
            full((ROW_BLOCK, D)), full((1, D)), full((1, D)),
            full((D, _MAIN_WIDTH)), full((D, GLA_RANK)), full((GLA_RANK, 256)), full((1, 256)),
        ],
        out_specs=[row_spec(s.shape[-1]) for s in out_shapes],
        out_shape=out_shapes,
        compiler_params=pltpu.CompilerParams(
            dimension_semantics=("parallel", "parallel"), vmem_limit_bytes=VMEM_LIMIT,
            allow_input_fusion=[False, False, False, False, False, True, True, False, False]),
        name="inproj",
    )(x, x, meta_pad, ln_g, ln_b, w_main, w_ga, wa2, ba)


def _gla_body(q_ref, k_ref, v_ref, r_ref, la_ref, g_ref, o_ref):
    n_groups = q_ref.shape[1] // ROW_BLOCK
    ri = lax.broadcasted_iota(jnp.int32, (ROW_BLOCK, ROW_BLOCK), 0)
    ci = lax.broadcasted_iota(jnp.int32, (ROW_BLOCK, ROW_BLOCK), 1)
    later = jnp.logical_and(ri // CHUNK == ci // CHUNK, ci > ri).astype(BF16)
    sr = lax.broadcasted_iota(jnp.int32, (GLA_HEADS * GLA_DV, GLA_HEADS * GLA_DK), 0)
    sc = lax.broadcasted_iota(jnp.int32, (GLA_HEADS * GLA_DV, GLA_HEADS * GLA_DK), 1)
    same_head = sr // GLA_DV == sc // GLA_DK
    gain = g_ref[...]
    st = jnp.zeros((GLA_HEADS * GLA_DV, GLA_HEADS * GLA_DK), F32)

    for grp in range(n_groups):
        g0 = grp * ROW_BLOCK
        la = la_ref[0, g0:g0 + ROW_BLOCK, :]
        hi, mid, lo = _split3_bf16(la)
        suffix = (jnp.dot(later, hi, preferred_element_type=F32)
                  + jnp.dot(later, mid, preferred_element_type=F32)
                  + jnp.dot(later, lo, preferred_element_type=F32))
        kdec = (k_ref[0, g0:g0 + ROW_BLOCK, :] * jnp.exp(suffix)).astype(BF16)
        for half in range(ROW_BLOCK // CHUNK):
            f = half * CHUNK
            r0 = g0 + f
            tot = suffix[f:f + 1, :] + la[f:f + 1, :]
            upd = lax.dot_general(v_ref[0, r0:r0 + CHUNK, :], kdec[f:f + CHUNK, :],
                                  (((0,), (0,)), ((), ())), preferred_element_type=F32)
            st = st * jnp.exp(tot) + jnp.where(same_head, upd, 0.0)
            o = lax.dot_general(q_ref[0, r0:r0 + CHUNK, :], st.astype(BF16),
                                (((1,), (1,)), ((), ())), preferred_element_type=F32)
            r = r_ref[0, r0:r0 + CHUNK, :].astype(F32)
            for h in range(GLA_HEADS):
                oh = o[:, h * GLA_DV:(h + 1) * GLA_DV]
                rh = r[:, h * GLA_DV:(h + 1) * GLA_DV]
                ms = jnp.mean(oh * oh, axis=-1, keepdims=True)
                out = oh * lax.rsqrt(ms + LN_EPS) * gain * (rh * jax.nn.sigmoid(rh))
                o_ref[0, r0:r0 + CHUNK, h * GLA_DV:(h + 1) * GLA_DV] = out.astype(BF16)


def _gla(gq, gk, gv, gr, la, norm_g):
    B, LP, _ = gq.shape

    def seq(width):
        return pl.BlockSpec((1, LP, width), lambda b: (b, 0, 0))

    return pl.pallas_call(
        _gla_body,
        grid=(B,),
        in_specs=[seq(256), seq(256), seq(512), seq(512), seq(256),
                  pl.BlockSpec((1, GLA_DV), lambda b: (0, 0))],
        out_specs=seq(512),
        out_shape=jax.ShapeDtypeStruct((B, LP, 512), BF16),
        compiler_params=pltpu.CompilerParams(
            dimension_semantics=("parallel",), vmem_limit_bytes=VMEM_LIMIT),
        name="gla",
    )(gq, gk, gv, gr, la, norm_g)


ATT_BLOCK = 256


def _diff_body(q_ref, k_ref, v_ref, lq1_ref, lk1_ref, lq2_ref, lk2_ref, g_ref, o_ref,
               dmask_ref):
    h = pl.program_id(1)
    slope = jnp.where(h == 0, 2.0 ** -2, jnp.where(h == 1, 2.0 ** -4, jnp.where(h == 2, 2.0 ** -6, 2.0 ** -8)))
    slope = slope.astype(F32) * LOG2E
    lam = (jnp.exp(jnp.sum(lq1_ref[...] * lk1_ref[...], axis=-1, keepdims=True))
           - jnp.exp(jnp.sum(lq2_ref[...] * lk2_ref[...], axis=-1, keepdims=True)) + LAMBDA_INIT)
    gain = g_ref[...] * (1.0 - LAMBDA_INIT)
    n_qblocks = (q_ref.shape[1] - ROW_BLOCK) // ATT_BLOCK
    nt = (((1,), (1,)), ((), ()))

    def split_q(q):
        lane = lax.broadcasted_iota(jnp.int32, q.shape, 1)
        zero = jnp.zeros_like(q)
        return jnp.where(lane < DIFF_DH, q, zero), jnp.where(lane >= DIFF_DH, q, zero)

    def with_ones(v):
        return jnp.concatenate([v, jnp.ones_like(v)], axis=1)

    def softmax_av(qz, kk, vext, add_bias):
        s = add_bias(lax.dot_general(qz, kk, nt, preferred_element_type=F32))
        m = jnp.max(s, axis=-1, keepdims=True)
        p = jnp.exp2(s - m).astype(BF16)
        return jnp.dot(p, vext, preferred_element_type=F32)

    def finish(a1, a2):
        o = a1[:, :DIFF_DV] / a1[:, DIFF_DV:] - lam * (a2[:, :DIFF_DV] / a2[:, DIFF_DV:])
        ms = jnp.mean(o * o, axis=-1, keepdims=True)
        return (o * lax.rsqrt(ms + LN_EPS) * gain).astype(BF16)

    r = lax.broadcasted_iota(jnp.int32, (ROW_BLOCK, ROW_BLOCK), 0)
    c = lax.broadcasted_iota(jnp.int32, (ROW_BLOCK, ROW_BLOCK), 1)
    ok = jnp.logical_and(c // CHUNK <= r // CHUNK, c >= FRONT_PAD)
    bias_lead = jnp.where(ok, -slope * jnp.abs(r - c).astype(F32), NEG_INF)
    q1z, q2z = split_q(q_ref[0, 0:ROW_BLOCK, :])
    k_lead = k_ref[0, 0:ROW_BLOCK, :]
    v_lead = with_ones(v_ref[0, 0:ROW_BLOCK, :])
    o_ref[0, 0:ROW_BLOCK, :] = finish(softmax_av(q1z, k_lead, v_lead, lambda s: s + bias_lead),
                                      softmax_av(q2z, k_lead, v_lead, lambda s: s + bias_lead))

    r = lax.broadcasted_iota(jnp.int32, (ATT_BLOCK, ATT_BLOCK), 0)
    c = lax.broadcasted_iota(jnp.int32, (ATT_BLOCK, ATT_BLOCK), 1)
    rel = jnp.where(c <= r, c, 2 * r - c).astype(F32)
    dmask_ref[...] = jnp.where(c // CHUNK <= r // CHUNK, slope * rel, NEG_INF)

    for jq in range(n_qblocks):
        qbase = ROW_BLOCK + jq * ATT_BLOCK
        n_keys = qbase + ATT_BLOCK
        q1z, q2z = split_q(q_ref[0, qbase:qbase + ATT_BLOCK, :])
        kk = k_ref[0, 0:n_keys, :]
        vext = with_ones(v_ref[0, 0:n_keys, :])
        col = lax.broadcasted_iota(jnp.int32, (1, qbase), 1)
        col_bias = jnp.where(col >= FRONT_PAD, slope * (col - qbase).astype(F32), NEG_INF)

        def add_bias(s, col_bias=col_bias, qbase=qbase):
            return jnp.concatenate([s[:, :qbase] + col_bias, s[:, qbase:] + dmask_ref[...]], axis=1)

        o_ref[0, qbase:qbase + ATT_BLOCK, :] = finish(softmax_av(q1z, kk, vext, add_bias),
                                                      softmax_av(q2z, kk, vext, add_bias))


def _diff_attention(dq, dk, dv, lq1, lk1, lq2, lk2, norm_g):
    B, LP, W = dq.shape
    small = pl.BlockSpec((1, DIFF_DH), lambda b, h: (0, 0))
    seq = pl.BlockSpec((1, LP, 2 * DIFF_DH), lambda b, h: (b, 0, h))
    return pl.pallas_call(
        _diff_body,
        grid=(B, DIFF_HEADS),
        in_specs=[seq, seq, seq, small, small, small, small,
                  pl.BlockSpec((1, DIFF_DV), lambda b, h: (0, 0))],
        out_specs=seq,
        out_shape=jax.ShapeDtypeStruct((B, LP, W), BF16),
        scratch_shapes=[pltpu.VMEM((ATT_BLOCK, ATT_BLOCK), F32)],
        compiler_params=pltpu.CompilerParams(
            dimension_semantics=("parallel", "parallel"), vmem_limit_bytes=VMEM_LIMIT),
        name="diff_attn",
    )(dq, dk, dv, lq1, lk1, lq2, lk2, norm_g)


def _route_body(og_ref, od_ref, h0_ref, wog_ref, wod_ref, g_ref, b_ref, rw_ref, rb_ref,
                h1_ref, route_ref, keys_ref, cnt_ref, carry_ref, *, rows_per_seq, id_bits):
    t = pl.program_id(0)

    @pl.when(t == 0)
    def _():
        carry_ref[...] = jnp.zeros_like(carry_ref)

    mix = (jnp.dot(og_ref[...], wog_ref[...], preferred_element_type=F32)
           + jnp.dot(od_ref[...], wod_ref[...], preferred_element_type=F32))
    h1 = _layer_norm(DEEPNORM_ALPHA * h0_ref[...] + mix, g_ref[...], b_ref[...])
    _store_token_tiled(h1_ref, h1)

    hh, hm, _ = _split3_bf16(h1)
    wh, wm, _ = _split3_bf16(rw_ref[...])
    head = jnp.dot(hh, jnp.concatenate([wh, wm], axis=1), preferred_element_type=F32)
    logits = rb_ref[...] + head[:, :LANES] + head[:, LANES:] + jnp.dot(hm, wh, preferred_element_type=F32)

    T = logits.shape[0]
    work = logits.T[:N_EXPERTS]
    expert = lax.broadcasted_iota(jnp.int32, (N_EXPERTS, T), 0)
    tok = t * T + lax.broadcasted_iota(jnp.int32, (1, T), 1)
    valid = (tok % rows_per_seq) >= FRONT_PAD
    top_v, top_i = [], []
    for _ in range(TOP_K):
        mx = jnp.max(work, axis=0, keepdims=True)
        idx = jnp.min(jnp.where(work == mx, expert, N_EXPERTS), axis=0, keepdims=True)
        top_v.append(mx)
        top_i.append(idx)
        work = jnp.where(expert == idx, -jnp.inf, work)
    ex = [jnp.exp(v - top_v[0]) for v in top_v]
    den = ex[0] + ex[1] + ex[2] + ex[3]
    gates = [lax.bitcast_convert_type(e / den, jnp.int32) for e in ex]
    keys = [jnp.where(valid, idx, N_EXPERTS) * (1 << id_bits) + (tok * TOP_K + k) for k, idx in enumerate(top_i)]

    onehot = jnp.zeros((N_EXPERTS, T), F32)
    for idx in top_i:
        onehot = onehot + (expert == idx).astype(F32)
    carry_ref[...] = carry_ref[...] + jnp.sum(jnp.where(valid, onehot, 0.0), axis=1, keepdims=True)
    cnt_ref[...] = carry_ref[...]

    by_lane = jnp.concatenate(top_i + keys + gates + [jnp.zeros((LANES - 3 * TOP_K, T), jnp.int32)], axis=0)
    route_ref[...] = by_lane.T
    keys_ref[...] = jnp.concatenate([key[:, q * LANES:(q + 1) * LANES] for key in keys for q in range(T // LANES)], axis=0)


def _outproj_route(og, od, h0, wo_g, wo_d, ln_g, ln_b, rw, rb, rows_per_seq):
    NP, D = h0.shape
    T = ROUTE_TILE

    def rows(width):
        return pl.BlockSpec((T, width), lambda t: (t, 0))

    def full(shape):
        return pl.BlockSpec(shape, lambda t: (0,) * len(shape))

    key_rows = T * TOP_K // LANES
    return pl.pallas_call(
        functools.partial(_route_body, rows_per_seq=rows_per_seq, id_bits=(NP * TOP_K - 1).bit_length()),
        grid=(NP // T,),
        in_specs=[rows(512), rows(512), rows(D), full((512, D)), full((512, D)),
                  full((1, D)), full((1, D)), full((D, LANES)), full((1, LANES))],
        out_specs=[pl.BlockSpec((T * TOKEN_SUBROWS, LANES), lambda t: (t, 0)), rows(LANES),
                   pl.BlockSpec((key_rows, LANES), lambda t: (t, 0)), full((N_EXPERTS, 1))],
        out_shape=[jax.ShapeDtypeStruct((NP * TOKEN_SUBROWS, LANES), F32),
                   jax.ShapeDtypeStruct((NP, LANES), jnp.int32),
                   jax.ShapeDtypeStruct((NP * TOP_K // LANES, LANES), jnp.int32),
                   jax.ShapeDtypeStruct((N_EXPERTS, 1), F32)],
        scratch_shapes=[pltpu.VMEM((N_EXPERTS, 1), F32)],
        compiler_params=pltpu.CompilerParams(
            dimension_semantics=("arbitrary",), vmem_limit_bytes=VMEM_LIMIT),
        name="outproj_route",
    )(og, od, h0, wo_g, wo_d, ln_g, ln_b, rw, rb)


def _expert_body(order_ref, p0_ref, bstart_ref, nblk_ref, ntot_ref,
                 h1t_hbm, wgu_ref, bg_ref, bu_ref, wd_ref, bd_ref, yk_hbm,
                 wg_s, wu_s, wd_s, xb0, xb1, xb2, yb0, yb1, yb2, zbuf, gsem, ssem, zsem,
                 *, n_tokens, rows_per_seq):
    e = pl.program_id(0)
    n_total = ntot_ref[0]
    R = MOE_ROWS
    TS = TOKEN_SUBROWS
    xb = (xb0, xb1, xb2)
    yb = (yb0, yb1, yb2)
    NB = len(xb)

    def gather_start(b, slot):
        p0 = p0_ref[b]
        for r in range(R):
            tok = order_ref[p0 + r] >> TOP_K_SHIFT
            pltpu.make_async_copy(
                h1t_hbm.at[pl.ds(pl.multiple_of(tok * TS, TS), TS), :],
                xb[slot].at[pl.ds(r * TS, TS), :], gsem.at[slot]).start(priority=r % 2)

    def gather_wait(slot):
        pltpu.make_async_copy(h1t_hbm.at[pl.ds(0, R * TS), :], xb[slot], gsem.at[slot]).wait()

    def scatter_start(b, slot):
        p0 = p0_ref[b]
        for r in range(R):
            row = order_ref[p0 + r] * TS
            pltpu.make_async_copy(
                yb[slot].at[pl.ds(r * TS, TS), :],
                yk_hbm.at[pl.ds(pl.multiple_of(row, TS), TS), :], ssem.at[slot]).start(priority=r % 2)

    def scatter_wait(slot):
        pltpu.make_async_copy(yb[slot], yk_hbm.at[pl.ds(0, R * TS), :], ssem.at[slot]).wait()

    @pl.when(e == 0)
    def _():
        zbuf[...] = jnp.zeros_like(zbuf)
        lead = FRONT_PAD * TS
        fills = [((s * rows_per_seq * TOP_K + k * FRONT_PAD) * TS, lead)
                 for s in range(n_tokens // rows_per_seq) for k in range(TOP_K)]
        copies = [pltpu.make_async_copy(zbuf.at[pl.ds(0, n), :], yk_hbm.at[pl.ds(o, n), :], zsem)
                  for o, n in fills]
        for cp in copies:
            cp.start()
        for cp in copies:
            cp.wait()
        gather_start(0, 0)
        gather_start(jnp.minimum(1, n_total - 1), 1)

    @pl.when(nblk_ref[e] > 0)
    def _():
        W, H = MXU_TILE, MXU_TILE // 2
        r = lax.broadcasted_iota(jnp.int32, (W, W), 0)
        c = lax.broadcasted_iota(jnp.int32, (W, W), 1)
        perm = (r == jnp.where(c < H, 2 * c, 2 * (c - H) + 1)).astype(BF16)
        for tt in range(2 * D_FF // W):
            wt = wgu_ref[0, :, tt * W:(tt + 1) * W].astype(BF16)
            sp = jnp.dot(wt, perm, preferred_element_type=F32)
            wg_s[:, tt * H:(tt + 1) * H] = sp[:, :H].astype(BF16)
            wu_s[:, tt * H:(tt + 1) * H] = sp[:, H:].astype(BF16)
        wd_s[...] = wd_ref[0].astype(BF16)

    def run_block(b, slot):
        prv = (slot + 2) % NB
        gather_wait(slot)
        gather_start(jnp.minimum(b + 2, n_total - 1), prv)
        x = _load_token_tiled(xb[slot], R).astype(BF16)
        gt = jnp.dot(x, wg_s[...], preferred_element_type=F32) + bg_ref[0]
        up = jnp.dot(x, wu_s[...], preferred_element_type=F32) + bu_ref[0]
        gt = jnp.minimum(gt, SWIGLU_LIMIT)
        up = jnp.clip(up, -SWIGLU_LIMIT, SWIGLU_LIMIT)
        act = (up + 1.0) * (gt * jax.nn.sigmoid(SWIGLU_ALPHA * gt))
        y = jnp.dot(act.astype(BF16), wd_s[...], preferred_element_type=F32) + bd_ref[0]
        _store_token_tiled(yb[slot], y)

        @pl.when(b >= 1)
        def _():
            scatter_wait(prv)
        scatter_start(b, slot)

    def block(b, carry):
        for slot in range(NB):
            pl.when(b % NB == slot)(functools.partial(run_block, b, slot))
        return carry

    b0 = bstart_ref[e]
    lax.fori_loop(b0, b0 + nblk_ref[e], block, 0)

    @pl.when(e == pl.num_programs(0) - 1)
    def _():
        last = n_total - 1
        for slot in range(NB):
            @pl.when(last % NB == slot)
            def _(slot=slot):
                gather_wait((slot + 1) % NB)
                gather_wait((slot + 2) % NB)
                scatter_wait(slot)


def _experts(order, blk_p0, blk_start, n_blk, n_total, h1t, w_gate_up, b_gate, b_up, w_down,
             b_down, n_tokens, rows_per_seq):
    D = D_MODEL
    R = MOE_ROWS
    buf = pltpu.VMEM((R * TOKEN_SUBROWS, LANES), F32)
    per_expert = lambda e, *_: (e, 0, 0)
    grid_spec = pltpu.PrefetchScalarGridSpec(
        num_scalar_prefetch=5,
        grid=(N_EXPERTS,),
        in_specs=[
            pl.BlockSpec(memory_space=pl.ANY),
            pl.BlockSpec((1, D, 2 * D_FF), per_expert),
            pl.BlockSpec((1, 1, D_FF), per_expert),
            pl.BlockSpec((1, 1, D_FF), per_expert),
            pl.BlockSpec((1, D_FF, D), per_expert),
            pl.BlockSpec((1, 1, D), per_expert),
        ],
        out_specs=pl.BlockSpec(memory_space=pl.ANY),
        scratch_shapes=[pltpu.VMEM((D, D_FF), BF16), pltpu.VMEM((D, D_FF), BF16), pltpu.VMEM((D_FF, D), BF16),
                        buf, buf, buf, buf, buf, buf, pltpu.VMEM((FRONT_PAD * TOKEN_SUBROWS, LANES), F32),
                        pltpu.SemaphoreType.DMA((3,)), pltpu.SemaphoreType.DMA((3,)), pltpu.SemaphoreType.DMA],
    )
    out_rows = TOP_K * n_tokens * TOKEN_SUBROWS
    return pl.pallas_call(
        functools.partial(_expert_body, n_tokens=n_tokens, rows_per_seq=rows_per_seq),
        grid_spec=grid_spec,
        out_shape=jax.ShapeDtypeStruct((out_rows, LANES), F32),
        compiler_params=pltpu.CompilerParams(
            dimension_semantics=("arbitrary",), vmem_limit_bytes=EXPERT_VMEM_LIMIT),
        name="experts",
    )(order, blk_p0, blk_start, n_blk, n_total, h1t, w_gate_up, b_gate, b_up, w_down, b_down)


COMBINE_BLOCKS = 4


def _combine_body(*refs):
    per_block = 2 + TOP_K
    g_ref, b_ref, o_ref = refs[COMBINE_BLOCKS * per_block:]
    for i in range(COMBINE_BLOCKS):
        h1_ref, *y_refs, route_ref = refs[i * per_block:(i + 1) * per_block]
        route = route_ref[...]
        y = jnp.zeros((ROW_BLOCK, D_MODEL), F32)
        for k, yk_ref in enumerate(y_refs):
            gate = lax.bitcast_convert_type(route[:, 2 * TOP_K + k:2 * TOP_K + k + 1], F32)
            y = y + gate * _load_token_tiled(yk_ref.reshape(ROW_BLOCK * TOKEN_SUBROWS, LANES), ROW_BLOCK)
        h1 = _load_token_tiled(h1_ref, ROW_BLOCK)
        o_ref[0, i * ROW_BLOCK:(i + 1) * ROW_BLOCK, :] = _layer_norm(
            DEEPNORM_ALPHA * h1 + y, g_ref[...], b_ref[...])


def _combine(h1t, yk, route, ln_g, ln_b, batch, seq_len):
    nb = seq_len // ROW_BLOCK
    blocks_per_seq = nb + 1
    tile_rows = ROW_BLOCK * TOKEN_SUBROWS
    yk = yk.reshape(-1, TOP_K, TOKEN_SUBROWS, LANES)

    def token_block(i):
        return lambda b, j: b * blocks_per_seq + COMBINE_BLOCKS * j + i + 1

    in_specs, operands = [], []
    for i in range(COMBINE_BLOCKS):
        tb = token_block(i)
        in_specs.append(pl.BlockSpec((tile_rows, LANES), lambda b, j, tb=tb: (tb(b, j), 0)))
        in_specs += [pl.BlockSpec((ROW_BLOCK, None, TOKEN_SUBROWS, LANES), lambda b, j, tb=tb, k=k: (tb(b, j), k, 0, 0))
                     for k in range(TOP_K)]
        in_specs.append(pl.BlockSpec((ROW_BLOCK, LANES), lambda b, j, tb=tb: (tb(b, j), 0)))
        operands += [h1t] + [yk] * TOP_K + [route]
    vec = pl.BlockSpec((1, D_MODEL), lambda b, j: (0, 0))
    return pl.pallas_call(
        _combine_body,
        grid=(batch, nb // COMBINE_BLOCKS),
        in_specs=in_specs + [vec, vec],
        out_specs=pl.BlockSpec((1, COMBINE_BLOCKS * ROW_BLOCK, D_MODEL), lambda b, j: (b, j, 0)),
        out_shape=jax.ShapeDtypeStruct((batch, seq_len, D_MODEL), F32),
        compiler_params=pltpu.CompilerParams(
            dimension_semantics=("parallel", "parallel"), vmem_limit_bytes=VMEM_LIMIT),
        name="combine",
    )(*operands, ln_g, ln_b)


def kernel(x, meta_tokens, ln_emb_g, ln_emb_b, w_in, gla_wa2, gla_ba, gla_norm_g, diff_lambda_q1, diff_lambda_k1, diff_lambda_q2, diff_lambda_k2, diff_norm_g, w_out, ln1_g, ln1_b, router_w, router_b, w_gate_up, b_gate_up, w_down, b_down, ln2_g, ln2_b):
    B, S, D = x.shape
    LP = S + ROW_BLOCK
    NP = B * LP
    row = lambda v: v.reshape(1, -1)

    w = w_in[0]
    w_main = jnp.concatenate([w[:, :1536], w[:, 1552:]], axis=1).astype(BF16)
    w_ga = w[:, 1536:1552].astype(BF16)
    meta_pad = jnp.pad(meta_tokens, ((FRONT_PAD, 0), (0, 0)))

    h0, gq, gk, gv, gr, la, dq, dk, dv = _inproj(
        x, meta_pad, row(ln_emb_g), row(ln_emb_b), w_main, w_ga,
        gla_wa2[0].astype(BF16), row(gla_ba[0]))

    og = _gla(gq, gk, gv, gr, la, row(gla_norm_g[0]))
    od = _diff_attention(dq, dk, dv, row(diff_lambda_q1[0]), row(diff_lambda_k1[0]),
                         row(diff_lambda_q2[0]), row(diff_lambda_k2[0]), row(diff_norm_g[0]))

    wo = w_out[0].astype(BF16)
    rw_pad = jnp.pad(router_w[0], ((0, 0), (0, LANES - N_EXPERTS)))
    rb_pad = jnp.pad(row(router_b[0]), ((0, 0), (0, LANES - N_EXPERTS)))
    h1t, route, keys, counts = _outproj_route(
        og.reshape(NP, 512), od.reshape(NP, 512), h0.reshape(NP, D), wo[:512], wo[512:],
        row(ln1_g[0]), row(ln1_b[0]), rw_pad, rb_pad, LP)

    R = MOE_ROWS
    i32 = jnp.int32
    counts = counts[:, 0].astype(i32)
    n_blk = (counts + R - 1) // R
    blk_end = jnp.cumsum(n_blk)
    blk_start = blk_end - n_blk
    grp_start = jnp.cumsum(counts) - counts
    n_total = blk_end[-1:]
    n_assign_max = B * (S + N_META) * TOP_K
    max_blocks = (n_assign_max + N_EXPERTS * (R - 1)) // R
    g = jnp.minimum(jnp.arange(max_blocks, dtype=i32), n_total[0] - 1)
    is_e = (jnp.minimum(jnp.sum(g[:, None] >= blk_end[None, :], axis=1), N_EXPERTS - 1)[:, None]
            == jnp.arange(N_EXPERTS)[None, :])
    pick = lambda v: jnp.sum(jnp.where(is_e, v[None, :], 0), axis=1)
    local = (g - pick(blk_start)) * R
    blk_p0 = (pick(grp_start) + local).astype(i32)

    assert R <= B * FRONT_PAD * TOP_K
    order = jnp.sort(keys.reshape(-1), stable=False) & ((1 << (NP * TOP_K - 1).bit_length()) - 1)

    bgu = b_gate_up[0].reshape(N_EXPERTS, D_FF, 2)
    yk = _experts(order, blk_p0, blk_start.astype(i32), n_blk.astype(i32), n_total.astype(i32),
                  h1t, w_gate_up[0],
                  bgu[:, :, 0].reshape(N_EXPERTS, 1, D_FF), bgu[:, :, 1].reshape(N_EXPERTS, 1, D_FF),
                  w_down[0], b_down[0].reshape(N_EXPERTS, 1, D), NP, LP)

    return _combine(h1t, yk, route, row(ln2_g[0]), row(ln2_b[0]), B, S)
```
